```python
import jax
import jax.numpy as jnp
from jax import lax
import numpy as np

D_MODEL = 2048
BATCH = 16
SEQ = 2048
DEPTH = 2

MEM_LEN = 256
N_EVEN = (DEPTH + 1) // 2
N_ODD = DEPTH // 2
ROPE_THETA = 10000.0
LN_EPS = 1e-5
RMS_EPS = 1e-6
DEEPNORM_ALPHA = (2.0 * DEPTH) ** 0.25
DEEPNORM_BETA = (8.0 * DEPTH) ** -0.25
Q_BLOCK = 128
MIX_WIDTH = D_MODEL

MLA_HEADS = 8
MLA_NOPE = 128
MLA_ROPE = 64
MLA_V = 128
MLA_Q_RANK = 512
MLA_KV_RANK = 256
GLA_HEADS = 4
GLA_DK = 128
GLA_DV = 256
GLA_GATE_RANK = 16
GLA_TAU = 16.0
GLA_CHUNK = 64
DIL_HEADS = 8
DIL_HEAD_DIM = 128
DIL_BRANCHES = ((128, 1), (512, 4), (2048, 16))
RWKV_HEADS = 16
RWKV_HEAD_DIM = 64
RWKV_DECAY_RANK = 96
RWKV_A_RANK = 96
RWKV_GATE_RANK = 256
RWKV_GN_EPS = 64e-5
XA_HEADS = 4
XA_HEAD_DIM = D_MODEL // XA_HEADS
D_FF = -(-8 * D_MODEL // (3 * 256)) * 256

MLA_OUT = MLA_HEADS * MLA_V
GLA_OUT = GLA_HEADS * GLA_DV
DIL_WIDTH = DIL_HEADS * DIL_HEAD_DIM
RWKV_WIDTH = RWKV_HEADS * RWKV_HEAD_DIM
EVEN_IN_WIDTHS = (MLA_Q_RANK, MLA_KV_RANK, MLA_ROPE, GLA_HEADS * GLA_DK, GLA_HEADS * GLA_DK, GLA_OUT, GLA_OUT, GLA_GATE_RANK)
RWKV_IN_WIDTHS = (RWKV_WIDTH, RWKV_WIDTH, RWKV_WIDTH, RWKV_DECAY_RANK, RWKV_A_RANK, RWKV_GATE_RANK)
EVEN_IN = sum(EVEN_IN_WIDTHS)
RWKV_IN = sum(RWKV_IN_WIDTHS)
ODD_IN = 3 * DIL_WIDTH + RWKV_IN

kernel_name = "hybrid_mla_gla_dilated_rwkv7_deepnorm"


def _split(t, widths):
    points, acc = [], 0
    for w in widths[:-1]:
        acc += w
        points.append(acc)
    return jnp.split(t, points, axis=-1)


def layer_norm(x, g, b, eps=LN_EPS):
    xf = x.astype(jnp.float32)
    mu = jnp.mean(xf, axis=-1, keepdims=True)
    var = jnp.mean(jnp.square(xf - mu), axis=-1, keepdims=True)
    return ((xf - mu) * lax.rsqrt(var + eps)).astype(x.dtype) * g + b


def rms_norm(x, g):
    xf = x.astype(jnp.float32)
    return (xf * lax.rsqrt(jnp.mean(xf * xf, axis=-1, keepdims=True) + RMS_EPS)).astype(x.dtype) * g


def rope_tables(seq_len, dim):
    inv = ROPE_THETA ** (-jnp.arange(0, dim, 2, dtype=jnp.float32) / dim)
    ang = jnp.arange(seq_len, dtype=jnp.float32)[:, None] * inv[None, :]
    return jnp.cos(ang), jnp.sin(ang)


def apply_rope(x, cos, sin):
    x1, x2 = jnp.split(x, 2, axis=-1)
    c, s = cos.astype(x.dtype), sin.astype(x.dtype)
    return jnp.concatenate([x1 * c - x2 * s, x1 * s + x2 * c], axis=-1)


def token_shift(t, mu):
    prev = jnp.pad(t, ((0, 0), (1, 0), (0, 0)))[:, :-1]
    return t + (prev - t) * mu


def causal_block_attention(q, k, v, scale):
    B, H, S, dk = q.shape
    nb = S // Q_BLOCK
    qb = q.reshape(B, H, nb, Q_BLOCK, dk).transpose(2, 0, 1, 3, 4)
    kpos = jnp.arange(S)

    def one_block(args):
        qi, i = args
        s = jnp.einsum('bhqd,bhkd->bhqk', qi, k).astype(jnp.float32) * scale
        qpos = i * Q_BLOCK + jnp.arange(Q_BLOCK)
        s = jnp.where(kpos[None, :] <= qpos[:, None], s, -jnp.inf)
        p = jax.nn.softmax(s, axis=-1).astype(v.dtype)
        return jnp.einsum('bhqk,bhkd->bhqd', p, v)

    out = lax.map(one_block, (qb, jnp.arange(nb)))
    return out.transpose(1, 2, 0, 3, 4).reshape(B, H, S, v.shape[-1])


def gla(q, k, v, r, gate_lr, w_gate2, b_gate, norm_g, norm_b):
    B, S, _ = q.shape
    H, dk, dv, C = GLA_HEADS, GLA_DK, GLA_DV, GLA_CHUNK
    nc = S // C
    f32 = jnp.float32
    log_a = jax.nn.log_sigmoid((gate_lr @ w_gate2 + b_gate).astype(f32)) / GLA_TAU

    def chunks(t, d):
        return t.astype(f32).reshape(B, nc, C, H, d).transpose(0, 3, 1, 2, 4)

    qc = chunks(q, dk) * (dk ** -0.5)
    kc = chunks(k, dk)
    vc = chunks(v, dv)
    b = jnp.cumsum(chunks(log_a, dk), axis=3)
    b_last = b[:, :, :, -1:, :]
    q_dec = qc * jnp.exp(b)
    k_inv = kc * jnp.exp(-b)
    k_end = kc * jnp.exp(b_last - b)
    causal = jnp.tril(jnp.ones((C, C), dtype=bool))
    att = jnp.where(causal, jnp.einsum('bhnid,bhnjd->bhnij', q_dec, k_inv), 0.0)
    o_intra = jnp.einsum('bhnij,bhnjv->bhniv', att, vc)

    def step(state, xs):
        q_t, k_t, v_t, dec_t = xs
        o_t = jnp.einsum('bhcd,bhdv->bhcv', q_t, state)
        state = state * dec_t[..., None] + jnp.einsum('bhcd,bhcv->bhdv', k_t, v_t)
        return state, o_t

    xs = (jnp.moveaxis(q_dec, 2, 0), jnp.moveaxis(k_end, 2, 0), jnp.moveaxis(vc, 2, 0),
          jnp.moveaxis(jnp.exp(b_last[:, :, :, 0, :]), 2, 0))
    _, o_inter = lax.scan(step, jnp.zeros((B, H, dk, dv), f32), xs)
    o = o_intra + jnp.moveaxis(o_inter, 0, 2)
    o = o.transpose(0, 2, 3, 1, 4).reshape(B, S, H, dv)
    o = layer_norm(o, norm_g, norm_b).reshape(B, S, H * dv)
    return (o * jax.nn.silu(r.astype(f32))).astype(q.dtype)


def dilated_branch(q, k, v, window, dil):
    B, H, S, dh = q.shape
    span = window // dil
    L = S // dil
    nb = -(-L // span)
    Lp = nb * span

    def residues(t):
        t = t.reshape(B, H, L, dil, dh).transpose(0, 1, 3, 2, 4)
        t = jnp.pad(t, ((0, 0), (0, 0), (0, 0), (0, Lp - L), (0, 0)))
        return t.reshape(B, H, dil, nb, span, dh)

    def with_prev(t):
        prev = jnp.pad(t, ((0, 0), (0, 0), (0, 0), (1, 0), (0, 0), (0, 0)))[:, :, :, :-1]
        return jnp.concatenate([prev, t], axis=4)

    qb = residues(q)
    kw, vw = with_prev(residues(k)), with_prev(residues(v))
    s = jnp.einsum('bhrnqd,bhrnkd->bhrnqk', qb, kw).astype(jnp.float32) * (dh ** -0.5)
    qi = jnp.arange(span)[:, None] + span
    kj = jnp.arange(2 * span)[None, :]
    dist = qi - kj
    in_band = (dist >= 0) & (dist <= span)
    has_prev = (jnp.arange(nb) > 0)[:, None, None] | (kj >= span)[None]
    valid = in_band[None] & has_prev
    s = jnp.where(valid, s, -jnp.inf)
    m = jnp.max(s, axis=-1, keepdims=True)
    p = jnp.exp(s - m)
    den = jnp.sum(p, axis=-1, keepdims=True)
    o = jnp.einsum('bhrnqk,bhrnkd->bhrnqd', (p / den).astype(v.dtype), vw)
    lse = (m + jnp.log(den))[..., 0]

    def back(t):
        t = t.reshape((B, H, dil, Lp) + t.shape[5:])[:, :, :, :L]
        return jnp.moveaxis(t, 2, 3).reshape((B, H, S) + t.shape[4:])

    return back(o), back(lse)


def dilated_mixture(q, k, v):
    outs, lses = [], []
    for window, dil in DIL_BRANCHES:
        o, lse = dilated_branch(q, k, v, window, dil)
        outs.append(o)
        lses.append(lse)
    wts = jax.nn.softmax(jnp.stack(lses, axis=0), axis=0)
    return jnp.sum(wts[..., None] * jnp.stack(outs, axis=0).astype(jnp.float32), axis=0)


def rwkv7(r, k, v, w_lr, a_lr, g_lr, w0, w_decay2, a0, w_a2, w_gate2, k_k, k_a, r_k, gn_g, gn_b):
    B, S, _ = r.shape
    H, N = RWKV_HEADS, RWKV_HEAD_DIM
    f32 = jnp.float32
    w = -jax.nn.softplus(-(w0 + jnp.tanh(w_lr) @ w_decay2).astype(f32)) - 0.5
    decay = jnp.exp(-jnp.exp(w))
    a = jax.nn.sigmoid((a0 + a_lr @ w_a2).astype(f32))
    g = jax.nn.sigmoid(g_lr) @ w_gate2

    def heads(t):
        return t.astype(f32).reshape(B, S, H, N)

    kk = heads(k * k_k)
    kk = kk / jnp.maximum(jnp.sqrt(jnp.sum(kk * kk, axis=-1, keepdims=True)), 1e-12)
    ah = a.reshape(B, S, H, N)
    kh = heads(k) * (1.0 + (ah - 1.0) * k_a.astype(f32).reshape(H, N))
    rh, vh, wh = heads(r), heads(v), decay.reshape(B, S, H, N)

    def step(state, xs):
        r_t, w_t, k_t, v_t, kk_t, a_t = xs
        sa = jnp.einsum('bhij,bhj->bhi', state, -kk_t)
        state = (state * w_t[:, :, None, :] + sa[..., None] * (kk_t * a_t)[:, :, None, :]
                 + v_t[..., None] * k_t[:, :, None, :])
        return state, jnp.einsum('bhij,bhj->bhi', state, r_t)

    tm = lambda t: jnp.moveaxis(t, 1, 0)
    _, y = lax.scan(step, jnp.zeros((B, H, N, N), f32), (tm(rh), tm(wh), tm(kh), tm(vh), tm(kk), tm(ah)))
    y = jnp.moveaxis(y, 0, 1)
    mu = jnp.mean(y, axis=-1, keepdims=True)
    var = jnp.mean(jnp.square(y - mu), axis=-1, keepdims=True)
    y = ((y - mu) * lax.rsqrt(var + RWKV_GN_EPS)).reshape(B, S, H * N) * gn_g + gn_b
    bonus = jnp.sum(rh * kh * r_k.astype(f32), axis=-1, keepdims=True) * vh
    y = y + bonus.reshape(B, S, H * N)
    return (y * g).astype(r.dtype)


def even_mixer(x, cos, sin, w_in, q_norm, w_uq, kv_norm, w_ukv, w_gate2, b_gate, norm_g, norm_b, w_out):
    B, S, _ = x.shape
    H = MLA_HEADS
    c_q, c_kv, k_pe, q_g, k_g, v_g, r_g, lr_g = _split(x @ w_in, EVEN_IN_WIDTHS)
    q = (rms_norm(c_q, q_norm) @ w_uq).reshape(B, S, H, MLA_NOPE + MLA_ROPE).transpose(0, 2, 1, 3)
    kv = (rms_norm(c_kv, kv_norm) @ w_ukv).reshape(B, S, H, MLA_NOPE + MLA_V).transpose(0, 2, 1, 3)
    q_pe = apply_rope(q[..., MLA_NOPE:], cos, sin)
    k_pe = jnp.broadcast_to(apply_rope(k_pe[:, None], cos, sin), (B, H, S, MLA_ROPE))
    qf = jnp.concatenate([q[..., :MLA_NOPE], q_pe], axis=-1)
    kf = jnp.concatenate([kv[..., :MLA_NOPE], k_pe], axis=-1)
    a_out = causal_block_attention(qf, kf, kv[..., MLA_NOPE:], (MLA_NOPE + MLA_ROPE) ** -0.5)
    a_out = a_out.transpose(0, 2, 1, 3).reshape(B, S, MLA_OUT)
    b_out = gla(q_g, k_g, v_g, r_g, lr_g, w_gate2, b_gate, norm_g, norm_b)
    return jnp.concatenate([a_out, b_out], axis=-1) @ w_out


def odd_mixer(x, cos, sin, w_in, mu, w0, w_decay2, a0, w_a2, w_gate2, k_k, k_a, r_k, gn_g, gn_b, w_out):
    B, S, _ = x.shape
    h = x @ w_in
    c_in, d_in = h[..., :3 * DIL_WIDTH], h[..., 3 * DIL_WIDTH:]
    q, k, v = [t.reshape(B, S, DIL_HEADS, DIL_HEAD_DIM).transpose(0, 2, 1, 3) for t in jnp.split(c_in, 3, axis=-1)]
    q, k = apply_rope(q, cos, sin), apply_rope(k, cos, sin)
    c_out = dilated_mixture(q, k, v).transpose(0, 2, 1, 3).reshape(B, S, DIL_WIDTH).astype(x.dtype)
    r, kd, vd, w_lr, a_lr, g_lr = _split(token_shift(d_in, mu), RWKV_IN_WIDTHS)
    d_out = rwkv7(r, kd, vd, w_lr, a_lr, g_lr, w0, w_decay2, a0, w_a2, w_gate2, k_k, k_a, r_k, gn_g, gn_b)
    return jnp.concatenate([c_out, d_out], axis=-1) @ w_out


def cross_attention(x, mem, w_q, w_k, w_v, w_o):
    B, S, _ = x.shape
    M = mem.shape[1]
    q = (x @ w_q).reshape(B, S, XA_HEADS, XA_HEAD_DIM)
    k = (mem @ w_k).reshape(B, M, XA_HEADS, XA_HEAD_DIM)
    v = (mem @ w_v).reshape(B, M, XA_HEADS, XA_HEAD_DIM)
    s = jnp.einsum('bqhd,bkhd->bhqk', q, k).astype(jnp.float32) * (XA_HEAD_DIM ** -0.5)
    p = jax.nn.softmax(s, axis=-1).astype(v.dtype)
    o = jnp.einsum('bhqk,bkhd->bqhd', p, v).reshape(B, S, XA_HEADS * XA_HEAD_DIM)
    return o @ w_o


def swiglu(x, w_gate, w_up, w_down):
    return (jax.nn.silu(x @ w_gate) * (x @ w_up)) @ w_down


def _fwd_setup_inputs(seed: int = 0) -> dict:
    key = jax.random.key(seed)
    ks = iter(jax.random.split(key, 48))
    D = D_MODEL
    NE, NO, L = N_EVEN, N_ODD, DEPTH

    def nrm(shape, scale):
        return jax.random.normal(next(ks), shape, jnp.float32) * scale

    def gain(shape):
        return 1.0 + nrm(shape, 0.02)

    def unif(shape, lo, hi):
        return jax.random.uniform(next(ks), shape, jnp.float32, lo, hi)

    inp = {}
    inp['x'] = nrm((BATCH, SEQ, D), 1.0)
    inp['mem'] = nrm((BATCH, MEM_LEN, D), 1.0)
    inp['ev_w_in'] = nrm((NE, D, EVEN_IN), D ** -0.5)
    inp['ev_mla_q_norm'] = gain((NE, MLA_Q_RANK))
    inp['ev_mla_w_uq'] = nrm((NE, MLA_Q_RANK, MLA_HEADS * (MLA_NOPE + MLA_ROPE)), MLA_Q_RANK ** -0.5)
    inp['ev_mla_kv_norm'] = gain((NE, MLA_KV_RANK))
    inp['ev_mla_w_ukv'] = nrm((NE, MLA_KV_RANK, MLA_HEADS * (MLA_NOPE + MLA_V)), MLA_KV_RANK ** -0.5)
    inp['ev_gla_w_gate2'] = nrm((NE, GLA_GATE_RANK, GLA_HEADS * GLA_DK), GLA_GATE_RANK ** -0.5)
    inp['ev_gla_b_gate'] = nrm((NE, GLA_HEADS * GLA_DK), 0.1)
    inp['ev_gla_norm_g'] = gain((NE, GLA_DV))
    inp['ev_gla_norm_b'] = nrm((NE, GLA_DV), 0.02)
    inp['ev_w_out'] = nrm((NE, MIX_WIDTH, D), MIX_WIDTH ** -0.5 * DEEPNORM_BETA)
    inp['od_w_in'] = nrm((NO, D, ODD_IN), D ** -0.5)
    inp['od_rwkv_mu'] = unif((NO, RWKV_IN), 0.0, 1.0)
    inp['od_rwkv_w0'] = unif((NO, RWKV_WIDTH), -6.0, -1.0)
    inp['od_rwkv_w_decay2'] = nrm((NO, RWKV_DECAY_RANK, RWKV_WIDTH), RWKV_DECAY_RANK ** -0.5)
    inp['od_rwkv_a0'] = nrm((NO, RWKV_WIDTH), 0.1)
    inp['od_rwkv_w_a2'] = nrm((NO, RWKV_A_RANK, RWKV_WIDTH), RWKV_A_RANK ** -0.5)
    inp['od_rwkv_w_gate2'] = nrm((NO, RWKV_GATE_RANK, RWKV_WIDTH), RWKV_GATE_RANK ** -0.5)
    inp['od_rwkv_k_k'] = 0.85 + nrm((NO, RWKV_WIDTH), 0.02)
    inp['od_rwkv_k_a'] = gain((NO, RWKV_WIDTH))
    inp['od_rwkv_r_k'] = nrm((NO, RWKV_HEADS, RWKV_HEAD_DIM), 0.1)
    inp['od_rwkv_gn_g'] = gain((NO, RWKV_WIDTH))
    inp['od_rwkv_gn_b'] = nrm((NO, RWKV_WIDTH), 0.02)
    inp['od_w_out'] = nrm((NO, MIX_WIDTH, D), MIX_WIDTH ** -0.5 * DEEPNORM_BETA)
    inp['ln_mix_g'] = gain((L, D))
    inp['ln_mix_b'] = nrm((L, D), 0.02)
    inp['xa_w_q'] = nrm((L, D, D), D ** -0.5)
    inp['xa_w_k'] = nrm((L, D, D), D ** -0.5)
    inp['xa_w_v'] = nrm((L, D, D), D ** -0.5)
    inp['xa_w_o'] = nrm((L, D, D), D ** -0.5 * DEEPNORM_BETA)
    inp['ln_xa_g'] = gain((L, D))
    inp['ln_xa_b'] = nrm((L, D), 0.02)
    inp['ffn_w_gate'] = nrm((L, D, D_FF), D ** -0.5)
    inp['ffn_w_up'] = nrm((L, D, D_FF), D ** -0.5)
    inp['ffn_w_down'] = nrm((L, D_FF, D), D_FF ** -0.5 * DEEPNORM_BETA)
    inp['ln_ffn_g'] = gain((L, D))
    inp['ln_ffn_b'] = nrm((L, D), 0.02)
    return inp


def _fwd_reference(x, mem, ev_w_in, ev_mla_q_norm, ev_mla_w_uq, ev_mla_kv_norm, ev_mla_w_ukv, ev_gla_w_gate2,
              ev_gla_b_gate, ev_gla_norm_g, ev_gla_norm_b, ev_w_out, od_w_in, od_rwkv_mu, od_rwkv_w0,
              od_rwkv_w_decay2, od_rwkv_a0, od_rwkv_w_a2, od_rwkv_w_gate2, od_rwkv_k_k, od_rwkv_k_a, od_rwkv_r_k,
              od_rwkv_gn_g, od_rwkv_gn_b, od_w_out, ln_mix_g, ln_mix_b, xa_w_q, xa_w_k, xa_w_v, xa_w_o,
              ln_xa_g, ln_xa_b, ffn_w_gate, ffn_w_up, ffn_w_down, ln_ffn_g, ln_ffn_b):
    S = x.shape[1]
    cos_pe, sin_pe = rope_tables(S, MLA_ROPE)
    cos_c, sin_c = rope_tables(S, DIL_HEAD_DIM)
    h = x
    for layer in range(DEPTH):
        i = layer // 2
        if layer % 2 == 0:
            mix = even_mixer(h, cos_pe, sin_pe, ev_w_in[i], ev_mla_q_norm[i], ev_mla_w_uq[i], ev_mla_kv_norm[i],
                             ev_mla_w_ukv[i], ev_gla_w_gate2[i], ev_gla_b_gate[i], ev_gla_norm_g[i],
                             ev_gla_norm_b[i], ev_w_out[i])
        else:
            mix = odd_mixer(h, cos_c, sin_c, od_w_in[i], od_rwkv_mu[i], od_rwkv_w0[i], od_rwkv_w_decay2[i],
                            od_rwkv_a0[i], od_rwkv_w_a2[i], od_rwkv_w_gate2[i], od_rwkv_k_k[i], od_rwkv_k_a[i],
                            od_rwkv_r_k[i], od_rwkv_gn_g[i], od_rwkv_gn_b[i], od_w_out[i])
        h = layer_norm(DEEPNORM_ALPHA * h + mix, ln_mix_g[layer], ln_mix_b[layer])
        xa = cross_attention(h, mem, xa_w_q[layer], xa_w_k[layer], xa_w_v[layer], xa_w_o[layer])
        h = layer_norm(DEEPNORM_ALPHA * h + xa, ln_xa_g[layer], ln_xa_b[layer])
        ff = swiglu(h, ffn_w_gate[layer], ffn_w_up[layer], ffn_w_down[layer])
        h = layer_norm(DEEPNORM_ALPHA * h + ff, ln_ffn_g[layer], ln_ffn_b[layer])
    return h


import jax as _jax
import jax.numpy as _jnp

TWIN_FORMAT = 'train_step'
FWD_PARAMS = ['x', 'mem', 'ev_w_in', 'ev_mla_q_norm', 'ev_mla_w_uq', 'ev_mla_kv_norm', 'ev_mla_w_ukv', 'ev_gla_w_gate2', 'ev_gla_b_gate', 'ev_gla_norm_g', 'ev_gla_norm_b', 'ev_w_out', 'od_w_in', 'od_rwkv_mu', 'od_rwkv_w0', 'od_rwkv_w_decay2', 'od_rwkv_a0', 'od_rwkv_w_a2', 'od_rwkv_w_gate2', 'od_rwkv_k_k', 'od_rwkv_k_a', 'od_rwkv_r_k', 'od_rwkv_gn_g', 'od_rwkv_gn_b', 'od_w_out', 'ln_mix_g', 'ln_mix_b', 'xa_w_q', 'xa_w_k', 'xa_w_v', 'xa_w_o', 'ln_xa_g', 'ln_xa_b', 'ffn_w_gate', 'ffn_w_up', 'ffn_w_down', 'ln_ffn_g', 'ln_ffn_b']
TWIN_WEIGHTS = ['ev_w_in', 'ev_mla_q_norm', 'ev_mla_w_uq', 'ev_mla_kv_norm', 'ev_mla_w_ukv', 'ev_gla_w_gate2', 'ev_gla_b_gate', 'ev_gla_norm_g', 'ev_gla_norm_b', 'ev_w_out', 'od_w_in', 'od_rwkv_mu', 'od_rwkv_w0', 'od_rwkv_w_decay2', 'od_rwkv_a0', 'od_rwkv_w_a2', 'od_rwkv_w_gate2', 'od_rwkv_k_k', 'od_rwkv_k_a', 'od_rwkv_r_k', 'od_rwkv_gn_g', 'od_rwkv_gn_b', 'od_w_out', 'ln_mix_g', 'ln_mix_b', 'xa_w_q', 'xa_w_k', 'xa_w_v', 'xa_w_o', 'ln_xa_g', 'ln_xa_b', 'ffn_w_gate', 'ffn_w_up', 'ffn_w_down', 'ln_ffn_g', 'ln_ffn_b']
TWIN_DIFF_INPUT = 'x'
TWIN_INPUTS = ['x', 'mem', 'ev_w_in', 'ev_mla_q_norm', 'ev_mla_w_uq', 'ev_mla_kv_norm', 'ev_mla_w_ukv', 'ev_gla_w_gate2', 'ev_gla_b_gate', 'ev_gla_norm_g', 'ev_gla_norm_b', 'ev_w_out', 'od_w_in', 'od_rwkv_mu', 'od_rwkv_w0', 'od_rwkv_w_decay2', 'od_rwkv_a0', 'od_rwkv_w_a2', 'od_rwkv_w_gate2', 'od_rwkv_k_k', 'od_rwkv_k_a', 'od_rwkv_r_k', 'od_rwkv_gn_g', 'od_rwkv_gn_b', 'od_w_out', 'ln_mix_g', 'ln_mix_b', 'xa_w_q', 'xa_w_k', 'xa_w_v', 'xa_w_o', 'ln_xa_g', 'ln_xa_b', 'ffn_w_gate', 'ffn_w_up', 'ffn_w_down', 'ln_ffn_g', 'ln_ffn_b', 'loss_target', 'm_ev_w_in', 'm_ev_mla_q_norm', 'm_ev_mla_w_uq', 'm_ev_mla_kv_norm', 'm_ev_mla_w_ukv', 'm_ev_gla_w_gate2', 'm_ev_gla_b_gate', 'm_ev_gla_norm_g', 'm_ev_gla_norm_b', 'm_ev_w_out', 'm_od_w_in', 'm_od_rwkv_mu', 'm_od_rwkv_w0', 'm_od_rwkv_w_decay2', 'm_od_rwkv_a0', 'm_od_rwkv_w_a2', 'm_od_rwkv_w_gate2', 'm_od_rwkv_k_k', 'm_od_rwkv_k_a', 'm_od_rwkv_r_k', 'm_od_rwkv_gn_g', 'm_od_rwkv_gn_b', 'm_od_w_out', 'm_ln_mix_g', 'm_ln_mix_b', 'm_xa_w_q', 'm_xa_w_k', 'm_xa_w_v', 'm_xa_w_o', 'm_ln_xa_g', 'm_ln_xa_b', 'm_ffn_w_gate', 'm_ffn_w_up', 'm_ffn_w_down', 'm_ln_ffn_g', 'm_ln_ffn_b', 'v_ev_w_in', 'v_ev_mla_q_norm', 'v_ev_mla_w_uq', 'v_ev_mla_kv_norm', 'v_ev_mla_w_ukv', 'v_ev_gla_w_gate2', 'v_ev_gla_b_gate', 'v_ev_gla_norm_g', 'v_ev_gla_norm_b', 'v_ev_w_out', 'v_od_w_in', 'v_od_rwkv_mu', 'v_od_rwkv_w0', 'v_od_rwkv_w_decay2', 'v_od_rwkv_a0', 'v_od_rwkv_w_a2', 'v_od_rwkv_w_gate2', 'v_od_rwkv_k_k', 'v_od_rwkv_k_a', 'v_od_rwkv_r_k', 'v_od_rwkv_gn_g', 'v_od_rwkv_gn_b', 'v_od_w_out', 'v_ln_mix_g', 'v_ln_mix_b', 'v_xa_w_q', 'v_xa_w_k', 'v_xa_w_v', 'v_xa_w_o', 'v_ln_xa_g', 'v_ln_xa_b', 'v_ffn_w_gate', 'v_ffn_w_up', 'v_ffn_w_down', 'v_ln_ffn_g', 'v_ln_ffn_b']
TWIN_OUTPUTS = ['loss', 'grad_x', 'grad_ev_w_in', 'grad_ev_mla_q_norm', 'grad_ev_mla_w_uq', 'grad_ev_mla_kv_norm', 'grad_ev_mla_w_ukv', 'grad_ev_gla_w_gate2', 'grad_ev_gla_b_gate', 'grad_ev_gla_norm_g', 'grad_ev_gla_norm_b', 'grad_ev_w_out', 'grad_od_w_in', 'grad_od_rwkv_mu', 'grad_od_rwkv_w0', 'grad_od_rwkv_w_decay2', 'grad_od_rwkv_a0', 'grad_od_rwkv_w_a2', 'grad_od_rwkv_w_gate2', 'grad_od_rwkv_k_k', 'grad_od_rwkv_k_a', 'grad_od_rwkv_r_k', 'grad_od_rwkv_gn_g', 'grad_od_rwkv_gn_b', 'grad_od_w_out', 'grad_ln_mix_g', 'grad_ln_mix_b', 'grad_xa_w_q', 'grad_xa_w_k', 'grad_xa_w_v', 'grad_xa_w_o', 'grad_ln_xa_g', 'grad_ln_xa_b', 'grad_ffn_w_gate', 'grad_ffn_w_up', 'grad_ffn_w_down', 'grad_ln_ffn_g', 'grad_ln_ffn_b', 'delta_ev_w_in', 'delta_ev_mla_q_norm', 'delta_ev_mla_w_uq', 'delta_ev_mla_kv_norm', 'delta_ev_mla_w_ukv', 'delta_ev_gla_w_gate2', 'delta_ev_gla_b_gate', 'delta_ev_gla_norm_g', 'delta_ev_gla_norm_b', 'delta_ev_w_out', 'delta_od_w_in', 'delta_od_rwkv_mu', 'delta_od_rwkv_w0', 'delta_od_rwkv_w_decay2', 'delta_od_rwkv_a0', 'delta_od_rwkv_w_a2', 'delta_od_rwkv_w_gate2', 'delta_od_rwkv_k_k', 'delta_od_rwkv_k_a', 'delta_od_rwkv_r_k', 'delta_od_rwkv_gn_g', 'delta_od_rwkv_gn_b', 'delta_od_w_out', 'delta_ln_mix_g', 'delta_ln_mix_b', 'delta_xa_w_q', 'delta_xa_w_k', 'delta_xa_w_v', 'delta_xa_w_o', 'delta_ln_xa_g', 'delta_ln_xa_b', 'delta_ffn_w_gate', 'delta_ffn_w_up', 'delta_ffn_w_down', 'delta_ln_ffn_g', 'delta_ln_ffn_b', 'new_m_ev_w_in', 'new_m_ev_mla_q_norm', 'new_m_ev_mla_w_uq', 'new_m_ev_mla_kv_norm', 'new_m_ev_mla_w_ukv', 'new_m_ev_gla_w_gate2', 'new_m_ev_gla_b_gate', 'new_m_ev_gla_norm_g', 'new_m_ev_gla_norm_b', 'new_m_ev_w_out', 'new_m_od_w_in', 'new_m_od_rwkv_mu', 'new_m_od_rwkv_w0', 'new_m_od_rwkv_w_decay2', 'new_m_od_rwkv_a0', 'new_m_od_rwkv_w_a2', 'new_m_od_rwkv_w_gate2', 'new_m_od_rwkv_k_k', 'new_m_od_rwkv_k_a', 'new_m_od_rwkv_r_k', 'new_m_od_rwkv_gn_g', 'new_m_od_rwkv_gn_b', 'new_m_od_w_out', 'new_m_ln_mix_g', 'new_m_ln_mix_b', 'new_m_xa_w_q', 'new_m_xa_w_k', 'new_m_xa_w_v', 'new_m_xa_w_o', 'new_m_ln_xa_g', 'new_m_ln_xa_b', 'new_m_ffn_w_gate', 'new_m_ffn_w_up', 'new_m_ffn_w_down', 'new_m_ln_ffn_g', 'new_m_ln_ffn_b', 'new_v_ev_w_in', 'new_v_ev_mla_q_norm', 'new_v_ev_mla_w_uq', 'new_v_ev_mla_kv_norm', 'new_v_ev_mla_w_ukv', 'new_v_ev_gla_w_gate2', 'new_v_ev_gla_b_gate', 'new_v_ev_gla_norm_g', 'new_v_ev_gla_norm_b', 'new_v_ev_w_out', 'new_v_od_w_in', 'new_v_od_rwkv_mu', 'new_v_od_rwkv_w0', 'new_v_od_rwkv_w_decay2', 'new_v_od_rwkv_a0', 'new_v_od_rwkv_w_a2', 'new_v_od_rwkv_w_gate2', 'new_v_od_rwkv_k_k', 'new_v_od_rwkv_k_a', 'new_v_od_rwkv_r_k', 'new_v_od_rwkv_gn_g', 'new_v_od_rwkv_gn_b', 'new_v_od_w_out', 'new_v_ln_mix_g', 'new_v_ln_mix_b', 'new_v_xa_w_q', 'new_v_xa_w_k', 'new_v_xa_w_v', 'new_v_xa_w_o', 'new_v_ln_xa_g', 'new_v_ln_xa_b', 'new_v_ffn_w_gate', 'new_v_ffn_w_up', 'new_v_ffn_w_down', 'new_v_ln_ffn_g', 'new_v_ln_ffn_b']
TWIN_LEAF_KINDS = {'loss': 'loss', 'grad_x': 'grad_x', 'grad_ev_w_in': 'grad_w', 'grad_ev_mla_q_norm': 'grad_w', 'grad_ev_mla_w_uq': 'grad_w', 'grad_ev_mla_kv_norm': 'grad_w', 'grad_ev_mla_w_ukv': 'grad_w', 'grad_ev_gla_w_gate2': 'grad_w', 'grad_ev_gla_b_gate': 'grad_w', 'grad_ev_gla_norm_g': 'grad_w', 'grad_ev_gla_norm_b': 'grad_w', 'grad_ev_w_out': 'grad_w', 'grad_od_w_in': 'grad_w', 'grad_od_rwkv_mu': 'grad_w', 'grad_od_rwkv_w0': 'grad_w', 'grad_od_rwkv_w_decay2': 'grad_w', 'grad_od_rwkv_a0': 'grad_w', 'grad_od_rwkv_w_a2': 'grad_w', 'grad_od_rwkv_w_gate2': 'grad_w', 'grad_od_rwkv_k_k': 'grad_w', 'grad_od_rwkv_k_a': 'grad_w', 'grad_od_rwkv_r_k': 'grad_w', 'grad_od_rwkv_gn_g': 'grad_w', 'grad_od_rwkv_gn_b': 'grad_w', 'grad_od_w_out': 'grad_w', 'grad_ln_mix_g': 'grad_w', 'grad_ln_mix_b': 'grad_w', 'grad_xa_w_q': 'grad_w', 'grad_xa_w_k': 'grad_w', 'grad_xa_w_v': 'grad_w', 'grad_xa_w_o': 'grad_w', 'grad_ln_xa_g': 'grad_w', 'grad_ln_xa_b': 'grad_w', 'grad_ffn_w_gate': 'grad_w', 'grad_ffn_w_up': 'grad_w', 'grad_ffn_w_down': 'grad_w', 'grad_ln_ffn_g': 'grad_w', 'grad_ln_ffn_b': 'grad_w', 'delta_ev_w_in': 'delta_w', 'delta_ev_mla_q_norm': 'delta_w', 'delta_ev_mla_w_uq': 'delta_w', 'delta_ev_mla_kv_norm': 'delta_w', 'delta_ev_mla_w_ukv': 'delta_w', 'delta_ev_gla_w_gate2': 'delta_w', 'delta_ev_gla_b_gate': 'delta_w', 'delta_ev_gla_norm_g': 'delta_w', 'delta_ev_gla_norm_b': 'delta_w', 'delta_ev_w_out': 'delta_w', 'delta_od_w_in': 'delta_w', 'delta_od_rwkv_mu': 'delta_w', 'delta_od_rwkv_w0': 'delta_w', 'delta_od_rwkv_w_decay2': 'delta_w', 'delta_od_rwkv_a0': 'delta_w', 'delta_od_rwkv_w_a2': 'delta_w', 'delta_od_rwkv_w_gate2': 'delta_w', 'delta_od_rwkv_k_k': 'delta_w', 'delta_od_rwkv_k_a': 'delta_w', 'delta_od_rwkv_r_k': 'delta_w', 'delta_od_rwkv_gn_g': 'delta_w', 'delta_od_rwkv_gn_b': 'delta_w', 'delta_od_w_out': 'delta_w', 'delta_ln_mix_g': 'delta_w', 'delta_ln_mix_b': 'delta_w', 'delta_xa_w_q': 'delta_w', 'delta_xa_w_k': 'delta_w', 'delta_xa_w_v': 'delta_w', 'delta_xa_w_o': 'delta_w', 'delta_ln_xa_g': 'delta_w', 'delta_ln_xa_b': 'delta_w', 'delta_ffn_w_gate': 'delta_w', 'delta_ffn_w_up': 'delta_w', 'delta_ffn_w_down': 'delta_w', 'delta_ln_ffn_g': 'delta_w', 'delta_ln_ffn_b': 'delta_w', 'new_m_ev_w_in': 'new_m', 'new_m_ev_mla_q_norm': 'new_m', 'new_m_ev_mla_w_uq': 'new_m', 'new_m_ev_mla_kv_norm': 'new_m', 'new_m_ev_mla_w_ukv': 'new_m', 'new_m_ev_gla_w_gate2': 'new_m', 'new_m_ev_gla_b_gate': 'new_m', 'new_m_ev_gla_norm_g': 'new_m', 'new_m_ev_gla_norm_b': 'new_m', 'new_m_ev_w_out': 'new_m', 'new_m_od_w_in': 'new_m', 'new_m_od_rwkv_mu': 'new_m', 'new_m_od_rwkv_w0': 'new_m', 'new_m_od_rwkv_w_decay2': 'new_m', 'new_m_od_rwkv_a0': 'new_m', 'new_m_od_rwkv_w_a2': 'new_m', 'new_m_od_rwkv_w_gate2': 'new_m', 'new_m_od_rwkv_k_k': 'new_m', 'new_m_od_rwkv_k_a': 'new_m', 'new_m_od_rwkv_r_k': 'new_m', 'new_m_od_rwkv_gn_g': 'new_m', 'new_m_od_rwkv_gn_b': 'new_m', 'new_m_od_w_out': 'new_m', 'new_m_ln_mix_g': 'new_m', 'new_m_ln_mix_b': 'new_m', 'new_m_xa_w_q': 'new_m', 'new_m_xa_w_k': 'new_m', 'new_m_xa_w_v': 'new_m', 'new_m_xa_w_o': 'new_m', 'new_m_ln_xa_g': 'new_m', 'new_m_ln_xa_b': 'new_m', 'new_m_ffn_w_gate': 'new_m', 'new_m_ffn_w_up': 'new_m', 'new_m_ffn_w_down': 'new_m', 'new_m_ln_ffn_g': 'new_m', 'new_m_ln_ffn_b': 'new_m', 'new_v_ev_w_in': 'new_v', 'new_v_ev_mla_q_norm': 'new_v', 'new_v_ev_mla_w_uq': 'new_v', 'new_v_ev_mla_kv_norm': 'new_v', 'new_v_ev_mla_w_ukv': 'new_v', 'new_v_ev_gla_w_gate2': 'new_v', 'new_v_ev_gla_b_gate': 'new_v', 'new_v_ev_gla_norm_g': 'new_v', 'new_v_ev_gla_norm_b': 'new_v', 'new_v_ev_w_out': 'new_v', 'new_v_od_w_in': 'new_v', 'new_v_od_rwkv_mu': 'new_v', 'new_v_od_rwkv_w0': 'new_v', 'new_v_od_rwkv_w_decay2': 'new_v', 'new_v_od_rwkv_a0': 'new_v', 'new_v_od_rwkv_w_a2': 'new_v', 'new_v_od_rwkv_w_gate2': 'new_v', 'new_v_od_rwkv_k_k': 'new_v', 'new_v_od_rwkv_k_a': 'new_v', 'new_v_od_rwkv_r_k': 'new_v', 'new_v_od_rwkv_gn_g': 'new_v', 'new_v_od_rwkv_gn_b': 'new_v', 'new_v_od_w_out': 'new_v', 'new_v_ln_mix_g': 'new_v', 'new_v_ln_mix_b': 'new_v', 'new_v_xa_w_q': 'new_v', 'new_v_xa_w_k': 'new_v', 'new_v_xa_w_v': 'new_v', 'new_v_xa_w_o': 'new_v', 'new_v_ln_xa_g': 'new_v', 'new_v_ln_xa_b': 'new_v', 'new_v_ffn_w_gate': 'new_v', 'new_v_ffn_w_up': 'new_v', 'new_v_ffn_w_down': 'new_v', 'new_v_ln_ffn_g': 'new_v', 'new_v_ln_ffn_b': 'new_v'}


def _forward(args):
    return _fwd_reference(*[args[k] for k in FWD_PARAMS])


def _output_shape():
    out = _jax.eval_shape(lambda: _forward(_fwd_setup_inputs(0)))
    return out.shape, out.dtype

N_MICROBATCH = 1
ADAM_LR = 0.001
ADAM_B1 = 0.9
ADAM_B2 = 0.999
ADAM_EPS = 1e-08
ADAM_WD = 0.01
ADAM_STEP = 10
PER_EXAMPLE_BATCH_AXIS = {'x': 0, 'mem': 0, 'loss_target': 0}
SHARED_INPUTS = []
_WEIGHT_DTYPES = {'ev_w_in': _jnp.float32, 'ev_mla_q_norm': _jnp.float32, 'ev_mla_w_uq': _jnp.float32, 'ev_mla_kv_norm': _jnp.float32, 'ev_mla_w_ukv': _jnp.float32, 'ev_gla_w_gate2': _jnp.float32, 'ev_gla_b_gate': _jnp.float32, 'ev_gla_norm_g': _jnp.float32, 'ev_gla_norm_b': _jnp.float32, 'ev_w_out': _jnp.float32, 'od_w_in': _jnp.float32, 'od_rwkv_mu': _jnp.float32, 'od_rwkv_w0': _jnp.float32, 'od_rwkv_w_decay2': _jnp.float32, 'od_rwkv_a0': _jnp.float32, 'od_rwkv_w_a2': _jnp.float32, 'od_rwkv_w_gate2': _jnp.float32, 'od_rwkv_k_k': _jnp.float32, 'od_rwkv_k_a': _jnp.float32, 'od_rwkv_r_k': _jnp.float32, 'od_rwkv_gn_g': _jnp.float32, 'od_rwkv_gn_b': _jnp.float32, 'od_w_out': _jnp.float32, 'ln_mix_g': _jnp.float32, 'ln_mix_b': _jnp.float32, 'xa_w_q': _jnp.float32, 'xa_w_k': _jnp.float32, 'xa_w_v': _jnp.float32, 'xa_w_o': _jnp.float32, 'ln_xa_g': _jnp.float32, 'ln_xa_b': _jnp.float32, 'ffn_w_gate': _jnp.float32, 'ffn_w_up': _jnp.float32, 'ffn_w_down': _jnp.float32, 'ln_ffn_g': _jnp.float32, 'ln_ffn_b': _jnp.float32}
MOMENT_SCALE = {'ev_w_in': 2.137770e-02, 'ev_mla_q_norm': 8.591213e-03, 'ev_mla_w_uq': 4.912381e-03, 'ev_mla_kv_norm': 1.801751e-02, 'ev_mla_w_ukv': 6.386057e-03, 'ev_gla_w_gate2': 3.509262e-03, 'ev_gla_b_gate': 1.541355e-02, 'ev_gla_norm_g': 4.159670e-02, 'ev_gla_norm_b': 5.359997e-02, 'ev_w_out': 2.990307e-02, 'od_w_in': 1.359462e-02, 'od_rwkv_mu': 2.836506e-02, 'od_rwkv_w0': 6.800378e-03, 'od_rwkv_w_decay2': 8.917144e-04, 'od_rwkv_a0': 6.559756e-03, 'od_rwkv_w_a2': 6.129422e-03, 'od_rwkv_w_gate2': 1.774756e-02, 'od_rwkv_k_k': 2.241231e-02, 'od_rwkv_k_a': 1.845103e-02, 'od_rwkv_r_k': 4.065455e-02, 'od_rwkv_gn_g': 1.672684e-02, 'od_rwkv_gn_b': 3.907290e-02, 'od_w_out': 2.645741e-02, 'ln_mix_g': 5.059141e-01, 'ln_mix_b': 2.373598e-01, 'xa_w_q': 3.346861e-03, 'xa_w_k': 3.355568e-03, 'xa_w_v': 3.897427e-03, 'xa_w_o': 7.768425e-03, 'ln_xa_g': 5.075554e-01, 'ln_xa_b': 2.377009e-01, 'ffn_w_gate': 1.164173e-02, 'ffn_w_up': 1.126216e-02, 'ffn_w_down': 3.736659e-02, 'ln_ffn_g': 1.135258e+01, 'ln_ffn_b': 4.756852e-01}


def _to_microbatches(a, axis):
    t = _jnp.moveaxis(a, axis, 0)
    t = t.reshape((N_MICROBATCH, t.shape[0] // N_MICROBATCH) + t.shape[1:])
    return _jnp.moveaxis(t, 1, axis + 1)


def setup_inputs(seed: int = 0) -> dict:
    inp = _fwd_setup_inputs(seed)
    key = _jax.random.fold_in(_jax.random.key(seed), 7919)
    shape, _ = _output_shape()
    out = dict(inp)
    out["loss_target"] = _jax.random.normal(_jax.random.fold_in(key, 0), shape, _jnp.float32)
    for i, name in enumerate(TWIN_WEIGHTS):
        w = inp[name].astype(_jnp.float32)
        if MOMENT_SCALE is None:
            s = _jnp.sqrt(_jnp.mean(_jnp.square(w)) + 1e-30)
        else:
            s = MOMENT_SCALE[name]
        km, kv = _jax.random.split(_jax.random.fold_in(key, i + 1))
        out[name] = w
        out["m_" + name] = s * _jax.random.normal(km, w.shape, _jnp.float32)
        out["v_" + name] = (s * s) * _jax.random.uniform(kv, w.shape, _jnp.float32, 0.5, 1.5)
    if N_MICROBATCH > 1:
        for name, axis in PER_EXAMPLE_BATCH_AXIS.items():
            out[name] = _to_microbatches(out[name], axis)
    return {'x': out['x'], 'mem': out['mem'], 'ev_w_in': out['ev_w_in'], 'ev_mla_q_norm': out['ev_mla_q_norm'], 'ev_mla_w_uq': out['ev_mla_w_uq'], 'ev_mla_kv_norm': out['ev_mla_kv_norm'], 'ev_mla_w_ukv': out['ev_mla_w_ukv'], 'ev_gla_w_gate2': out['ev_gla_w_gate2'], 'ev_gla_b_gate': out['ev_gla_b_gate'], 'ev_gla_norm_g': out['ev_gla_norm_g'], 'ev_gla_norm_b': out['ev_gla_norm_b'], 'ev_w_out': out['ev_w_out'], 'od_w_in': out['od_w_in'], 'od_rwkv_mu': out['od_rwkv_mu'], 'od_rwkv_w0': out['od_rwkv_w0'], 'od_rwkv_w_decay2': out['od_rwkv_w_decay2'], 'od_rwkv_a0': out['od_rwkv_a0'], 'od_rwkv_w_a2': out['od_rwkv_w_a2'], 'od_rwkv_w_gate2': out['od_rwkv_w_gate2'], 'od_rwkv_k_k': out['od_rwkv_k_k'], 'od_rwkv_k_a': out['od_rwkv_k_a'], 'od_rwkv_r_k': out['od_rwkv_r_k'], 'od_rwkv_gn_g': out['od_rwkv_gn_g'], 'od_rwkv_gn_b': out['od_rwkv_gn_b'], 'od_w_out': out['od_w_out'], 'ln_mix_g': out['ln_mix_g'], 'ln_mix_b': out['ln_mix_b'], 'xa_w_q': out['xa_w_q'], 'xa_w_k': out['xa_w_k'], 'xa_w_v': out['xa_w_v'], 'xa_w_o': out['xa_w_o'], 'ln_xa_g': out['ln_xa_g'], 'ln_xa_b': out['ln_xa_b'], 'ffn_w_gate': out['ffn_w_gate'], 'ffn_w_up': out['ffn_w_up'], 'ffn_w_down': out['ffn_w_down'], 'ln_ffn_g': out['ln_ffn_g'], 'ln_ffn_b': out['ln_ffn_b'], 'loss_target': out['loss_target'], 'm_ev_w_in': out['m_ev_w_in'], 'm_ev_mla_q_norm': out['m_ev_mla_q_norm'], 'm_ev_mla_w_uq': out['m_ev_mla_w_uq'], 'm_ev_mla_kv_norm': out['m_ev_mla_kv_norm'], 'm_ev_mla_w_ukv': out['m_ev_mla_w_ukv'], 'm_ev_gla_w_gate2': out['m_ev_gla_w_gate2'], 'm_ev_gla_b_gate': out['m_ev_gla_b_gate'], 'm_ev_gla_norm_g': out['m_ev_gla_norm_g'], 'm_ev_gla_norm_b': out['m_ev_gla_norm_b'], 'm_ev_w_out': out['m_ev_w_out'], 'm_od_w_in': out['m_od_w_in'], 'm_od_rwkv_mu': out['m_od_rwkv_mu'], 'm_od_rwkv_w0': out['m_od_rwkv_w0'], 'm_od_rwkv_w_decay2': out['m_od_rwkv_w_decay2'], 'm_od_rwkv_a0': out['m_od_rwkv_a0'], 'm_od_rwkv_w_a2': out['m_od_rwkv_w_a2'], 'm_od_rwkv_w_gate2': out['m_od_rwkv_w_gate2'], 'm_od_rwkv_k_k': out['m_od_rwkv_k_k'], 'm_od_rwkv_k_a': out['m_od_rwkv_k_a'], 'm_od_rwkv_r_k': out['m_od_rwkv_r_k'], 'm_od_rwkv_gn_g': out['m_od_rwkv_gn_g'], 'm_od_rwkv_gn_b': out['m_od_rwkv_gn_b'], 'm_od_w_out': out['m_od_w_out'], 'm_ln_mix_g': out['m_ln_mix_g'], 'm_ln_mix_b': out['m_ln_mix_b'], 'm_xa_w_q': out['m_xa_w_q'], 'm_xa_w_k': out['m_xa_w_k'], 'm_xa_w_v': out['m_xa_w_v'], 'm_xa_w_o': out['m_xa_w_o'], 'm_ln_xa_g': out['m_ln_xa_g'], 'm_ln_xa_b': out['m_ln_xa_b'], 'm_ffn_w_gate': out['m_ffn_w_gate'], 'm_ffn_w_up': out['m_ffn_w_up'], 'm_ffn_w_down': out['m_ffn_w_down'], 'm_ln_ffn_g': out['m_ln_ffn_g'], 'm_ln_ffn_b': out['m_ln_ffn_b'], 'v_ev_w_in': out['v_ev_w_in'], 'v_ev_mla_q_norm': out['v_ev_mla_q_norm'], 'v_ev_mla_w_uq': out['v_ev_mla_w_uq'], 'v_ev_mla_kv_norm': out['v_ev_mla_kv_norm'], 'v_ev_mla_w_ukv': out['v_ev_mla_w_ukv'], 'v_ev_gla_w_gate2': out['v_ev_gla_w_gate2'], 'v_ev_gla_b_gate': out['v_ev_gla_b_gate'], 'v_ev_gla_norm_g': out['v_ev_gla_norm_g'], 'v_ev_gla_norm_b': out['v_ev_gla_norm_b'], 'v_ev_w_out': out['v_ev_w_out'], 'v_od_w_in': out['v_od_w_in'], 'v_od_rwkv_mu': out['v_od_rwkv_mu'], 'v_od_rwkv_w0': out['v_od_rwkv_w0'], 'v_od_rwkv_w_decay2': out['v_od_rwkv_w_decay2'], 'v_od_rwkv_a0': out['v_od_rwkv_a0'], 'v_od_rwkv_w_a2': out['v_od_rwkv_w_a2'], 'v_od_rwkv_w_gate2': out['v_od_rwkv_w_gate2'], 'v_od_rwkv_k_k': out['v_od_rwkv_k_k'], 'v_od_rwkv_k_a': out['v_od_rwkv_k_a'], 'v_od_rwkv_r_k': out['v_od_rwkv_r_k'], 'v_od_rwkv_gn_g': out['v_od_rwkv_gn_g'], 'v_od_rwkv_gn_b': out['v_od_rwkv_gn_b'], 'v_od_w_out': out['v_od_w_out'], 'v_ln_mix_g': out['v_ln_mix_g'], 'v_ln_mix_b': out['v_ln_mix_b'], 'v_xa_w_q': out['v_xa_w_q'], 'v_xa_w_k': out['v_xa_w_k'], 'v_xa_w_v': out['v_xa_w_v'], 'v_xa_w_o': out['v_xa_w_o'], 'v_ln_xa_g': out['v_ln_xa_g'], 'v_ln_xa_b': out['v_ln_xa_b'], 'v_ffn_w_gate': out['v_ffn_w_gate'], 'v_ffn_w_up': out['v_ffn_w_up'], 'v_ffn_w_down': out['v_ffn_w_down'], 'v_ln_ffn_g': out['v_ln_ffn_g'], 'v_ln_ffn_b': out['v_ln_ffn_b']}


def _loss(weights, diff, rest, loss_target):
    with _jax.named_scope("forward"):
        args = {**rest, TWIN_DIFF_INPUT: diff, **{k: w.astype(_WEIGHT_DTYPES[k]) for k, w in weights.items()}}
        y = _forward(args)
    with _jax.named_scope("loss_head"):
        err = _jnp.square(y.astype(_jnp.float32) - loss_target)
        return 0.5 * _jnp.sum(_jnp.mean(err, axis=-1)) if err.ndim else 0.5 * err


def _adamw(w, g, m, v):
    m = ADAM_B1 * m + (1.0 - ADAM_B1) * g
    v = ADAM_B2 * v + (1.0 - ADAM_B2) * _jnp.square(g)
    m_hat = m / (1.0 - ADAM_B1 ** ADAM_STEP)
    v_hat = v / (1.0 - ADAM_B2 ** ADAM_STEP)
    delta = -ADAM_LR * (m_hat / (_jnp.sqrt(v_hat) + ADAM_EPS) + ADAM_WD * w)
    return delta, m, v


def reference(x, mem, ev_w_in, ev_mla_q_norm, ev_mla_w_uq, ev_mla_kv_norm, ev_mla_w_ukv, ev_gla_w_gate2, ev_gla_b_gate, ev_gla_norm_g, ev_gla_norm_b, ev_w_out, od_w_in, od_rwkv_mu, od_rwkv_w0, od_rwkv_w_decay2, od_rwkv_a0, od_rwkv_w_a2, od_rwkv_w_gate2, od_rwkv_k_k, od_rwkv_k_a, od_rwkv_r_k, od_rwkv_gn_g, od_rwkv_gn_b, od_w_out, ln_mix_g, ln_mix_b, xa_w_q, xa_w_k, xa_w_v, xa_w_o, ln_xa_g, ln_xa_b, ffn_w_gate, ffn_w_up, ffn_w_down, ln_ffn_g, ln_ffn_b, loss_target, m_ev_w_in, m_ev_mla_q_norm, m_ev_mla_w_uq, m_ev_mla_kv_norm, m_ev_mla_w_ukv, m_ev_gla_w_gate2, m_ev_gla_b_gate, m_ev_gla_norm_g, m_ev_gla_norm_b, m_ev_w_out, m_od_w_in, m_od_rwkv_mu, m_od_rwkv_w0, m_od_rwkv_w_decay2, m_od_rwkv_a0, m_od_rwkv_w_a2, m_od_rwkv_w_gate2, m_od_rwkv_k_k, m_od_rwkv_k_a, m_od_rwkv_r_k, m_od_rwkv_gn_g, m_od_rwkv_gn_b, m_od_w_out, m_ln_mix_g, m_ln_mix_b, m_xa_w_q, m_xa_w_k, m_xa_w_v, m_xa_w_o, m_ln_xa_g, m_ln_xa_b, m_ffn_w_gate, m_ffn_w_up, m_ffn_w_down, m_ln_ffn_g, m_ln_ffn_b, v_ev_w_in, v_ev_mla_q_norm, v_ev_mla_w_uq, v_ev_mla_kv_norm, v_ev_mla_w_ukv, v_ev_gla_w_gate2, v_ev_gla_b_gate, v_ev_gla_norm_g, v_ev_gla_norm_b, v_ev_w_out, v_od_w_in, v_od_rwkv_mu, v_od_rwkv_w0, v_od_rwkv_w_decay2, v_od_rwkv_a0, v_od_rwkv_w_a2, v_od_rwkv_w_gate2, v_od_rwkv_k_k, v_od_rwkv_k_a, v_od_rwkv_r_k, v_od_rwkv_gn_g, v_od_rwkv_gn_b, v_od_w_out, v_ln_mix_g, v_ln_mix_b, v_xa_w_q, v_xa_w_k, v_xa_w_v, v_xa_w_o, v_ln_xa_g, v_ln_xa_b, v_ffn_w_gate, v_ffn_w_up, v_ffn_w_down, v_ln_ffn_g, v_ln_ffn_b):
    given = dict(x=x, mem=mem, ev_w_in=ev_w_in, ev_mla_q_norm=ev_mla_q_norm, ev_mla_w_uq=ev_mla_w_uq, ev_mla_kv_norm=ev_mla_kv_norm, ev_mla_w_ukv=ev_mla_w_ukv, ev_gla_w_gate2=ev_gla_w_gate2, ev_gla_b_gate=ev_gla_b_gate, ev_gla_norm_g=ev_gla_norm_g, ev_gla_norm_b=ev_gla_norm_b, ev_w_out=ev_w_out, od_w_in=od_w_in, od_rwkv_mu=od_rwkv_mu, od_rwkv_w0=od_rwkv_w0, od_rwkv_w_decay2=od_rwkv_w_decay2, od_rwkv_a0=od_rwkv_a0, od_rwkv_w_a2=od_rwkv_w_a2, od_rwkv_w_gate2=od_rwkv_w_gate2, od_rwkv_k_k=od_rwkv_k_k, od_rwkv_k_a=od_rwkv_k_a, od_rwkv_r_k=od_rwkv_r_k, od_rwkv_gn_g=od_rwkv_gn_g, od_rwkv_gn_b=od_rwkv_gn_b, od_w_out=od_w_out, ln_mix_g=ln_mix_g, ln_mix_b=ln_mix_b, xa_w_q=xa_w_q, xa_w_k=xa_w_k, xa_w_v=xa_w_v, xa_w_o=xa_w_o, ln_xa_g=ln_xa_g, ln_xa_b=ln_xa_b, ffn_w_gate=ffn_w_gate, ffn_w_up=ffn_w_up, ffn_w_down=ffn_w_down, ln_ffn_g=ln_ffn_g, ln_ffn_b=ln_ffn_b, loss_target=loss_target, m_ev_w_in=m_ev_w_in, m_ev_mla_q_norm=m_ev_mla_q_norm, m_ev_mla_w_uq=m_ev_mla_w_uq, m_ev_mla_kv_norm=m_ev_mla_kv_norm, m_ev_mla_w_ukv=m_ev_mla_w_ukv, m_ev_gla_w_gate2=m_ev_gla_w_gate2, m_ev_gla_b_gate=m_ev_gla_b_gate, m_ev_gla_norm_g=m_ev_gla_norm_g, m_ev_gla_norm_b=m_ev_gla_norm_b, m_ev_w_out=m_ev_w_out, m_od_w_in=m_od_w_in, m_od_rwkv_mu=m_od_rwkv_mu, m_od_rwkv_w0=m_od_rwkv_w0, m_od_rwkv_w_decay2=m_od_rwkv_w_decay2, m_od_rwkv_a0=m_od_rwkv_a0, m_od_rwkv_w_a2=m_od_rwkv_w_a2, m_od_rwkv_w_gate2=m_od_rwkv_w_gate2, m_od_rwkv_k_k=m_od_rwkv_k_k, m_od_rwkv_k_a=m_od_rwkv_k_a, m_od_rwkv_r_k=m_od_rwkv_r_k, m_od_rwkv_gn_g=m_od_rwkv_gn_g, m_od_rwkv_gn_b=m_od_rwkv_gn_b, m_od_w_out=m_od_w_out, m_ln_mix_g=m_ln_mix_g, m_ln_mix_b=m_ln_mix_b, m_xa_w_q=m_xa_w_q, m_xa_w_k=m_xa_w_k, m_xa_w_v=m_xa_w_v, m_xa_w_o=m_xa_w_o, m_ln_xa_g=m_ln_xa_g, m_ln_xa_b=m_ln_xa_b, m_ffn_w_gate=m_ffn_w_gate, m_ffn_w_up=m_ffn_w_up, m_ffn_w_down=m_ffn_w_down, m_ln_ffn_g=m_ln_ffn_g, m_ln_ffn_b=m_ln_ffn_b, v_ev_w_in=v_ev_w_in, v_ev_mla_q_norm=v_ev_mla_q_norm, v_ev_mla_w_uq=v_ev_mla_w_uq, v_ev_mla_kv_norm=v_ev_mla_kv_norm, v_ev_mla_w_ukv=v_ev_mla_w_ukv, v_ev_gla_w_gate2=v_ev_gla_w_gate2, v_ev_gla_b_gate=v_ev_gla_b_gate, v_ev_gla_norm_g=v_ev_gla_norm_g, v_ev_gla_norm_b=v_ev_gla_norm_b, v_ev_w_out=v_ev_w_out, v_od_w_in=v_od_w_in, v_od_rwkv_mu=v_od_rwkv_mu, v_od_rwkv_w0=v_od_rwkv_w0, v_od_rwkv_w_decay2=v_od_rwkv_w_decay2, v_od_rwkv_a0=v_od_rwkv_a0, v_od_rwkv_w_a2=v_od_rwkv_w_a2, v_od_rwkv_w_gate2=v_od_rwkv_w_gate2, v_od_rwkv_k_k=v_od_rwkv_k_k, v_od_rwkv_k_a=v_od_rwkv_k_a, v_od_rwkv_r_k=v_od_rwkv_r_k, v_od_rwkv_gn_g=v_od_rwkv_gn_g, v_od_rwkv_gn_b=v_od_rwkv_gn_b, v_od_w_out=v_od_w_out, v_ln_mix_g=v_ln_mix_g, v_ln_mix_b=v_ln_mix_b, v_xa_w_q=v_xa_w_q, v_xa_w_k=v_xa_w_k, v_xa_w_v=v_xa_w_v, v_xa_w_o=v_xa_w_o, v_ln_xa_g=v_ln_xa_g, v_ln_xa_b=v_ln_xa_b, v_ffn_w_gate=v_ffn_w_gate, v_ffn_w_up=v_ffn_w_up, v_ffn_w_down=v_ffn_w_down, v_ln_ffn_g=v_ln_ffn_g, v_ln_ffn_b=v_ln_ffn_b)
    weights = {n: given[n] for n in TWIN_WEIGHTS}
    shared = {n: given[n] for n in SHARED_INPUTS}
    per_example = {n: given[n] for n in ['x', 'mem']}
    grad_fn = _jax.value_and_grad(_loss, argnums=(0, 1))

    def one_microbatch(ex, loss_target):
        ex = dict(ex)
        diff = ex.pop(TWIN_DIFF_INPUT)
        return grad_fn(weights, diff, {**shared, **ex}, loss_target)

    if N_MICROBATCH == 1:
        loss, (grad_w, grad_x) = one_microbatch(per_example, given["loss_target"])
    else:
        def body(carry, xs):
            loss_sum, grad_sum = carry
            l_k, (gw_k, gx_k) = one_microbatch(xs[0], xs[1])
            with _jax.named_scope("update"):
                return (loss_sum + l_k, _jax.tree.map(_jnp.add, grad_sum, gw_k)), gx_k

        init = (_jnp.zeros((), _jnp.float32), _jax.tree.map(_jnp.zeros_like, weights))
        (loss, grad_w), grad_x = _jax.lax.scan(body, init, (per_example, given["loss_target"]))
    with _jax.named_scope("update"):
        delta_w, new_m, new_v = {}, {}, {}
        for n in TWIN_WEIGHTS:
            delta_w[n], new_m[n], new_v[n] = _adamw(weights[n], grad_w[n], given["m_" + n], given["v_" + n])
    return (loss, grad_x, *[grad_w[n] for n in TWIN_WEIGHTS], *[delta_w[n] for n in TWIN_WEIGHTS],
            *[new_m[n] for n in TWIN_WEIGHTS], *[new_v[n] for n in TWIN_WEIGHTS])
```

```python
import functools
import math

import jax
import jax.numpy as jnp
from jax import lax
from jax.experimental import pallas as pl
from jax.experimental.pallas import tpu as pltpu

F32 = jnp.float32
BF16 = jnp.bfloat16
VMEM_LIMIT = 56 * 1024 * 1024
ROWS_VMEM = 20 * 1024 * 1024

N_DEV = 8
DEPTH = 2
ALPHA = (2.0 * DEPTH) ** 0.25
LN_EPS = 1e-5
RMS_EPS = 1e-6
RWKV_GN_EPS = 64e-5
ADAM_LR, ADAM_B1, ADAM_B2, ADAM_EPS, ADAM_WD, ADAM_STEP = 0.001, 0.9, 0.999, 1e-08, 0.01, 10
NEG_INF = float("-inf")


def _pcall(body, **kw):
    return pl.pallas_call(body, **kw)


def _cparams(**kw):
    return pltpu.CompilerParams(vmem_limit_bytes=VMEM_LIMIT, **kw)


def _dg(a, b, ca, cb, batch):
    nb = 1 if batch else 0
    dims = (((ca + nb,), (cb + nb,)), ((0,), (0,)) if batch else ((), ()))
    return lax.dot_general(a.astype(BF16), b.astype(BF16), dims, preferred_element_type=F32)


@functools.partial(jax.custom_vjp, nondiff_argnums=(2, 3, 4))
def _mm(a, b, ta, tb, batch):
    return _dg(a, b, 0 if ta else 1, 1 if tb else 0, batch)


def _mm_fwd(a, b, ta, tb, batch):
    return _mm(a, b, ta, tb, batch), (a, b)


def _mm_bwd(ta, tb, batch, res, g):
    a, b = res
    if not ta and not tb:
        da, db = _mm(g, b, False, True, batch), _mm(a, g, True, False, batch)
    elif not ta and tb:
        da, db = _mm(g, b, False, False, batch), _mm(g, a, True, False, batch)
    elif ta and not tb:
        da, db = _mm(b, g, False, True, batch), _mm(a, g, False, False, batch)
    else:
        da, db = _mm(b, g, True, True, batch), _mm(g, a, True, True, batch)
    return da.astype(a.dtype), db.astype(b.dtype)


_mm.defvjp(_mm_fwd, _mm_bwd)


def mm(a, b, ta=False, tb=False):
    return _mm(a, b, ta, tb, a.ndim == 3)


def _bs(block, imap):
    return pl.BlockSpec(block, imap)


def _rev_imap(imap, n):
    def r(*idx):
        return imap(*idx[:-1], n - 1 - idx[-1])
    return r


def p_fwd(name, f, grid, ins, outs, carry=None, save_carry=None):
    n_in, n_out = len(ins), len(outs)

    def body(*refs):
        in_refs = refs[:n_in]
        out_refs = refs[n_in:n_in + n_out]
        rest = refs[n_in + n_out:]
        vals = [r[...] for r in in_refs]
        if carry is None:
            res = f(*vals)
        else:
            if save_carry is not None:
                sv_ref, c_ref = rest
            else:
                (c_ref,) = rest

            @pl.when(pl.program_id(len(grid) - 1) == 0)
            def _():
                c_ref[...] = jnp.zeros(c_ref.shape, c_ref.dtype)

            c = c_ref[...]
            if save_carry is not None:
                sv_ref[...] = c
            res = f(c, *vals)
            c_ref[...] = res[0]
            res = res[1:]
        if not isinstance(res, (tuple, list)):
            res = (res,)
        for r, v in zip(out_refs, res):
            r[...] = v.astype(r.dtype)

    out_shape = [jax.ShapeDtypeStruct(s, d) for (s, d, _, _) in outs]
    out_specs = [_bs(b, m) for (_, _, b, m) in outs]
    scratch = []
    if carry is not None:
        if save_carry is not None:
            out_shape.append(jax.ShapeDtypeStruct(save_carry[0], carry[1]))
            out_specs.append(_bs(save_carry[1], save_carry[2]))
        scratch.append(pltpu.VMEM(carry[0], carry[1]))
    return _pcall(
        body, name=name, grid=grid,
        in_specs=[_bs(b, m) for (_, b, m) in ins],
        out_specs=out_specs, out_shape=out_shape, scratch_shapes=scratch,
        compiler_params=_cparams(),
    )(*[a for (a, _, _) in ins])


def p_bwd(name, f, grid, ins, cts, wrt, carry=None, saved=None):
    n_in, n_ct, n_w = len(ins), len(cts), len(wrt)
    rev = carry is not None
    n_last = grid[-1]

    def fix(imap):
        return _rev_imap(imap, n_last) if rev else imap

    def body(*refs):
        in_refs = refs[:n_in]
        ct_refs = refs[n_in:n_in + n_ct]
        k = n_in + n_ct
        if rev:
            sv_ref = refs[k]
            k += 1
        out_refs = refs[k:k + n_w]
        rest = refs[k + n_w:]
        vals = [r[...] for r in in_refs]
        ct_vals = [r[...].astype(F32) for r in ct_refs]
        widx = [w[0] for w in wrt]

        if rev:
            (dc_ref,) = rest

            @pl.when(pl.program_id(len(grid) - 1) == 0)
            def _():
                dc_ref[...] = jnp.zeros(dc_ref.shape, dc_ref.dtype)

            c_in = sv_ref[...]

            def g(c, *dv):
                full = list(vals)
                for i, d in zip(widx, dv):
                    full[i] = d
                return tuple(f(c, *full))

            _, vjp = jax.vjp(g, c_in, *[vals[i] for i in widx])
            grads = vjp((dc_ref[...],) + tuple(ct_vals))
            dc_ref[...] = grads[0]
            grads = grads[1:]
        else:
            def g(*dv):
                full = list(vals)
                for i, d in zip(widx, dv):
                    full[i] = d
                r = f(*full)
                return tuple(r) if isinstance(r, (tuple, list)) else (r,)

            _, vjp = jax.vjp(g, *[vals[i] for i in widx])
            grads = vjp(tuple(ct_vals))

        for w, o_ref, gr in zip(wrt, out_refs, grads):
            acc = w[1]
            if acc is None:
                o_ref[...] = gr.astype(o_ref.dtype)
            else:
                first = None
                for ax in acc:
                    c0 = pl.program_id(ax) == 0
                    first = c0 if first is None else jnp.logical_and(first, c0)

                @pl.when(first)
                def _():
                    o_ref[...] = jnp.zeros(o_ref.shape, o_ref.dtype)

                o_ref[...] += gr.astype(o_ref.dtype)

    in_specs = [_bs(b, fix(m)) for (_, b, m) in ins] + [_bs(b, fix(m)) for (_, b, m) in cts]
    args = [a for (a, _, _) in ins] + [a for (a, _, _) in cts]
    if rev:
        in_specs.append(_bs(saved[1], fix(saved[2])))
        args.append(saved[0])
    out_shape, out_specs = [], []
    for w in wrt:
        a, b, m = ins[w[0]]
        if len(w) > 2:
            m = w[2]
        out_shape.append(jax.ShapeDtypeStruct(a.shape, F32))
        out_specs.append(_bs(b, fix(m)))
    scratch = [pltpu.VMEM(carry[0], carry[1])] if rev else []
    return _pcall(
        body, name=name, grid=grid, in_specs=in_specs, out_specs=out_specs,
        out_shape=out_shape, scratch_shapes=scratch, compiler_params=_cparams(),
    )(*args)


def _rows(name, f, row_ins, params, out_widths, tm, fwd=True, cts=None, wrt_rows=(), wrt_params=()):
    t = row_ins[0].shape[0]
    width = sum(a.shape[1] for a in row_ins)
    width += sum(out_widths) if fwd else sum(c.shape[1] for c in cts) + sum(row_ins[i].shape[1] for i in wrt_rows)
    tm = min(tm, t)
    while tm > 8 and 2 * 4 * tm * width > ROWS_VMEM:
        tm //= 2
    rmap = lambda i: (i, 0)
    pmap = lambda i: (0, 0)
    ins = [(a, (tm, a.shape[1]), rmap) for a in row_ins] + [(p, p.shape, pmap) for p in params]
    if fwd:
        outs = [((t, w), F32, (tm, w), rmap) for w in out_widths]
        return p_fwd(name, f, (t // tm,), ins, outs)
    ct_specs = [(c, (tm, c.shape[1]), rmap) for c in cts]
    wrt = [(i, None) for i in wrt_rows] + [(len(row_ins) + i, (0,)) for i in wrt_params]
    return p_bwd(name, f, (t // tm,), ins, ct_specs, wrt)


def _pick(n, cands):
    for c in cands:
        if n % c == 0:
            return c
    return n


def matmul(name, a, b, ta=False, tb=False, out_dtype=F32):
    m = a.shape[1] if ta else a.shape[0]
    k = a.shape[0] if ta else a.shape[1]
    n = b.shape[0] if tb else b.shape[1]
    assert (b.shape[1] if tb else b.shape[0]) == k, (a.shape, b.shape, ta, tb)
    bm, bn, bk = _pick(m, (512, 256, 128)), _pick(n, (512, 256, 128)), _pick(k, (512, 256, 128))
    nk = k // bk

    def body(a_ref, b_ref, o_ref, acc_ref):
        @pl.when(pl.program_id(2) == 0)
        def _():
            acc_ref[...] = jnp.zeros(acc_ref.shape, F32)

        acc_ref[...] += mm(a_ref[...], b_ref[...], ta, tb)

        @pl.when(pl.program_id(2) == nk - 1)
        def _():
            o_ref[...] = acc_ref[...].astype(o_ref.dtype)

    a_spec = _bs((bk, bm), lambda i, j, l: (l, i)) if ta else _bs((bm, bk), lambda i, j, l: (i, l))
    b_spec = _bs((bn, bk), lambda i, j, l: (j, l)) if tb else _bs((bk, bn), lambda i, j, l: (l, j))
    return _pcall(
        body, name=name, grid=(m // bm, n // bn, nk),
        in_specs=[a_spec, b_spec], out_specs=_bs((bm, bn), lambda i, j, l: (i, j)),
        out_shape=jax.ShapeDtypeStruct((m, n), out_dtype),
        scratch_shapes=[pltpu.VMEM((bm, bn), F32)],
        compiler_params=_cparams(dimension_semantics=("parallel", "parallel", "arbitrary")),
    )(a, b)


def _to_heads(x, b, h):
    t, w = x.shape
    return x.reshape(b, t // b, h, w // h).transpose(0, 2, 1, 3)


def _from_heads(x):
    b, h, s, d = x.shape
    return x.transpose(0, 2, 1, 3).reshape(b * s, h * d)


RW_STEPS = 8
RW_G = 4


def rwkv_group(st, wc, kkc, ac, khc, rc, v8):
    ys = []
    for t in range(RW_STEPS):
        col = lambda x: x[:, :, t:t + 1]
        v_t = v8[:, t:t + 1, :]
        kkr = col(kkc)
        nrm = jnp.sqrt(jnp.sum(kkr * kkr, axis=1, keepdims=True))
        kk = kkr / jnp.maximum(nrm, 1e-12)
        sa = jnp.sum(st * (-kk), axis=1, keepdims=True)
        st = st * col(wc) + (kk * col(ac)) * sa + col(khc) * v_t
        ys.append(jnp.sum(st * col(rc), axis=1, keepdims=True))
    return st, jnp.concatenate(ys, axis=1)


def rwkv_pre_f(kd, w_lr, a_lr, g_lr, w0, wd2, a0, wa2, wg2, k_k, k_a):
    wpre = w0 + mm(jnp.tanh(w_lr), wd2)
    w = -jax.nn.softplus(-wpre) - 0.5
    decay = jnp.exp(-jnp.exp(w))
    a = jax.nn.sigmoid(a0 + mm(a_lr, wa2))
    g = mm(jax.nn.sigmoid(g_lr), wg2)
    kkraw = kd * k_k
    kh = kd * (1.0 + (a - 1.0) * k_a)
    return decay, a, g, kkraw, kh


def rwkv_post_f(y, r, kh, v, g, gn_g, gn_b, r_k):
    mu = jnp.mean(y, axis=-1, keepdims=True)
    var = jnp.mean(jnp.square(y - mu), axis=-1, keepdims=True)
    yn = (y - mu) * lax.rsqrt(var + RWKV_GN_EPS) * gn_g + gn_b
    bonus = jnp.sum(r * kh * r_k, axis=-1, keepdims=True) * v
    return (yn + bonus) * g


def _to_cols(xh):
    b, h, s, d = xh.shape
    return xh.reshape(b, h, s // RW_STEPS, RW_STEPS, d).transpose(0, 1, 2, 4, 3)


def _from_cols(xc):
    b, h, n, d, k = xc.shape
    return xc.transpose(0, 1, 2, 4, 3).reshape(b, h, n * k, d)


def rwkv_block(bsz, r, kd, vd, w_lr, a_lr, g_lr, prm, cts=None):
    t = r.shape[0]
    hh, n = 16, 64
    tm = 256
    pre_rows = [kd, w_lr, a_lr, g_lr]
    pre_prm = [prm[k] for k in ("w0", "wd2", "a0", "wa2", "wg2", "k_k", "k_a")]
    decay, a, g, kkraw, kh = _rows("rwkv_pre", rwkv_pre_f, pre_rows, pre_prm, [1024] * 5, tm)
    heads = lambda x: _to_heads(x, bsz, hh)
    rh, khh, vh, gh = heads(r), heads(kh), heads(vd), heads(g)
    cols = [_to_cols(heads(x)) for x in (decay, kkraw, a)] + [_to_cols(khh), _to_cols(rh)]
    s = t // bsz
    ng = s // RW_STEPS
    v8 = vh.reshape(bsz, hh, ng, RW_STEPS, n)
    cb, rb, sb = (None, RW_G, None, n, RW_STEPS), (None, RW_G, None, RW_STEPS, n), (None, RW_G, None, n, n)
    cm = lambda b, h, i: (b, h, i, 0, 0)
    sc_ins = [(x, cb, cm) for x in cols] + [(v8, rb, cm)]
    grid = (bsz, hh // RW_G, ng)
    y8, sv = p_fwd("rwkv_scan", rwkv_group, grid, sc_ins, [((bsz, hh, ng, RW_STEPS, n), F32, rb, cm)],
                   carry=((RW_G, n, n), F32), save_carry=((bsz, hh, ng, n, n), sb, cm))
    yh = y8.reshape(bsz, hh, s, n)
    ts = 512
    hb = (None, None, ts, n)
    hm = lambda h, b, i: (b, h, i, 0)
    pb = (None, 1, n)
    pm = lambda h, b, i: (h, 0, 0)
    gn_g, gn_b, r_k = (prm[k].reshape(hh, 1, n) for k in ("gn_g", "gn_b", "r_k"))
    post_ins = [(x, hb, hm) for x in (yh, rh, khh, vh, gh)] + [(p, pb, pm) for p in (gn_g, gn_b, r_k)]
    pgrid = (hh, bsz, s // ts)
    if cts is None:
        (oh,) = p_fwd("rwkv_post", rwkv_post_f, pgrid, post_ins, [((bsz, hh, s, n), F32, hb, hm)])
        return _from_heads(oh)
    doh = _to_heads(cts, bsz, hh)
    dyh, drh1, dkhh1, dvh1, dgh, dgn_g, dgn_b, dr_k = p_bwd(
        "rwkv_post_bwd", rwkv_post_f, pgrid, post_ins, [(doh, hb, hm)],
        [(i, None) for i in range(5)] + [(5 + i, (1, 2)) for i in range(3)])
    dy8 = dyh.reshape(bsz, hh, ng, RW_STEPS, n)
    dcols_v = p_bwd("rwkv_scan_bwd", rwkv_group, grid, sc_ins, [(dy8, rb, cm)], [(i, None) for i in range(6)],
                    carry=((RW_G, n, n), F32), saved=(sv, sb, cm))
    ddecay, dkkraw, da, dkh2, dr2 = (_from_heads(_from_cols(x)) for x in dcols_v[:5])
    dv = _from_heads(dcols_v[5].reshape(bsz, hh, s, n) + dvh1)
    dr = dr2 + _from_heads(drh1)
    dkh = dkh2 + _from_heads(dkhh1)
    dg = _from_heads(dgh)
    res = _rows("rwkv_pre_bwd", rwkv_pre_f, pre_rows, pre_prm, None, tm, fwd=False,
                cts=[ddecay, da, dg, dkkraw, dkh], wrt_rows=(0, 1, 2, 3), wrt_params=tuple(range(7)))
    dkd, dw_lr, da_lr, dg_lr = res[:4]
    dprm = dict(zip(("w0", "wd2", "a0", "wa2", "wg2", "k_k", "k_a"), res[4:]))
    dprm.update(gn_g=dgn_g.reshape(1, -1), gn_b=dgn_b.reshape(1, -1), r_k=dr_k.reshape(hh, n))
    return (dr, dkd, dv, dw_lr, da_lr, dg_lr), dprm


GLA_C, GLA_DK, GLA_DV, GLA_H, GLA_TAU = 64, 128, 256, 4, 16.0


def gla_chunk_f(st, q, k, v, lr, r, w2, bg, ng, nb):
    la = jax.nn.log_sigmoid(mm(lr, w2) + bg) / GLA_TAU
    ri = lax.broadcasted_iota(jnp.int32, (GLA_C, GLA_C), 0)
    ci = lax.broadcasted_iota(jnp.int32, (GLA_C, GLA_C), 1)
    causal = ci <= ri
    b = jnp.dot(causal.astype(F32), la, precision=lax.Precision.HIGHEST, preferred_element_type=F32)
    b_last = jnp.sum(la, axis=0, keepdims=True)
    q_dec = (q * (GLA_DK ** -0.5)) * jnp.exp(b)
    k_inv = k * jnp.exp(-b)
    k_end = k * jnp.exp(b_last - b)
    att = jnp.where(causal, mm(q_dec, k_inv, tb=True), 0.0)
    o = mm(att, v) + mm(q_dec, st, tb=True)
    st_new = st * jnp.exp(b_last) + mm(v, k_end, ta=True)
    mu = jnp.mean(o, axis=-1, keepdims=True)
    var = jnp.mean(jnp.square(o - mu), axis=-1, keepdims=True)
    on = (o - mu) * lax.rsqrt(var + LN_EPS) * ng + nb
    return st_new, on * jax.nn.silu(r)


def gla_block(bsz, q, k, v, r, lr, w2, bg, ng, nb, cts=None):
    t = q.shape[0]
    nc = t // bsz // GLA_C
    grid = (GLA_H, bsz, nc)
    rm = lambda h, b, c: (b * nc + c, h)
    ins = [(q, (GLA_C, GLA_DK), rm), (k, (GLA_C, GLA_DK), rm), (v, (GLA_C, GLA_DV), rm),
           (jnp.broadcast_to(lr[None], (GLA_H,) + lr.shape), (None, GLA_C, lr.shape[1]), lambda h, b, c: (h, b * nc + c, 0)),
           (r, (GLA_C, GLA_DV), rm),
           (w2, (w2.shape[0], GLA_DK), lambda h, b, c: (0, h)), (bg, (1, GLA_DK), lambda h, b, c: (0, h)),
           (ng, (1, GLA_DV), lambda h, b, c: (0, 0)), (nb, (1, GLA_DV), lambda h, b, c: (0, 0))]
    ob = (GLA_C, GLA_DV)
    sshape, sblock = (GLA_H, bsz, nc, GLA_DV, GLA_DK), (None, None, None, GLA_DV, GLA_DK)
    sm = lambda h, b, c: (h, b, c, 0, 0)
    carry = ((GLA_DV, GLA_DK), F32)
    if cts is None:
        out, sv = p_fwd("gla_scan", gla_chunk_f, grid, ins, [((t, GLA_H * GLA_DV), F32, ob, rm)],
                        carry=carry, save_carry=(sshape, sblock, sm))
        return out, sv
    dout, sv = cts
    return p_bwd("gla_scan_bwd", gla_chunk_f, grid, ins, [(dout, ob, rm)],
                 [(0, None), (1, None), (2, None), (3, None), (4, None), (5, (1, 2)), (6, (1, 2)), (7, (0, 1, 2)), (8, (0, 1, 2))],
                 carry=carry, saved=(sv, sblock, sm))


def _softmax_rows(sc):
    m = lax.stop_gradient(jnp.max(sc, axis=-1, keepdims=True))
    e = jnp.exp(sc - m)
    return e / jnp.sum(e, axis=-1, keepdims=True)


def mla_attn_f(qn, r1, r2, kn, kr1, kr2, v):
    tq, s = qn.shape[0], kn.shape[0]
    sc = (mm(qn, kn, tb=True) + mm(r1, kr1, tb=True) + mm(r2, kr2, tb=True)) * (192.0 ** -0.5)
    qpos = pl.program_id(2) * tq + lax.broadcasted_iota(jnp.int32, (tq, s), 0)
    kpos = lax.broadcasted_iota(jnp.int32, (tq, s), 1)
    sc = jnp.where(kpos <= qpos, sc, NEG_INF)
    return mm(_softmax_rows(sc), v)


def mla_attn(bsz, qn, r1, r2, kn, kr1, kr2, v, cts=None, tq=256):
    t = qn.shape[0]
    s = t // bsz
    nq = s // tq
    hh = 8
    grid = (bsz, hh, nq)
    qm = lambda b, h, i: (b * nq + i, h)
    km_ = lambda b, h, i: (b, h)
    ins = [(qn, (tq, 128), qm),
           (r1, (None, None, tq, 32), lambda b, h, i: (b, h, i, 0)), (r2, (None, None, tq, 32), lambda b, h, i: (b, h, i, 0)),
           (kn, (s, 128), km_), (kr1, (s, 32), lambda b, h, i: (b, 0)), (kr2, (s, 32), lambda b, h, i: (b, 0)),
           (v, (s, 128), km_)]
    if cts is None:
        return p_fwd("mla_attn", mla_attn_f, grid, ins, [((t, 1024), F32, (tq, 128), qm)])[0]
    return p_bwd("mla_attn_bwd", mla_attn_f, grid, ins, [(cts, (tq, 128), qm)],
                 [(0, None), (1, None), (2, None), (3, (2,)), (4, (1, 2)), (5, (1, 2)), (6, (2,))])


def xattn_f(q, k, v):
    sc = mm(q, k, tb=True) * (512.0 ** -0.5)
    return mm(_softmax_rows(sc), v)


def xattn(bsz, q, k, v, cts=None, tq=512):
    t = q.shape[0]
    nq = t // bsz // tq
    mlen = k.shape[0] // bsz
    grid = (bsz, 4, nq)
    qm = lambda b, h, i: (b * nq + i, h)
    km_ = lambda b, h, i: (b, h)
    ins = [(q, (tq, 512), qm), (k, (mlen, 512), km_), (v, (mlen, 512), km_)]
    if cts is None:
        return p_fwd("xattn", xattn_f, grid, ins, [((t, 2048), F32, (tq, 512), qm)])[0]
    return p_bwd("xattn_bwd", xattn_f, grid, ins, [(cts, (tq, 512), qm)], [(0, None), (1, (2,)), (2, (2,))])


DIL_SPAN = 128
DIL_BRANCHES = ((128, 1), (512, 4), (2048, 16))


def dil_attn_f(q1, q2, k1c, k2c, vc, k1p, k2p, vp):
    gb, sp = q1.shape[0], DIL_SPAN
    scale = 128.0 ** -0.5
    sc_c = (mm(q1, k1c, tb=True) + mm(q2, k2c, tb=True)) * scale
    sc_p = (mm(q1, k1p, tb=True) + mm(q2, k2p, tb=True)) * scale
    ql = lax.broadcasted_iota(jnp.int32, (gb, sp, sp), 1)
    kl = lax.broadcasted_iota(jnp.int32, (gb, sp, sp), 2)
    has_prev = pl.program_id(1) > 0
    sc_c = jnp.where(kl <= ql, sc_c, NEG_INF)
    sc_p = jnp.where(jnp.logical_and(kl >= ql, has_prev), sc_p, NEG_INF)
    m = lax.stop_gradient(jnp.maximum(jnp.max(sc_c, axis=-1, keepdims=True), jnp.max(sc_p, axis=-1, keepdims=True)))
    e_c, e_p = jnp.exp(sc_c - m), jnp.exp(sc_p - m)
    den = jnp.sum(e_c, axis=-1, keepdims=True) + jnp.sum(e_p, axis=-1, keepdims=True)
    o = mm(e_c / den, vc) + mm(e_p / den, vp)
    return o, m + jnp.log(den)


def dil_branch(q1, q2, k1, k2, v, cts=None, gb=8):
    g, l, _ = q1.shape
    nb = l // DIL_SPAN
    grid = (g // gb, nb)
    cm = lambda i, n: (i, n, 0)
    pm = lambda i, n: (i, jnp.maximum(n - 1, 0), 0)
    b64, b128, b1 = (gb, DIL_SPAN, 64), (gb, DIL_SPAN, 128), (gb, DIL_SPAN, 1)
    ins = [(q1, b64, cm), (q2, b64, cm), (k1, b64, cm), (k2, b64, cm), (v, b128, cm),
           (k1, b64, pm), (k2, b64, pm), (v, b128, pm)]
    if cts is None:
        return p_fwd("dil_attn", dil_attn_f, grid, ins, [((g, l, 128), F32, b128, cm), ((g, l, 1), F32, b1, cm)])
    do, dlse = cts
    dq1, dq2, dk1c, dk2c, dvc, dk1p, dk2p, dvp = p_bwd(
        "dil_attn_bwd", dil_attn_f, grid, ins, [(do, b128, cm), (dlse, b1, cm)],
        [(i, None) for i in range(5)] + [(i, None, cm) for i in (5, 6, 7)])

    def fold(dc, dp):
        return dc + jnp.pad(dp[:, DIL_SPAN:], ((0, 0), (0, DIL_SPAN), (0, 0)))

    return dq1, dq2, fold(dk1c, dk1p), fold(dk2c, dk2p), fold(dvc, dvp)


def dil_mix_f(o1, o2, o3, l1, l2, l3):
    m = lax.stop_gradient(jnp.maximum(jnp.maximum(l1, l2), l3))
    e1, e2, e3 = jnp.exp(l1 - m), jnp.exp(l2 - m), jnp.exp(l3 - m)
    den = e1 + e2 + e3
    return (e1 / den) * o1 + (e2 / den) * o2 + (e3 / den) * o3


def _to_res(xh, dil):
    b, h, s, d = xh.shape
    return xh.reshape(b, h, s // dil, dil, d).transpose(0, 1, 3, 2, 4).reshape(b * h * dil, s // dil, d)


def _from_res(xr, b, h, dil):
    g, l, d = xr.shape
    return xr.reshape(b, h, dil, l, d).transpose(0, 1, 3, 2, 4).reshape(b, h, l * dil, d)


def dil_block(bsz, q1, q2, k1, k2, v, cts=None):
    hh = 8
    heads = [_to_heads(x, bsz, hh) for x in (q1, q2, k1, k2, v)]
    s = heads[0].shape[2]
    outs, res_in = [], []
    for window, dil in DIL_BRANCHES:
        assert window // dil == DIL_SPAN and (s // dil) % DIL_SPAN == 0
        rin = [_to_res(x, dil) for x in heads]
        o, lse = dil_branch(*rin)
        res_in.append(rin)
        outs.append((_from_res(o, bsz, hh, dil), _from_res(lse, bsz, hh, dil)))
    tq = 512
    ob, lb = (None, None, tq, 128), (None, None, tq, 1)
    hm = lambda b, h, i: (b, h, i, 0)
    mix_ins = [(o, ob, hm) for (o, _) in outs] + [(l, lb, hm) for (_, l) in outs]
    grid = (bsz, hh, s // tq)
    if cts is None:
        (mix,) = p_fwd("dil_mix", dil_mix_f, grid, mix_ins, [((bsz, hh, s, 128), F32, ob, hm)])
        return _from_heads(mix)
    dmix = _to_heads(cts, bsz, hh)
    dml = p_bwd("dil_mix_bwd", dil_mix_f, grid, mix_ins, [(dmix, ob, hm)], [(i, None) for i in range(6)])
    tot = None
    for j, (window, dil) in enumerate(DIL_BRANCHES):
        do, dl = _to_res(dml[j], dil), _to_res(dml[3 + j], dil)
        gr = dil_branch(*res_in[j], cts=(do, dl))
        gr = [_from_res(x, bsz, hh, dil) for x in gr]
        tot = gr if tot is None else [a + b for a, b in zip(tot, gr)]
    return tuple(_from_heads(x) for x in tot)


_ANY = pl.BlockSpec(memory_space=pl.ANY)


def _me_and_peers():
    x, y, c = lax.axis_index("x"), lax.axis_index("y"), lax.axis_index("c")
    me = 4 * x + 2 * y + c
    peers = []
    for k in range(1, N_DEV):
        px = 1 - x if k & 4 else x
        py = 1 - y if k & 2 else y
        pc = 1 - c if k & 1 else c
        peers.append(((px, py, pc), 4 * px + 2 * py + pc))
    return me, peers


def _exchange(name, x, scatter):
    shape = x.shape[1:] if scatter else x.shape

    def body(x_ref, out_ref, send_sems, recv_sems, local_sem):
        me, peers = _me_and_peers()
        src_me = x_ref.at[me] if scatter else x_ref
        local = pltpu.make_async_copy(src_me, out_ref.at[me], local_sem)
        local.start()
        sends = []
        for k, (dev, idx) in enumerate(peers):
            cp = pltpu.make_async_remote_copy(
                src_ref=x_ref.at[idx] if scatter else x_ref, dst_ref=out_ref.at[me],
                send_sem=send_sems.at[k], recv_sem=recv_sems.at[k],
                device_id=dev, device_id_type=pl.DeviceIdType.MESH)
            cp.start()
            sends.append(cp)
        for k, (dev, idx) in enumerate(peers):
            pltpu.make_async_remote_copy(
                src_ref=src_me, dst_ref=out_ref.at[idx], send_sem=send_sems.at[k], recv_sem=recv_sems.at[k],
                device_id=dev, device_id_type=pl.DeviceIdType.MESH).wait_recv()
        for cp in sends:
            cp.wait_send()
        local.wait()

    return _pcall(
        body, name=name, in_specs=[_ANY], out_specs=_ANY,
        out_shape=jax.ShapeDtypeStruct((N_DEV,) + tuple(shape), x.dtype),
        scratch_shapes=[pltpu.SemaphoreType.DMA((N_DEV - 1,)), pltpu.SemaphoreType.DMA((N_DEV - 1,)),
                        pltpu.SemaphoreType.DMA],
        compiler_params=pltpu.CompilerParams(has_side_effects=True),
    )(x)


def all_gather(name, x):
    return _exchange(name, x, False)


def reduce_scatter_exchange(name, x):
    return _exchange(name, x, True)


def ln_res_f(h, r, g, b):
    x = ALPHA * h + r
    mu = jnp.mean(x, axis=-1, keepdims=True)
    var = jnp.mean(jnp.square(x - mu), axis=-1, keepdims=True)
    return (x - mu) * lax.rsqrt(var + LN_EPS) * g + b


def rms_f(x, g):
    return x * lax.rsqrt(jnp.mean(x * x, axis=-1, keepdims=True) + RMS_EPS) * g


def rope_f(x1, x2, c, s):
    return x1 * c - x2 * s, x1 * s + x2 * c


def swiglu_f(g, u):
    return jax.nn.silu(g) * u


def tshift_f(d, dprev, mu):
    return d + (dprev - d) * mu


def loss_f(y, tgt):
    e = y - tgt
    return e / y.shape[-1], 0.5 * jnp.mean(e * e, axis=-1, keepdims=True)


def adamw(name, w, m, v, gstack):
    shp = w.shape
    c = shp[-1]
    r = math.prod(shp[:-1])
    br = r
    for cand in (512, 256, 128, 64, 32, 16, 8):
        if r % cand == 0 and cand * c <= 128 * 1024:
            br = cand
            break
    k = gstack.shape[0]

    def body(w_ref, m_ref, v_ref, g_ref, go_ref, d_ref, mo_ref, vo_ref):
        g = g_ref[0]
        for j in range(1, k):
            g = g + g_ref[j]
        m_new = ADAM_B1 * m_ref[...] + (1.0 - ADAM_B1) * g
        v_new = ADAM_B2 * v_ref[...] + (1.0 - ADAM_B2) * jnp.square(g)
        m_hat = m_new / (1.0 - ADAM_B1 ** ADAM_STEP)
        v_hat = v_new / (1.0 - ADAM_B2 ** ADAM_STEP)
        go_ref[...] = g
        d_ref[...] = -ADAM_LR * (m_hat / (jnp.sqrt(v_hat) + ADAM_EPS) + ADAM_WD * w_ref[...])
        mo_ref[...] = m_new
        vo_ref[...] = v_new

    spec = _bs((br, c), lambda i: (i, 0))
    outs = _pcall(
        body, name=name, grid=(r // br,),
        in_specs=[spec, spec, spec, _bs((k, br, c), lambda i: (0, i, 0))],
        out_specs=[spec] * 4, out_shape=[jax.ShapeDtypeStruct((r, c), F32)] * 4,
        compiler_params=_cparams(),
    )(w.reshape(r, c), m.reshape(r, c), v.reshape(r, c), gstack.reshape(k, r, c))
    return tuple(o.reshape(shp) for o in outs)


EV_W = (512, 256, 64, 512, 512, 1024, 1024, 16)
EV_IN, EV_PAD = 3920, 4096
OD_IN, OD_PAD = 6592, 6656
TM = 256


def _offsets(widths):
    offs, acc = [], 0
    for w in widths:
        offs.append((acc, acc + w))
        acc += w
    return offs


def _rope_tables(seq, dim):
    inv = 10000.0 ** (-jnp.arange(0, dim, 2, dtype=F32) / dim)
    ang = jnp.arange(seq, dtype=F32)[:, None] * inv[None, :]
    return jnp.cos(ang), jnp.sin(ang)


def _halves(x, nh):
    t, w = x.shape
    x3 = x.reshape(t, nh, w // nh)
    hd = w // nh // 2
    return x3[:, :, :hd].reshape(t, nh * hd), x3[:, :, hd:].reshape(t, nh * hd)


def _unhalves(x1, x2, nh):
    t = x1.shape[0]
    return jnp.concatenate([x1.reshape(t, nh, -1), x2.reshape(t, nh, -1)], axis=2).reshape(t, -1)


def _ln(name, h, r, g, b, cts=None):
    if cts is None:
        return _rows(name, ln_res_f, [h, r], [g, b], [h.shape[1]], TM)[0]
    return _rows(name + "_bwd", ln_res_f, [h, r], [g, b], None, TM, fwd=False, cts=[cts], wrt_rows=(0, 1), wrt_params=(0, 1))


def _tail_fwd(l, h, mem2, bsz, p):
    qx = matmul(f"xa_q{l}", h, p["xa_w_q"][l])
    kx = matmul(f"xa_k{l}", mem2, p["xa_w_k"][l])
    vx = matmul(f"xa_v{l}", mem2, p["xa_w_v"][l])
    ox = xattn(bsz, qx, kx, vx)
    xa = matmul(f"xa_o{l}", ox, p["xa_w_o"][l])
    h2 = _ln(f"ln_xa{l}", h, xa, p["ln_xa_g"][l:l + 1], p["ln_xa_b"][l:l + 1])
    gg = matmul(f"ffn_g{l}", h2, p["ffn_w_gate"][l])
    uu = matmul(f"ffn_u{l}", h2, p["ffn_w_up"][l])
    act = _rows(f"swiglu{l}", swiglu_f, [gg, uu], [], [gg.shape[1]], TM)[0]
    ff = matmul(f"ffn_d{l}", act, p["ffn_w_down"][l])
    h3 = _ln(f"ln_ffn{l}", h2, ff, p["ln_ffn_g"][l:l + 1], p["ln_ffn_b"][l:l + 1])
    return h3, (h, qx, kx, vx, ox, xa, h2, gg, uu, act, ff)


def _tail_bwd(l, dh3, saved, mem2, bsz, p, gr):
    h, qx, kx, vx, ox, xa, h2, gg, uu, act, ff = saved
    dh2, dff, gr["ln_ffn_g"][l], gr["ln_ffn_b"][l] = _ln(f"ln_ffn{l}", h2, ff, p["ln_ffn_g"][l:l + 1], p["ln_ffn_b"][l:l + 1], cts=dh3)
    dact = matmul(f"ffn_d_dx{l}", dff, p["ffn_w_down"][l], tb=True)
    gr["ffn_w_down"][l] = matmul(f"ffn_d_dw{l}", act, dff, ta=True)
    dgg, duu = _rows(f"swiglu_bwd{l}", swiglu_f, [gg, uu], [], None, TM, fwd=False, cts=[dact], wrt_rows=(0, 1))
    gr["ffn_w_gate"][l] = matmul(f"ffn_g_dw{l}", h2, dgg, ta=True)
    gr["ffn_w_up"][l] = matmul(f"ffn_u_dw{l}", h2, duu, ta=True)
    dh2 = dh2 + matmul(f"ffn_g_dx{l}", dgg, p["ffn_w_gate"][l], tb=True) + matmul(f"ffn_u_dx{l}", duu, p["ffn_w_up"][l], tb=True)
    dh, dxa, gr["ln_xa_g"][l], gr["ln_xa_b"][l] = _ln(f"ln_xa{l}", h, xa, p["ln_xa_g"][l:l + 1], p["ln_xa_b"][l:l + 1], cts=dh2)
    dox = matmul(f"xa_o_dx{l}", dxa, p["xa_w_o"][l], tb=True)
    gr["xa_w_o"][l] = matmul(f"xa_o_dw{l}", ox, dxa, ta=True)
    dqx, dkx, dvx = xattn(bsz, qx, kx, vx, cts=dox)
    gr["xa_w_q"][l] = matmul(f"xa_q_dw{l}", h, dqx, ta=True)
    gr["xa_w_k"][l] = matmul(f"xa_k_dw{l}", mem2, dkx, ta=True)
    gr["xa_w_v"][l] = matmul(f"xa_v_dw{l}", mem2, dvx, ta=True)
    return dh + matmul(f"xa_q_dx{l}", dqx, p["xa_w_q"][l], tb=True)


def _uq_perm(w):
    w3 = w.reshape(w.shape[0], 8, 192)
    return jnp.concatenate([w3[:, :, :128].reshape(-1, 1024), w3[:, :, 128:160].reshape(-1, 256),
                            w3[:, :, 160:].reshape(-1, 256)], axis=1)


def _uq_unperm(g):
    r = g.shape[0]
    return jnp.concatenate([g[:, :1024].reshape(r, 8, 128), g[:, 1024:1280].reshape(r, 8, 32),
                            g[:, 1280:].reshape(r, 8, 32)], axis=2).reshape(r, 1536)


def _ukv_perm(w):
    w3 = w.reshape(w.shape[0], 8, 256)
    return jnp.concatenate([w3[:, :, :128].reshape(-1, 1024), w3[:, :, 128:].reshape(-1, 1024)], axis=1)


def _ukv_unperm(g):
    r = g.shape[0]
    return jnp.concatenate([g[:, :1024].reshape(r, 8, 128), g[:, 1024:].reshape(r, 8, 128)], axis=2).reshape(r, 2048)


def _pad_cols(w, n):
    return jnp.pad(w, ((0, 0), (0, n - w.shape[1])))


def _shift_prev(x, bsz):
    t, w = x.shape
    x3 = x.reshape(bsz, t // bsz, w)
    return jnp.pad(x3, ((0, 0), (1, 0), (0, 0)))[:, :-1].reshape(t, w)


def _shift_next(x, bsz):
    t, w = x.shape
    x3 = x.reshape(bsz, t // bsz, w)
    return jnp.pad(x3[:, 1:], ((0, 0), (0, 1), (0, 0))).reshape(t, w)


def device_step(x, mem, tgt, p):
    bsz, seq, d = x.shape
    t = bsz * seq
    x2, mem2, tgt2 = x.reshape(t, d), mem.reshape(bsz * mem.shape[1], d), tgt.reshape(t, d)
    gr = {k: [None] * DEPTH for k in ("ln_mix_g", "ln_mix_b", "xa_w_q", "xa_w_k", "xa_w_v", "xa_w_o", "ln_xa_g", "ln_xa_b",
                                      "ffn_w_gate", "ffn_w_up", "ffn_w_down", "ln_ffn_g", "ln_ffn_b")}
    cos_pe, sin_pe = _rope_tables(seq, 64)
    cos_c, sin_c = _rope_tables(seq, 128)
    cq, sq = jnp.tile(cos_pe, (bsz, 8)), jnp.tile(sin_pe, (bsz, 8))
    ck, sk = jnp.tile(cos_pe, (bsz, 1)), jnp.tile(sin_pe, (bsz, 1))
    cd, sd = jnp.tile(cos_c, (bsz, 8)), jnp.tile(sin_c, (bsz, 8))

    w_in0 = _pad_cols(p["ev_w_in"][0], EV_PAD)
    w_uq, w_ukv = _uq_perm(p["ev_mla_w_uq"][0]), _ukv_perm(p["ev_mla_w_ukv"][0])
    z0 = matmul("ev_in", x2, w_in0)
    c_q, c_kv, k_pe, q_g, k_g, v_g, r_g, lr_g = (z0[:, a:b] for a, b in _offsets(EV_W))
    qr = _rows("q_rms", rms_f, [c_q], [p["ev_mla_q_norm"]], [512], TM)[0]
    kvr = _rows("kv_rms", rms_f, [c_kv], [p["ev_mla_kv_norm"]], [256], TM)[0]
    q = matmul("mla_uq", qr, w_uq)
    kv = matmul("mla_ukv", kvr, w_ukv)
    qn, qp1, qp2 = q[:, :1024], q[:, 1024:1280], q[:, 1280:]
    kn, vv = kv[:, :1024], kv[:, 1024:]
    kp1, kp2 = k_pe[:, :32], k_pe[:, 32:]
    r1, r2 = _rows("rope_q", rope_f, [qp1, qp2, cq, sq], [], [256, 256], TM)
    kr1, kr2 = _rows("rope_k", rope_f, [kp1, kp2, ck, sk], [], [32, 32], TM)
    r1h, r2h = _to_heads(r1, bsz, 8), _to_heads(r2, bsz, 8)
    a_out = mla_attn(bsz, qn, r1h, r2h, kn, kr1, kr2, vv)
    gla_prm = (p["ev_gla_w_gate2"][0], p["ev_gla_b_gate"], p["ev_gla_norm_g"], p["ev_gla_norm_b"])
    b_out, gla_sv = gla_block(bsz, q_g, k_g, v_g, r_g, lr_g, *gla_prm)
    mixin0 = jnp.concatenate([a_out, b_out], axis=1)
    mix0 = matmul("ev_out", mixin0, p["ev_w_out"][0])
    h1 = _ln("ln_mix0", x2, mix0, p["ln_mix_g"][0:1], p["ln_mix_b"][0:1])
    h3, tail0 = _tail_fwd(0, h1, mem2, bsz, p)

    w_in1 = _pad_cols(p["od_w_in"][0], OD_PAD)
    z1 = matmul("od_in", h3, w_in1)
    dq_, dk_, dv_ = z1[:, :1024], z1[:, 1024:2048], z1[:, 2048:3072]
    d_in = z1[:, 3072:OD_IN]
    q1, q2 = _halves(dq_, 8)
    k1, k2 = _halves(dk_, 8)
    qd1, qd2 = _rows("rope_dq", rope_f, [q1, q2, cd, sd], [], [512, 512], TM)
    kd1, kd2 = _rows("rope_dk", rope_f, [k1, k2, cd, sd], [], [512, 512], TM)
    c_out = dil_block(bsz, qd1, qd2, kd1, kd2, dv_)
    d_prev = _shift_prev(d_in, bsz)
    mu = p["od_rwkv_mu"]
    ds = _rows("tshift", tshift_f, [d_in, d_prev], [mu], [d_in.shape[1]], TM)[0]
    rw_in = tuple(ds[:, a:b] for a, b in _offsets((1024, 1024, 1024, 96, 96, 256)))
    rw_prm = dict(w0=p["od_rwkv_w0"], wd2=p["od_rwkv_w_decay2"][0], a0=p["od_rwkv_a0"], wa2=p["od_rwkv_w_a2"][0],
                  wg2=p["od_rwkv_w_gate2"][0], k_k=p["od_rwkv_k_k"], k_a=p["od_rwkv_k_a"], r_k=p["od_rwkv_r_k"][0],
                  gn_g=p["od_rwkv_gn_g"], gn_b=p["od_rwkv_gn_b"])
    d_out = rwkv_block(bsz, *rw_in, rw_prm)
    mixin1 = jnp.concatenate([c_out, d_out], axis=1)
    mix1 = matmul("od_out", mixin1, p["od_w_out"][0])
    h4 = _ln("ln_mix1", h3, mix1, p["ln_mix_g"][1:2], p["ln_mix_b"][1:2])
    y, tail1 = _tail_fwd(1, h4, mem2, bsz, p)

    dy, row_loss = _rows("loss", loss_f, [y, tgt2], [], [d, 1], TM)
    loss = jnp.sum(row_loss)

    dh4 = _tail_bwd(1, dy, tail1, mem2, bsz, p, gr)
    dh3, dmix1, gr["ln_mix_g"][1], gr["ln_mix_b"][1] = _ln("ln_mix1", h3, mix1, p["ln_mix_g"][1:2], p["ln_mix_b"][1:2], cts=dh4)
    dmixin1 = matmul("od_out_dx", dmix1, p["od_w_out"][0], tb=True)
    gr["od_w_out"] = matmul("od_out_dw", mixin1, dmix1, ta=True)[None]
    dc_out, dd_out = dmixin1[:, :1024], dmixin1[:, 1024:]
    drw_in, drw_prm = rwkv_block(bsz, *rw_in, rw_prm, cts=dd_out)
    dds = jnp.concatenate(drw_in, axis=1)
    dd_in, dd_prev, dmu = _rows("tshift_bwd", tshift_f, [d_in, d_prev], [mu], None, TM, fwd=False, cts=[dds],
                                wrt_rows=(0, 1), wrt_params=(0,))
    dd_in = dd_in + _shift_next(dd_prev, bsz)
    dqd1, dqd2, dkd1, dkd2, ddv = dil_block(bsz, qd1, qd2, kd1, kd2, dv_, cts=dc_out)
    dq1, dq2 = _rows("rope_dq_bwd", rope_f, [q1, q2, cd, sd], [], None, TM, fwd=False, cts=[dqd1, dqd2], wrt_rows=(0, 1))
    dk1, dk2 = _rows("rope_dk_bwd", rope_f, [k1, k2, cd, sd], [], None, TM, fwd=False, cts=[dkd1, dkd2], wrt_rows=(0, 1))
    dz1 = jnp.concatenate([_unhalves(dq1, dq2, 8), _unhalves(dk1, dk2, 8), ddv, dd_in,
                           jnp.zeros((t, OD_PAD - OD_IN), F32)], axis=1)
    gr["od_w_in"] = matmul("od_in_dw", h3, dz1, ta=True)[:, :OD_IN][None]
    dh3 = dh3 + matmul("od_in_dx", dz1, w_in1, tb=True)
    gr["od_rwkv_mu"] = dmu
    gr["od_rwkv_w0"], gr["od_rwkv_w_decay2"], gr["od_rwkv_a0"] = drw_prm["w0"], drw_prm["wd2"][None], drw_prm["a0"]
    gr["od_rwkv_w_a2"], gr["od_rwkv_w_gate2"] = drw_prm["wa2"][None], drw_prm["wg2"][None]
    gr["od_rwkv_k_k"], gr["od_rwkv_k_a"], gr["od_rwkv_r_k"] = drw_prm["k_k"], drw_prm["k_a"], drw_prm["r_k"][None]
    gr["od_rwkv_gn_g"], gr["od_rwkv_gn_b"] = drw_prm["gn_g"], drw_prm["gn_b"]

    dh1 = _tail_bwd(0, dh3, tail0, mem2, bsz, p, gr)
    dx2, dmix0, gr["ln_mix_g"][0], gr["ln_mix_b"][0] = _ln("ln_mix0", x2, mix0, p["ln_mix_g"][0:1], p["ln_mix_b"][0:1], cts=dh1)
    dmixin0 = matmul("ev_out_dx", dmix0, p["ev_w_out"][0], tb=True)
    gr["ev_w_out"] = matmul("ev_out_dw", mixin0, dmix0, ta=True)[None]
    da_out, db_out = dmixin0[:, :1024], dmixin0[:, 1024:]
    dq_g, dk_g, dv_g, dlr4, dr_g, dw2, dbg, dng, dnb = gla_block(bsz, q_g, k_g, v_g, r_g, lr_g, *gla_prm, cts=(db_out, gla_sv))
    dlr_g = jnp.sum(dlr4, axis=0)
    dqn, dr1h, dr2h, dkn, dkr1, dkr2, dvv = mla_attn(bsz, qn, r1h, r2h, kn, kr1, kr2, vv, cts=da_out)
    dqp1, dqp2 = _rows("rope_q_bwd", rope_f, [qp1, qp2, cq, sq], [], None, TM, fwd=False,
                       cts=[_from_heads(dr1h), _from_heads(dr2h)], wrt_rows=(0, 1))
    dkp1, dkp2 = _rows("rope_k_bwd", rope_f, [kp1, kp2, ck, sk], [], None, TM, fwd=False, cts=[dkr1, dkr2], wrt_rows=(0, 1))
    dq = jnp.concatenate([dqn, dqp1, dqp2], axis=1)
    dkv = jnp.concatenate([dkn, dvv], axis=1)
    dqr = matmul("mla_uq_dx", dq, w_uq, tb=True)
    gr["ev_mla_w_uq"] = _uq_unperm(matmul("mla_uq_dw", qr, dq, ta=True))[None]
    dkvr = matmul("mla_ukv_dx", dkv, w_ukv, tb=True)
    gr["ev_mla_w_ukv"] = _ukv_unperm(matmul("mla_ukv_dw", kvr, dkv, ta=True))[None]
    dc_q, gr["ev_mla_q_norm"] = _rows("q_rms_bwd", rms_f, [c_q], [p["ev_mla_q_norm"]], None, TM, fwd=False, cts=[dqr],
                                      wrt_rows=(0,), wrt_params=(0,))
    dc_kv, gr["ev_mla_kv_norm"] = _rows("kv_rms_bwd", rms_f, [c_kv], [p["ev_mla_kv_norm"]], None, TM, fwd=False, cts=[dkvr],
                                        wrt_rows=(0,), wrt_params=(0,))
    dz0 = jnp.concatenate([dc_q, dc_kv, dkp1, dkp2, dq_g, dk_g, dv_g, dr_g, dlr_g,
                           jnp.zeros((t, EV_PAD - EV_IN), F32)], axis=1)
    gr["ev_w_in"] = matmul("ev_in_dw", x2, dz0, ta=True)[:, :EV_IN][None]
    dx2 = dx2 + matmul("ev_in_dx", dz0, w_in0, tb=True)
    gr["ev_gla_w_gate2"], gr["ev_gla_b_gate"] = dw2[None], dbg
    gr["ev_gla_norm_g"], gr["ev_gla_norm_b"] = dng, dnb
    for k in list(gr):
        if isinstance(gr[k], list):
            gr[k] = jnp.stack([g[0] if k.startswith("ln_") else g for g in gr[k]])
    return loss, dx2.reshape(bsz, seq, d), gr


WEIGHTS = ['ev_w_in', 'ev_mla_q_norm', 'ev_mla_w_uq', 'ev_mla_kv_norm', 'ev_mla_w_ukv', 'ev_gla_w_gate2', 'ev_gla_b_gate',
           'ev_gla_norm_g', 'ev_gla_norm_b', 'ev_w_out', 'od_w_in', 'od_rwkv_mu', 'od_rwkv_w0', 'od_rwkv_w_decay2',
           'od_rwkv_a0', 'od_rwkv_w_a2', 'od_rwkv_w_gate2', 'od_rwkv_k_k', 'od_rwkv_k_a', 'od_rwkv_r_k', 'od_rwkv_gn_g',
           'od_rwkv_gn_b', 'od_w_out', 'ln_mix_g', 'ln_mix_b', 'xa_w_q', 'xa_w_k', 'xa_w_v', 'xa_w_o', 'ln_xa_g', 'ln_xa_b',
           'ffn_w_gate', 'ffn_w_up', 'ffn_w_down', 'ln_ffn_g', 'ln_ffn_b']
BIG = {'ev_w_in': -1, 'ev_mla_w_uq': -1, 'ev_mla_w_ukv': -1, 'ev_w_out': -2, 'od_w_in': -1, 'od_w_out': -2,
       'xa_w_q': -2, 'xa_w_k': -2, 'xa_w_v': -2, 'xa_w_o': -2, 'ffn_w_gate': -1, 'ffn_w_up': -1, 'ffn_w_down': -2}
SMALL = ['ev_gla_w_gate2', 'od_rwkv_mu', 'od_rwkv_w0', 'od_rwkv_w_decay2', 'od_rwkv_a0', 'od_rwkv_w_a2', 'od_rwkv_w_gate2',
         'od_rwkv_k_k', 'od_rwkv_k_a', 'od_rwkv_gn_g', 'od_rwkv_gn_b']
REPL = ['ev_mla_q_norm', 'ev_mla_kv_norm', 'ev_gla_b_gate', 'ev_gla_norm_g', 'ev_gla_norm_b', 'od_rwkv_r_k',
        'ln_mix_g', 'ln_mix_b', 'ln_xa_g', 'ln_xa_b', 'ln_ffn_g', 'ln_ffn_b']
PACK_COLS = 128


def _unshard(g, shape, axis):
    axis %= len(shape)
    full = list(shape)
    full[axis] *= N_DEV
    return jnp.moveaxis(g, 0, axis).reshape(full)


def _shard_stack(gfull, axis):
    axis %= gfull.ndim
    shp = list(gfull.shape)
    shp[axis:axis + 1] = [N_DEV, shp[axis] // N_DEV]
    return jnp.moveaxis(gfull.reshape(shp), axis, 0)


def _pack(arrs):
    lead = arrs[0].shape[0]
    flat = jnp.concatenate([a.reshape(lead, -1) for a in arrs], axis=1)
    n = flat.shape[1]
    rows = -(-n // (8 * PACK_COLS)) * 8
    return jnp.pad(flat, ((0, 0), (0, rows * PACK_COLS - n))).reshape(lead, rows, PACK_COLS)


def _unpack(buf, shapes):
    lead = buf.shape[0]
    flat = buf.reshape(lead, -1)
    out, off = [], 0
    for shp in shapes:
        n = math.prod(shp)
        out.append(flat[:, off:off + n].reshape((lead,) + tuple(shp)))
        off += n
    return out


def train_step(x, mem, loss_target, w, m, v):
    p = {}
    for k, ax in BIG.items():
        p[k] = _unshard(all_gather("ag_" + k, w[k].astype(BF16)), w[k].shape, ax)
    small_loc = _pack([w[k][None] for k in SMALL])[0]
    small_all = _unpack(all_gather("ag_small", small_loc), [w[k].shape for k in SMALL])
    for k, g in zip(SMALL, small_all):
        p[k] = _unshard(g, w[k].shape, -1)
    for k in REPL:
        p[k] = w[k]
    loss, dx, gr = device_step(x, mem, loss_target, p)
    loss = lax.psum(loss, ("x", "y", "c"))

    stacks = {}
    for k, ax in BIG.items():
        stacks[k] = reduce_scatter_exchange("rs_" + k, _shard_stack(gr[k], ax))
    small_send = _pack([_shard_stack(gr[k], -1) for k in SMALL])
    small_recv = _unpack(reduce_scatter_exchange("rs_small", small_send), [w[k].shape for k in SMALL])
    stacks.update(zip(SMALL, small_recv))
    repl_loc = _pack([gr[k][None] for k in REPL])[0]
    repl_all = _unpack(all_gather("ag_repl_grads", repl_loc), [w[k].shape for k in REPL])
    stacks.update(zip(REPL, repl_all))

    grads, deltas, new_m, new_v = [], [], [], []
    for k in WEIGHTS:
        g, dl, mn, vn = adamw("adamw_" + k, w[k], m[k], v[k], stacks[k])
        grads.append(g), deltas.append(dl), new_m.append(mn), new_v.append(vn)
    return (loss, dx, *grads, *deltas, *new_m, *new_v)


def kernel(x, mem, ev_w_in, ev_mla_q_norm, ev_mla_w_uq, ev_mla_kv_norm, ev_mla_w_ukv, ev_gla_w_gate2, ev_gla_b_gate, ev_gla_norm_g, ev_gla_norm_b, ev_w_out, od_w_in, od_rwkv_mu, od_rwkv_w0, od_rwkv_w_decay2, od_rwkv_a0, od_rwkv_w_a2, od_rwkv_w_gate2, od_rwkv_k_k, od_rwkv_k_a, od_rwkv_r_k, od_rwkv_gn_g, od_rwkv_gn_b, od_w_out, ln_mix_g, ln_mix_b, xa_w_q, xa_w_k, xa_w_v, xa_w_o, ln_xa_g, ln_xa_b, ffn_w_gate, ffn_w_up, ffn_w_down, ln_ffn_g, ln_ffn_b, loss_target, m_ev_w_in, m_ev_mla_q_norm, m_ev_mla_w_uq, m_ev_mla_kv_norm, m_ev_mla_w_ukv, m_ev_gla_w_gate2, m_ev_gla_b_gate, m_ev_gla_norm_g, m_ev_gla_norm_b, m_ev_w_out, m_od_w_in, m_od_rwkv_mu, m_od_rwkv_w0, m_od_rwkv_w_decay2, m_od_rwkv_a0, m_od_rwkv_w_a2, m_od_rwkv_w_gate2, m_od_rwkv_k_k, m_od_rwkv_k_a, m_od_rwkv_r_k, m_od_rwkv_gn_g, m_od_rwkv_gn_b, m_od_w_out, m_ln_mix_g, m_ln_mix_b, m_xa_w_q, m_xa_w_k, m_xa_w_v, m_xa_w_o, m_ln_xa_g, m_ln_xa_b, m_ffn_w_gate, m_ffn_w_up, m_ffn_w_down, m_ln_ffn_g, m_ln_ffn_b, v_ev_w_in, v_ev_mla_q_norm, v_ev_mla_w_uq, v_ev_mla_kv_norm, v_ev_mla_w_ukv, v_ev_gla_w_gate2, v_ev_gla_b_gate, v_ev_gla_norm_g, v_ev_gla_norm_b, v_ev_w_out, v_od_w_in, v_od_rwkv_mu, v_od_rwkv_w0, v_od_rwkv_w_decay2, v_od_rwkv_a0, v_od_rwkv_w_a2, v_od_rwkv_w_gate2, v_od_rwkv_k_k, v_od_rwkv_k_a, v_od_rwkv_r_k, v_od_rwkv_gn_g, v_od_rwkv_gn_b, v_od_w_out, v_ln_mix_g, v_ln_mix_b, v_xa_w_q, v_xa_w_k, v_xa_w_v, v_xa_w_o, v_ln_xa_g, v_ln_xa_b, v_ffn_w_gate, v_ffn_w_up, v_ffn_w_down, v_ln_ffn_g, v_ln_ffn_b):
    given = dict(locals())
    w = {k: given[k] for k in WEIGHTS}
    m = {k: given["m_" + k] for k in WEIGHTS}
    v = {k: given["v_" + k] for k in WEIGHTS}
    return train_step(given["x"], given["mem"], given["loss_target"], w, m, v)
```

```python
import functools
import math

import jax
import jax.numpy as jnp
from jax import lax
from jax.experimental import pallas as pl
from jax.experimental.pallas import tpu as pltpu

F32 = jnp.float32
BF16 = jnp.bfloat16
VMEM_LIMIT = 56 * 1024 * 1024
ROWS_VMEM = 20 * 1024 * 1024

N_DEV = 8
DEPTH = 2
ALPHA = (2.0 * DEPTH) ** 0.25
LN_EPS = 1e-5
RMS_EPS = 1e-6
RWKV_GN_EPS = 64e-5
ADAM_LR, ADAM_B1, ADAM_B2, ADAM_EPS, ADAM_WD, ADAM_STEP = 0.001, 0.9, 0.999, 1e-08, 0.01, 10
NEG_INF = float("-inf")


def _pcall(body, **kw):
    return pl.pallas_call(body, **kw)


def _cparams(**kw):
    return pltpu.CompilerParams(vmem_limit_bytes=VMEM_LIMIT, **kw)


def _dg(a, b, ca, cb, batch):
    nb = 1 if batch else 0
    dims = (((ca + nb,), (cb + nb,)), ((0,), (0,)) if batch else ((), ()))
    return lax.dot_general(a.astype(BF16), b.astype(BF16), dims, preferred_element_type=F32)


@functools.partial(jax.custom_vjp, nondiff_argnums=(2, 3, 4))
def _mm(a, b, ta, tb, batch):
    return _dg(a, b, 0 if ta else 1, 1 if tb else 0, batch)


def _mm_fwd(a, b, ta, tb, batch):
    return _mm(a, b, ta, tb, batch), (a, b)


def _mm_bwd(ta, tb, batch, res, g):
    a, b = res
    if not ta and not tb:
        da, db = _mm(g, b, False, True, batch), _mm(a, g, True, False, batch)
    elif not ta and tb:
        da, db = _mm(g, b, False, False, batch), _mm(g, a, True, False, batch)
    elif ta and not tb:
        da, db = _mm(b, g, False, True, batch), _mm(a, g, False, False, batch)
    else:
        da, db = _mm(b, g, True, True, batch), _mm(g, a, True, True, batch)
    return da.astype(a.dtype), db.astype(b.dtype)


_mm.defvjp(_mm_fwd, _mm_bwd)


def mm(a, b, ta=False, tb=False):
    return _mm(a, b, ta, tb, a.ndim == 3)


def _bs(block, imap):
    return pl.BlockSpec(block, imap)


def _rev_imap(imap, n):
    def r(*idx):
        return imap(*idx[:-1], n - 1 - idx[-1])
    return r


def p_fwd(name, f, grid, ins, outs, carry=None, save_carry=None):
    n_in, n_out = len(ins), len(outs)

    def body(*refs):
        in_refs = refs[:n_in]
        out_refs = refs[n_in:n_in + n_out]
        rest = refs[n_in + n_out:]
        vals = [r[...] for r in in_refs]
        if carry is None:
            res = f(*vals)
        else:
            if save_carry is not None:
                sv_ref, c_ref = rest
            else:
                (c_ref,) = rest

            @pl.when(pl.program_id(len(grid) - 1) == 0)
            def _():
                c_ref[...] = jnp.zeros(c_ref.shape, c_ref.dtype)

            c = c_ref[...]
            if save_carry is not None:
                sv_ref[...] = c
            res = f(c, *vals)
            c_ref[...] = res[0]
            res = res[1:]
        if not isinstance(res, (tuple, list)):
            res = (res,)
        for r, v in zip(out_refs, res):
            r[...] = v.astype(r.dtype)

    out_shape = [jax.ShapeDtypeStruct(s, d) for (s, d, _, _) in outs]
    out_specs = [_bs(b, m) for (_, _, b, m) in outs]
    scratch = []
    if carry is not None:
        if save_carry is not None:
            out_shape.append(jax.ShapeDtypeStruct(save_carry[0], carry[1]))
            out_specs.append(_bs(save_carry[1], save_carry[2]))
        scratch.append(pltpu.VMEM(carry[0], carry[1]))
    return _pcall(
        body, name=name, grid=grid,
        in_specs=[_bs(b, m) for (_, b, m) in ins],
        out_specs=out_specs, out_shape=out_shape, scratch_shapes=scratch,
        compiler_params=_cparams(),
    )(*[a for (a, _, _) in ins])


def p_bwd(name, f, grid, ins, cts, wrt, carry=None, saved=None):
    n_in, n_ct, n_w = len(ins), len(cts), len(wrt)
    rev = carry is not None
    n_last = grid[-1]

    def fix(imap):
        return _rev_imap(imap, n_last) if rev else imap

    def body(*refs):
        in_refs = refs[:n_in]
        ct_refs = refs[n_in:n_in + n_ct]
        k = n_in + n_ct
        if rev:
            sv_ref = refs[k]
            k += 1
        out_refs = refs[k:k + n_w]
        rest = refs[k + n_w:]
        vals = [r[...] for r in in_refs]
        ct_vals = [r[...].astype(F32) for r in ct_refs]
        widx = [w[0] for w in wrt]

        if rev:
            (dc_ref,) = rest

            @pl.when(pl.program_id(len(grid) - 1) == 0)
            def _():
                dc_ref[...] = jnp.zeros(dc_ref.shape, dc_ref.dtype)

            c_in = sv_ref[...]

            def g(c, *dv):
                full = list(vals)
                for i, d in zip(widx, dv):
                    full[i] = d
                return tuple(f(c, *full))

            _, vjp = jax.vjp(g, c_in, *[vals[i] for i in widx])
            grads = vjp((dc_ref[...],) + tuple(ct_vals))
            dc_ref[...] = grads[0]
            grads = grads[1:]
        else:
            def g(*dv):
                full = list(vals)
                for i, d in zip(widx, dv):
                    full[i] = d
                r = f(*full)
                return tuple(r) if isinstance(r, (tuple, list)) else (r,)

            _, vjp = jax.vjp(g, *[vals[i] for i in widx])
            grads = vjp(tuple(ct_vals))

        for w, o_ref, gr in zip(wrt, out_refs, grads):
            acc = w[1]
            if acc is None:
                o_ref[...] = gr.astype(o_ref.dtype)
            else:
                first = None
                for ax in acc:
                    c0 = pl.program_id(ax) == 0
                    first = c0 if first is None else jnp.logical_and(first, c0)

                @pl.when(first)
                def _():
                    o_ref[...] = jnp.zeros(o_ref.shape, o_ref.dtype)

                o_ref[...] += gr.astype(o_ref.dtype)

    in_specs = [_bs(b, fix(m)) for (_, b, m) in ins] + [_bs(b, fix(m)) for (_, b, m) in cts]
    args = [a for (a, _, _) in ins] + [a for (a, _, _) in cts]
    if rev:
        in_specs.append(_bs(saved[1], fix(saved[2])))
        args.append(saved[0])
    out_shape, out_specs = [], []
    for w in wrt:
        a, b, m = ins[w[0]]
        if len(w) > 2:
            m = w[2]
        out_shape.append(jax.ShapeDtypeStruct(a.shape, F32))
        out_specs.append(_bs(b, fix(m)))
    scratch = [pltpu.VMEM(carry[0], carry[1])] if rev else []
    return _pcall(
        body, name=name, grid=grid, in_specs=in_specs, out_specs=out_specs,
        out_shape=out_shape, scratch_shapes=scratch, compiler_params=_cparams(),
    )(*args)


def _rows(name, f, row_ins, params, out_widths, tm, fwd=True, cts=None, wrt_rows=(), wrt_params=()):
    t = row_ins[0].shape[0]
    width = sum(a.shape[1] for a in row_ins)
    width += sum(out_widths) if fwd else sum(c.shape[1] for c in cts) + sum(row_ins[i].shape[1] for i in wrt_rows)
    tm = min(tm, t)
    while tm > 8 and 2 * 4 * tm * width > ROWS_VMEM:
        tm //= 2
    rmap = lambda i: (i, 0)
    pmap = lambda i: (0, 0)
    ins = [(a, (tm, a.shape[1]), rmap) for a in row_ins] + [(p, p.shape, pmap) for p in params]
    if fwd:
        outs = [((t, w), F32, (tm, w), rmap) for w in out_widths]
        return p_fwd(name, f, (t // tm,), ins, outs)
    ct_specs = [(c, (tm, c.shape[1]), rmap) for c in cts]
    wrt = [(i, None) for i in wrt_rows] + [(len(row_ins) + i, (0,)) for i in wrt_params]
    return p_bwd(name, f, (t // tm,), ins, ct_specs, wrt)


def _pick(n, cands):
    for c in cands:
        if n % c == 0:
            return c
    return n


def _wide(n, cap=1664):
    best = None
    for w in range(128, min(n, cap) + 1, 128):
        if n % w == 0:
            best = w
    return best or n


def matmul(name, a, b, ta=False, tb=False, out_dtype=F32):
    m = a.shape[1] if ta else a.shape[0]
    k = a.shape[0] if ta else a.shape[1]
    n = b.shape[0] if tb else b.shape[1]
    assert (b.shape[1] if tb else b.shape[0]) == k, (a.shape, b.shape, ta, tb)
    bm, bn, bk = _pick(m, (1024, 512, 256, 128)), _wide(n), _pick(k, (512, 256, 128))
    nk = k // bk

    def body(a_ref, b_ref, o_ref, acc_ref):
        @pl.when(pl.program_id(2) == 0)
        def _():
            acc_ref[...] = jnp.zeros(acc_ref.shape, F32)

        acc_ref[...] += mm(a_ref[...], b_ref[...], ta, tb)

        @pl.when(pl.program_id(2) == nk - 1)
        def _():
            o_ref[...] = acc_ref[...].astype(o_ref.dtype)

    a_spec = _bs((bk, bm), lambda i, j, l: (l, i)) if ta else _bs((bm, bk), lambda i, j, l: (i, l))
    b_spec = _bs((bn, bk), lambda i, j, l: (j, l)) if tb else _bs((bk, bn), lambda i, j, l: (l, j))
    return _pcall(
        body, name=name, grid=(m // bm, n // bn, nk),
        in_specs=[a_spec, b_spec], out_specs=_bs((bm, bn), lambda i, j, l: (i, j)),
        out_shape=jax.ShapeDtypeStruct((m, n), out_dtype),
        scratch_shapes=[pltpu.VMEM((bm, bn), F32)],
        compiler_params=_cparams(dimension_semantics=("parallel", "parallel", "arbitrary")),
    )(a, b)


def _to_heads(x, b, h):
    t, w = x.shape
    return x.reshape(b, t // b, h, w // h).transpose(0, 2, 1, 3)


def _from_heads(x):
    b, h, s, d = x.shape
    return x.transpose(0, 2, 1, 3).reshape(b * s, h * d)


RW_STEPS = 8
RW_G = 16


def rwkv_group(st, w8, kk8, ka8, kh8, r8, vc):
    ys = []
    for t in range(RW_STEPS):
        row = lambda x: x[:, t:t + 1, :]
        sa = jnp.sum(st * row(kk8), axis=2, keepdims=True)
        st = st * row(w8) - sa * row(ka8) + vc[:, :, t:t + 1] * row(kh8)
        ys.append(jnp.sum(st * row(r8), axis=2, keepdims=True))
    return st, jnp.concatenate(ys, axis=2)


def rwkv_prehead_f(kkraw, a):
    nrm = jnp.sqrt(jnp.sum(kkraw * kkraw, axis=-1, keepdims=True))
    kk = kkraw / jnp.maximum(nrm, 1e-12)
    return kk, kk * a


def rwkv_pre_f(kd, w_lr, a_lr, g_lr, w0, wd2, a0, wa2, wg2, k_k, k_a):
    wpre = w0 + mm(jnp.tanh(w_lr), wd2)
    w = -jax.nn.softplus(-wpre) - 0.5
    decay = jnp.exp(-jnp.exp(w))
    a = jax.nn.sigmoid(a0 + mm(a_lr, wa2))
    g = mm(jax.nn.sigmoid(g_lr), wg2)
    kkraw = kd * k_k
    kh = kd * (1.0 + (a - 1.0) * k_a)
    return decay, a, g, kkraw, kh


def rwkv_post_f(y, r, kh, v, g, gn_g, gn_b, r_k):
    mu = jnp.mean(y, axis=-1, keepdims=True)
    var = jnp.mean(jnp.square(y - mu), axis=-1, keepdims=True)
    yn = (y - mu) * lax.rsqrt(var + RWKV_GN_EPS) * gn_g + gn_b
    bonus = jnp.sum(r * kh * r_k, axis=-1, keepdims=True) * v
    return (yn + bonus) * g


def _to_cols(xh):
    b, h, s, d = xh.shape
    return xh.reshape(b, h, s // RW_STEPS, RW_STEPS, d).transpose(0, 1, 2, 4, 3)


def _from_cols(xc):
    b, h, n, d, k = xc.shape
    return xc.transpose(0, 1, 2, 4, 3).reshape(b, h, n * k, d)


def rwkv_block(bsz, r, kd, vd, w_lr, a_lr, g_lr, prm, cts=None):
    t = r.shape[0]
    hh, n = 16, 64
    tm = 256
    pre_rows = [kd, w_lr, a_lr, g_lr]
    pre_prm = [prm[k] for k in ("w0", "wd2", "a0", "wa2", "wg2", "k_k", "k_a")]
    decay, a, g, kkraw, kh = _rows("rwkv_pre", rwkv_pre_f, pre_rows, pre_prm, [1024] * 5, tm)
    heads = lambda x: _to_heads(x, bsz, hh)
    rh, khh, vh, gh, dech, kkrawh, ah = (heads(x) for x in (r, kh, vd, g, decay, kkraw, a))
    s = t // bsz
    ng = s // RW_STEPS
    ts = 512
    hb = (None, None, ts, n)
    hm = lambda h, b, i: (b, h, i, 0)
    pgrid = (hh, bsz, s // ts)
    ph_ins = [(kkrawh, hb, hm), (ah, hb, hm)]
    kkh, kah = p_fwd("rwkv_prehead", rwkv_prehead_f, pgrid, ph_ins, [((bsz, hh, s, n), F32, hb, hm)] * 2)
    grp = lambda x: x.reshape(bsz, hh, ng, RW_STEPS, n)
    cb, rb, sb = (None, RW_G, None, n, RW_STEPS), (None, RW_G, None, RW_STEPS, n), (None, RW_G, None, n, n)
    cm = lambda b, h, i: (b, h, i, 0, 0)
    sc_ins = [(grp(x), rb, cm) for x in (dech, kkh, kah, khh, rh)] + [(_to_cols(vh), cb, cm)]
    grid = (bsz, hh // RW_G, ng)
    yc, sv = p_fwd("rwkv_scan", rwkv_group, grid, sc_ins, [((bsz, hh, ng, n, RW_STEPS), F32, cb, cm)],
                   carry=((RW_G, n, n), F32), save_carry=((bsz, hh, ng, n, n), sb, cm))
    yh = _from_cols(yc)
    pb = (None, 1, n)
    pm = lambda h, b, i: (h, 0, 0)
    gn_g, gn_b, r_k = (prm[k].reshape(hh, 1, n) for k in ("gn_g", "gn_b", "r_k"))
    post_ins = [(x, hb, hm) for x in (yh, rh, khh, vh, gh)] + [(p, pb, pm) for p in (gn_g, gn_b, r_k)]
    if cts is None:
        (oh,) = p_fwd("rwkv_post", rwkv_post_f, pgrid, post_ins, [((bsz, hh, s, n), F32, hb, hm)])
        return _from_heads(oh)
    doh = _to_heads(cts, bsz, hh)
    dyh, drh1, dkhh1, dvh1, dgh, dgn_g, dgn_b, dr_k = p_bwd(
        "rwkv_post_bwd", rwkv_post_f, pgrid, post_ins, [(doh, hb, hm)],
        [(i, None) for i in range(5)] + [(5 + i, (1, 2)) for i in range(3)])
    drows_v = p_bwd("rwkv_scan_bwd", rwkv_group, grid, sc_ins, [(_to_cols(dyh), cb, cm)], [(i, None) for i in range(6)],
                    carry=((RW_G, n, n), F32), saved=(sv, sb, cm))
    ddech, dkkh, dkah, dkhh2, drh2 = (x.reshape(bsz, hh, s, n) for x in drows_v[:5])
    dkkrawh, dah = p_bwd("rwkv_prehead_bwd", rwkv_prehead_f, pgrid, ph_ins, [(dkkh, hb, hm), (dkah, hb, hm)],
                         [(0, None), (1, None)])
    ddecay, dkkraw, da = _from_heads(ddech), _from_heads(dkkrawh), _from_heads(dah)
    dv = _from_heads(_from_cols(drows_v[5]) + dvh1)
    dr = _from_heads(drh2 + drh1)
    dkh = _from_heads(dkhh2 + dkhh1)
    dg = _from_heads(dgh)
    res = _rows("rwkv_pre_bwd", rwkv_pre_f, pre_rows, pre_prm, None, tm, fwd=False,
                cts=[ddecay, da, dg, dkkraw, dkh], wrt_rows=(0, 1, 2, 3), wrt_params=tuple(range(7)))
    dkd, dw_lr, da_lr, dg_lr = res[:4]
    dprm = dict(zip(("w0", "wd2", "a0", "wa2", "wg2", "k_k", "k_a"), res[4:]))
    dprm.update(gn_g=dgn_g.reshape(1, -1), gn_b=dgn_b.reshape(1, -1), r_k=dr_k.reshape(hh, n))
    return (dr, dkd, dv, dw_lr, da_lr, dg_lr), dprm


GLA_C, GLA_DK, GLA_DV, GLA_H, GLA_TAU = 64, 128, 256, 4, 16.0


def gla_chunk_f(st, q, k, v, lr, r, w2, bg, ng, nb):
    la = jax.nn.log_sigmoid(mm(lr, w2) + bg) / GLA_TAU
    ri = lax.broadcasted_iota(jnp.int32, (GLA_C, GLA_C), 0)
    ci = lax.broadcasted_iota(jnp.int32, (GLA_C, GLA_C), 1)
    causal = ci <= ri
    b = jnp.dot(causal.astype(F32), la, precision=lax.Precision.HIGHEST, preferred_element_type=F32)
    b_last = jnp.sum(la, axis=0, keepdims=True)
    q_dec = (q * (GLA_DK ** -0.5)) * jnp.exp(b)
    k_inv = k * jnp.exp(-b)
    k_end = k * jnp.exp(b_last - b)
    att = jnp.where(causal, mm(q_dec, k_inv, tb=True), 0.0)
    o = mm(att, v) + mm(q_dec, st, tb=True)
    st_new = st * jnp.exp(b_last) + mm(v, k_end, ta=True)
    mu = jnp.mean(o, axis=-1, keepdims=True)
    var = jnp.mean(jnp.square(o - mu), axis=-1, keepdims=True)
    on = (o - mu) * lax.rsqrt(var + LN_EPS) * ng + nb
    return st_new, on * jax.nn.silu(r)


def gla_block(bsz, q, k, v, r, lr, w2, bg, ng, nb, cts=None):
    t = q.shape[0]
    nc = t // bsz // GLA_C
    grid = (GLA_H, bsz, nc)
    rm = lambda h, b, c: (b * nc + c, h)
    ins = [(q, (GLA_C, GLA_DK), rm), (k, (GLA_C, GLA_DK), rm), (v, (GLA_C, GLA_DV), rm),
           (jnp.broadcast_to(lr[None], (GLA_H,) + lr.shape), (None, GLA_C, lr.shape[1]), lambda h, b, c: (h, b * nc + c, 0)),
           (r, (GLA_C, GLA_DV), rm),
           (w2, (w2.shape[0], GLA_DK), lambda h, b, c: (0, h)), (bg, (1, GLA_DK), lambda h, b, c: (0, h)),
           (ng, (1, GLA_DV), lambda h, b, c: (0, 0)), (nb, (1, GLA_DV), lambda h, b, c: (0, 0))]
    ob = (GLA_C, GLA_DV)
    sshape, sblock = (GLA_H, bsz, nc, GLA_DV, GLA_DK), (None, None, None, GLA_DV, GLA_DK)
    sm = lambda h, b, c: (h, b, c, 0, 0)
    carry = ((GLA_DV, GLA_DK), F32)
    if cts is None:
        out, sv = p_fwd("gla_scan", gla_chunk_f, grid, ins, [((t, GLA_H * GLA_DV), F32, ob, rm)],
                        carry=carry, save_carry=(sshape, sblock, sm))
        return out, sv
    dout, sv = cts
    return p_bwd("gla_scan_bwd", gla_chunk_f, grid, ins, [(dout, ob, rm)],
                 [(0, None), (1, None), (2, None), (3, None), (4, None), (5, (1, 2)), (6, (1, 2)), (7, (0, 1, 2)), (8, (0, 1, 2))],
                 carry=carry, saved=(sv, sblock, sm))


def _softmax_rows(sc):
    m = lax.stop_gradient(jnp.max(sc, axis=-1, keepdims=True))
    e = jnp.exp(sc - m)
    return e / jnp.sum(e, axis=-1, keepdims=True)


def mla_attn_f(qn, r1, r2, kn, kr1, kr2, v):
    tq, s = qn.shape[0], kn.shape[0]
    sc = (mm(qn, kn, tb=True) + mm(r1, kr1, tb=True) + mm(r2, kr2, tb=True)) * (192.0 ** -0.5)
    qpos = pl.program_id(2) * tq + lax.broadcasted_iota(jnp.int32, (tq, s), 0)
    kpos = lax.broadcasted_iota(jnp.int32, (tq, s), 1)
    sc = jnp.where(kpos <= qpos, sc, NEG_INF)
    return mm(_softmax_rows(sc), v)


def mla_attn(bsz, qn, r1, r2, kn, kr1, kr2, v, cts=None, tq=256):
    t = qn.shape[0]
    s = t // bsz
    nq = s // tq
    hh = 8
    grid = (bsz, hh, nq)
    qm = lambda b, h, i: (b * nq + i, h)
    km_ = lambda b, h, i: (b, h)
    ins = [(qn, (tq, 128), qm),
           (r1, (None, None, tq, 32), lambda b, h, i: (b, h, i, 0)), (r2, (None, None, tq, 32), lambda b, h, i: (b, h, i, 0)),
           (kn, (s, 128), km_), (kr1, (s, 32), lambda b, h, i: (b, 0)), (kr2, (s, 32), lambda b, h, i: (b, 0)),
           (v, (s, 128), km_)]
    if cts is None:
        return p_fwd("mla_attn", mla_attn_f, grid, ins, [((t, 1024), F32, (tq, 128), qm)])[0]
    return p_bwd("mla_attn_bwd", mla_attn_f, grid, ins, [(cts, (tq, 128), qm)],
                 [(0, None), (1, None), (2, None), (3, (2,)), (4, (1, 2)), (5, (1, 2)), (6, (2,))])


def xattn_f(q, k, v):
    sc = mm(q, k, tb=True) * (512.0 ** -0.5)
    return mm(_softmax_rows(sc), v)


def xattn(bsz, q, k, v, cts=None, tq=512):
    t = q.shape[0]
    nq = t // bsz // tq
    mlen = k.shape[0] // bsz
    grid = (bsz, 4, nq)
    qm = lambda b, h, i: (b * nq + i, h)
    km_ = lambda b, h, i: (b, h)
    ins = [(q, (tq, 512), qm), (k, (mlen, 512), km_), (v, (mlen, 512), km_)]
    if cts is None:
        return p_fwd("xattn", xattn_f, grid, ins, [((t, 2048), F32, (tq, 512), qm)])[0]
    return p_bwd("xattn_bwd", xattn_f, grid, ins, [(cts, (tq, 512), qm)], [(0, None), (1, (2,)), (2, (2,))])


DIL_SPAN = 128
DIL_BRANCHES = ((128, 1), (512, 4), (2048, 16))


def dil_attn_f(q1, q2, k1c, k2c, vc, k1p, k2p, vp):
    gb, sp = q1.shape[0], DIL_SPAN
    scale = 128.0 ** -0.5
    sc_c = (mm(q1, k1c, tb=True) + mm(q2, k2c, tb=True)) * scale
    sc_p = (mm(q1, k1p, tb=True) + mm(q2, k2p, tb=True)) * scale
    ql = lax.broadcasted_iota(jnp.int32, (gb, sp, sp), 1)
    kl = lax.broadcasted_iota(jnp.int32, (gb, sp, sp), 2)
    has_prev = pl.program_id(1) > 0
    sc_c = jnp.where(kl <= ql, sc_c, NEG_INF)
    sc_p = jnp.where(jnp.logical_and(kl >= ql, has_prev), sc_p, NEG_INF)
    m = lax.stop_gradient(jnp.maximum(jnp.max(sc_c, axis=-1, keepdims=True), jnp.max(sc_p, axis=-1, keepdims=True)))
    e_c, e_p = jnp.exp(sc_c - m), jnp.exp(sc_p - m)
    den = jnp.sum(e_c, axis=-1, keepdims=True) + jnp.sum(e_p, axis=-1, keepdims=True)
    o = mm(e_c / den, vc) + mm(e_p / den, vp)
    return o, m + jnp.log(den)


def dil_branch(q1, q2, k1, k2, v, cts=None, gb=8):
    g, l, _ = q1.shape
    nb = l // DIL_SPAN
    grid = (g // gb, nb)
    cm = lambda i, n: (i, n, 0)
    pm = lambda i, n: (i, jnp.maximum(n - 1, 0), 0)
    b64, b128, b1 = (gb, DIL_SPAN, 64), (gb, DIL_SPAN, 128), (gb, DIL_SPAN, 1)
    ins = [(q1, b64, cm), (q2, b64, cm), (k1, b64, cm), (k2, b64, cm), (v, b128, cm),
           (k1, b64, pm), (k2, b64, pm), (v, b128, pm)]
    if cts is None:
        return p_fwd("dil_attn", dil_attn_f, grid, ins, [((g, l, 128), F32, b128, cm), ((g, l, 1), F32, b1, cm)])
    do, dlse = cts
    dq1, dq2, dk1c, dk2c, dvc, dk1p, dk2p, dvp = p_bwd(
        "dil_attn_bwd", dil_attn_f, grid, ins, [(do, b128, cm), (dlse, b1, cm)],
        [(i, None) for i in range(5)] + [(i, None, cm) for i in (5, 6, 7)])

    def fold(dc, dp):
        return dc + jnp.pad(dp[:, DIL_SPAN:], ((0, 0), (0, DIL_SPAN), (0, 0)))

    return dq1, dq2, fold(dk1c, dk1p), fold(dk2c, dk2p), fold(dvc, dvp)


def dil_mix_f(o1, o2, o3, l1, l2, l3):
    m = lax.stop_gradient(jnp.maximum(jnp.maximum(l1, l2), l3))
    e1, e2, e3 = jnp.exp(l1 - m), jnp.exp(l2 - m), jnp.exp(l3 - m)
    den = e1 + e2 + e3
    return (e1 / den) * o1 + (e2 / den) * o2 + (e3 / den) * o3


def _to_res(xh, dil):
    b, h, s, d = xh.shape
    return xh.reshape(b, h, s // dil, dil, d).transpose(0, 1, 3, 2, 4).reshape(b * h * dil, s // dil, d)


def _from_res(xr, b, h, dil):
    g, l, d = xr.shape
    return xr.reshape(b, h, dil, l, d).transpose(0, 1, 3, 2, 4).reshape(b, h, l * dil, d)


def dil_block(bsz, q1, q2, k1, k2, v, cts=None):
    hh = 8
    heads = [_to_heads(x, bsz, hh) for x in (q1, q2, k1, k2, v)]
    s = heads[0].shape[2]
    outs, res_in = [], []
    for window, dil in DIL_BRANCHES:
        assert window // dil == DIL_SPAN and (s // dil) % DIL_SPAN == 0
        rin = [_to_res(x, dil) for x in heads]
        o, lse = dil_branch(*rin)
        res_in.append(rin)
        outs.append((_from_res(o, bsz, hh, dil), _from_res(lse, bsz, hh, dil)))
    tq = 512
    ob, lb = (None, None, tq, 128), (None, None, tq, 1)
    hm = lambda b, h, i: (b, h, i, 0)
    mix_ins = [(o, ob, hm) for (o, _) in outs] + [(l, lb, hm) for (_, l) in outs]
    grid = (bsz, hh, s // tq)
    if cts is None:
        (mix,) = p_fwd("dil_mix", dil_mix_f, grid, mix_ins, [((bsz, hh, s, 128), F32, ob, hm)])
        return _from_heads(mix)
    dmix = _to_heads(cts, bsz, hh)
    dml = p_bwd("dil_mix_bwd", dil_mix_f, grid, mix_ins, [(dmix, ob, hm)], [(i, None) for i in range(6)])
    tot = None
    for j, (window, dil) in enumerate(DIL_BRANCHES):
        do, dl = _to_res(dml[j], dil), _to_res(dml[3 + j], dil)
        gr = dil_branch(*res_in[j], cts=(do, dl))
        gr = [_from_res(x, bsz, hh, dil) for x in gr]
        tot = gr if tot is None else [a + b for a, b in zip(tot, gr)]
    return tuple(_from_heads(x) for x in tot)


_ANY = pl.BlockSpec(memory_space=pl.ANY)


def _me_and_peers():
    x, y, c = lax.axis_index("x"), lax.axis_index("y"), lax.axis_index("c")
    me = 4 * x + 2 * y + c
    peers = []
    for k in range(1, N_DEV):
        px = 1 - x if k & 4 else x
        py = 1 - y if k & 2 else y
        pc = 1 - c if k & 1 else c
        peers.append(((px, py, pc), 4 * px + 2 * py + pc))
    return me, peers


def _exchange(name, x, scatter):
    shape = x.shape[1:] if scatter else x.shape

    def body(x_ref, out_ref, send_sems, recv_sems, local_sem):
        me, peers = _me_and_peers()
        src_me = x_ref.at[me] if scatter else x_ref
        local = pltpu.make_async_copy(src_me, out_ref.at[me], local_sem)
        local.start()
        sends = []
        for k, (dev, idx) in enumerate(peers):
            cp = pltpu.make_async_remote_copy(
                src_ref=x_ref.at[idx] if scatter else x_ref, dst_ref=out_ref.at[me],
                send_sem=send_sems.at[k], recv_sem=recv_sems.at[k],
                device_id=dev, device_id_type=pl.DeviceIdType.MESH)
            cp.start()
            sends.append(cp)
        for k, (dev, idx) in enumerate(peers):
            pltpu.make_async_remote_copy(
                src_ref=src_me, dst_ref=out_ref.at[idx], send_sem=send_sems.at[k], recv_sem=recv_sems.at[k],
                device_id=dev, device_id_type=pl.DeviceIdType.MESH).wait_recv()
        for cp in sends:
            cp.wait_send()
        local.wait()

    return _pcall(
        body, name=name, in_specs=[_ANY], out_specs=_ANY,
        out_shape=jax.ShapeDtypeStruct((N_DEV,) + tuple(shape), x.dtype),
        scratch_shapes=[pltpu.SemaphoreType.DMA((N_DEV - 1,)), pltpu.SemaphoreType.DMA((N_DEV - 1,)),
                        pltpu.SemaphoreType.DMA],
        compiler_params=pltpu.CompilerParams(has_side_effects=True),
    )(x)


def all_gather(name, x):
    return _exchange(name, x, False)


def reduce_scatter_exchange(name, x):
    return _exchange(name, x, True)


def all_gather_2level(name, x):
    def body(x_ref, out_ref, send_sems, recv_sems, local_sem):
        x_, y_, c_ = lax.axis_index("x"), lax.axis_index("y"), lax.axis_index("c")
        sibling = (x_, y_, 1 - c_)
        chips = [(1 - x_, y_), (x_, 1 - y_), (1 - x_, 1 - y_)]

        def slot(px, py, pc):
            return out_ref.at[4 * px + 2 * py + pc]

        def copy(k, block, to, src=None):
            return pltpu.make_async_remote_copy(
                src_ref=slot(*block) if src is None else src, dst_ref=slot(*block),
                send_sem=send_sems.at[k], recv_sem=recv_sems.at[k], device_id=to, device_id_type=pl.DeviceIdType.MESH)

        me = (x_, y_, c_)
        mine = pltpu.make_async_copy(x_ref, slot(*me), local_sem)
        mine.start()
        first = [copy(0, me, sibling, src=x_ref)]
        first += [copy(1 + j, me, (*chip, c_), src=x_ref) for j, chip in enumerate(chips)]
        for cp in first:
            cp.start()
        passed = [copy(4 + j, (*chip, c_), sibling) for j, chip in enumerate(chips)]
        for j, chip in enumerate(chips):
            copy(1 + j, (*chip, c_), me).wait_recv()
            passed[j].start()
        copy(0, sibling, me).wait_recv()
        for j, chip in enumerate(chips):
            copy(4 + j, (*chip, 1 - c_), me).wait_recv()
        for cp in first + passed:
            cp.wait_send()
        mine.wait()

    return _pcall(
        body, name=name, in_specs=[_ANY], out_specs=_ANY,
        out_shape=jax.ShapeDtypeStruct((N_DEV,) + tuple(x.shape), x.dtype),
        scratch_shapes=[pltpu.SemaphoreType.DMA((7,)), pltpu.SemaphoreType.DMA((7,)), pltpu.SemaphoreType.DMA],
        compiler_params=pltpu.CompilerParams(has_side_effects=True),
    )(x)


def sibling_swap(name, x):
    shape = (x.shape[0],) + tuple(x.shape[2:])

    def body(x_ref, mine_ref, theirs_ref, send_sem, recv_sem, local_sem):
        x_, y_, c_ = lax.axis_index("x"), lax.axis_index("y"), lax.axis_index("c")
        local = pltpu.make_async_copy(x_ref.at[:, c_], mine_ref, local_sem)
        local.start()
        cp = pltpu.make_async_remote_copy(
            src_ref=x_ref.at[:, 1 - c_], dst_ref=theirs_ref, send_sem=send_sem, recv_sem=recv_sem,
            device_id=(x_, y_, 1 - c_), device_id_type=pl.DeviceIdType.MESH)
        cp.start()
        cp.wait()
        local.wait()

    return _pcall(
        body, name=name, in_specs=[_ANY], out_specs=[_ANY, _ANY],
        out_shape=[jax.ShapeDtypeStruct(shape, x.dtype)] * 2,
        scratch_shapes=[pltpu.SemaphoreType.DMA, pltpu.SemaphoreType.DMA, pltpu.SemaphoreType.DMA],
        compiler_params=pltpu.CompilerParams(has_side_effects=True),
    )(x)


def pair_add(name, a, b, out_dtype):
    shp = a.shape
    c = shp[-1]
    r = math.prod(shp[:-1])
    br = r
    for cand in (2048, 1024, 512, 256, 128, 64, 32, 16, 8):
        if r % cand == 0 and cand * c <= 512 * 1024:
            br = cand
            break

    def body(a_ref, b_ref, o_ref):
        o_ref[...] = (a_ref[...] + b_ref[...]).astype(o_ref.dtype)

    spec = _bs((br, c), lambda i: (i, 0))
    out = _pcall(body, name=name, grid=(r // br,), in_specs=[spec, spec], out_specs=spec,
                 out_shape=jax.ShapeDtypeStruct((r, c), out_dtype), compiler_params=_cparams())(a.reshape(r, c), b.reshape(r, c))
    return out.reshape(shp)


def chip_exchange(name, p):
    def body(p_ref, out_ref, send_sems, recv_sems, local_sem):
        x_, y_, c_ = lax.axis_index("x"), lax.axis_index("y"), lax.axis_index("c")
        my_chip = 2 * x_ + y_
        chips = [(1 - x_, y_), (x_, 1 - y_), (1 - x_, 1 - y_)]
        local = pltpu.make_async_copy(p_ref.at[my_chip], out_ref.at[my_chip], local_sem)
        local.start()
        sends = []
        for k, (px, py) in enumerate(chips):
            cp = pltpu.make_async_remote_copy(
                src_ref=p_ref.at[2 * px + py], dst_ref=out_ref.at[my_chip], send_sem=send_sems.at[k], recv_sem=recv_sems.at[k],
                device_id=(px, py, c_), device_id_type=pl.DeviceIdType.MESH)
            cp.start()
            sends.append(cp)
        for k, (px, py) in enumerate(chips):
            pltpu.make_async_remote_copy(
                src_ref=p_ref.at[my_chip], dst_ref=out_ref.at[2 * px + py], send_sem=send_sems.at[k], recv_sem=recv_sems.at[k],
                device_id=(px, py, c_), device_id_type=pl.DeviceIdType.MESH).wait_recv()
        for cp in sends:
            cp.wait_send()
        local.wait()

    return _pcall(
        body, name=name, in_specs=[_ANY], out_specs=_ANY,
        out_shape=jax.ShapeDtypeStruct(p.shape, p.dtype),
        scratch_shapes=[pltpu.SemaphoreType.DMA((3,)), pltpu.SemaphoreType.DMA((3,)), pltpu.SemaphoreType.DMA],
        compiler_params=pltpu.CompilerParams(has_side_effects=True),
    )(p)


def reduce_scatter_2level(name, stack):
    x4 = stack.reshape((4, 2) + tuple(stack.shape[1:]))
    mine, theirs = sibling_swap(name + "_d2d", x4)
    part = pair_add(name + "_add", mine, theirs, BF16)
    return chip_exchange(name + "_ici", part)


def ln_res_f(h, r, g, b):
    x = ALPHA * h + r
    mu = jnp.mean(x, axis=-1, keepdims=True)
    var = jnp.mean(jnp.square(x - mu), axis=-1, keepdims=True)
    return (x - mu) * lax.rsqrt(var + LN_EPS) * g + b


def rms_f(x, g):
    return x * lax.rsqrt(jnp.mean(x * x, axis=-1, keepdims=True) + RMS_EPS) * g


def rope_f(x1, x2, c, s):
    return x1 * c - x2 * s, x1 * s + x2 * c


def swiglu_f(g, u):
    return jax.nn.silu(g) * u


def tshift_f(d, dprev, mu):
    return d + (dprev - d) * mu


def loss_f(y, tgt):
    e = y - tgt
    return e / y.shape[-1], 0.5 * jnp.mean(e * e, axis=-1, keepdims=True)


def adamw(name, w, m, v, gstack):
    shp = w.shape
    c = shp[-1]
    r = math.prod(shp[:-1])
    br = r
    for cand in (512, 256, 128, 64, 32, 16, 8):
        if r % cand == 0 and cand * c <= 128 * 1024:
            br = cand
            break
    k = gstack.shape[0]

    def body(w_ref, m_ref, v_ref, g_ref, go_ref, d_ref, mo_ref, vo_ref):
        g = g_ref[0].astype(F32)
        for j in range(1, k):
            g = g + g_ref[j].astype(F32)
        m_new = ADAM_B1 * m_ref[...] + (1.0 - ADAM_B1) * g
        v_new = ADAM_B2 * v_ref[...] + (1.0 - ADAM_B2) * jnp.square(g)
        m_hat = m_new / (1.0 - ADAM_B1 ** ADAM_STEP)
        v_hat = v_new / (1.0 - ADAM_B2 ** ADAM_STEP)
        go_ref[...] = g
        d_ref[...] = -ADAM_LR * (m_hat / (jnp.sqrt(v_hat) + ADAM_EPS) + ADAM_WD * w_ref[...])
        mo_ref[...] = m_new
        vo_ref[...] = v_new

    spec = _bs((br, c), lambda i: (i, 0))
    outs = _pcall(
        body, name=name, grid=(r // br,),
        in_specs=[spec, spec, spec, _bs((k, br, c), lambda i: (0, i, 0))],
        out_specs=[spec] * 4, out_shape=[jax.ShapeDtypeStruct((r, c), F32)] * 4,
        compiler_params=_cparams(),
    )(w.reshape(r, c), m.reshape(r, c), v.reshape(r, c), gstack.reshape(k, r, c))
    return tuple(o.reshape(shp) for o in outs)


EV_W = (512, 256, 64, 512, 512, 1024, 1024, 16)
EV_IN, EV_PAD = 3920, 4096
OD_IN, OD_PAD = 6592, 6656
TM = 256


def _offsets(widths):
    offs, acc = [], 0
    for w in widths:
        offs.append((acc, acc + w))
        acc += w
    return offs


def _rope_tables(seq, dim):
    inv = 10000.0 ** (-jnp.arange(0, dim, 2, dtype=F32) / dim)
    ang = jnp.arange(seq, dtype=F32)[:, None] * inv[None, :]
    return jnp.cos(ang), jnp.sin(ang)


def _halves(x, nh):
    t, w = x.shape
    x3 = x.reshape(t, nh, w // nh)
    hd = w // nh // 2
    return x3[:, :, :hd].reshape(t, nh * hd), x3[:, :, hd:].reshape(t, nh * hd)


def _unhalves(x1, x2, nh):
    t = x1.shape[0]
    return jnp.concatenate([x1.reshape(t, nh, -1), x2.reshape(t, nh, -1)], axis=2).reshape(t, -1)


def _ln(name, h, r, g, b, cts=None):
    if cts is None:
        return _rows(name, ln_res_f, [h, r], [g, b], [h.shape[1]], TM)[0]
    return _rows(name + "_bwd", ln_res_f, [h, r], [g, b], None, TM, fwd=False, cts=[cts], wrt_rows=(0, 1), wrt_params=(0, 1))


def _tail_fwd(l, h, mem2, bsz, p):
    qx = matmul(f"xa_q{l}", h, p["xa_w_q"][l])
    kx = matmul(f"xa_k{l}", mem2, p["xa_w_k"][l])
    vx = matmul(f"xa_v{l}", mem2, p["xa_w_v"][l])
    ox = xattn(bsz, qx, kx, vx)
    xa = matmul(f"xa_o{l}", ox, p["xa_w_o"][l])
    h2 = _ln(f"ln_xa{l}", h, xa, p["ln_xa_g"][l:l + 1], p["ln_xa_b"][l:l + 1])
    gg = matmul(f"ffn_g{l}", h2, p["ffn_w_gate"][l])
    uu = matmul(f"ffn_u{l}", h2, p["ffn_w_up"][l])
    act = _rows(f"swiglu{l}", swiglu_f, [gg, uu], [], [gg.shape[1]], TM)[0]
    ff = matmul(f"ffn_d{l}", act, p["ffn_w_down"][l])
    h3 = _ln(f"ln_ffn{l}", h2, ff, p["ln_ffn_g"][l:l + 1], p["ln_ffn_b"][l:l + 1])
    return h3, (h, qx, kx, vx, ox, xa, h2, gg, uu, act, ff)


def _tail_bwd(l, dh3, saved, mem2, bsz, p, gr):
    h, qx, kx, vx, ox, xa, h2, gg, uu, act, ff = saved
    dh2, dff, gr["ln_ffn_g"][l], gr["ln_ffn_b"][l] = _ln(f"ln_ffn{l}", h2, ff, p["ln_ffn_g"][l:l + 1], p["ln_ffn_b"][l:l + 1], cts=dh3)
    dact = matmul(f"ffn_d_dx{l}", dff, p["ffn_w_down"][l], tb=True)
    gr["ffn_w_down"][l] = matmul(f"ffn_d_dw{l}", act, dff, ta=True)
    dgg, duu = _rows(f"swiglu_bwd{l}", swiglu_f, [gg, uu], [], None, TM, fwd=False, cts=[dact], wrt_rows=(0, 1))
    gr["ffn_w_gate"][l] = matmul(f"ffn_g_dw{l}", h2, dgg, ta=True)
    gr["ffn_w_up"][l] = matmul(f"ffn_u_dw{l}", h2, duu, ta=True)
    dh2 = dh2 + matmul(f"ffn_g_dx{l}", dgg, p["ffn_w_gate"][l], tb=True) + matmul(f"ffn_u_dx{l}", duu, p["ffn_w_up"][l], tb=True)
    dh, dxa, gr["ln_xa_g"][l], gr["ln_xa_b"][l] = _ln(f"ln_xa{l}", h, xa, p["ln_xa_g"][l:l + 1], p["ln_xa_b"][l:l + 1], cts=dh2)
    dox = matmul(f"xa_o_dx{l}", dxa, p["xa_w_o"][l], tb=True)
    gr["xa_w_o"][l] = matmul(f"xa_o_dw{l}", ox, dxa, ta=True)
    dqx, dkx, dvx = xattn(bsz, qx, kx, vx, cts=dox)
    gr["xa_w_q"][l] = matmul(f"xa_q_dw{l}", h, dqx, ta=True)
    gr["xa_w_k"][l] = matmul(f"xa_k_dw{l}", mem2, dkx, ta=True)
    gr["xa_w_v"][l] = matmul(f"xa_v_dw{l}", mem2, dvx, ta=True)
    return dh + matmul(f"xa_q_dx{l}", dqx, p["xa_w_q"][l], tb=True)


def _uq_perm(w):
    w3 = w.reshape(w.shape[0], 8, 192)
    return jnp.concatenate([w3[:, :, :128].reshape(-1, 1024), w3[:, :, 128:160].reshape(-1, 256),
                            w3[:, :, 160:].reshape(-1, 256)], axis=1)


def _uq_unperm(g):
    r = g.shape[0]
    return jnp.concatenate([g[:, :1024].reshape(r, 8, 128), g[:, 1024:1280].reshape(r, 8, 32),
                            g[:, 1280:].reshape(r, 8, 32)], axis=2).reshape(r, 1536)


def _ukv_perm(w):
    w3 = w.reshape(w.shape[0], 8, 256)
    return jnp.concatenate([w3[:, :, :128].reshape(-1, 1024), w3[:, :, 128:].reshape(-1, 1024)], axis=1)


def _ukv_unperm(g):
    r = g.shape[0]
    return jnp.concatenate([g[:, :1024].reshape(r, 8, 128), g[:, 1024:].reshape(r, 8, 128)], axis=2).reshape(r, 2048)


def _pad_cols(w, n):
    return jnp.pad(w, ((0, 0), (0, n - w.shape[1])))


def _shift_prev(x, bsz):
    t, w = x.shape
    x3 = x.reshape(bsz, t // bsz, w)
    return jnp.pad(x3, ((0, 0), (1, 0), (0, 0)))[:, :-1].reshape(t, w)


def _shift_next(x, bsz):
    t, w = x.shape
    x3 = x.reshape(bsz, t // bsz, w)
    return jnp.pad(x3[:, 1:], ((0, 0), (0, 1), (0, 0))).reshape(t, w)


def device_step(x, mem, tgt, p):
    bsz, seq, d = x.shape
    t = bsz * seq
    x2, mem2, tgt2 = x.reshape(t, d), mem.reshape(bsz * mem.shape[1], d), tgt.reshape(t, d)
    gr = {k: [None] * DEPTH for k in ("ln_mix_g", "ln_mix_b", "xa_w_q", "xa_w_k", "xa_w_v", "xa_w_o", "ln_xa_g", "ln_xa_b",
                                      "ffn_w_gate", "ffn_w_up", "ffn_w_down", "ln_ffn_g", "ln_ffn_b")}
    cos_pe, sin_pe = _rope_tables(seq, 64)
    cos_c, sin_c = _rope_tables(seq, 128)
    cq, sq = jnp.tile(cos_pe, (bsz, 8)), jnp.tile(sin_pe, (bsz, 8))
    ck, sk = jnp.tile(cos_pe, (bsz, 1)), jnp.tile(sin_pe, (bsz, 1))
    cd, sd = jnp.tile(cos_c, (bsz, 8)), jnp.tile(sin_c, (bsz, 8))

    w_in0 = _pad_cols(p["ev_w_in"][0], EV_PAD)
    w_uq, w_ukv = _uq_perm(p["ev_mla_w_uq"][0]), _ukv_perm(p["ev_mla_w_ukv"][0])
    z0 = matmul("ev_in", x2, w_in0)
    c_q, c_kv, k_pe, q_g, k_g, v_g, r_g, lr_g = (z0[:, a:b] for a, b in _offsets(EV_W))
    qr = _rows("q_rms", rms_f, [c_q], [p["ev_mla_q_norm"]], [512], TM)[0]
    kvr = _rows("kv_rms", rms_f, [c_kv], [p["ev_mla_kv_norm"]], [256], TM)[0]
    q = matmul("mla_uq", qr, w_uq)
    kv = matmul("mla_ukv", kvr, w_ukv)
    qn, qp1, qp2 = q[:, :1024], q[:, 1024:1280], q[:, 1280:]
    kn, vv = kv[:, :1024], kv[:, 1024:]
    kp1, kp2 = k_pe[:, :32], k_pe[:, 32:]
    r1, r2 = _rows("rope_q", rope_f, [qp1, qp2, cq, sq], [], [256, 256], TM)
    kr1, kr2 = _rows("rope_k", rope_f, [kp1, kp2, ck, sk], [], [32, 32], TM)
    r1h, r2h = _to_heads(r1, bsz, 8), _to_heads(r2, bsz, 8)
    a_out = mla_attn(bsz, qn, r1h, r2h, kn, kr1, kr2, vv)
    gla_prm = (p["ev_gla_w_gate2"][0], p["ev_gla_b_gate"], p["ev_gla_norm_g"], p["ev_gla_norm_b"])
    b_out, gla_sv = gla_block(bsz, q_g, k_g, v_g, r_g, lr_g, *gla_prm)
    mixin0 = jnp.concatenate([a_out, b_out], axis=1)
    mix0 = matmul("ev_out", mixin0, p["ev_w_out"][0])
    h1 = _ln("ln_mix0", x2, mix0, p["ln_mix_g"][0:1], p["ln_mix_b"][0:1])
    h3, tail0 = _tail_fwd(0, h1, mem2, bsz, p)

    w_in1 = _pad_cols(p["od_w_in"][0], OD_PAD)
    z1 = matmul("od_in", h3, w_in1)
    dq_, dk_, dv_ = z1[:, :1024], z1[:, 1024:2048], z1[:, 2048:3072]
    d_in = z1[:, 3072:OD_IN]
    q1, q2 = _halves(dq_, 8)
    k1, k2 = _halves(dk_, 8)
    qd1, qd2 = _rows("rope_dq", rope_f, [q1, q2, cd, sd], [], [512, 512], TM)
    kd1, kd2 = _rows("rope_dk", rope_f, [k1, k2, cd, sd], [], [512, 512], TM)
    c_out = dil_block(bsz, qd1, qd2, kd1, kd2, dv_)
    d_prev = _shift_prev(d_in, bsz)
    mu = p["od_rwkv_mu"]
    ds = _rows("tshift", tshift_f, [d_in, d_prev], [mu], [d_in.shape[1]], TM)[0]
    rw_in = tuple(ds[:, a:b] for a, b in _offsets((1024, 1024, 1024, 96, 96, 256)))
    rw_prm = dict(w0=p["od_rwkv_w0"], wd2=p["od_rwkv_w_decay2"][0], a0=p["od_rwkv_a0"], wa2=p["od_rwkv_w_a2"][0],
                  wg2=p["od_rwkv_w_gate2"][0], k_k=p["od_rwkv_k_k"], k_a=p["od_rwkv_k_a"], r_k=p["od_rwkv_r_k"][0],
                  gn_g=p["od_rwkv_gn_g"], gn_b=p["od_rwkv_gn_b"])
    d_out = rwkv_block(bsz, *rw_in, rw_prm)
    mixin1 = jnp.concatenate([c_out, d_out], axis=1)
    mix1 = matmul("od_out", mixin1, p["od_w_out"][0])
    h4 = _ln("ln_mix1", h3, mix1, p["ln_mix_g"][1:2], p["ln_mix_b"][1:2])
    y, tail1 = _tail_fwd(1, h4, mem2, bsz, p)

    dy, row_loss = _rows("loss", loss_f, [y, tgt2], [], [d, 1], TM)
    loss = jnp.sum(row_loss)

    dh4 = _tail_bwd(1, dy, tail1, mem2, bsz, p, gr)
    dh3, dmix1, gr["ln_mix_g"][1], gr["ln_mix_b"][1] = _ln("ln_mix1", h3, mix1, p["ln_mix_g"][1:2], p["ln_mix_b"][1:2], cts=dh4)
    dmixin1 = matmul("od_out_dx", dmix1, p["od_w_out"][0], tb=True)
    gr["od_w_out"] = matmul("od_out_dw", mixin1, dmix1, ta=True)[None]
    dc_out, dd_out = dmixin1[:, :1024], dmixin1[:, 1024:]
    drw_in, drw_prm = rwkv_block(bsz, *rw_in, rw_prm, cts=dd_out)
    dds = jnp.concatenate(drw_in, axis=1)
    dd_in, dd_prev, dmu = _rows("tshift_bwd", tshift_f, [d_in, d_prev], [mu], None, TM, fwd=False, cts=[dds],
                                wrt_rows=(0, 1), wrt_params=(0,))
    dd_in = dd_in + _shift_next(dd_prev, bsz)
    dqd1, dqd2, dkd1, dkd2, ddv = dil_block(bsz, qd1, qd2, kd1, kd2, dv_, cts=dc_out)
    dq1, dq2 = _rows("rope_dq_bwd", rope_f, [q1, q2, cd, sd], [], None, TM, fwd=False, cts=[dqd1, dqd2], wrt_rows=(0, 1))
    dk1, dk2 = _rows("rope_dk_bwd", rope_f, [k1, k2, cd, sd], [], None, TM, fwd=False, cts=[dkd1, dkd2], wrt_rows=(0, 1))
    dz1 = jnp.concatenate([_unhalves(dq1, dq2, 8), _unhalves(dk1, dk2, 8), ddv, dd_in,
                           jnp.zeros((t, OD_PAD - OD_IN), F32)], axis=1)
    gr["od_w_in"] = matmul("od_in_dw", h3, dz1, ta=True)[:, :OD_IN][None]
    dh3 = dh3 + matmul("od_in_dx", dz1, w_in1, tb=True)
    gr["od_rwkv_mu"] = dmu
    gr["od_rwkv_w0"], gr["od_rwkv_w_decay2"], gr["od_rwkv_a0"] = drw_prm["w0"], drw_prm["wd2"][None], drw_prm["a0"]
    gr["od_rwkv_w_a2"], gr["od_rwkv_w_gate2"] = drw_prm["wa2"][None], drw_prm["wg2"][None]
    gr["od_rwkv_k_k"], gr["od_rwkv_k_a"], gr["od_rwkv_r_k"] = drw_prm["k_k"], drw_prm["k_a"], drw_prm["r_k"][None]
    gr["od_rwkv_gn_g"], gr["od_rwkv_gn_b"] = drw_prm["gn_g"], drw_prm["gn_b"]

    dh1 = _tail_bwd(0, dh3, tail0, mem2, bsz, p, gr)
    dx2, dmix0, gr["ln_mix_g"][0], gr["ln_mix_b"][0] = _ln("ln_mix0", x2, mix0, p["ln_mix_g"][0:1], p["ln_mix_b"][0:1], cts=dh1)
    dmixin0 = matmul("ev_out_dx", dmix0, p["ev_w_out"][0], tb=True)
    gr["ev_w_out"] = matmul("ev_out_dw", mixin0, dmix0, ta=True)[None]
    da_out, db_out = dmixin0[:, :1024], dmixin0[:, 1024:]
    dq_g, dk_g, dv_g, dlr4, dr_g, dw2, dbg, dng, dnb = gla_block(bsz, q_g, k_g, v_g, r_g, lr_g, *gla_prm, cts=(db_out, gla_sv))
    dlr_g = jnp.sum(dlr4, axis=0)
    dqn, dr1h, dr2h, dkn, dkr1, dkr2, dvv = mla_attn(bsz, qn, r1h, r2h, kn, kr1, kr2, vv, cts=da_out)
    dqp1, dqp2 = _rows("rope_q_bwd", rope_f, [qp1, qp2, cq, sq], [], None, TM, fwd=False,
                       cts=[_from_heads(dr1h), _from_heads(dr2h)], wrt_rows=(0, 1))
    dkp1, dkp2 = _rows("rope_k_bwd", rope_f, [kp1, kp2, ck, sk], [], None, TM, fwd=False, cts=[dkr1, dkr2], wrt_rows=(0, 1))
    dq = jnp.concatenate([dqn, dqp1, dqp2], axis=1)
    dkv = jnp.concatenate([dkn, dvv], axis=1)
    dqr = matmul("mla_uq_dx", dq, w_uq, tb=True)
    gr["ev_mla_w_uq"] = _uq_unperm(matmul("mla_uq_dw", qr, dq, ta=True))[None]
    dkvr = matmul("mla_ukv_dx", dkv, w_ukv, tb=True)
    gr["ev_mla_w_ukv"] = _ukv_unperm(matmul("mla_ukv_dw", kvr, dkv, ta=True))[None]
    dc_q, gr["ev_mla_q_norm"] = _rows("q_rms_bwd", rms_f, [c_q], [p["ev_mla_q_norm"]], None, TM, fwd=False, cts=[dqr],
                                      wrt_rows=(0,), wrt_params=(0,))
    dc_kv, gr["ev_mla_kv_norm"] = _rows("kv_rms_bwd", rms_f, [c_kv], [p["ev_mla_kv_norm"]], None, TM, fwd=False, cts=[dkvr],
                                        wrt_rows=(0,), wrt_params=(0,))
    dz0 = jnp.concatenate([dc_q, dc_kv, dkp1, dkp2, dq_g, dk_g, dv_g, dr_g, dlr_g,
                           jnp.zeros((t, EV_PAD - EV_IN), F32)], axis=1)
    gr["ev_w_in"] = matmul("ev_in_dw", x2, dz0, ta=True)[:, :EV_IN][None]
    dx2 = dx2 + matmul("ev_in_dx", dz0, w_in0, tb=True)
    gr["ev_gla_w_gate2"], gr["ev_gla_b_gate"] = dw2[None], dbg
    gr["ev_gla_norm_g"], gr["ev_gla_norm_b"] = dng, dnb
    for k in list(gr):
        if isinstance(gr[k], list):
            gr[k] = jnp.stack([g[0] if k.startswith("ln_") else g for g in gr[k]])
    return loss, dx2.reshape(bsz, seq, d), gr


WEIGHTS = ['ev_w_in', 'ev_mla_q_norm', 'ev_mla_w_uq', 'ev_mla_kv_norm', 'ev_mla_w_ukv', 'ev_gla_w_gate2', 'ev_gla_b_gate',
           'ev_gla_norm_g', 'ev_gla_norm_b', 'ev_w_out', 'od_w_in', 'od_rwkv_mu', 'od_rwkv_w0', 'od_rwkv_w_decay2',
           'od_rwkv_a0', 'od_rwkv_w_a2', 'od_rwkv_w_gate2', 'od_rwkv_k_k', 'od_rwkv_k_a', 'od_rwkv_r_k', 'od_rwkv_gn_g',
           'od_rwkv_gn_b', 'od_w_out', 'ln_mix_g', 'ln_mix_b', 'xa_w_q', 'xa_w_k', 'xa_w_v', 'xa_w_o', 'ln_xa_g', 'ln_xa_b',
           'ffn_w_gate', 'ffn_w_up', 'ffn_w_down', 'ln_ffn_g', 'ln_ffn_b']
BIG = {'ev_w_in': -1, 'ev_mla_w_uq': -1, 'ev_mla_w_ukv': -1, 'ev_w_out': -2, 'od_w_in': -1, 'od_w_out': -2,
       'xa_w_q': -2, 'xa_w_k': -2, 'xa_w_v': -2, 'xa_w_o': -2, 'ffn_w_gate': -1, 'ffn_w_up': -1, 'ffn_w_down': -2}
SMALL = ['ev_gla_w_gate2', 'od_rwkv_mu', 'od_rwkv_w0', 'od_rwkv_w_decay2', 'od_rwkv_a0', 'od_rwkv_w_a2', 'od_rwkv_w_gate2',
         'od_rwkv_k_k', 'od_rwkv_k_a', 'od_rwkv_gn_g', 'od_rwkv_gn_b']
REPL = ['ev_mla_q_norm', 'ev_mla_kv_norm', 'ev_gla_b_gate', 'ev_gla_norm_g', 'ev_gla_norm_b', 'od_rwkv_r_k',
        'ln_mix_g', 'ln_mix_b', 'ln_xa_g', 'ln_xa_b', 'ln_ffn_g', 'ln_ffn_b']
PACK_COLS = 128


def _unshard(g, shape, axis):
    axis %= len(shape)
    full = list(shape)
    full[axis] *= N_DEV
    return jnp.moveaxis(g, 0, axis).reshape(full)


def _shard_stack(gfull, axis):
    axis %= gfull.ndim
    shp = list(gfull.shape)
    shp[axis:axis + 1] = [N_DEV, shp[axis] // N_DEV]
    return jnp.moveaxis(gfull.reshape(shp), axis, 0)


def _pack(arrs):
    lead = arrs[0].shape[0]
    flat = jnp.concatenate([a.reshape(lead, -1) for a in arrs], axis=1)
    n = flat.shape[1]
    rows = -(-n // (8 * PACK_COLS)) * 8
    return jnp.pad(flat, ((0, 0), (0, rows * PACK_COLS - n))).reshape(lead, rows, PACK_COLS)


def _unpack(buf, shapes):
    lead = buf.shape[0]
    flat = buf.reshape(lead, -1)
    out, off = [], 0
    for shp in shapes:
        n = math.prod(shp)
        out.append(flat[:, off:off + n].reshape((lead,) + tuple(shp)))
        off += n
    return out


def train_step(x, mem, loss_target, w, m, v):
    p = {}
    for k, ax in BIG.items():
        p[k] = _unshard(all_gather_2level("ag_" + k, w[k].astype(BF16)), w[k].shape, ax)
    small_loc = _pack([w[k][None] for k in SMALL])[0]
    small_all = _unpack(all_gather("ag_small", small_loc), [w[k].shape for k in SMALL])
    for k, g in zip(SMALL, small_all):
        p[k] = _unshard(g, w[k].shape, -1)
    for k in REPL:
        p[k] = w[k]
    loss, dx, gr = device_step(x, mem, loss_target, p)
    loss = lax.psum(loss, ("x", "y", "c"))

    stacks = {}
    for k, ax in BIG.items():
        stacks[k] = reduce_scatter_2level("rs_" + k, _shard_stack(gr[k], ax))
    small_send = _pack([_shard_stack(gr[k], -1) for k in SMALL])
    small_recv = _unpack(reduce_scatter_exchange("rs_small", small_send), [w[k].shape for k in SMALL])
    stacks.update(zip(SMALL, small_recv))
    repl_loc = _pack([gr[k][None] for k in REPL])[0]
    repl_all = _unpack(all_gather("ag_repl_grads", repl_loc), [w[k].shape for k in REPL])
    stacks.update(zip(REPL, repl_all))

    grads, deltas, new_m, new_v = [], [], [], []
    for k in WEIGHTS:
        g, dl, mn, vn = adamw("adamw_" + k, w[k], m[k], v[k], stacks[k])
        grads.append(g), deltas.append(dl), new_m.append(mn), new_v.append(vn)
    return (loss, dx, *grads, *deltas, *new_m, *new_v)


def kernel(x, mem, ev_w_in, ev_mla_q_norm, ev_mla_w_uq, ev_mla_kv_norm, ev_mla_w_ukv, ev_gla_w_gate2, ev_gla_b_gate, ev_gla_norm_g, ev_gla_norm_b, ev_w_out, od_w_in, od_rwkv_mu, od_rwkv_w0, od_rwkv_w_decay2, od_rwkv_a0, od_rwkv_w_a2, od_rwkv_w_gate2, od_rwkv_k_k, od_rwkv_k_a, od_rwkv_r_k, od_rwkv_gn_g, od_rwkv_gn_b, od_w_out, ln_mix_g, ln_mix_b, xa_w_q, xa_w_k, xa_w_v, xa_w_o, ln_xa_g, ln_xa_b, ffn_w_gate, ffn_w_up, ffn_w_down, ln_ffn_g, ln_ffn_b, loss_target, m_ev_w_in, m_ev_mla_q_norm, m_ev_mla_w_uq, m_ev_mla_kv_norm, m_ev_mla_w_ukv, m_ev_gla_w_gate2, m_ev_gla_b_gate, m_ev_gla_norm_g, m_ev_gla_norm_b, m_ev_w_out, m_od_w_in, m_od_rwkv_mu, m_od_rwkv_w0, m_od_rwkv_w_decay2, m_od_rwkv_a0, m_od_rwkv_w_a2, m_od_rwkv_w_gate2, m_od_rwkv_k_k, m_od_rwkv_k_a, m_od_rwkv_r_k, m_od_rwkv_gn_g, m_od_rwkv_gn_b, m_od_w_out, m_ln_mix_g, m_ln_mix_b, m_xa_w_q, m_xa_w_k, m_xa_w_v, m_xa_w_o, m_ln_xa_g, m_ln_xa_b, m_ffn_w_gate, m_ffn_w_up, m_ffn_w_down, m_ln_ffn_g, m_ln_ffn_b, v_ev_w_in, v_ev_mla_q_norm, v_ev_mla_w_uq, v_ev_mla_kv_norm, v_ev_mla_w_ukv, v_ev_gla_w_gate2, v_ev_gla_b_gate, v_ev_gla_norm_g, v_ev_gla_norm_b, v_ev_w_out, v_od_w_in, v_od_rwkv_mu, v_od_rwkv_w0, v_od_rwkv_w_decay2, v_od_rwkv_a0, v_od_rwkv_w_a2, v_od_rwkv_w_gate2, v_od_rwkv_k_k, v_od_rwkv_k_a, v_od_rwkv_r_k, v_od_rwkv_gn_g, v_od_rwkv_gn_b, v_od_w_out, v_ln_mix_g, v_ln_mix_b, v_xa_w_q, v_xa_w_k, v_xa_w_v, v_xa_w_o, v_ln_xa_g, v_ln_xa_b, v_ffn_w_gate, v_ffn_w_up, v_ffn_w_down, v_ln_ffn_g, v_ln_ffn_b):
    given = dict(locals())
    w = {k: given[k] for k in WEIGHTS}
    m = {k: given["m_" + k] for k in WEIGHTS}
    v = {k: given["v_" + k] for k in WEIGHTS}
    return train_step(given["x"], given["mem"], given["loss_target"], w, m, v)
```

```python
import functools
import math

import jax
import jax.numpy as jnp
from jax import lax
from jax.experimental import pallas as pl
from jax.experimental.pallas import tpu as pltpu

F32 = jnp.float32
BF16 = jnp.bfloat16
VMEM_LIMIT = 56 * 1024 * 1024
ROWS_VMEM = 20 * 1024 * 1024

N_DEV = 8
DEPTH = 2
ALPHA = (2.0 * DEPTH) ** 0.25
LN_EPS = 1e-5
RMS_EPS = 1e-6
RWKV_GN_EPS = 64e-5
ADAM_LR, ADAM_B1, ADAM_B2, ADAM_EPS, ADAM_WD, ADAM_STEP = 0.001, 0.9, 0.999, 1e-08, 0.01, 10
NEG_INF = float("-inf")


def _pcall(body, **kw):
    return pl.pallas_call(body, **kw)


def _cparams(**kw):
    return pltpu.CompilerParams(vmem_limit_bytes=VMEM_LIMIT, **kw)


def _dg(a, b, ca, cb, batch):
    nb = 1 if batch else 0
    dims = (((ca + nb,), (cb + nb,)), ((0,), (0,)) if batch else ((), ()))
    return lax.dot_general(a.astype(BF16), b.astype(BF16), dims, preferred_element_type=F32)


@functools.partial(jax.custom_vjp, nondiff_argnums=(2, 3, 4))
def _mm(a, b, ta, tb, batch):
    return _dg(a, b, 0 if ta else 1, 1 if tb else 0, batch)


def _mm_fwd(a, b, ta, tb, batch):
    return _mm(a, b, ta, tb, batch), (a, b)


def _mm_bwd(ta, tb, batch, res, g):
    a, b = res
    if not ta and not tb:
        da, db = _mm(g, b, False, True, batch), _mm(a, g, True, False, batch)
    elif not ta and tb:
        da, db = _mm(g, b, False, False, batch), _mm(g, a, True, False, batch)
    elif ta and not tb:
        da, db = _mm(b, g, False, True, batch), _mm(a, g, False, False, batch)
    else:
        da, db = _mm(b, g, True, True, batch), _mm(g, a, True, True, batch)
    return da.astype(a.dtype), db.astype(b.dtype)


_mm.defvjp(_mm_fwd, _mm_bwd)


def mm(a, b, ta=False, tb=False):
    return _mm(a, b, ta, tb, a.ndim == 3)


def _bs(block, imap):
    return pl.BlockSpec(block, imap)


def _rev_imap(imap, n):
    def r(*idx):
        return imap(*idx[:-1], n - 1 - idx[-1])
    return r


def p_fwd(name, f, grid, ins, outs, carry=None, save_carry=None):
    n_in, n_out = len(ins), len(outs)

    def body(*refs):
        in_refs = refs[:n_in]
        out_refs = refs[n_in:n_in + n_out]
        rest = refs[n_in + n_out:]
        vals = [r[...] for r in in_refs]
        if carry is None:
            res = f(*vals)
        else:
            if save_carry is not None:
                sv_ref, c_ref = rest
            else:
                (c_ref,) = rest

            @pl.when(pl.program_id(len(grid) - 1) == 0)
            def _():
                c_ref[...] = jnp.zeros(c_ref.shape, c_ref.dtype)

            c = c_ref[...]
            if save_carry is not None:
                sv_ref[...] = c
            res = f(c, *vals)
            c_ref[...] = res[0]
            res = res[1:]
        if not isinstance(res, (tuple, list)):
            res = (res,)
        for r, v in zip(out_refs, res):
            r[...] = v.astype(r.dtype)

    out_shape = [jax.ShapeDtypeStruct(s, d) for (s, d, _, _) in outs]
    out_specs = [_bs(b, m) for (_, _, b, m) in outs]
    scratch = []
    if carry is not None:
        if save_carry is not None:
            out_shape.append(jax.ShapeDtypeStruct(save_carry[0], carry[1]))
            out_specs.append(_bs(save_carry[1], save_carry[2]))
        scratch.append(pltpu.VMEM(carry[0], carry[1]))
    return _pcall(
        body, name=name, grid=grid,
        in_specs=[_bs(b, m) for (_, b, m) in ins],
        out_specs=out_specs, out_shape=out_shape, scratch_shapes=scratch,
        compiler_params=_cparams(),
    )(*[a for (a, _, _) in ins])


def p_bwd(name, f, grid, ins, cts, wrt, carry=None, saved=None):
    n_in, n_ct, n_w = len(ins), len(cts), len(wrt)
    rev = carry is not None
    n_last = grid[-1]

    def fix(imap):
        return _rev_imap(imap, n_last) if rev else imap

    def body(*refs):
        in_refs = refs[:n_in]
        ct_refs = refs[n_in:n_in + n_ct]
        k = n_in + n_ct
        if rev:
            sv_ref = refs[k]
            k += 1
        out_refs = refs[k:k + n_w]
        rest = refs[k + n_w:]
        vals = [r[...] for r in in_refs]
        ct_vals = [r[...].astype(F32) for r in ct_refs]
        widx = [w[0] for w in wrt]

        if rev:
            (dc_ref,) = rest

            @pl.when(pl.program_id(len(grid) - 1) == 0)
            def _():
                dc_ref[...] = jnp.zeros(dc_ref.shape, dc_ref.dtype)

            c_in = sv_ref[...]

            def g(c, *dv):
                full = list(vals)
                for i, d in zip(widx, dv):
                    full[i] = d
                return tuple(f(c, *full))

            _, vjp = jax.vjp(g, c_in, *[vals[i] for i in widx])
            grads = vjp((dc_ref[...],) + tuple(ct_vals))
            dc_ref[...] = grads[0]
            grads = grads[1:]
        else:
            def g(*dv):
                full = list(vals)
                for i, d in zip(widx, dv):
                    full[i] = d
                r = f(*full)
                return tuple(r) if isinstance(r, (tuple, list)) else (r,)

            _, vjp = jax.vjp(g, *[vals[i] for i in widx])
            grads = vjp(tuple(ct_vals))

        for w, o_ref, gr in zip(wrt, out_refs, grads):
            acc = w[1]
            if acc is None:
                o_ref[...] = gr.astype(o_ref.dtype)
            else:
                first = None
                for ax in acc:
                    c0 = pl.program_id(ax) == 0
                    first = c0 if first is None else jnp.logical_and(first, c0)

                @pl.when(first)
                def _():
                    o_ref[...] = jnp.zeros(o_ref.shape, o_ref.dtype)

                o_ref[...] += gr.astype(o_ref.dtype)

    in_specs = [_bs(b, fix(m)) for (_, b, m) in ins] + [_bs(b, fix(m)) for (_, b, m) in cts]
    args = [a for (a, _, _) in ins] + [a for (a, _, _) in cts]
    if rev:
        in_specs.append(_bs(saved[1], fix(saved[2])))
        args.append(saved[0])
    out_shape, out_specs = [], []
    for w in wrt:
        a, b, m = ins[w[0]]
        if len(w) > 2:
            m = w[2]
        out_shape.append(jax.ShapeDtypeStruct(a.shape, F32))
        out_specs.append(_bs(b, fix(m)))
    scratch = [pltpu.VMEM(carry[0], carry[1])] if rev else []
    return _pcall(
        body, name=name, grid=grid, in_specs=in_specs, out_specs=out_specs,
        out_shape=out_shape, scratch_shapes=scratch, compiler_params=_cparams(),
    )(*args)


def _rows(name, f, row_ins, params, out_widths, tm, fwd=True, cts=None, wrt_rows=(), wrt_params=()):
    t = row_ins[0].shape[0]
    width = sum(a.shape[1] for a in row_ins)
    width += sum(out_widths) if fwd else sum(c.shape[1] for c in cts) + sum(row_ins[i].shape[1] for i in wrt_rows)
    tm = min(tm, t)
    while tm > 8 and 2 * 4 * tm * width > ROWS_VMEM:
        tm //= 2
    rmap = lambda i: (i, 0)
    pmap = lambda i: (0, 0)
    ins = [(a, (tm, a.shape[1]), rmap) for a in row_ins] + [(p, p.shape, pmap) for p in params]
    if fwd:
        outs = [((t, w), F32, (tm, w), rmap) for w in out_widths]
        return p_fwd(name, f, (t // tm,), ins, outs)
    ct_specs = [(c, (tm, c.shape[1]), rmap) for c in cts]
    wrt = [(i, None) for i in wrt_rows] + [(len(row_ins) + i, (0,)) for i in wrt_params]
    return p_bwd(name, f, (t // tm,), ins, ct_specs, wrt)


def _pick(n, cands):
    for c in cands:
        if n % c == 0:
            return c
    return n


def _wide(n, cap=1664):
    best = None
    for w in range(128, min(n, cap) + 1, 128):
        if n % w == 0:
            best = w
    return best or n


def matmul(name, a, b, ta=False, tb=False, out_dtype=F32, add=None):
    m = a.shape[1] if ta else a.shape[0]
    k = a.shape[0] if ta else a.shape[1]
    n = b.shape[0] if tb else b.shape[1]
    assert (b.shape[1] if tb else b.shape[0]) == k, (a.shape, b.shape, ta, tb)
    bm, bn, bk = _pick(m, (1024, 512, 256, 128)), _wide(n), _pick(k, (512, 256, 128))
    nk = k // bk

    def body(a_ref, b_ref, *rest):
        o_ref, acc_ref = rest[-2:]

        @pl.when(pl.program_id(2) == 0)
        def _():
            acc_ref[...] = jnp.zeros(acc_ref.shape, F32) if add is None else rest[0][...].astype(F32)

        acc_ref[...] += mm(a_ref[...], b_ref[...], ta, tb)

        @pl.when(pl.program_id(2) == nk - 1)
        def _():
            o_ref[...] = acc_ref[...].astype(o_ref.dtype)

    a_spec = _bs((bk, bm), lambda i, j, l: (l, i)) if ta else _bs((bm, bk), lambda i, j, l: (i, l))
    b_spec = _bs((bn, bk), lambda i, j, l: (j, l)) if tb else _bs((bk, bn), lambda i, j, l: (l, j))
    o_spec = _bs((bm, bn), lambda i, j, l: (i, j))
    return _pcall(
        body, name=name, grid=(m // bm, n // bn, nk),
        in_specs=[a_spec, b_spec] + ([] if add is None else [o_spec]), out_specs=o_spec,
        out_shape=jax.ShapeDtypeStruct((m, n), out_dtype),
        scratch_shapes=[pltpu.VMEM((bm, bn), F32)],
        compiler_params=_cparams(dimension_semantics=("parallel", "parallel", "arbitrary")),
    )(*((a, b) if add is None else (a, b, add)))


def _to_heads(x, b, h):
    t, w = x.shape
    return x.reshape(b, t // b, h, w // h).transpose(0, 2, 1, 3)


def _from_heads(x):
    b, h, s, d = x.shape
    return x.transpose(0, 2, 1, 3).reshape(b * s, h * d)


RW_STEPS = 8
RW_G = 16


def rwkv_group(st, w8, kk8, ka8, kh8, r8, vc):
    ys = []
    for t in range(RW_STEPS):
        row = lambda x: x[:, t:t + 1, :]
        sa = jnp.sum(st * row(kk8), axis=2, keepdims=True)
        st = st * row(w8) - sa * row(ka8) + vc[:, :, t:t + 1] * row(kh8)
        ys.append(jnp.sum(st * row(r8), axis=2, keepdims=True))
    return st, jnp.concatenate(ys, axis=2)


def rwkv_prehead_f(kkraw, a):
    nrm = jnp.sqrt(jnp.sum(kkraw * kkraw, axis=-1, keepdims=True))
    kk = kkraw / jnp.maximum(nrm, 1e-12)
    return kk, kk * a


def rwkv_pre_f(kd, w_lr, a_lr, g_lr, w0, wd2, a0, wa2, wg2, k_k, k_a):
    wpre = w0 + mm(jnp.tanh(w_lr), wd2)
    w = -jax.nn.softplus(-wpre) - 0.5
    decay = jnp.exp(-jnp.exp(w))
    a = jax.nn.sigmoid(a0 + mm(a_lr, wa2))
    g = mm(jax.nn.sigmoid(g_lr), wg2)
    kkraw = kd * k_k
    kh = kd * (1.0 + (a - 1.0) * k_a)
    return decay, a, g, kkraw, kh


def rwkv_post_f(y, r, kh, v, g, gn_g, gn_b, r_k):
    mu = jnp.mean(y, axis=-1, keepdims=True)
    var = jnp.mean(jnp.square(y - mu), axis=-1, keepdims=True)
    yn = (y - mu) * lax.rsqrt(var + RWKV_GN_EPS) * gn_g + gn_b
    bonus = jnp.sum(r * kh * r_k, axis=-1, keepdims=True) * v
    return (yn + bonus) * g


def _to_cols(xh):
    b, h, s, d = xh.shape
    return xh.reshape(b, h, s // RW_STEPS, RW_STEPS, d).transpose(0, 1, 2, 4, 3)


def _from_cols(xc):
    b, h, n, d, k = xc.shape
    return xc.transpose(0, 1, 2, 4, 3).reshape(b, h, n * k, d)


def rwkv_block(bsz, r, kd, vd, w_lr, a_lr, g_lr, prm, cts=None):
    t = r.shape[0]
    hh, n = 16, 64
    tm = 256
    pre_rows = [kd, w_lr, a_lr, g_lr]
    pre_prm = [prm[k] for k in ("w0", "wd2", "a0", "wa2", "wg2", "k_k", "k_a")]
    decay, a, g, kkraw, kh = _rows("rwkv_pre", rwkv_pre_f, pre_rows, pre_prm, [1024] * 5, tm)
    heads = lambda x: _to_heads(x, bsz, hh)
    rh, khh, vh, gh, dech, kkrawh, ah = (heads(x) for x in (r, kh, vd, g, decay, kkraw, a))
    s = t // bsz
    ng = s // RW_STEPS
    ts = 512
    hb = (None, None, ts, n)
    hm = lambda h, b, i: (b, h, i, 0)
    pgrid = (hh, bsz, s // ts)
    ph_ins = [(kkrawh, hb, hm), (ah, hb, hm)]
    kkh, kah = p_fwd("rwkv_prehead", rwkv_prehead_f, pgrid, ph_ins, [((bsz, hh, s, n), F32, hb, hm)] * 2)
    grp = lambda x: x.reshape(bsz, hh, ng, RW_STEPS, n)
    cb, rb, sb = (None, RW_G, None, n, RW_STEPS), (None, RW_G, None, RW_STEPS, n), (None, RW_G, None, n, n)
    cm = lambda b, h, i: (b, h, i, 0, 0)
    sc_ins = [(grp(x), rb, cm) for x in (dech, kkh, kah, khh, rh)] + [(_to_cols(vh), cb, cm)]
    grid = (bsz, hh // RW_G, ng)
    yc, sv = p_fwd("rwkv_scan", rwkv_group, grid, sc_ins, [((bsz, hh, ng, n, RW_STEPS), F32, cb, cm)],
                   carry=((RW_G, n, n), F32), save_carry=((bsz, hh, ng, n, n), sb, cm))
    yh = _from_cols(yc)
    pb = (None, 1, n)
    pm = lambda h, b, i: (h, 0, 0)
    gn_g, gn_b, r_k = (prm[k].reshape(hh, 1, n) for k in ("gn_g", "gn_b", "r_k"))
    post_ins = [(x, hb, hm) for x in (yh, rh, khh, vh, gh)] + [(p, pb, pm) for p in (gn_g, gn_b, r_k)]
    if cts is None:
        (oh,) = p_fwd("rwkv_post", rwkv_post_f, pgrid, post_ins, [((bsz, hh, s, n), F32, hb, hm)])
        return _from_heads(oh)
    doh = _to_heads(cts, bsz, hh)
    dyh, drh1, dkhh1, dvh1, dgh, dgn_g, dgn_b, dr_k = p_bwd(
        "rwkv_post_bwd", rwkv_post_f, pgrid, post_ins, [(doh, hb, hm)],
        [(i, None) for i in range(5)] + [(5 + i, (1, 2)) for i in range(3)])
    drows_v = p_bwd("rwkv_scan_bwd", rwkv_group, grid, sc_ins, [(_to_cols(dyh), cb, cm)], [(i, None) for i in range(6)],
                    carry=((RW_G, n, n), F32), saved=(sv, sb, cm))
    ddech, dkkh, dkah, dkhh2, drh2 = (x.reshape(bsz, hh, s, n) for x in drows_v[:5])
    dkkrawh, dah = p_bwd("rwkv_prehead_bwd", rwkv_prehead_f, pgrid, ph_ins, [(dkkh, hb, hm), (dkah, hb, hm)],
                         [(0, None), (1, None)])
    ddecay, dkkraw, da = _from_heads(ddech), _from_heads(dkkrawh), _from_heads(dah)
    dv = _from_heads(_from_cols(drows_v[5]) + dvh1)
    dr = _from_heads(drh2 + drh1)
    dkh = _from_heads(dkhh2 + dkhh1)
    dg = _from_heads(dgh)
    res = _rows("rwkv_pre_bwd", rwkv_pre_f, pre_rows, pre_prm, None, tm, fwd=False,
                cts=[ddecay, da, dg, dkkraw, dkh], wrt_rows=(0, 1, 2, 3), wrt_params=tuple(range(7)))
    dkd, dw_lr, da_lr, dg_lr = res[:4]
    dprm = dict(zip(("w0", "wd2", "a0", "wa2", "wg2", "k_k", "k_a"), res[4:]))
    dprm.update(gn_g=dgn_g.reshape(1, -1), gn_b=dgn_b.reshape(1, -1), r_k=dr_k.reshape(hh, n))
    return (dr, dkd, dv, dw_lr, da_lr, dg_lr), dprm


GLA_C, GLA_DK, GLA_DV, GLA_H, GLA_TAU = 64, 128, 256, 4, 16.0


def gla_chunk_f(st, q, k, v, lr, r, w2, bg, ng, nb):
    la = jax.nn.log_sigmoid(mm(lr, w2) + bg) / GLA_TAU
    ri = lax.broadcasted_iota(jnp.int32, (GLA_C, GLA_C), 0)
    ci = lax.broadcasted_iota(jnp.int32, (GLA_C, GLA_C), 1)
    causal = ci <= ri
    b = jnp.dot(causal.astype(F32), la, precision=lax.Precision.HIGHEST, preferred_element_type=F32)
    b_last = jnp.sum(la, axis=0, keepdims=True)
    q_dec = (q * (GLA_DK ** -0.5)) * jnp.exp(b)
    k_inv = k * jnp.exp(-b)
    k_end = k * jnp.exp(b_last - b)
    att = jnp.where(causal, mm(q_dec, k_inv, tb=True), 0.0)
    o = mm(att, v) + mm(q_dec, st, tb=True)
    st_new = st * jnp.exp(b_last) + mm(v, k_end, ta=True)
    mu = jnp.mean(o, axis=-1, keepdims=True)
    var = jnp.mean(jnp.square(o - mu), axis=-1, keepdims=True)
    on = (o - mu) * lax.rsqrt(var + LN_EPS) * ng + nb
    return st_new, on * jax.nn.silu(r)


def gla_block(bsz, q, k, v, r, lr, w2, bg, ng, nb, cts=None):
    t = q.shape[0]
    nc = t // bsz // GLA_C
    grid = (GLA_H, bsz, nc)
    rm = lambda h, b, c: (b * nc + c, h)
    ins = [(q, (GLA_C, GLA_DK), rm), (k, (GLA_C, GLA_DK), rm), (v, (GLA_C, GLA_DV), rm),
           (jnp.broadcast_to(lr[None], (GLA_H,) + lr.shape), (None, GLA_C, lr.shape[1]), lambda h, b, c: (h, b * nc + c, 0)),
           (r, (GLA_C, GLA_DV), rm),
           (w2, (w2.shape[0], GLA_DK), lambda h, b, c: (0, h)), (bg, (1, GLA_DK), lambda h, b, c: (0, h)),
           (ng, (1, GLA_DV), lambda h, b, c: (0, 0)), (nb, (1, GLA_DV), lambda h, b, c: (0, 0))]
    ob = (GLA_C, GLA_DV)
    sshape, sblock = (GLA_H, bsz, nc, GLA_DV, GLA_DK), (None, None, None, GLA_DV, GLA_DK)
    sm = lambda h, b, c: (h, b, c, 0, 0)
    carry = ((GLA_DV, GLA_DK), F32)
    if cts is None:
        out, sv = p_fwd("gla_scan", gla_chunk_f, grid, ins, [((t, GLA_H * GLA_DV), F32, ob, rm)],
                        carry=carry, save_carry=(sshape, sblock, sm))
        return out, sv
    dout, sv = cts
    return p_bwd("gla_scan_bwd", gla_chunk_f, grid, ins, [(dout, ob, rm)],
                 [(0, None), (1, None), (2, None), (3, None), (4, None), (5, (1, 2)), (6, (1, 2)), (7, (0, 1, 2)), (8, (0, 1, 2))],
                 carry=carry, saved=(sv, sblock, sm))


def _softmax_rows(sc):
    m = lax.stop_gradient(jnp.max(sc, axis=-1, keepdims=True))
    e = jnp.exp(sc - m)
    return e / jnp.sum(e, axis=-1, keepdims=True)


def mla_attn_f(qn, r1, r2, kn, kr1, kr2, v):
    tq, s = qn.shape[0], kn.shape[0]
    sc = (mm(qn, kn, tb=True) + mm(r1, kr1, tb=True) + mm(r2, kr2, tb=True)) * (192.0 ** -0.5)
    qpos = pl.program_id(2) * tq + lax.broadcasted_iota(jnp.int32, (tq, s), 0)
    kpos = lax.broadcasted_iota(jnp.int32, (tq, s), 1)
    sc = jnp.where(kpos <= qpos, sc, NEG_INF)
    return mm(_softmax_rows(sc), v)


def mla_attn(bsz, qn, r1, r2, kn, kr1, kr2, v, cts=None, tq=256):
    t = qn.shape[0]
    s = t // bsz
    nq = s // tq
    hh = 8
    grid = (bsz, hh, nq)
    qm = lambda b, h, i: (b * nq + i, h)
    km_ = lambda b, h, i: (b, h)
    ins = [(qn, (tq, 128), qm),
           (r1, (None, None, tq, 32), lambda b, h, i: (b, h, i, 0)), (r2, (None, None, tq, 32), lambda b, h, i: (b, h, i, 0)),
           (kn, (s, 128), km_), (kr1, (s, 32), lambda b, h, i: (b, 0)), (kr2, (s, 32), lambda b, h, i: (b, 0)),
           (v, (s, 128), km_)]
    if cts is None:
        return p_fwd("mla_attn", mla_attn_f, grid, ins, [((t, 1024), F32, (tq, 128), qm)])[0]
    return p_bwd("mla_attn_bwd", mla_attn_f, grid, ins, [(cts, (tq, 128), qm)],
                 [(0, None), (1, None), (2, None), (3, (2,)), (4, (1, 2)), (5, (1, 2)), (6, (2,))])


def xattn_f(q, k, v):
    sc = mm(q, k, tb=True) * (512.0 ** -0.5)
    return mm(_softmax_rows(sc), v)


def xattn(bsz, q, k, v, cts=None, tq=512):
    t = q.shape[0]
    nq = t // bsz // tq
    mlen = k.shape[0] // bsz
    grid = (bsz, 4, nq)
    qm = lambda b, h, i: (b * nq + i, h)
    km_ = lambda b, h, i: (b, h)
    ins = [(q, (tq, 512), qm), (k, (mlen, 512), km_), (v, (mlen, 512), km_)]
    if cts is None:
        return p_fwd("xattn", xattn_f, grid, ins, [((t, 2048), F32, (tq, 512), qm)])[0]
    return p_bwd("xattn_bwd", xattn_f, grid, ins, [(cts, (tq, 512), qm)], [(0, None), (1, (2,)), (2, (2,))])


DIL_SPAN = 128
DIL_BRANCHES = ((128, 1), (512, 4), (2048, 16))


def dil_attn_f(q1, q2, k1c, k2c, vc, k1p, k2p, vp):
    gb, sp = q1.shape[0], DIL_SPAN
    scale = 128.0 ** -0.5
    sc_c = (mm(q1, k1c, tb=True) + mm(q2, k2c, tb=True)) * scale
    sc_p = (mm(q1, k1p, tb=True) + mm(q2, k2p, tb=True)) * scale
    ql = lax.broadcasted_iota(jnp.int32, (gb, sp, sp), 1)
    kl = lax.broadcasted_iota(jnp.int32, (gb, sp, sp), 2)
    has_prev = pl.program_id(1) > 0
    sc_c = jnp.where(kl <= ql, sc_c, NEG_INF)
    sc_p = jnp.where(jnp.logical_and(kl >= ql, has_prev), sc_p, NEG_INF)
    m = lax.stop_gradient(jnp.maximum(jnp.max(sc_c, axis=-1, keepdims=True), jnp.max(sc_p, axis=-1, keepdims=True)))
    e_c, e_p = jnp.exp(sc_c - m), jnp.exp(sc_p - m)
    den = jnp.sum(e_c, axis=-1, keepdims=True) + jnp.sum(e_p, axis=-1, keepdims=True)
    o = mm(e_c / den, vc) + mm(e_p / den, vp)
    return o, m + jnp.log(den)


def dil_branch(q1, q2, k1, k2, v, cts=None, gb=8):
    g, l, _ = q1.shape
    nb = l // DIL_SPAN
    grid = (g // gb, nb)
    cm = lambda i, n: (i, n, 0)
    pm = lambda i, n: (i, jnp.maximum(n - 1, 0), 0)
    b64, b128, b1 = (gb, DIL_SPAN, 64), (gb, DIL_SPAN, 128), (gb, DIL_SPAN, 1)
    ins = [(q1, b64, cm), (q2, b64, cm), (k1, b64, cm), (k2, b64, cm), (v, b128, cm),
           (k1, b64, pm), (k2, b64, pm), (v, b128, pm)]
    if cts is None:
        return p_fwd("dil_attn", dil_attn_f, grid, ins, [((g, l, 128), F32, b128, cm), ((g, l, 1), F32, b1, cm)])
    do, dlse = cts
    dq1, dq2, dk1c, dk2c, dvc, dk1p, dk2p, dvp = p_bwd(
        "dil_attn_bwd", dil_attn_f, grid, ins, [(do, b128, cm), (dlse, b1, cm)],
        [(i, None) for i in range(5)] + [(i, None, cm) for i in (5, 6, 7)])

    def fold(dc, dp):
        return dc + jnp.pad(dp[:, DIL_SPAN:], ((0, 0), (0, DIL_SPAN), (0, 0)))

    return dq1, dq2, fold(dk1c, dk1p), fold(dk2c, dk2p), fold(dvc, dvp)


def dil_mix_f(o1, o2, o3, l1, l2, l3):
    m = lax.stop_gradient(jnp.maximum(jnp.maximum(l1, l2), l3))
    e1, e2, e3 = jnp.exp(l1 - m), jnp.exp(l2 - m), jnp.exp(l3 - m)
    den = e1 + e2 + e3
    return (e1 / den) * o1 + (e2 / den) * o2 + (e3 / den) * o3


def _to_res(xh, dil):
    b, h, s, d = xh.shape
    return xh.reshape(b, h, s // dil, dil, d).transpose(0, 1, 3, 2, 4).reshape(b * h * dil, s // dil, d)


def _from_res(xr, b, h, dil):
    g, l, d = xr.shape
    return xr.reshape(b, h, dil, l, d).transpose(0, 1, 3, 2, 4).reshape(b, h, l * dil, d)


def dil_block(bsz, q1, q2, k1, k2, v, cts=None):
    hh = 8
    heads = [_to_heads(x, bsz, hh) for x in (q1, q2, k1, k2, v)]
    s = heads[0].shape[2]
    outs, res_in = [], []
    for window, dil in DIL_BRANCHES:
        assert window // dil == DIL_SPAN and (s // dil) % DIL_SPAN == 0
        rin = [_to_res(x, dil) for x in heads]
        o, lse = dil_branch(*rin)
        res_in.append(rin)
        outs.append((_from_res(o, bsz, hh, dil), _from_res(lse, bsz, hh, dil)))
    tq = 512
    ob, lb = (None, None, tq, 128), (None, None, tq, 1)
    hm = lambda b, h, i: (b, h, i, 0)
    mix_ins = [(o, ob, hm) for (o, _) in outs] + [(l, lb, hm) for (_, l) in outs]
    grid = (bsz, hh, s // tq)
    if cts is None:
        (mix,) = p_fwd("dil_mix", dil_mix_f, grid, mix_ins, [((bsz, hh, s, 128), F32, ob, hm)])
        return _from_heads(mix)
    dmix = _to_heads(cts, bsz, hh)
    dml = p_bwd("dil_mix_bwd", dil_mix_f, grid, mix_ins, [(dmix, ob, hm)], [(i, None) for i in range(6)])
    tot = None
    for j, (window, dil) in enumerate(DIL_BRANCHES):
        do, dl = _to_res(dml[j], dil), _to_res(dml[3 + j], dil)
        gr = dil_branch(*res_in[j], cts=(do, dl))
        gr = [_from_res(x, bsz, hh, dil) for x in gr]
        tot = gr if tot is None else [a + b for a, b in zip(tot, gr)]
    return tuple(_from_heads(x) for x in tot)


_ANY = pl.BlockSpec(memory_space=pl.ANY)


def _me_and_peers():
    x, y, c = lax.axis_index("x"), lax.axis_index("y"), lax.axis_index("c")
    me = 4 * x + 2 * y + c
    peers = []
    for k in range(1, N_DEV):
        px = 1 - x if k & 4 else x
        py = 1 - y if k & 2 else y
        pc = 1 - c if k & 1 else c
        peers.append(((px, py, pc), 4 * px + 2 * py + pc))
    return me, peers


def _exchange(name, x, scatter):
    shape = x.shape[1:] if scatter else x.shape

    def body(x_ref, out_ref, send_sems, recv_sems, local_sem):
        me, peers = _me_and_peers()
        src_me = x_ref.at[me] if scatter else x_ref
        local = pltpu.make_async_copy(src_me, out_ref.at[me], local_sem)
        local.start()
        sends = []
        for k, (dev, idx) in enumerate(peers):
            cp = pltpu.make_async_remote_copy(
                src_ref=x_ref.at[idx] if scatter else x_ref, dst_ref=out_ref.at[me],
                send_sem=send_sems.at[k], recv_sem=recv_sems.at[k],
                device_id=dev, device_id_type=pl.DeviceIdType.MESH)
            cp.start()
            sends.append(cp)
        for k, (dev, idx) in enumerate(peers):
            pltpu.make_async_remote_copy(
                src_ref=src_me, dst_ref=out_ref.at[idx], send_sem=send_sems.at[k], recv_sem=recv_sems.at[k],
                device_id=dev, device_id_type=pl.DeviceIdType.MESH).wait_recv()
        for cp in sends:
            cp.wait_send()
        local.wait()

    return _pcall(
        body, name=name, in_specs=[_ANY], out_specs=_ANY,
        out_shape=jax.ShapeDtypeStruct((N_DEV,) + tuple(shape), x.dtype),
        scratch_shapes=[pltpu.SemaphoreType.DMA((N_DEV - 1,)), pltpu.SemaphoreType.DMA((N_DEV - 1,)),
                        pltpu.SemaphoreType.DMA],
        compiler_params=pltpu.CompilerParams(has_side_effects=True),
    )(x)


def all_gather(name, x):
    return _exchange(name, x, False)


def reduce_scatter_exchange(name, x):
    return _exchange(name, x, True)


def all_gather_2level(name, x):
    def body(x_ref, out_ref, send_sems, recv_sems, local_sem):
        x_, y_, c_ = lax.axis_index("x"), lax.axis_index("y"), lax.axis_index("c")
        sibling = (x_, y_, 1 - c_)
        chips = [(1 - x_, y_), (x_, 1 - y_), (1 - x_, 1 - y_)]

        def slot(px, py, pc):
            return out_ref.at[4 * px + 2 * py + pc]

        def copy(k, block, to, src=None):
            return pltpu.make_async_remote_copy(
                src_ref=slot(*block) if src is None else src, dst_ref=slot(*block),
                send_sem=send_sems.at[k], recv_sem=recv_sems.at[k], device_id=to, device_id_type=pl.DeviceIdType.MESH)

        me = (x_, y_, c_)
        mine = pltpu.make_async_copy(x_ref, slot(*me), local_sem)
        mine.start()
        first = [copy(0, me, sibling, src=x_ref)]
        first += [copy(1 + j, me, (*chip, c_), src=x_ref) for j, chip in enumerate(chips)]
        for cp in first:
            cp.start()
        passed = [copy(4 + j, (*chip, c_), sibling) for j, chip in enumerate(chips)]
        for j, chip in enumerate(chips):
            copy(1 + j, (*chip, c_), me).wait_recv()
            passed[j].start()
        copy(0, sibling, me).wait_recv()
        for j, chip in enumerate(chips):
            copy(4 + j, (*chip, 1 - c_), me).wait_recv()
        for cp in first + passed:
            cp.wait_send()
        mine.wait()

    return _pcall(
        body, name=name, in_specs=[_ANY], out_specs=_ANY,
        out_shape=jax.ShapeDtypeStruct((N_DEV,) + tuple(x.shape), x.dtype),
        scratch_shapes=[pltpu.SemaphoreType.DMA((7,)), pltpu.SemaphoreType.DMA((7,)), pltpu.SemaphoreType.DMA],
        compiler_params=pltpu.CompilerParams(has_side_effects=True),
    )(x)


def sibling_swap(name, x):
    shape = (4,) + tuple(x.shape[1:])

    def body(x_ref, mine_ref, theirs_ref, send_sems, recv_sems, local_sems):
        x_, y_, c_ = lax.axis_index("x"), lax.axis_index("y"), lax.axis_index("c")
        locals_, sends = [], []
        for j in range(4):
            lc = pltpu.make_async_copy(x_ref.at[2 * j + c_], mine_ref.at[j], local_sems.at[j])
            lc.start()
            locals_.append(lc)
            cp = pltpu.make_async_remote_copy(
                src_ref=x_ref.at[2 * j + 1 - c_], dst_ref=theirs_ref.at[j], send_sem=send_sems.at[j], recv_sem=recv_sems.at[j],
                device_id=(x_, y_, 1 - c_), device_id_type=pl.DeviceIdType.MESH)
            cp.start()
            sends.append(cp)
        for cp in sends:
            cp.wait()
        for lc in locals_:
            lc.wait()

    return _pcall(
        body, name=name, in_specs=[_ANY], out_specs=[_ANY, _ANY],
        out_shape=[jax.ShapeDtypeStruct(shape, x.dtype)] * 2,
        scratch_shapes=[pltpu.SemaphoreType.DMA((4,)), pltpu.SemaphoreType.DMA((4,)), pltpu.SemaphoreType.DMA((4,))],
        compiler_params=pltpu.CompilerParams(has_side_effects=True),
    )(x)


def pair_add(name, a, b, out_dtype):
    shp = a.shape
    c = shp[-1]
    r = math.prod(shp[:-1])
    br = r
    for cand in (2048, 1024, 512, 256, 128, 64, 32, 16, 8):
        if r % cand == 0 and cand * c <= 512 * 1024:
            br = cand
            break

    def body(a_ref, b_ref, o_ref):
        o_ref[...] = (a_ref[...] + b_ref[...]).astype(o_ref.dtype)

    spec = _bs((br, c), lambda i: (i, 0))
    out = _pcall(body, name=name, grid=(r // br,), in_specs=[spec, spec], out_specs=spec,
                 out_shape=jax.ShapeDtypeStruct((r, c), out_dtype), compiler_params=_cparams())(a.reshape(r, c), b.reshape(r, c))
    return out.reshape(shp)


def chip_exchange(name, p):
    def body(p_ref, out_ref, send_sems, recv_sems, local_sem):
        x_, y_, c_ = lax.axis_index("x"), lax.axis_index("y"), lax.axis_index("c")
        my_chip = 2 * x_ + y_
        chips = [(1 - x_, y_), (x_, 1 - y_), (1 - x_, 1 - y_)]
        local = pltpu.make_async_copy(p_ref.at[my_chip], out_ref.at[my_chip], local_sem)
        local.start()
        sends = []
        for k, (px, py) in enumerate(chips):
            cp = pltpu.make_async_remote_copy(
                src_ref=p_ref.at[2 * px + py], dst_ref=out_ref.at[my_chip], send_sem=send_sems.at[k], recv_sem=recv_sems.at[k],
                device_id=(px, py, c_), device_id_type=pl.DeviceIdType.MESH)
            cp.start()
            sends.append(cp)
        for k, (px, py) in enumerate(chips):
            pltpu.make_async_remote_copy(
                src_ref=p_ref.at[my_chip], dst_ref=out_ref.at[2 * px + py], send_sem=send_sems.at[k], recv_sem=recv_sems.at[k],
                device_id=(px, py, c_), device_id_type=pl.DeviceIdType.MESH).wait_recv()
        for cp in sends:
            cp.wait_send()
        local.wait()

    return _pcall(
        body, name=name, in_specs=[_ANY], out_specs=_ANY,
        out_shape=jax.ShapeDtypeStruct(p.shape, p.dtype),
        scratch_shapes=[pltpu.SemaphoreType.DMA((3,)), pltpu.SemaphoreType.DMA((3,)), pltpu.SemaphoreType.DMA],
        compiler_params=pltpu.CompilerParams(has_side_effects=True),
    )(p)


def reduce_scatter_2level(name, stack):
    mine, theirs = sibling_swap(name + "_d2d", stack)
    part = pair_add(name + "_add", mine, theirs, BF16)
    return chip_exchange(name + "_ici", part)


def ln_res_f(h, r, g, b):
    x = ALPHA * h + r
    mu = jnp.mean(x, axis=-1, keepdims=True)
    var = jnp.mean(jnp.square(x - mu), axis=-1, keepdims=True)
    return (x - mu) * lax.rsqrt(var + LN_EPS) * g + b


def rms_f(x, g):
    return x * lax.rsqrt(jnp.mean(x * x, axis=-1, keepdims=True) + RMS_EPS) * g


def rope_f(x1, x2, c, s):
    return x1 * c - x2 * s, x1 * s + x2 * c


def swiglu_f(g, u):
    return jax.nn.silu(g) * u


def tshift_f(d, dprev, mu):
    return d + (dprev - d) * mu


def loss_f(y, tgt):
    e = y - tgt
    return e / y.shape[-1], 0.5 * jnp.mean(e * e, axis=-1, keepdims=True)


def adamw(name, w, m, v, gstack):
    shp = w.shape
    c = shp[-1]
    r = math.prod(shp[:-1])
    br = r
    for cand in (512, 256, 128, 64, 32, 16, 8):
        if r % cand == 0 and cand * c <= 128 * 1024:
            br = cand
            break
    k = gstack.shape[0]

    def body(w_ref, m_ref, v_ref, g_ref, go_ref, d_ref, mo_ref, vo_ref):
        g = g_ref[0].astype(F32)
        for j in range(1, k):
            g = g + g_ref[j].astype(F32)
        m_new = ADAM_B1 * m_ref[...] + (1.0 - ADAM_B1) * g
        v_new = ADAM_B2 * v_ref[...] + (1.0 - ADAM_B2) * jnp.square(g)
        m_hat = m_new / (1.0 - ADAM_B1 ** ADAM_STEP)
        v_hat = v_new / (1.0 - ADAM_B2 ** ADAM_STEP)
        go_ref[...] = g
        d_ref[...] = -ADAM_LR * (m_hat / (jnp.sqrt(v_hat) + ADAM_EPS) + ADAM_WD * w_ref[...])
        mo_ref[...] = m_new
        vo_ref[...] = v_new

    spec = _bs((br, c), lambda i: (i, 0))
    outs = _pcall(
        body, name=name, grid=(r // br,),
        in_specs=[spec, spec, spec, _bs((k, br, c), lambda i: (0, i, 0))],
        out_specs=[spec] * 4, out_shape=[jax.ShapeDtypeStruct((r, c), F32)] * 4,
        compiler_params=_cparams(),
    )(w.reshape(r, c), m.reshape(r, c), v.reshape(r, c), gstack.reshape(k, r, c))
    return tuple(o.reshape(shp) for o in outs)


EV_W = (512, 256, 64, 512, 512, 1024, 1024, 16)
EV_IN, EV_PAD = 3920, 4096
OD_IN, OD_PAD = 6592, 6656
TM = 256


def _offsets(widths):
    offs, acc = [], 0
    for w in widths:
        offs.append((acc, acc + w))
        acc += w
    return offs


def _rope_tables(seq, dim):
    inv = 10000.0 ** (-jnp.arange(0, dim, 2, dtype=F32) / dim)
    ang = jnp.arange(seq, dtype=F32)[:, None] * inv[None, :]
    return jnp.cos(ang), jnp.sin(ang)


def _halves(x, nh):
    t, w = x.shape
    x3 = x.reshape(t, nh, w // nh)
    hd = w // nh // 2
    return x3[:, :, :hd].reshape(t, nh * hd), x3[:, :, hd:].reshape(t, nh * hd)


def _unhalves(x1, x2, nh):
    t = x1.shape[0]
    return jnp.concatenate([x1.reshape(t, nh, -1), x2.reshape(t, nh, -1)], axis=2).reshape(t, -1)


def _ln(name, h, r, g, b, cts=None):
    if cts is None:
        return _rows(name, ln_res_f, [h, r], [g, b], [h.shape[1]], TM)[0]
    return _rows(name + "_bwd", ln_res_f, [h, r], [g, b], None, TM, fwd=False, cts=[cts], wrt_rows=(0, 1), wrt_params=(0, 1))


def _tail_fwd(l, h, mem2, bsz, p):
    qx = matmul(f"xa_q{l}", h, p["xa_w_q"][l])
    kx = matmul(f"xa_k{l}", mem2, p["xa_w_k"][l])
    vx = matmul(f"xa_v{l}", mem2, p["xa_w_v"][l])
    ox = xattn(bsz, qx, kx, vx)
    xa = matmul(f"xa_o{l}", ox, p["xa_w_o"][l])
    h2 = _ln(f"ln_xa{l}", h, xa, p["ln_xa_g"][l:l + 1], p["ln_xa_b"][l:l + 1])
    gg = matmul(f"ffn_g{l}", h2, p["ffn_w_gate"][l])
    uu = matmul(f"ffn_u{l}", h2, p["ffn_w_up"][l])
    act = _rows(f"swiglu{l}", swiglu_f, [gg, uu], [], [gg.shape[1]], TM)[0]
    ff = matmul(f"ffn_d{l}", act, p["ffn_w_down"][l])
    h3 = _ln(f"ln_ffn{l}", h2, ff, p["ln_ffn_g"][l:l + 1], p["ln_ffn_b"][l:l + 1])
    return h3, (h, qx, kx, vx, ox, xa, h2, gg, uu, act, ff)


def _tail_bwd(l, dh3, saved, mem2, bsz, p, gr):
    h, qx, kx, vx, ox, xa, h2, gg, uu, act, ff = saved
    dh2, dff, gr["ln_ffn_g"][l], gr["ln_ffn_b"][l] = _ln(f"ln_ffn{l}", h2, ff, p["ln_ffn_g"][l:l + 1], p["ln_ffn_b"][l:l + 1], cts=dh3)
    dact = matmul(f"ffn_d_dx{l}", dff, p["ffn_w_down"][l], tb=True)
    gr["ffn_w_down"][l] = matmul(f"ffn_d_dw{l}", act, dff, ta=True)
    dgg, duu = _rows(f"swiglu_bwd{l}", swiglu_f, [gg, uu], [], None, TM, fwd=False, cts=[dact], wrt_rows=(0, 1))
    gr["ffn_w_gate"][l] = matmul(f"ffn_g_dw{l}", h2, dgg, ta=True)
    gr["ffn_w_up"][l] = matmul(f"ffn_u_dw{l}", h2, duu, ta=True)
    dh2 = matmul(f"ffn_g_dx{l}", dgg, p["ffn_w_gate"][l], tb=True, add=dh2)
    dh2 = matmul(f"ffn_u_dx{l}", duu, p["ffn_w_up"][l], tb=True, add=dh2)
    dh, dxa, gr["ln_xa_g"][l], gr["ln_xa_b"][l] = _ln(f"ln_xa{l}", h, xa, p["ln_xa_g"][l:l + 1], p["ln_xa_b"][l:l + 1], cts=dh2)
    dox = matmul(f"xa_o_dx{l}", dxa, p["xa_w_o"][l], tb=True)
    gr["xa_w_o"][l] = matmul(f"xa_o_dw{l}", ox, dxa, ta=True)
    dqx, dkx, dvx = xattn(bsz, qx, kx, vx, cts=dox)
    gr["xa_w_q"][l] = matmul(f"xa_q_dw{l}", h, dqx, ta=True)
    gr["xa_w_k"][l] = matmul(f"xa_k_dw{l}", mem2, dkx, ta=True)
    gr["xa_w_v"][l] = matmul(f"xa_v_dw{l}", mem2, dvx, ta=True)
    return matmul(f"xa_q_dx{l}", dqx, p["xa_w_q"][l], tb=True, add=dh)


def _uq_perm(w):
    w3 = w.reshape(w.shape[0], 8, 192)
    return jnp.concatenate([w3[:, :, :128].reshape(-1, 1024), w3[:, :, 128:160].reshape(-1, 256),
                            w3[:, :, 160:].reshape(-1, 256)], axis=1)


def _uq_unperm(g):
    r = g.shape[0]
    return jnp.concatenate([g[:, :1024].reshape(r, 8, 128), g[:, 1024:1280].reshape(r, 8, 32),
                            g[:, 1280:].reshape(r, 8, 32)], axis=2).reshape(r, 1536)


def _ukv_perm(w):
    w3 = w.reshape(w.shape[0], 8, 256)
    return jnp.concatenate([w3[:, :, :128].reshape(-1, 1024), w3[:, :, 128:].reshape(-1, 1024)], axis=1)


def _ukv_unperm(g):
    r = g.shape[0]
    return jnp.concatenate([g[:, :1024].reshape(r, 8, 128), g[:, 1024:].reshape(r, 8, 128)], axis=2).reshape(r, 2048)


def _pad_cols(w, n):
    return jnp.pad(w, ((0, 0), (0, n - w.shape[1])))


def _shift_prev(x, bsz):
    t, w = x.shape
    x3 = x.reshape(bsz, t // bsz, w)
    return jnp.pad(x3, ((0, 0), (1, 0), (0, 0)))[:, :-1].reshape(t, w)


def _shift_next(x, bsz):
    t, w = x.shape
    x3 = x.reshape(bsz, t // bsz, w)
    return jnp.pad(x3[:, 1:], ((0, 0), (0, 1), (0, 0))).reshape(t, w)


def device_step(x, mem, tgt, p):
    bsz, seq, d = x.shape
    t = bsz * seq
    x2, mem2, tgt2 = x.reshape(t, d), mem.reshape(bsz * mem.shape[1], d), tgt.reshape(t, d)
    gr = {k: [None] * DEPTH for k in ("ln_mix_g", "ln_mix_b", "xa_w_q", "xa_w_k", "xa_w_v", "xa_w_o", "ln_xa_g", "ln_xa_b",
                                      "ffn_w_gate", "ffn_w_up", "ffn_w_down", "ln_ffn_g", "ln_ffn_b")}
    cos_pe, sin_pe = _rope_tables(seq, 64)
    cos_c, sin_c = _rope_tables(seq, 128)
    cq, sq = jnp.tile(cos_pe, (bsz, 8)), jnp.tile(sin_pe, (bsz, 8))
    ck, sk = jnp.tile(cos_pe, (bsz, 1)), jnp.tile(sin_pe, (bsz, 1))
    cd, sd = jnp.tile(cos_c, (bsz, 8)), jnp.tile(sin_c, (bsz, 8))

    w_in0 = _pad_cols(p["ev_w_in"][0], EV_PAD)
    w_uq, w_ukv = _uq_perm(p["ev_mla_w_uq"][0]), _ukv_perm(p["ev_mla_w_ukv"][0])
    z0 = matmul("ev_in", x2, w_in0)
    c_q, c_kv, k_pe, q_g, k_g, v_g, r_g, lr_g = (z0[:, a:b] for a, b in _offsets(EV_W))
    qr = _rows("q_rms", rms_f, [c_q], [p["ev_mla_q_norm"]], [512], TM)[0]
    kvr = _rows("kv_rms", rms_f, [c_kv], [p["ev_mla_kv_norm"]], [256], TM)[0]
    q = matmul("mla_uq", qr, w_uq)
    kv = matmul("mla_ukv", kvr, w_ukv)
    qn, qp1, qp2 = q[:, :1024], q[:, 1024:1280], q[:, 1280:]
    kn, vv = kv[:, :1024], kv[:, 1024:]
    kp1, kp2 = k_pe[:, :32], k_pe[:, 32:]
    r1, r2 = _rows("rope_q", rope_f, [qp1, qp2, cq, sq], [], [256, 256], TM)
    kr1, kr2 = _rows("rope_k", rope_f, [kp1, kp2, ck, sk], [], [32, 32], TM)
    r1h, r2h = _to_heads(r1, bsz, 8), _to_heads(r2, bsz, 8)
    a_out = mla_attn(bsz, qn, r1h, r2h, kn, kr1, kr2, vv)
    gla_prm = (p["ev_gla_w_gate2"][0], p["ev_gla_b_gate"], p["ev_gla_norm_g"], p["ev_gla_norm_b"])
    b_out, gla_sv = gla_block(bsz, q_g, k_g, v_g, r_g, lr_g, *gla_prm)
    mixin0 = jnp.concatenate([a_out, b_out], axis=1)
    mix0 = matmul("ev_out", mixin0, p["ev_w_out"][0])
    h1 = _ln("ln_mix0", x2, mix0, p["ln_mix_g"][0:1], p["ln_mix_b"][0:1])
    h3, tail0 = _tail_fwd(0, h1, mem2, bsz, p)

    w_in1 = _pad_cols(p["od_w_in"][0], OD_PAD)
    z1 = matmul("od_in", h3, w_in1)
    dq_, dk_, dv_ = z1[:, :1024], z1[:, 1024:2048], z1[:, 2048:3072]
    d_in = z1[:, 3072:OD_IN]
    q1, q2 = _halves(dq_, 8)
    k1, k2 = _halves(dk_, 8)
    qd1, qd2 = _rows("rope_dq", rope_f, [q1, q2, cd, sd], [], [512, 512], TM)
    kd1, kd2 = _rows("rope_dk", rope_f, [k1, k2, cd, sd], [], [512, 512], TM)
    c_out = dil_block(bsz, qd1, qd2, kd1, kd2, dv_)
    d_prev = _shift_prev(d_in, bsz)
    mu = p["od_rwkv_mu"]
    ds = _rows("tshift", tshift_f, [d_in, d_prev], [mu], [d_in.shape[1]], TM)[0]
    rw_in = tuple(ds[:, a:b] for a, b in _offsets((1024, 1024, 1024, 96, 96, 256)))
    rw_prm = dict(w0=p["od_rwkv_w0"], wd2=p["od_rwkv_w_decay2"][0], a0=p["od_rwkv_a0"], wa2=p["od_rwkv_w_a2"][0],
                  wg2=p["od_rwkv_w_gate2"][0], k_k=p["od_rwkv_k_k"], k_a=p["od_rwkv_k_a"], r_k=p["od_rwkv_r_k"][0],
                  gn_g=p["od_rwkv_gn_g"], gn_b=p["od_rwkv_gn_b"])
    d_out = rwkv_block(bsz, *rw_in, rw_prm)
    mixin1 = jnp.concatenate([c_out, d_out], axis=1)
    mix1 = matmul("od_out", mixin1, p["od_w_out"][0])
    h4 = _ln("ln_mix1", h3, mix1, p["ln_mix_g"][1:2], p["ln_mix_b"][1:2])
    y, tail1 = _tail_fwd(1, h4, mem2, bsz, p)

    dy, row_loss = _rows("loss", loss_f, [y, tgt2], [], [d, 1], TM)
    loss = jnp.sum(row_loss)

    dh4 = _tail_bwd(1, dy, tail1, mem2, bsz, p, gr)
    dh3, dmix1, gr["ln_mix_g"][1], gr["ln_mix_b"][1] = _ln("ln_mix1", h3, mix1, p["ln_mix_g"][1:2], p["ln_mix_b"][1:2], cts=dh4)
    dmixin1 = matmul("od_out_dx", dmix1, p["od_w_out"][0], tb=True)
    gr["od_w_out"] = matmul("od_out_dw", mixin1, dmix1, ta=True)[None]
    dc_out, dd_out = dmixin1[:, :1024], dmixin1[:, 1024:]
    drw_in, drw_prm = rwkv_block(bsz, *rw_in, rw_prm, cts=dd_out)
    dds = jnp.concatenate(drw_in, axis=1)
    dd_in, dd_prev, dmu = _rows("tshift_bwd", tshift_f, [d_in, d_prev], [mu], None, TM, fwd=False, cts=[dds],
                                wrt_rows=(0, 1), wrt_params=(0,))
    dd_in = dd_in + _shift_next(dd_prev, bsz)
    dqd1, dqd2, dkd1, dkd2, ddv = dil_block(bsz, qd1, qd2, kd1, kd2, dv_, cts=dc_out)
    dq1, dq2 = _rows("rope_dq_bwd", rope_f, [q1, q2, cd, sd], [], None, TM, fwd=False, cts=[dqd1, dqd2], wrt_rows=(0, 1))
    dk1, dk2 = _rows("rope_dk_bwd", rope_f, [k1, k2, cd, sd], [], None, TM, fwd=False, cts=[dkd1, dkd2], wrt_rows=(0, 1))
    dz1 = jnp.concatenate([_unhalves(dq1, dq2, 8), _unhalves(dk1, dk2, 8), ddv, dd_in,
                           jnp.zeros((t, OD_PAD - OD_IN), F32)], axis=1)
    gr["od_w_in"] = matmul("od_in_dw", h3, dz1, ta=True)[:, :OD_IN][None]
    dh3 = matmul("od_in_dx", dz1, w_in1, tb=True, add=dh3)
    gr["od_rwkv_mu"] = dmu
    gr["od_rwkv_w0"], gr["od_rwkv_w_decay2"], gr["od_rwkv_a0"] = drw_prm["w0"], drw_prm["wd2"][None], drw_prm["a0"]
    gr["od_rwkv_w_a2"], gr["od_rwkv_w_gate2"] = drw_prm["wa2"][None], drw_prm["wg2"][None]
    gr["od_rwkv_k_k"], gr["od_rwkv_k_a"], gr["od_rwkv_r_k"] = drw_prm["k_k"], drw_prm["k_a"], drw_prm["r_k"][None]
    gr["od_rwkv_gn_g"], gr["od_rwkv_gn_b"] = drw_prm["gn_g"], drw_prm["gn_b"]

    dh1 = _tail_bwd(0, dh3, tail0, mem2, bsz, p, gr)
    dx2, dmix0, gr["ln_mix_g"][0], gr["ln_mix_b"][0] = _ln("ln_mix0", x2, mix0, p["ln_mix_g"][0:1], p["ln_mix_b"][0:1], cts=dh1)
    dmixin0 = matmul("ev_out_dx", dmix0, p["ev_w_out"][0], tb=True)
    gr["ev_w_out"] = matmul("ev_out_dw", mixin0, dmix0, ta=True)[None]
    da_out, db_out = dmixin0[:, :1024], dmixin0[:, 1024:]
    dq_g, dk_g, dv_g, dlr4, dr_g, dw2, dbg, dng, dnb = gla_block(bsz, q_g, k_g, v_g, r_g, lr_g, *gla_prm, cts=(db_out, gla_sv))
    dlr_g = jnp.sum(dlr4, axis=0)
    dqn, dr1h, dr2h, dkn, dkr1, dkr2, dvv = mla_attn(bsz, qn, r1h, r2h, kn, kr1, kr2, vv, cts=da_out)
    dqp1, dqp2 = _rows("rope_q_bwd", rope_f, [qp1, qp2, cq, sq], [], None, TM, fwd=False,
                       cts=[_from_heads(dr1h), _from_heads(dr2h)], wrt_rows=(0, 1))
    dkp1, dkp2 = _rows("rope_k_bwd", rope_f, [kp1, kp2, ck, sk], [], None, TM, fwd=False, cts=[dkr1, dkr2], wrt_rows=(0, 1))
    dq = jnp.concatenate([dqn, dqp1, dqp2], axis=1)
    dkv = jnp.concatenate([dkn, dvv], axis=1)
    dqr = matmul("mla_uq_dx", dq, w_uq, tb=True)
    gr["ev_mla_w_uq"] = _uq_unperm(matmul("mla_uq_dw", qr, dq, ta=True))[None]
    dkvr = matmul("mla_ukv_dx", dkv, w_ukv, tb=True)
    gr["ev_mla_w_ukv"] = _ukv_unperm(matmul("mla_ukv_dw", kvr, dkv, ta=True))[None]
    dc_q, gr["ev_mla_q_norm"] = _rows("q_rms_bwd", rms_f, [c_q], [p["ev_mla_q_norm"]], None, TM, fwd=False, cts=[dqr],
                                      wrt_rows=(0,), wrt_params=(0,))
    dc_kv, gr["ev_mla_kv_norm"] = _rows("kv_rms_bwd", rms_f, [c_kv], [p["ev_mla_kv_norm"]], None, TM, fwd=False, cts=[dkvr],
                                        wrt_rows=(0,), wrt_params=(0,))
    dz0 = jnp.concatenate([dc_q, dc_kv, dkp1, dkp2, dq_g, dk_g, dv_g, dr_g, dlr_g,
                           jnp.zeros((t, EV_PAD - EV_IN), F32)], axis=1)
    gr["ev_w_in"] = matmul("ev_in_dw", x2, dz0, ta=True)[:, :EV_IN][None]
    dx2 = matmul("ev_in_dx", dz0, w_in0, tb=True, add=dx2)
    gr["ev_gla_w_gate2"], gr["ev_gla_b_gate"] = dw2[None], dbg
    gr["ev_gla_norm_g"], gr["ev_gla_norm_b"] = dng, dnb
    for k in list(gr):
        if isinstance(gr[k], list):
            gr[k] = jnp.stack([g[0] if k.startswith("ln_") else g for g in gr[k]])
    return loss, dx2.reshape(bsz, seq, d), gr


WEIGHTS = ['ev_w_in', 'ev_mla_q_norm', 'ev_mla_w_uq', 'ev_mla_kv_norm', 'ev_mla_w_ukv', 'ev_gla_w_gate2', 'ev_gla_b_gate',
           'ev_gla_norm_g', 'ev_gla_norm_b', 'ev_w_out', 'od_w_in', 'od_rwkv_mu', 'od_rwkv_w0', 'od_rwkv_w_decay2',
           'od_rwkv_a0', 'od_rwkv_w_a2', 'od_rwkv_w_gate2', 'od_rwkv_k_k', 'od_rwkv_k_a', 'od_rwkv_r_k', 'od_rwkv_gn_g',
           'od_rwkv_gn_b', 'od_w_out', 'ln_mix_g', 'ln_mix_b', 'xa_w_q', 'xa_w_k', 'xa_w_v', 'xa_w_o', 'ln_xa_g', 'ln_xa_b',
           'ffn_w_gate', 'ffn_w_up', 'ffn_w_down', 'ln_ffn_g', 'ln_ffn_b']
BIG = {'ev_w_in': -1, 'ev_mla_w_uq': -1, 'ev_mla_w_ukv': -1, 'ev_w_out': -2, 'od_w_in': -1, 'od_w_out': -2,
       'xa_w_q': -2, 'xa_w_k': -2, 'xa_w_v': -2, 'xa_w_o': -2, 'ffn_w_gate': -1, 'ffn_w_up': -1, 'ffn_w_down': -2}
SMALL = ['ev_gla_w_gate2', 'od_rwkv_mu', 'od_rwkv_w0', 'od_rwkv_w_decay2', 'od_rwkv_a0', 'od_rwkv_w_a2', 'od_rwkv_w_gate2',
         'od_rwkv_k_k', 'od_rwkv_k_a', 'od_rwkv_gn_g', 'od_rwkv_gn_b']
REPL = ['ev_mla_q_norm', 'ev_mla_kv_norm', 'ev_gla_b_gate', 'ev_gla_norm_g', 'ev_gla_norm_b', 'od_rwkv_r_k',
        'ln_mix_g', 'ln_mix_b', 'ln_xa_g', 'ln_xa_b', 'ln_ffn_g', 'ln_ffn_b']
PACK_COLS = 128


def _unshard(g, shape, axis):
    axis %= len(shape)
    full = list(shape)
    full[axis] *= N_DEV
    return jnp.moveaxis(g, 0, axis).reshape(full)


def _shard_stack(gfull, axis):
    axis %= gfull.ndim
    shp = list(gfull.shape)
    shp[axis:axis + 1] = [N_DEV, shp[axis] // N_DEV]
    return jnp.moveaxis(gfull.reshape(shp), axis, 0)


def _pack(arrs):
    lead = arrs[0].shape[0]
    flat = jnp.concatenate([a.reshape(lead, -1) for a in arrs], axis=1)
    n = flat.shape[1]
    rows = -(-n // (8 * PACK_COLS)) * 8
    return jnp.pad(flat, ((0, 0), (0, rows * PACK_COLS - n))).reshape(lead, rows, PACK_COLS)


def _unpack(buf, shapes):
    lead = buf.shape[0]
    flat = buf.reshape(lead, -1)
    out, off = [], 0
    for shp in shapes:
        n = math.prod(shp)
        out.append(flat[:, off:off + n].reshape((lead,) + tuple(shp)))
        off += n
    return out


def train_step(x, mem, loss_target, w, m, v):
    p = {}
    for k, ax in BIG.items():
        p[k] = _unshard(all_gather_2level("ag_" + k, w[k].astype(BF16)), w[k].shape, ax)
    small_loc = _pack([w[k][None] for k in SMALL])[0]
    small_all = _unpack(all_gather("ag_small", small_loc), [w[k].shape for k in SMALL])
    for k, g in zip(SMALL, small_all):
        p[k] = _unshard(g, w[k].shape, -1)
    for k in REPL:
        p[k] = w[k]
    loss, dx, gr = device_step(x, mem, loss_target, p)
    loss = lax.psum(loss, ("x", "y", "c"))

    stacks = {}
    for k, ax in BIG.items():
        stacks[k] = reduce_scatter_2level("rs_" + k, _shard_stack(gr[k], ax))
    small_send = _pack([_shard_stack(gr[k], -1) for k in SMALL])
    small_recv = _unpack(reduce_scatter_exchange("rs_small", small_send), [w[k].shape for k in SMALL])
    stacks.update(zip(SMALL, small_recv))
    repl_loc = _pack([gr[k][None] for k in REPL])[0]
    repl_all = _unpack(all_gather("ag_repl_grads", repl_loc), [w[k].shape for k in REPL])
    stacks.update(zip(REPL, repl_all))

    grads, deltas, new_m, new_v = [], [], [], []
    for k in WEIGHTS:
        g, dl, mn, vn = adamw("adamw_" + k, w[k], m[k], v[k], stacks[k])
        grads.append(g), deltas.append(dl), new_m.append(mn), new_v.append(vn)
    return (loss, dx, *grads, *deltas, *new_m, *new_v)


def kernel(x, mem, ev_w_in, ev_mla_q_norm, ev_mla_w_uq, ev_mla_kv_norm, ev_mla_w_ukv, ev_gla_w_gate2, ev_gla_b_gate, ev_gla_norm_g, ev_gla_norm_b, ev_w_out, od_w_in, od_rwkv_mu, od_rwkv_w0, od_rwkv_w_decay2, od_rwkv_a0, od_rwkv_w_a2, od_rwkv_w_gate2, od_rwkv_k_k, od_rwkv_k_a, od_rwkv_r_k, od_rwkv_gn_g, od_rwkv_gn_b, od_w_out, ln_mix_g, ln_mix_b, xa_w_q, xa_w_k, xa_w_v, xa_w_o, ln_xa_g, ln_xa_b, ffn_w_gate, ffn_w_up, ffn_w_down, ln_ffn_g, ln_ffn_b, loss_target, m_ev_w_in, m_ev_mla_q_norm, m_ev_mla_w_uq, m_ev_mla_kv_norm, m_ev_mla_w_ukv, m_ev_gla_w_gate2, m_ev_gla_b_gate, m_ev_gla_norm_g, m_ev_gla_norm_b, m_ev_w_out, m_od_w_in, m_od_rwkv_mu, m_od_rwkv_w0, m_od_rwkv_w_decay2, m_od_rwkv_a0, m_od_rwkv_w_a2, m_od_rwkv_w_gate2, m_od_rwkv_k_k, m_od_rwkv_k_a, m_od_rwkv_r_k, m_od_rwkv_gn_g, m_od_rwkv_gn_b, m_od_w_out, m_ln_mix_g, m_ln_mix_b, m_xa_w_q, m_xa_w_k, m_xa_w_v, m_xa_w_o, m_ln_xa_g, m_ln_xa_b, m_ffn_w_gate, m_ffn_w_up, m_ffn_w_down, m_ln_ffn_g, m_ln_ffn_b, v_ev_w_in, v_ev_mla_q_norm, v_ev_mla_w_uq, v_ev_mla_kv_norm, v_ev_mla_w_ukv, v_ev_gla_w_gate2, v_ev_gla_b_gate, v_ev_gla_norm_g, v_ev_gla_norm_b, v_ev_w_out, v_od_w_in, v_od_rwkv_mu, v_od_rwkv_w0, v_od_rwkv_w_decay2, v_od_rwkv_a0, v_od_rwkv_w_a2, v_od_rwkv_w_gate2, v_od_rwkv_k_k, v_od_rwkv_k_a, v_od_rwkv_r_k, v_od_rwkv_gn_g, v_od_rwkv_gn_b, v_od_w_out, v_ln_mix_g, v_ln_mix_b, v_xa_w_q, v_xa_w_k, v_xa_w_v, v_xa_w_o, v_ln_xa_g, v_ln_xa_b, v_ffn_w_gate, v_ffn_w_up, v_ffn_w_down, v_ln_ffn_g, v_ln_ffn_b):
    given = dict(locals())
    w = {k: given[k] for k in WEIGHTS}
    m = {k: given["m_" + k] for k in WEIGHTS}
    v = {k: given["v_" + k] for k in WEIGHTS}
    return train_step(given["x"], given["mem"], given["loss_target"], w, m, v)
```

```python
import functools
import math

import jax
import jax.numpy as jnp
from jax import lax
from jax.experimental import pallas as pl
from jax.experimental.pallas import tpu as pltpu

F32 = jnp.float32
BF16 = jnp.bfloat16
VMEM_LIMIT = 56 * 1024 * 1024
ROWS_VMEM = 20 * 1024 * 1024

N_DEV = 8
DEPTH = 2
ALPHA = (2.0 * DEPTH) ** 0.25
LN_EPS = 1e-5
RMS_EPS = 1e-6
RWKV_GN_EPS = 64e-5
ADAM_LR, ADAM_B1, ADAM_B2, ADAM_EPS, ADAM_WD, ADAM_STEP = 0.001, 0.9, 0.999, 1e-08, 0.01, 10
NEG_INF = float("-inf")


def _pcall(body, **kw):
    return pl.pallas_call(body, **kw)


def _cparams(**kw):
    return pltpu.CompilerParams(vmem_limit_bytes=VMEM_LIMIT, **kw)


def _dg(a, b, ca, cb, batch):
    nb = 1 if batch else 0
    dims = (((ca + nb,), (cb + nb,)), ((0,), (0,)) if batch else ((), ()))
    return lax.dot_general(a.astype(BF16), b.astype(BF16), dims, preferred_element_type=F32)


@functools.partial(jax.custom_vjp, nondiff_argnums=(2, 3, 4))
def _mm(a, b, ta, tb, batch):
    return _dg(a, b, 0 if ta else 1, 1 if tb else 0, batch)


def _mm_fwd(a, b, ta, tb, batch):
    return _mm(a, b, ta, tb, batch), (a, b)


def _mm_bwd(ta, tb, batch, res, g):
    a, b = res
    if not ta and not tb:
        da, db = _mm(g, b, False, True, batch), _mm(a, g, True, False, batch)
    elif not ta and tb:
        da, db = _mm(g, b, False, False, batch), _mm(g, a, True, False, batch)
    elif ta and not tb:
        da, db = _mm(b, g, False, True, batch), _mm(a, g, False, False, batch)
    else:
        da, db = _mm(b, g, True, True, batch), _mm(g, a, True, True, batch)
    return da.astype(a.dtype), db.astype(b.dtype)


_mm.defvjp(_mm_fwd, _mm_bwd)


def mm(a, b, ta=False, tb=False):
    return _mm(a, b, ta, tb, a.ndim == 3)


def _bs(block, imap):
    return pl.BlockSpec(block, imap)


def _rev_imap(imap, n):
    def r(*idx):
        return imap(*idx[:-1], n - 1 - idx[-1])
    return r


def p_fwd(name, f, grid, ins, outs, carry=None, save_carry=None):
    n_in, n_out = len(ins), len(outs)

    def body(*refs):
        in_refs = refs[:n_in]
        out_refs = refs[n_in:n_in + n_out]
        rest = refs[n_in + n_out:]
        vals = [r[...] for r in in_refs]
        if carry is None:
            res = f(*vals)
        else:
            if save_carry is not None:
                sv_ref, c_ref = rest
            else:
                (c_ref,) = rest

            @pl.when(pl.program_id(len(grid) - 1) == 0)
            def _():
                c_ref[...] = jnp.zeros(c_ref.shape, c_ref.dtype)

            c = c_ref[...]
            if save_carry is not None:
                sv_ref[...] = c
            res = f(c, *vals)
            c_ref[...] = res[0]
            res = res[1:]
        if not isinstance(res, (tuple, list)):
            res = (res,)
        for r, v in zip(out_refs, res):
            r[...] = v.astype(r.dtype)

    out_shape = [jax.ShapeDtypeStruct(s, d) for (s, d, _, _) in outs]
    out_specs = [_bs(b, m) for (_, _, b, m) in outs]
    scratch = []
    if carry is not None:
        if save_carry is not None:
            out_shape.append(jax.ShapeDtypeStruct(save_carry[0], carry[1]))
            out_specs.append(_bs(save_carry[1], save_carry[2]))
        scratch.append(pltpu.VMEM(carry[0], carry[1]))
    return _pcall(
        body, name=name, grid=grid,
        in_specs=[_bs(b, m) for (_, b, m) in ins],
        out_specs=out_specs, out_shape=out_shape, scratch_shapes=scratch,
        compiler_params=_cparams(),
    )(*[a for (a, _, _) in ins])


def p_bwd(name, f, grid, ins, cts, wrt, carry=None, saved=None):
    n_in, n_ct, n_w = len(ins), len(cts), len(wrt)
    rev = carry is not None
    n_last = grid[-1]

    def fix(imap):
        return _rev_imap(imap, n_last) if rev else imap

    def body(*refs):
        in_refs = refs[:n_in]
        ct_refs = refs[n_in:n_in + n_ct]
        k = n_in + n_ct
        if rev:
            sv_ref = refs[k]
            k += 1
        out_refs = refs[k:k + n_w]
        rest = refs[k + n_w:]
        vals = [r[...] for r in in_refs]
        ct_vals = [r[...].astype(F32) for r in ct_refs]
        widx = [w[0] for w in wrt]

        if rev:
            (dc_ref,) = rest

            @pl.when(pl.program_id(len(grid) - 1) == 0)
            def _():
                dc_ref[...] = jnp.zeros(dc_ref.shape, dc_ref.dtype)

            c_in = sv_ref[...]

            def g(c, *dv):
                full = list(vals)
                for i, d in zip(widx, dv):
                    full[i] = d
                return tuple(f(c, *full))

            _, vjp = jax.vjp(g, c_in, *[vals[i] for i in widx])
            grads = vjp((dc_ref[...],) + tuple(ct_vals))
            dc_ref[...] = grads[0]
            grads = grads[1:]
        else:
            def g(*dv):
                full = list(vals)
                for i, d in zip(widx, dv):
                    full[i] = d
                r = f(*full)
                return tuple(r) if isinstance(r, (tuple, list)) else (r,)

            _, vjp = jax.vjp(g, *[vals[i] for i in widx])
            grads = vjp(tuple(ct_vals))

        for w, o_ref, gr in zip(wrt, out_refs, grads):
            acc = w[1]
            if acc is None:
                o_ref[...] = gr.astype(o_ref.dtype)
            else:
                first = None
                for ax in acc:
                    c0 = pl.program_id(ax) == 0
                    first = c0 if first is None else jnp.logical_and(first, c0)

                @pl.when(first)
                def _():
                    o_ref[...] = jnp.zeros(o_ref.shape, o_ref.dtype)

                o_ref[...] += gr.astype(o_ref.dtype)

    in_specs = [_bs(b, fix(m)) for (_, b, m) in ins] + [_bs(b, fix(m)) for (_, b, m) in cts]
    args = [a for (a, _, _) in ins] + [a for (a, _, _) in cts]
    if rev:
        in_specs.append(_bs(saved[1], fix(saved[2])))
        args.append(saved[0])
    out_shape, out_specs = [], []
    for w in wrt:
        a, b, m = ins[w[0]]
        if len(w) > 2:
            m = w[2]
        out_shape.append(jax.ShapeDtypeStruct(a.shape, F32))
        out_specs.append(_bs(b, fix(m)))
    scratch = [pltpu.VMEM(carry[0], carry[1])] if rev else []
    return _pcall(
        body, name=name, grid=grid, in_specs=in_specs, out_specs=out_specs,
        out_shape=out_shape, scratch_shapes=scratch, compiler_params=_cparams(),
    )(*args)


def _rows(name, f, row_ins, params, out_widths, tm, fwd=True, cts=None, wrt_rows=(), wrt_params=()):
    t = row_ins[0].shape[0]
    width = sum(a.shape[1] for a in row_ins)
    width += sum(out_widths) if fwd else sum(c.shape[1] for c in cts) + sum(row_ins[i].shape[1] for i in wrt_rows)
    tm = min(tm, t)
    while tm > 8 and 2 * 4 * tm * width > ROWS_VMEM:
        tm //= 2
    rmap = lambda i: (i, 0)
    pmap = lambda i: (0, 0)
    ins = [(a, (tm, a.shape[1]), rmap) for a in row_ins] + [(p, p.shape, pmap) for p in params]
    if fwd:
        outs = [((t, w), F32, (tm, w), rmap) for w in out_widths]
        return p_fwd(name, f, (t // tm,), ins, outs)
    ct_specs = [(c, (tm, c.shape[1]), rmap) for c in cts]
    wrt = [(i, None) for i in wrt_rows] + [(len(row_ins) + i, (0,)) for i in wrt_params]
    return p_bwd(name, f, (t // tm,), ins, ct_specs, wrt)


def _pick(n, cands):
    for c in cands:
        if n % c == 0:
            return c
    return n


def _wide(n, cap=1664):
    best = None
    for w in range(128, min(n, cap) + 1, 128):
        if n % w == 0:
            best = w
    return best or n


def matmul(name, a, b, ta=False, tb=False, out_dtype=F32, add=None):
    m = a.shape[1] if ta else a.shape[0]
    k = a.shape[0] if ta else a.shape[1]
    n = b.shape[0] if tb else b.shape[1]
    assert (b.shape[1] if tb else b.shape[0]) == k, (a.shape, b.shape, ta, tb)
    bm, bn, bk = _pick(m, (1024, 512, 256, 128)), _wide(n), _pick(k, (512, 256, 128))
    nk = k // bk

    def body(a_ref, b_ref, *rest):
        o_ref, acc_ref = rest[-2:]

        @pl.when(pl.program_id(2) == 0)
        def _():
            acc_ref[...] = jnp.zeros(acc_ref.shape, F32) if add is None else rest[0][...].astype(F32)

        acc_ref[...] += mm(a_ref[...], b_ref[...], ta, tb)

        @pl.when(pl.program_id(2) == nk - 1)
        def _():
            o_ref[...] = acc_ref[...].astype(o_ref.dtype)

    a_spec = _bs((bk, bm), lambda i, j, l: (l, i)) if ta else _bs((bm, bk), lambda i, j, l: (i, l))
    b_spec = _bs((bn, bk), lambda i, j, l: (j, l)) if tb else _bs((bk, bn), lambda i, j, l: (l, j))
    o_spec = _bs((bm, bn), lambda i, j, l: (i, j))
    return _pcall(
        body, name=name, grid=(m // bm, n // bn, nk),
        in_specs=[a_spec, b_spec] + ([] if add is None else [o_spec]), out_specs=o_spec,
        out_shape=jax.ShapeDtypeStruct((m, n), out_dtype),
        scratch_shapes=[pltpu.VMEM((bm, bn), F32)],
        compiler_params=_cparams(dimension_semantics=("parallel", "parallel", "arbitrary")),
    )(*((a, b) if add is None else (a, b, add)))


def _to_heads(x, b, h):
    t, w = x.shape
    return x.reshape(b, t // b, h, w // h).transpose(0, 2, 1, 3)


def _from_heads(x):
    b, h, s, d = x.shape
    return x.transpose(0, 2, 1, 3).reshape(b * s, h * d)


RW_STEPS = 8
RW_G = 16


def rwkv_group(st, w8, kk8, ka8, kh8, r8, vc):
    ys = []
    for t in range(RW_STEPS):
        row = lambda x: x[:, t:t + 1, :]
        sa = jnp.sum(st * row(kk8), axis=2, keepdims=True)
        st = st * row(w8) - sa * row(ka8) + vc[:, :, t:t + 1] * row(kh8)
        ys.append(jnp.sum(st * row(r8), axis=2, keepdims=True))
    return st, jnp.concatenate(ys, axis=2)


def rwkv_prehead_f(kkraw, a):
    nrm = jnp.sqrt(jnp.sum(kkraw * kkraw, axis=-1, keepdims=True))
    kk = kkraw / jnp.maximum(nrm, 1e-12)
    return kk, kk * a


def rwkv_pre_f(kd, w_lr, a_lr, g_lr, w0, wd2, a0, wa2, wg2, k_k, k_a):
    wpre = w0 + mm(jnp.tanh(w_lr), wd2)
    w = -jax.nn.softplus(-wpre) - 0.5
    decay = jnp.exp(-jnp.exp(w))
    a = jax.nn.sigmoid(a0 + mm(a_lr, wa2))
    g = mm(jax.nn.sigmoid(g_lr), wg2)
    kkraw = kd * k_k
    kh = kd * (1.0 + (a - 1.0) * k_a)
    return decay, a, g, kkraw, kh


def rwkv_post_f(y, r, kh, v, g, gn_g, gn_b, r_k):
    mu = jnp.mean(y, axis=-1, keepdims=True)
    var = jnp.mean(jnp.square(y - mu), axis=-1, keepdims=True)
    yn = (y - mu) * lax.rsqrt(var + RWKV_GN_EPS) * gn_g + gn_b
    bonus = jnp.sum(r * kh * r_k, axis=-1, keepdims=True) * v
    return (yn + bonus) * g


def _to_cols(xh):
    b, h, s, d = xh.shape
    return xh.reshape(b, h, s // RW_STEPS, RW_STEPS, d).transpose(0, 1, 2, 4, 3)


def _from_cols(xc):
    b, h, n, d, k = xc.shape
    return xc.transpose(0, 1, 2, 4, 3).reshape(b, h, n * k, d)


def rwkv_block(bsz, r, kd, vd, w_lr, a_lr, g_lr, prm, cts=None):
    t = r.shape[0]
    hh, n = 16, 64
    tm = 256
    pre_rows = [kd, w_lr, a_lr, g_lr]
    pre_prm = [prm[k] for k in ("w0", "wd2", "a0", "wa2", "wg2", "k_k", "k_a")]
    decay, a, g, kkraw, kh = _rows("rwkv_pre", rwkv_pre_f, pre_rows, pre_prm, [1024] * 5, tm)
    heads = lambda x: _to_heads(x, bsz, hh)
    rh, khh, vh, gh, dech, kkrawh, ah = (heads(x) for x in (r, kh, vd, g, decay, kkraw, a))
    s = t // bsz
    ng = s // RW_STEPS
    ts = 512
    hb = (None, None, ts, n)
    hm = lambda h, b, i: (b, h, i, 0)
    pgrid = (hh, bsz, s // ts)
    ph_ins = [(kkrawh, hb, hm), (ah, hb, hm)]
    kkh, kah = p_fwd("rwkv_prehead", rwkv_prehead_f, pgrid, ph_ins, [((bsz, hh, s, n), F32, hb, hm)] * 2)
    grp = lambda x: x.reshape(bsz, hh, ng, RW_STEPS, n)
    cb, rb, sb = (None, RW_G, None, n, RW_STEPS), (None, RW_G, None, RW_STEPS, n), (None, RW_G, None, n, n)
    cm = lambda b, h, i: (b, h, i, 0, 0)
    sc_ins = [(grp(x), rb, cm) for x in (dech, kkh, kah, khh, rh)] + [(_to_cols(vh), cb, cm)]
    grid = (bsz, hh // RW_G, ng)
    yc, sv = p_fwd("rwkv_scan", rwkv_group, grid, sc_ins, [((bsz, hh, ng, n, RW_STEPS), F32, cb, cm)],
                   carry=((RW_G, n, n), F32), save_carry=((bsz, hh, ng, n, n), sb, cm))
    yh = _from_cols(yc)
    pb = (None, 1, n)
    pm = lambda h, b, i: (h, 0, 0)
    gn_g, gn_b, r_k = (prm[k].reshape(hh, 1, n) for k in ("gn_g", "gn_b", "r_k"))
    post_ins = [(x, hb, hm) for x in (yh, rh, khh, vh, gh)] + [(p, pb, pm) for p in (gn_g, gn_b, r_k)]
    if cts is None:
        (oh,) = p_fwd("rwkv_post", rwkv_post_f, pgrid, post_ins, [((bsz, hh, s, n), F32, hb, hm)])
        return _from_heads(oh)
    doh = _to_heads(cts, bsz, hh)
    dyh, drh1, dkhh1, dvh1, dgh, dgn_g, dgn_b, dr_k = p_bwd(
        "rwkv_post_bwd", rwkv_post_f, pgrid, post_ins, [(doh, hb, hm)],
        [(i, None) for i in range(5)] + [(5 + i, (1, 2)) for i in range(3)])
    drows_v = p_bwd("rwkv_scan_bwd", rwkv_group, grid, sc_ins, [(_to_cols(dyh), cb, cm)], [(i, None) for i in range(6)],
                    carry=((RW_G, n, n), F32), saved=(sv, sb, cm))
    ddech, dkkh, dkah, dkhh2, drh2 = (x.reshape(bsz, hh, s, n) for x in drows_v[:5])
    dkkrawh, dah = p_bwd("rwkv_prehead_bwd", rwkv_prehead_f, pgrid, ph_ins, [(dkkh, hb, hm), (dkah, hb, hm)],
                         [(0, None), (1, None)])
    ddecay, dkkraw, da = _from_heads(ddech), _from_heads(dkkrawh), _from_heads(dah)
    dv = _from_heads(_from_cols(drows_v[5]) + dvh1)
    dr = _from_heads(drh2 + drh1)
    dkh = _from_heads(dkhh2 + dkhh1)
    dg = _from_heads(dgh)
    res = _rows("rwkv_pre_bwd", rwkv_pre_f, pre_rows, pre_prm, None, tm, fwd=False,
                cts=[ddecay, da, dg, dkkraw, dkh], wrt_rows=(0, 1, 2, 3), wrt_params=tuple(range(7)))
    dkd, dw_lr, da_lr, dg_lr = res[:4]
    dprm = dict(zip(("w0", "wd2", "a0", "wa2", "wg2", "k_k", "k_a"), res[4:]))
    dprm.update(gn_g=dgn_g.reshape(1, -1), gn_b=dgn_b.reshape(1, -1), r_k=dr_k.reshape(hh, n))
    return (dr, dkd, dv, dw_lr, da_lr, dg_lr), dprm


GLA_C, GLA_DK, GLA_DV, GLA_H, GLA_TAU = 64, 128, 256, 4, 16.0


def gla_chunk_f(st, q, k, v, lr, r, w2, bg, ng, nb):
    la = jax.nn.log_sigmoid(mm(lr, w2) + bg) / GLA_TAU
    ri = lax.broadcasted_iota(jnp.int32, (GLA_C, GLA_C), 0)
    ci = lax.broadcasted_iota(jnp.int32, (GLA_C, GLA_C), 1)
    causal = ci <= ri
    b = jnp.dot(causal.astype(F32), la, precision=lax.Precision.HIGHEST, preferred_element_type=F32)
    b_last = jnp.sum(la, axis=0, keepdims=True)
    q_dec = (q * (GLA_DK ** -0.5)) * jnp.exp(b)
    k_inv = k * jnp.exp(-b)
    k_end = k * jnp.exp(b_last - b)
    att = jnp.where(causal, mm(q_dec, k_inv, tb=True), 0.0)
    o = mm(att, v) + mm(q_dec, st, tb=True)
    st_new = st * jnp.exp(b_last) + mm(v, k_end, ta=True)
    mu = jnp.mean(o, axis=-1, keepdims=True)
    var = jnp.mean(jnp.square(o - mu), axis=-1, keepdims=True)
    on = (o - mu) * lax.rsqrt(var + LN_EPS) * ng + nb
    return st_new, on * jax.nn.silu(r)


def gla_block(bsz, q, k, v, r, lr, w2, bg, ng, nb, cts=None):
    t = q.shape[0]
    nc = t // bsz // GLA_C
    grid = (GLA_H, bsz, nc)
    rm = lambda h, b, c: (b * nc + c, h)
    ins = [(q, (GLA_C, GLA_DK), rm), (k, (GLA_C, GLA_DK), rm), (v, (GLA_C, GLA_DV), rm),
           (jnp.broadcast_to(lr[None], (GLA_H,) + lr.shape), (None, GLA_C, lr.shape[1]), lambda h, b, c: (h, b * nc + c, 0)),
           (r, (GLA_C, GLA_DV), rm),
           (w2, (w2.shape[0], GLA_DK), lambda h, b, c: (0, h)), (bg, (1, GLA_DK), lambda h, b, c: (0, h)),
           (ng, (1, GLA_DV), lambda h, b, c: (0, 0)), (nb, (1, GLA_DV), lambda h, b, c: (0, 0))]
    ob = (GLA_C, GLA_DV)
    sshape, sblock = (GLA_H, bsz, nc, GLA_DV, GLA_DK), (None, None, None, GLA_DV, GLA_DK)
    sm = lambda h, b, c: (h, b, c, 0, 0)
    carry = ((GLA_DV, GLA_DK), F32)
    if cts is None:
        out, sv = p_fwd("gla_scan", gla_chunk_f, grid, ins, [((t, GLA_H * GLA_DV), F32, ob, rm)],
                        carry=carry, save_carry=(sshape, sblock, sm))
        return out, sv
    dout, sv = cts
    return p_bwd("gla_scan_bwd", gla_chunk_f, grid, ins, [(dout, ob, rm)],
                 [(0, None), (1, None), (2, None), (3, None), (4, None), (5, (1, 2)), (6, (1, 2)), (7, (0, 1, 2)), (8, (0, 1, 2))],
                 carry=carry, saved=(sv, sblock, sm))


def _softmax_rows(sc):
    m = lax.stop_gradient(jnp.max(sc, axis=-1, keepdims=True))
    e = jnp.exp(sc - m)
    return e / jnp.sum(e, axis=-1, keepdims=True)


def mla_attn_f(qn, r1, r2, kn, kr1, kr2, v):
    tq, s = qn.shape[0], kn.shape[0]
    sc = (mm(qn, kn, tb=True) + mm(r1, kr1, tb=True) + mm(r2, kr2, tb=True)) * (192.0 ** -0.5)
    qpos = pl.program_id(2) * tq + lax.broadcasted_iota(jnp.int32, (tq, s), 0)
    kpos = lax.broadcasted_iota(jnp.int32, (tq, s), 1)
    sc = jnp.where(kpos <= qpos, sc, NEG_INF)
    return mm(_softmax_rows(sc), v)


def mla_attn(bsz, qn, r1, r2, kn, kr1, kr2, v, cts=None, tq=256):
    t = qn.shape[0]
    s = t // bsz
    nq = s // tq
    hh = 8
    grid = (bsz, hh, nq)
    qm = lambda b, h, i: (b * nq + i, h)
    km_ = lambda b, h, i: (b, h)
    ins = [(qn, (tq, 128), qm),
           (r1, (None, None, tq, 32), lambda b, h, i: (b, h, i, 0)), (r2, (None, None, tq, 32), lambda b, h, i: (b, h, i, 0)),
           (kn, (s, 128), km_), (kr1, (s, 32), lambda b, h, i: (b, 0)), (kr2, (s, 32), lambda b, h, i: (b, 0)),
           (v, (s, 128), km_)]
    if cts is None:
        return p_fwd("mla_attn", mla_attn_f, grid, ins, [((t, 1024), F32, (tq, 128), qm)])[0]
    return p_bwd("mla_attn_bwd", mla_attn_f, grid, ins, [(cts, (tq, 128), qm)],
                 [(0, None), (1, None), (2, None), (3, (2,)), (4, (1, 2)), (5, (1, 2)), (6, (2,))])


def xattn_f(q, k, v):
    sc = mm(q, k, tb=True) * (512.0 ** -0.5)
    return mm(_softmax_rows(sc), v)


def xattn(bsz, q, k, v, cts=None, tq=512):
    t = q.shape[0]
    nq = t // bsz // tq
    mlen = k.shape[0] // bsz
    grid = (bsz, 4, nq)
    qm = lambda b, h, i: (b * nq + i, h)
    km_ = lambda b, h, i: (b, h)
    ins = [(q, (tq, 512), qm), (k, (mlen, 512), km_), (v, (mlen, 512), km_)]
    if cts is None:
        return p_fwd("xattn", xattn_f, grid, ins, [((t, 2048), F32, (tq, 512), qm)])[0]
    return p_bwd("xattn_bwd", xattn_f, grid, ins, [(cts, (tq, 512), qm)], [(0, None), (1, (2,)), (2, (2,))])


DIL_SPAN = 128
DIL_BRANCHES = ((128, 1), (512, 4), (2048, 16))


def dil_attn_f(q1, q2, k1c, k2c, vc, k1p, k2p, vp):
    gb, sp = q1.shape[0], DIL_SPAN
    scale = 128.0 ** -0.5
    sc_c = (mm(q1, k1c, tb=True) + mm(q2, k2c, tb=True)) * scale
    sc_p = (mm(q1, k1p, tb=True) + mm(q2, k2p, tb=True)) * scale
    ql = lax.broadcasted_iota(jnp.int32, (gb, sp, sp), 1)
    kl = lax.broadcasted_iota(jnp.int32, (gb, sp, sp), 2)
    has_prev = pl.program_id(1) > 0
    sc_c = jnp.where(kl <= ql, sc_c, NEG_INF)
    sc_p = jnp.where(jnp.logical_and(kl >= ql, has_prev), sc_p, NEG_INF)
    m = lax.stop_gradient(jnp.maximum(jnp.max(sc_c, axis=-1, keepdims=True), jnp.max(sc_p, axis=-1, keepdims=True)))
    e_c, e_p = jnp.exp(sc_c - m), jnp.exp(sc_p - m)
    den = jnp.sum(e_c, axis=-1, keepdims=True) + jnp.sum(e_p, axis=-1, keepdims=True)
    o = mm(e_c / den, vc) + mm(e_p / den, vp)
    return o, m + jnp.log(den)


def dil_branch(q1, q2, k1, k2, v, cts=None, gb=8):
    g, l, _ = q1.shape
    nb = l // DIL_SPAN
    grid = (g // gb, nb)
    cm = lambda i, n: (i, n, 0)
    pm = lambda i, n: (i, jnp.maximum(n - 1, 0), 0)
    b64, b128, b1 = (gb, DIL_SPAN, 64), (gb, DIL_SPAN, 128), (gb, DIL_SPAN, 1)
    ins = [(q1, b64, cm), (q2, b64, cm), (k1, b64, cm), (k2, b64, cm), (v, b128, cm),
           (k1, b64, pm), (k2, b64, pm), (v, b128, pm)]
    if cts is None:
        return p_fwd("dil_attn", dil_attn_f, grid, ins, [((g, l, 128), F32, b128, cm), ((g, l, 1), F32, b1, cm)])
    do, dlse = cts
    dq1, dq2, dk1c, dk2c, dvc, dk1p, dk2p, dvp = p_bwd(
        "dil_attn_bwd", dil_attn_f, grid, ins, [(do, b128, cm), (dlse, b1, cm)],
        [(i, None) for i in range(5)] + [(i, None, cm) for i in (5, 6, 7)])

    def fold(dc, dp):
        return dc + jnp.pad(dp[:, DIL_SPAN:], ((0, 0), (0, DIL_SPAN), (0, 0)))

    return dq1, dq2, fold(dk1c, dk1p), fold(dk2c, dk2p), fold(dvc, dvp)


def dil_mix_f(o1, o2, o3, l1, l2, l3):
    m = lax.stop_gradient(jnp.maximum(jnp.maximum(l1, l2), l3))
    e1, e2, e3 = jnp.exp(l1 - m), jnp.exp(l2 - m), jnp.exp(l3 - m)
    den = e1 + e2 + e3
    return (e1 / den) * o1 + (e2 / den) * o2 + (e3 / den) * o3


def _to_res(xh, dil):
    b, h, s, d = xh.shape
    return xh.reshape(b, h, s // dil, dil, d).transpose(0, 1, 3, 2, 4).reshape(b * h * dil, s // dil, d)


def _from_res(xr, b, h, dil):
    g, l, d = xr.shape
    return xr.reshape(b, h, dil, l, d).transpose(0, 1, 3, 2, 4).reshape(b, h, l * dil, d)


def dil_block(bsz, q1, q2, k1, k2, v, cts=None):
    hh = 8
    heads = [_to_heads(x, bsz, hh) for x in (q1, q2, k1, k2, v)]
    s = heads[0].shape[2]
    outs, res_in = [], []
    for window, dil in DIL_BRANCHES:
        assert window // dil == DIL_SPAN and (s // dil) % DIL_SPAN == 0
        rin = [_to_res(x, dil) for x in heads]
        o, lse = dil_branch(*rin)
        res_in.append(rin)
        outs.append((_from_res(o, bsz, hh, dil), _from_res(lse, bsz, hh, dil)))
    tq = 512
    ob, lb = (None, None, tq, 128), (None, None, tq, 1)
    hm = lambda b, h, i: (b, h, i, 0)
    mix_ins = [(o, ob, hm) for (o, _) in outs] + [(l, lb, hm) for (_, l) in outs]
    grid = (bsz, hh, s // tq)
    if cts is None:
        (mix,) = p_fwd("dil_mix", dil_mix_f, grid, mix_ins, [((bsz, hh, s, 128), F32, ob, hm)])
        return _from_heads(mix)
    dmix = _to_heads(cts, bsz, hh)
    dml = p_bwd("dil_mix_bwd", dil_mix_f, grid, mix_ins, [(dmix, ob, hm)], [(i, None) for i in range(6)])
    tot = None
    for j, (window, dil) in enumerate(DIL_BRANCHES):
        do, dl = _to_res(dml[j], dil), _to_res(dml[3 + j], dil)
        gr = dil_branch(*res_in[j], cts=(do, dl))
        gr = [_from_res(x, bsz, hh, dil) for x in gr]
        tot = gr if tot is None else [a + b for a, b in zip(tot, gr)]
    return tuple(_from_heads(x) for x in tot)


_ANY = pl.BlockSpec(memory_space=pl.ANY)


def _me_and_peers():
    x, y, c = lax.axis_index("x"), lax.axis_index("y"), lax.axis_index("c")
    me = 4 * x + 2 * y + c
    peers = []
    for k in range(1, N_DEV):
        px = 1 - x if k & 4 else x
        py = 1 - y if k & 2 else y
        pc = 1 - c if k & 1 else c
        peers.append(((px, py, pc), 4 * px + 2 * py + pc))
    return me, peers


def _exchange(name, x, scatter):
    shape = x.shape[1:] if scatter else x.shape

    def body(x_ref, out_ref, send_sems, recv_sems, local_sem):
        me, peers = _me_and_peers()
        src_me = x_ref.at[me] if scatter else x_ref
        local = pltpu.make_async_copy(src_me, out_ref.at[me], local_sem)
        local.start()
        sends = []
        for k, (dev, idx) in enumerate(peers):
            cp = pltpu.make_async_remote_copy(
                src_ref=x_ref.at[idx] if scatter else x_ref, dst_ref=out_ref.at[me],
                send_sem=send_sems.at[k], recv_sem=recv_sems.at[k],
                device_id=dev, device_id_type=pl.DeviceIdType.MESH)
            cp.start()
            sends.append(cp)
        for k, (dev, idx) in enumerate(peers):
            pltpu.make_async_remote_copy(
                src_ref=src_me, dst_ref=out_ref.at[idx], send_sem=send_sems.at[k], recv_sem=recv_sems.at[k],
                device_id=dev, device_id_type=pl.DeviceIdType.MESH).wait_recv()
        for cp in sends:
            cp.wait_send()
        local.wait()

    return _pcall(
        body, name=name, in_specs=[_ANY], out_specs=_ANY,
        out_shape=jax.ShapeDtypeStruct((N_DEV,) + tuple(shape), x.dtype),
        scratch_shapes=[pltpu.SemaphoreType.DMA((N_DEV - 1,)), pltpu.SemaphoreType.DMA((N_DEV - 1,)),
                        pltpu.SemaphoreType.DMA],
        compiler_params=pltpu.CompilerParams(has_side_effects=True),
    )(x)


def all_gather(name, x):
    return _exchange(name, x, False)


def reduce_scatter_exchange(name, x):
    return _exchange(name, x, True)


def all_gather_2level(name, x):
    def body(x_ref, out_ref, send_sems, recv_sems, local_sem):
        x_, y_, c_ = lax.axis_index("x"), lax.axis_index("y"), lax.axis_index("c")
        sibling = (x_, y_, 1 - c_)
        chips = [(1 - x_, y_), (x_, 1 - y_), (1 - x_, 1 - y_)]

        def slot(px, py, pc):
            return out_ref.at[4 * px + 2 * py + pc]

        def copy(k, block, to, src=None):
            return pltpu.make_async_remote_copy(
                src_ref=slot(*block) if src is None else src, dst_ref=slot(*block),
                send_sem=send_sems.at[k], recv_sem=recv_sems.at[k], device_id=to, device_id_type=pl.DeviceIdType.MESH)

        me = (x_, y_, c_)
        mine = pltpu.make_async_copy(x_ref, slot(*me), local_sem)
        mine.start()
        first = [copy(0, me, sibling, src=x_ref)]
        first += [copy(1 + j, me, (*chip, c_), src=x_ref) for j, chip in enumerate(chips)]
        for cp in first:
            cp.start()
        passed = [copy(4 + j, (*chip, c_), sibling) for j, chip in enumerate(chips)]
        for j, chip in enumerate(chips):
            copy(1 + j, (*chip, c_), me).wait_recv()
            passed[j].start()
        copy(0, sibling, me).wait_recv()
        for j, chip in enumerate(chips):
            copy(4 + j, (*chip, 1 - c_), me).wait_recv()
        for cp in first + passed:
            cp.wait_send()
        mine.wait()

    return _pcall(
        body, name=name, in_specs=[_ANY], out_specs=_ANY,
        out_shape=jax.ShapeDtypeStruct((N_DEV,) + tuple(x.shape), x.dtype),
        scratch_shapes=[pltpu.SemaphoreType.DMA((7,)), pltpu.SemaphoreType.DMA((7,)), pltpu.SemaphoreType.DMA],
        compiler_params=pltpu.CompilerParams(has_side_effects=True),
    )(x)


def sibling_swap(name, x):
    shape = (4,) + tuple(x.shape[1:])

    def body(x_ref, theirs_ref, send_sems, recv_sems):
        x_, y_, c_ = lax.axis_index("x"), lax.axis_index("y"), lax.axis_index("c")
        sends = []
        for j in range(4):
            cp = pltpu.make_async_remote_copy(
                src_ref=x_ref.at[2 * j + 1 - c_], dst_ref=theirs_ref.at[j], send_sem=send_sems.at[j], recv_sem=recv_sems.at[j],
                device_id=(x_, y_, 1 - c_), device_id_type=pl.DeviceIdType.MESH)
            cp.start()
            sends.append(cp)
        for cp in sends:
            cp.wait()

    return _pcall(
        body, name=name, in_specs=[_ANY], out_specs=_ANY,
        out_shape=jax.ShapeDtypeStruct(shape, x.dtype),
        scratch_shapes=[pltpu.SemaphoreType.DMA((4,)), pltpu.SemaphoreType.DMA((4,))],
        compiler_params=pltpu.CompilerParams(has_side_effects=True),
    )(x)


def pair_add(name, stack, theirs, out_dtype):
    shp = theirs.shape
    c = shp[-1]
    r = math.prod(shp[1:-1])
    br = r
    for cand in (1024, 512, 256, 128, 64, 32, 16, 8):
        if r % cand == 0 and cand * c <= 256 * 1024:
            br = cand
            break

    def body(s0_ref, s1_ref, t_ref, o_ref):
        mine = jnp.where(lax.axis_index("c") == 0, s0_ref[...], s1_ref[...])
        o_ref[...] = (mine + t_ref[...]).astype(o_ref.dtype)

    s4 = stack.reshape(4, 2, r, c)
    out = _pcall(
        body, name=name, grid=(4, r // br),
        in_specs=[_bs((None, None, br, c), lambda j, i: (j, 0, i, 0)), _bs((None, None, br, c), lambda j, i: (j, 1, i, 0)),
                  _bs((None, br, c), lambda j, i: (j, i, 0))],
        out_specs=_bs((None, br, c), lambda j, i: (j, i, 0)),
        out_shape=jax.ShapeDtypeStruct((4, r, c), out_dtype), compiler_params=_cparams(),
    )(s4, s4, theirs.reshape(4, r, c))
    return out.reshape(shp)


def chip_exchange(name, p):
    def body(p_ref, out_ref, send_sems, recv_sems, local_sem):
        x_, y_, c_ = lax.axis_index("x"), lax.axis_index("y"), lax.axis_index("c")
        my_chip = 2 * x_ + y_
        chips = [(1 - x_, y_), (x_, 1 - y_), (1 - x_, 1 - y_)]
        local = pltpu.make_async_copy(p_ref.at[my_chip], out_ref.at[my_chip], local_sem)
        local.start()
        sends = []
        for k, (px, py) in enumerate(chips):
            cp = pltpu.make_async_remote_copy(
                src_ref=p_ref.at[2 * px + py], dst_ref=out_ref.at[my_chip], send_sem=send_sems.at[k], recv_sem=recv_sems.at[k],
                device_id=(px, py, c_), device_id_type=pl.DeviceIdType.MESH)
            cp.start()
            sends.append(cp)
        for k, (px, py) in enumerate(chips):
            pltpu.make_async_remote_copy(
                src_ref=p_ref.at[my_chip], dst_ref=out_ref.at[2 * px + py], send_sem=send_sems.at[k], recv_sem=recv_sems.at[k],
                device_id=(px, py, c_), device_id_type=pl.DeviceIdType.MESH).wait_recv()
        for cp in sends:
            cp.wait_send()
        local.wait()

    return _pcall(
        body, name=name, in_specs=[_ANY], out_specs=_ANY,
        out_shape=jax.ShapeDtypeStruct(p.shape, p.dtype),
        scratch_shapes=[pltpu.SemaphoreType.DMA((3,)), pltpu.SemaphoreType.DMA((3,)), pltpu.SemaphoreType.DMA],
        compiler_params=pltpu.CompilerParams(has_side_effects=True),
    )(p)


def reduce_scatter_2level(name, stack):
    theirs = sibling_swap(name + "_d2d", stack)
    part = pair_add(name + "_add", stack, theirs, BF16)
    return chip_exchange(name + "_ici", part)


def ln_res_f(h, r, g, b):
    x = ALPHA * h + r
    mu = jnp.mean(x, axis=-1, keepdims=True)
    var = jnp.mean(jnp.square(x - mu), axis=-1, keepdims=True)
    return (x - mu) * lax.rsqrt(var + LN_EPS) * g + b


def rms_f(x, g):
    return x * lax.rsqrt(jnp.mean(x * x, axis=-1, keepdims=True) + RMS_EPS) * g


def rope_f(x1, x2, c, s):
    return x1 * c - x2 * s, x1 * s + x2 * c


def swiglu_f(g, u):
    return jax.nn.silu(g) * u


def tshift_f(d, dprev, mu):
    return d + (dprev - d) * mu


def loss_f(y, tgt):
    e = y - tgt
    return e / y.shape[-1], 0.5 * jnp.mean(e * e, axis=-1, keepdims=True)


def adamw(name, w, m, v, gstack):
    shp = w.shape
    c = shp[-1]
    r = math.prod(shp[:-1])
    br = r
    for cand in (512, 256, 128, 64, 32, 16, 8):
        if r % cand == 0 and cand * c <= 128 * 1024:
            br = cand
            break
    k = gstack.shape[0]

    def body(w_ref, m_ref, v_ref, g_ref, go_ref, d_ref, mo_ref, vo_ref):
        g = g_ref[0].astype(F32)
        for j in range(1, k):
            g = g + g_ref[j].astype(F32)
        m_new = ADAM_B1 * m_ref[...] + (1.0 - ADAM_B1) * g
        v_new = ADAM_B2 * v_ref[...] + (1.0 - ADAM_B2) * jnp.square(g)
        m_hat = m_new / (1.0 - ADAM_B1 ** ADAM_STEP)
        v_hat = v_new / (1.0 - ADAM_B2 ** ADAM_STEP)
        go_ref[...] = g
        d_ref[...] = -ADAM_LR * (m_hat / (jnp.sqrt(v_hat) + ADAM_EPS) + ADAM_WD * w_ref[...])
        mo_ref[...] = m_new
        vo_ref[...] = v_new

    spec = _bs((br, c), lambda i: (i, 0))
    outs = _pcall(
        body, name=name, grid=(r // br,),
        in_specs=[spec, spec, spec, _bs((k, br, c), lambda i: (0, i, 0))],
        out_specs=[spec] * 4, out_shape=[jax.ShapeDtypeStruct((r, c), F32)] * 4,
        compiler_params=_cparams(),
    )(w.reshape(r, c), m.reshape(r, c), v.reshape(r, c), gstack.reshape(k, r, c))
    return tuple(o.reshape(shp) for o in outs)


EV_W = (512, 256, 64, 512, 512, 1024, 1024, 16)
EV_IN, EV_PAD = 3920, 4096
OD_IN, OD_PAD = 6592, 6656
TM = 256


def _offsets(widths):
    offs, acc = [], 0
    for w in widths:
        offs.append((acc, acc + w))
        acc += w
    return offs


def _rope_tables(seq, dim):
    inv = 10000.0 ** (-jnp.arange(0, dim, 2, dtype=F32) / dim)
    ang = jnp.arange(seq, dtype=F32)[:, None] * inv[None, :]
    return jnp.cos(ang), jnp.sin(ang)


def _halves(x, nh):
    t, w = x.shape
    x3 = x.reshape(t, nh, w // nh)
    hd = w // nh // 2
    return x3[:, :, :hd].reshape(t, nh * hd), x3[:, :, hd:].reshape(t, nh * hd)


def _unhalves(x1, x2, nh):
    t = x1.shape[0]
    return jnp.concatenate([x1.reshape(t, nh, -1), x2.reshape(t, nh, -1)], axis=2).reshape(t, -1)


def _ln(name, h, r, g, b, cts=None):
    if cts is None:
        return _rows(name, ln_res_f, [h, r], [g, b], [h.shape[1]], TM)[0]
    return _rows(name + "_bwd", ln_res_f, [h, r], [g, b], None, TM, fwd=False, cts=[cts], wrt_rows=(0, 1), wrt_params=(0, 1))


def _tail_fwd(l, h, mem2, bsz, p):
    qx = matmul(f"xa_q{l}", h, p["xa_w_q"][l])
    kx = matmul(f"xa_k{l}", mem2, p["xa_w_k"][l])
    vx = matmul(f"xa_v{l}", mem2, p["xa_w_v"][l])
    ox = xattn(bsz, qx, kx, vx)
    xa = matmul(f"xa_o{l}", ox, p["xa_w_o"][l])
    h2 = _ln(f"ln_xa{l}", h, xa, p["ln_xa_g"][l:l + 1], p["ln_xa_b"][l:l + 1])
    gg = matmul(f"ffn_g{l}", h2, p["ffn_w_gate"][l])
    uu = matmul(f"ffn_u{l}", h2, p["ffn_w_up"][l])
    act = _rows(f"swiglu{l}", swiglu_f, [gg, uu], [], [gg.shape[1]], TM)[0]
    ff = matmul(f"ffn_d{l}", act, p["ffn_w_down"][l])
    h3 = _ln(f"ln_ffn{l}", h2, ff, p["ln_ffn_g"][l:l + 1], p["ln_ffn_b"][l:l + 1])
    return h3, (h, qx, kx, vx, ox, xa, h2, gg, uu, act, ff)


def _tail_bwd(l, dh3, saved, mem2, bsz, p, gr):
    h, qx, kx, vx, ox, xa, h2, gg, uu, act, ff = saved
    dh2, dff, gr["ln_ffn_g"][l], gr["ln_ffn_b"][l] = _ln(f"ln_ffn{l}", h2, ff, p["ln_ffn_g"][l:l + 1], p["ln_ffn_b"][l:l + 1], cts=dh3)
    dact = matmul(f"ffn_d_dx{l}", dff, p["ffn_w_down"][l], tb=True)
    gr["ffn_w_down"][l] = matmul(f"ffn_d_dw{l}", act, dff, ta=True)
    dgg, duu = _rows(f"swiglu_bwd{l}", swiglu_f, [gg, uu], [], None, TM, fwd=False, cts=[dact], wrt_rows=(0, 1))
    gr["ffn_w_gate"][l] = matmul(f"ffn_g_dw{l}", h2, dgg, ta=True)
    gr["ffn_w_up"][l] = matmul(f"ffn_u_dw{l}", h2, duu, ta=True)
    dh2 = matmul(f"ffn_g_dx{l}", dgg, p["ffn_w_gate"][l], tb=True, add=dh2)
    dh2 = matmul(f"ffn_u_dx{l}", duu, p["ffn_w_up"][l], tb=True, add=dh2)
    dh, dxa, gr["ln_xa_g"][l], gr["ln_xa_b"][l] = _ln(f"ln_xa{l}", h, xa, p["ln_xa_g"][l:l + 1], p["ln_xa_b"][l:l + 1], cts=dh2)
    dox = matmul(f"xa_o_dx{l}", dxa, p["xa_w_o"][l], tb=True)
    gr["xa_w_o"][l] = matmul(f"xa_o_dw{l}", ox, dxa, ta=True)
    dqx, dkx, dvx = xattn(bsz, qx, kx, vx, cts=dox)
    gr["xa_w_q"][l] = matmul(f"xa_q_dw{l}", h, dqx, ta=True)
    gr["xa_w_k"][l] = matmul(f"xa_k_dw{l}", mem2, dkx, ta=True)
    gr["xa_w_v"][l] = matmul(f"xa_v_dw{l}", mem2, dvx, ta=True)
    return matmul(f"xa_q_dx{l}", dqx, p["xa_w_q"][l], tb=True, add=dh)


def _uq_perm(w):
    w3 = w.reshape(w.shape[0], 8, 192)
    return jnp.concatenate([w3[:, :, :128].reshape(-1, 1024), w3[:, :, 128:160].reshape(-1, 256),
                            w3[:, :, 160:].reshape(-1, 256)], axis=1)


def _uq_unperm(g):
    r = g.shape[0]
    return jnp.concatenate([g[:, :1024].reshape(r, 8, 128), g[:, 1024:1280].reshape(r, 8, 32),
                            g[:, 1280:].reshape(r, 8, 32)], axis=2).reshape(r, 1536)


def _ukv_perm(w):
    w3 = w.reshape(w.shape[0], 8, 256)
    return jnp.concatenate([w3[:, :, :128].reshape(-1, 1024), w3[:, :, 128:].reshape(-1, 1024)], axis=1)


def _ukv_unperm(g):
    r = g.shape[0]
    return jnp.concatenate([g[:, :1024].reshape(r, 8, 128), g[:, 1024:].reshape(r, 8, 128)], axis=2).reshape(r, 2048)


def _pad_cols(w, n):
    return jnp.pad(w, ((0, 0), (0, n - w.shape[1])))


def _shift_prev(x, bsz):
    t, w = x.shape
    x3 = x.reshape(bsz, t // bsz, w)
    return jnp.pad(x3, ((0, 0), (1, 0), (0, 0)))[:, :-1].reshape(t, w)


def _shift_next(x, bsz):
    t, w = x.shape
    x3 = x.reshape(bsz, t // bsz, w)
    return jnp.pad(x3[:, 1:], ((0, 0), (0, 1), (0, 0))).reshape(t, w)


def device_step(x, mem, tgt, p):
    bsz, seq, d = x.shape
    t = bsz * seq
    x2, mem2, tgt2 = x.reshape(t, d), mem.reshape(bsz * mem.shape[1], d), tgt.reshape(t, d)
    gr = {k: [None] * DEPTH for k in ("ln_mix_g", "ln_mix_b", "xa_w_q", "xa_w_k", "xa_w_v", "xa_w_o", "ln_xa_g", "ln_xa_b",
                                      "ffn_w_gate", "ffn_w_up", "ffn_w_down", "ln_ffn_g", "ln_ffn_b")}
    cos_pe, sin_pe = _rope_tables(seq, 64)
    cos_c, sin_c = _rope_tables(seq, 128)
    cq, sq = jnp.tile(cos_pe, (bsz, 8)), jnp.tile(sin_pe, (bsz, 8))
    ck, sk = jnp.tile(cos_pe, (bsz, 1)), jnp.tile(sin_pe, (bsz, 1))
    cd, sd = jnp.tile(cos_c, (bsz, 8)), jnp.tile(sin_c, (bsz, 8))

    w_in0 = _pad_cols(p["ev_w_in"][0], EV_PAD)
    w_uq, w_ukv = _uq_perm(p["ev_mla_w_uq"][0]), _ukv_perm(p["ev_mla_w_ukv"][0])
    z0 = matmul("ev_in", x2, w_in0)
    c_q, c_kv, k_pe, q_g, k_g, v_g, r_g, lr_g = (z0[:, a:b] for a, b in _offsets(EV_W))
    qr = _rows("q_rms", rms_f, [c_q], [p["ev_mla_q_norm"]], [512], TM)[0]
    kvr = _rows("kv_rms", rms_f, [c_kv], [p["ev_mla_kv_norm"]], [256], TM)[0]
    q = matmul("mla_uq", qr, w_uq)
    kv = matmul("mla_ukv", kvr, w_ukv)
    qn, qp1, qp2 = q[:, :1024], q[:, 1024:1280], q[:, 1280:]
    kn, vv = kv[:, :1024], kv[:, 1024:]
    kp1, kp2 = k_pe[:, :32], k_pe[:, 32:]
    r1, r2 = _rows("rope_q", rope_f, [qp1, qp2, cq, sq], [], [256, 256], TM)
    kr1, kr2 = _rows("rope_k", rope_f, [kp1, kp2, ck, sk], [], [32, 32], TM)
    r1h, r2h = _to_heads(r1, bsz, 8), _to_heads(r2, bsz, 8)
    a_out = mla_attn(bsz, qn, r1h, r2h, kn, kr1, kr2, vv)
    gla_prm = (p["ev_gla_w_gate2"][0], p["ev_gla_b_gate"], p["ev_gla_norm_g"], p["ev_gla_norm_b"])
    b_out, gla_sv = gla_block(bsz, q_g, k_g, v_g, r_g, lr_g, *gla_prm)
    mixin0 = jnp.concatenate([a_out, b_out], axis=1)
    mix0 = matmul("ev_out", mixin0, p["ev_w_out"][0])
    h1 = _ln("ln_mix0", x2, mix0, p["ln_mix_g"][0:1], p["ln_mix_b"][0:1])
    h3, tail0 = _tail_fwd(0, h1, mem2, bsz, p)

    w_in1 = _pad_cols(p["od_w_in"][0], OD_PAD)
    z1 = matmul("od_in", h3, w_in1)
    dq_, dk_, dv_ = z1[:, :1024], z1[:, 1024:2048], z1[:, 2048:3072]
    d_in = z1[:, 3072:OD_IN]
    q1, q2 = _halves(dq_, 8)
    k1, k2 = _halves(dk_, 8)
    qd1, qd2 = _rows("rope_dq", rope_f, [q1, q2, cd, sd], [], [512, 512], TM)
    kd1, kd2 = _rows("rope_dk", rope_f, [k1, k2, cd, sd], [], [512, 512], TM)
    c_out = dil_block(bsz, qd1, qd2, kd1, kd2, dv_)
    d_prev = _shift_prev(d_in, bsz)
    mu = p["od_rwkv_mu"]
    ds = _rows("tshift", tshift_f, [d_in, d_prev], [mu], [d_in.shape[1]], TM)[0]
    rw_in = tuple(ds[:, a:b] for a, b in _offsets((1024, 1024, 1024, 96, 96, 256)))
    rw_prm = dict(w0=p["od_rwkv_w0"], wd2=p["od_rwkv_w_decay2"][0], a0=p["od_rwkv_a0"], wa2=p["od_rwkv_w_a2"][0],
                  wg2=p["od_rwkv_w_gate2"][0], k_k=p["od_rwkv_k_k"], k_a=p["od_rwkv_k_a"], r_k=p["od_rwkv_r_k"][0],
                  gn_g=p["od_rwkv_gn_g"], gn_b=p["od_rwkv_gn_b"])
    d_out = rwkv_block(bsz, *rw_in, rw_prm)
    mixin1 = jnp.concatenate([c_out, d_out], axis=1)
    mix1 = matmul("od_out", mixin1, p["od_w_out"][0])
    h4 = _ln("ln_mix1", h3, mix1, p["ln_mix_g"][1:2], p["ln_mix_b"][1:2])
    y, tail1 = _tail_fwd(1, h4, mem2, bsz, p)

    dy, row_loss = _rows("loss", loss_f, [y, tgt2], [], [d, 1], TM)
    loss = jnp.sum(row_loss)

    dh4 = _tail_bwd(1, dy, tail1, mem2, bsz, p, gr)
    dh3, dmix1, gr["ln_mix_g"][1], gr["ln_mix_b"][1] = _ln("ln_mix1", h3, mix1, p["ln_mix_g"][1:2], p["ln_mix_b"][1:2], cts=dh4)
    dmixin1 = matmul("od_out_dx", dmix1, p["od_w_out"][0], tb=True)
    gr["od_w_out"] = matmul("od_out_dw", mixin1, dmix1, ta=True)[None]
    dc_out, dd_out = dmixin1[:, :1024], dmixin1[:, 1024:]
    drw_in, drw_prm = rwkv_block(bsz, *rw_in, rw_prm, cts=dd_out)
    dds = jnp.concatenate(drw_in, axis=1)
    dd_in, dd_prev, dmu = _rows("tshift_bwd", tshift_f, [d_in, d_prev], [mu], None, TM, fwd=False, cts=[dds],
                                wrt_rows=(0, 1), wrt_params=(0,))
    dd_in = dd_in + _shift_next(dd_prev, bsz)
    dqd1, dqd2, dkd1, dkd2, ddv = dil_block(bsz, qd1, qd2, kd1, kd2, dv_, cts=dc_out)
    dq1, dq2 = _rows("rope_dq_bwd", rope_f, [q1, q2, cd, sd], [], None, TM, fwd=False, cts=[dqd1, dqd2], wrt_rows=(0, 1))
    dk1, dk2 = _rows("rope_dk_bwd", rope_f, [k1, k2, cd, sd], [], None, TM, fwd=False, cts=[dkd1, dkd2], wrt_rows=(0, 1))
    dz1 = jnp.concatenate([_unhalves(dq1, dq2, 8), _unhalves(dk1, dk2, 8), ddv, dd_in,
                           jnp.zeros((t, OD_PAD - OD_IN), F32)], axis=1)
    gr["od_w_in"] = matmul("od_in_dw", h3, dz1, ta=True)[:, :OD_IN][None]
    dh3 = matmul("od_in_dx", dz1, w_in1, tb=True, add=dh3)
    gr["od_rwkv_mu"] = dmu
    gr["od_rwkv_w0"], gr["od_rwkv_w_decay2"], gr["od_rwkv_a0"] = drw_prm["w0"], drw_prm["wd2"][None], drw_prm["a0"]
    gr["od_rwkv_w_a2"], gr["od_rwkv_w_gate2"] = drw_prm["wa2"][None], drw_prm["wg2"][None]
    gr["od_rwkv_k_k"], gr["od_rwkv_k_a"], gr["od_rwkv_r_k"] = drw_prm["k_k"], drw_prm["k_a"], drw_prm["r_k"][None]
    gr["od_rwkv_gn_g"], gr["od_rwkv_gn_b"] = drw_prm["gn_g"], drw_prm["gn_b"]

    dh1 = _tail_bwd(0, dh3, tail0, mem2, bsz, p, gr)
    dx2, dmix0, gr["ln_mix_g"][0], gr["ln_mix_b"][0] = _ln("ln_mix0", x2, mix0, p["ln_mix_g"][0:1], p["ln_mix_b"][0:1], cts=dh1)
    dmixin0 = matmul("ev_out_dx", dmix0, p["ev_w_out"][0], tb=True)
    gr["ev_w_out"] = matmul("ev_out_dw", mixin0, dmix0, ta=True)[None]
    da_out, db_out = dmixin0[:, :1024], dmixin0[:, 1024:]
    dq_g, dk_g, dv_g, dlr4, dr_g, dw2, dbg, dng, dnb = gla_block(bsz, q_g, k_g, v_g, r_g, lr_g, *gla_prm, cts=(db_out, gla_sv))
    dlr_g = jnp.sum(dlr4, axis=0)
    dqn, dr1h, dr2h, dkn, dkr1, dkr2, dvv = mla_attn(bsz, qn, r1h, r2h, kn, kr1, kr2, vv, cts=da_out)
    dqp1, dqp2 = _rows("rope_q_bwd", rope_f, [qp1, qp2, cq, sq], [], None, TM, fwd=False,
                       cts=[_from_heads(dr1h), _from_heads(dr2h)], wrt_rows=(0, 1))
    dkp1, dkp2 = _rows("rope_k_bwd", rope_f, [kp1, kp2, ck, sk], [], None, TM, fwd=False, cts=[dkr1, dkr2], wrt_rows=(0, 1))
    dq = jnp.concatenate([dqn, dqp1, dqp2], axis=1)
    dkv = jnp.concatenate([dkn, dvv], axis=1)
    dqr = matmul("mla_uq_dx", dq, w_uq, tb=True)
    gr["ev_mla_w_uq"] = _uq_unperm(matmul("mla_uq_dw", qr, dq, ta=True))[None]
    dkvr = matmul("mla_ukv_dx", dkv, w_ukv, tb=True)
    gr["ev_mla_w_ukv"] = _ukv_unperm(matmul("mla_ukv_dw", kvr, dkv, ta=True))[None]
    dc_q, gr["ev_mla_q_norm"] = _rows("q_rms_bwd", rms_f, [c_q], [p["ev_mla_q_norm"]], None, TM, fwd=False, cts=[dqr],
                                      wrt_rows=(0,), wrt_params=(0,))
    dc_kv, gr["ev_mla_kv_norm"] = _rows("kv_rms_bwd", rms_f, [c_kv], [p["ev_mla_kv_norm"]], None, TM, fwd=False, cts=[dkvr],
                                        wrt_rows=(0,), wrt_params=(0,))
    dz0 = jnp.concatenate([dc_q, dc_kv, dkp1, dkp2, dq_g, dk_g, dv_g, dr_g, dlr_g,
                           jnp.zeros((t, EV_PAD - EV_IN), F32)], axis=1)
    gr["ev_w_in"] = matmul("ev_in_dw", x2, dz0, ta=True)[:, :EV_IN][None]
    dx2 = matmul("ev_in_dx", dz0, w_in0, tb=True, add=dx2)
    gr["ev_gla_w_gate2"], gr["ev_gla_b_gate"] = dw2[None], dbg
    gr["ev_gla_norm_g"], gr["ev_gla_norm_b"] = dng, dnb
    for k in list(gr):
        if isinstance(gr[k], list):
            gr[k] = jnp.stack([g[0] if k.startswith("ln_") else g for g in gr[k]])
    return loss, dx2.reshape(bsz, seq, d), gr


WEIGHTS = ['ev_w_in', 'ev_mla_q_norm', 'ev_mla_w_uq', 'ev_mla_kv_norm', 'ev_mla_w_ukv', 'ev_gla_w_gate2', 'ev_gla_b_gate',
           'ev_gla_norm_g', 'ev_gla_norm_b', 'ev_w_out', 'od_w_in', 'od_rwkv_mu', 'od_rwkv_w0', 'od_rwkv_w_decay2',
           'od_rwkv_a0', 'od_rwkv_w_a2', 'od_rwkv_w_gate2', 'od_rwkv_k_k', 'od_rwkv_k_a', 'od_rwkv_r_k', 'od_rwkv_gn_g',
           'od_rwkv_gn_b', 'od_w_out', 'ln_mix_g', 'ln_mix_b', 'xa_w_q', 'xa_w_k', 'xa_w_v', 'xa_w_o', 'ln_xa_g', 'ln_xa_b',
           'ffn_w_gate', 'ffn_w_up', 'ffn_w_down', 'ln_ffn_g', 'ln_ffn_b']
BIG = {'ev_w_in': -1, 'ev_mla_w_uq': -1, 'ev_mla_w_ukv': -1, 'ev_w_out': -2, 'od_w_in': -1, 'od_w_out': -2,
       'xa_w_q': -2, 'xa_w_k': -2, 'xa_w_v': -2, 'xa_w_o': -2, 'ffn_w_gate': -1, 'ffn_w_up': -1, 'ffn_w_down': -2}
SMALL = ['ev_gla_w_gate2', 'od_rwkv_mu', 'od_rwkv_w0', 'od_rwkv_w_decay2', 'od_rwkv_a0', 'od_rwkv_w_a2', 'od_rwkv_w_gate2',
         'od_rwkv_k_k', 'od_rwkv_k_a', 'od_rwkv_gn_g', 'od_rwkv_gn_b']
REPL = ['ev_mla_q_norm', 'ev_mla_kv_norm', 'ev_gla_b_gate', 'ev_gla_norm_g', 'ev_gla_norm_b', 'od_rwkv_r_k',
        'ln_mix_g', 'ln_mix_b', 'ln_xa_g', 'ln_xa_b', 'ln_ffn_g', 'ln_ffn_b']
PACK_COLS = 128


def _unshard(g, shape, axis):
    axis %= len(shape)
    full = list(shape)
    full[axis] *= N_DEV
    return jnp.moveaxis(g, 0, axis).reshape(full)


def _shard_stack(gfull, axis):
    axis %= gfull.ndim
    shp = list(gfull.shape)
    shp[axis:axis + 1] = [N_DEV, shp[axis] // N_DEV]
    return jnp.moveaxis(gfull.reshape(shp), axis, 0)


def _pack(arrs):
    lead = arrs[0].shape[0]
    flat = jnp.concatenate([a.reshape(lead, -1) for a in arrs], axis=1)
    n = flat.shape[1]
    rows = -(-n // (8 * PACK_COLS)) * 8
    return jnp.pad(flat, ((0, 0), (0, rows * PACK_COLS - n))).reshape(lead, rows, PACK_COLS)


def _unpack(buf, shapes):
    lead = buf.shape[0]
    flat = buf.reshape(lead, -1)
    out, off = [], 0
    for shp in shapes:
        n = math.prod(shp)
        out.append(flat[:, off:off + n].reshape((lead,) + tuple(shp)))
        off += n
    return out


def train_step(x, mem, loss_target, w, m, v):
    p = {}
    for k, ax in BIG.items():
        p[k] = _unshard(all_gather_2level("ag_" + k, w[k].astype(BF16)), w[k].shape, ax)
    small_loc = _pack([w[k][None] for k in SMALL])[0]
    small_all = _unpack(all_gather("ag_small", small_loc), [w[k].shape for k in SMALL])
    for k, g in zip(SMALL, small_all):
        p[k] = _unshard(g, w[k].shape, -1)
    for k in REPL:
        p[k] = w[k]
    loss, dx, gr = device_step(x, mem, loss_target, p)
    loss = lax.psum(loss, ("x", "y", "c"))

    stacks = {}
    for k, ax in BIG.items():
        stacks[k] = reduce_scatter_2level("rs_" + k, _shard_stack(gr[k], ax))
    small_send = _pack([_shard_stack(gr[k], -1) for k in SMALL])
    small_recv = _unpack(reduce_scatter_exchange("rs_small", small_send), [w[k].shape for k in SMALL])
    stacks.update(zip(SMALL, small_recv))
    repl_loc = _pack([gr[k][None] for k in REPL])[0]
    repl_all = _unpack(all_gather("ag_repl_grads", repl_loc), [w[k].shape for k in REPL])
    stacks.update(zip(REPL, repl_all))

    grads, deltas, new_m, new_v = [], [], [], []
    for k in WEIGHTS:
        g, dl, mn, vn = adamw("adamw_" + k, w[k], m[k], v[k], stacks[k])
        grads.append(g), deltas.append(dl), new_m.append(mn), new_v.append(vn)
    return (loss, dx, *grads, *deltas, *new_m, *new_v)


def kernel(x, mem, ev_w_in, ev_mla_q_norm, ev_mla_w_uq, ev_mla_kv_norm, ev_mla_w_ukv, ev_gla_w_gate2, ev_gla_b_gate, ev_gla_norm_g, ev_gla_norm_b, ev_w_out, od_w_in, od_rwkv_mu, od_rwkv_w0, od_rwkv_w_decay2, od_rwkv_a0, od_rwkv_w_a2, od_rwkv_w_gate2, od_rwkv_k_k, od_rwkv_k_a, od_rwkv_r_k, od_rwkv_gn_g, od_rwkv_gn_b, od_w_out, ln_mix_g, ln_mix_b, xa_w_q, xa_w_k, xa_w_v, xa_w_o, ln_xa_g, ln_xa_b, ffn_w_gate, ffn_w_up, ffn_w_down, ln_ffn_g, ln_ffn_b, loss_target, m_ev_w_in, m_ev_mla_q_norm, m_ev_mla_w_uq, m_ev_mla_kv_norm, m_ev_mla_w_ukv, m_ev_gla_w_gate2, m_ev_gla_b_gate, m_ev_gla_norm_g, m_ev_gla_norm_b, m_ev_w_out, m_od_w_in, m_od_rwkv_mu, m_od_rwkv_w0, m_od_rwkv_w_decay2, m_od_rwkv_a0, m_od_rwkv_w_a2, m_od_rwkv_w_gate2, m_od_rwkv_k_k, m_od_rwkv_k_a, m_od_rwkv_r_k, m_od_rwkv_gn_g, m_od_rwkv_gn_b, m_od_w_out, m_ln_mix_g, m_ln_mix_b, m_xa_w_q, m_xa_w_k, m_xa_w_v, m_xa_w_o, m_ln_xa_g, m_ln_xa_b, m_ffn_w_gate, m_ffn_w_up, m_ffn_w_down, m_ln_ffn_g, m_ln_ffn_b, v_ev_w_in, v_ev_mla_q_norm, v_ev_mla_w_uq, v_ev_mla_kv_norm, v_ev_mla_w_ukv, v_ev_gla_w_gate2, v_ev_gla_b_gate, v_ev_gla_norm_g, v_ev_gla_norm_b, v_ev_w_out, v_od_w_in, v_od_rwkv_mu, v_od_rwkv_w0, v_od_rwkv_w_decay2, v_od_rwkv_a0, v_od_rwkv_w_a2, v_od_rwkv_w_gate2, v_od_rwkv_k_k, v_od_rwkv_k_a, v_od_rwkv_r_k, v_od_rwkv_gn_g, v_od_rwkv_gn_b, v_od_w_out, v_ln_mix_g, v_ln_mix_b, v_xa_w_q, v_xa_w_k, v_xa_w_v, v_xa_w_o, v_ln_xa_g, v_ln_xa_b, v_ffn_w_gate, v_ffn_w_up, v_ffn_w_down, v_ln_ffn_g, v_ln_ffn_b):
    given = dict(locals())
    w = {k: given[k] for k in WEIGHTS}
    m = {k: given["m_" + k] for k in WEIGHTS}
    v = {k: given["v_" + k] for k in WEIGHTS}
    return train_step(given["x"], given["mem"], given["loss_target"], w, m, v)
```

```python
import functools
import math

import jax
import jax.numpy as jnp
from jax import lax
from jax.experimental import pallas as pl
from jax.experimental.pallas import tpu as pltpu

F32 = jnp.float32
BF16 = jnp.bfloat16
VMEM_LIMIT = 56 * 1024 * 1024
ROWS_VMEM = 20 * 1024 * 1024

N_DEV = 8
DEPTH = 2
ALPHA = (2.0 * DEPTH) ** 0.25
LN_EPS = 1e-5
RMS_EPS = 1e-6
RWKV_GN_EPS = 64e-5
ADAM_LR, ADAM_B1, ADAM_B2, ADAM_EPS, ADAM_WD, ADAM_STEP = 0.001, 0.9, 0.999, 1e-08, 0.01, 10
NEG_INF = float("-inf")


def _pcall(body, **kw):
    return pl.pallas_call(body, **kw)


def _cparams(**kw):
    return pltpu.CompilerParams(vmem_limit_bytes=VMEM_LIMIT, **kw)


def _dg(a, b, ca, cb, batch):
    nb = 1 if batch else 0
    dims = (((ca + nb,), (cb + nb,)), ((0,), (0,)) if batch else ((), ()))
    return lax.dot_general(a.astype(BF16), b.astype(BF16), dims, preferred_element_type=F32)


@functools.partial(jax.custom_vjp, nondiff_argnums=(2, 3, 4))
def _mm(a, b, ta, tb, batch):
    return _dg(a, b, 0 if ta else 1, 1 if tb else 0, batch)


def _mm_fwd(a, b, ta, tb, batch):
    return _mm(a, b, ta, tb, batch), (a, b)


def _mm_bwd(ta, tb, batch, res, g):
    a, b = res
    if not ta and not tb:
        da, db = _mm(g, b, False, True, batch), _mm(a, g, True, False, batch)
    elif not ta and tb:
        da, db = _mm(g, b, False, False, batch), _mm(g, a, True, False, batch)
    elif ta and not tb:
        da, db = _mm(b, g, False, True, batch), _mm(a, g, False, False, batch)
    else:
        da, db = _mm(b, g, True, True, batch), _mm(g, a, True, True, batch)
    return da.astype(a.dtype), db.astype(b.dtype)


_mm.defvjp(_mm_fwd, _mm_bwd)


def mm(a, b, ta=False, tb=False):
    return _mm(a, b, ta, tb, a.ndim == 3)


def _bs(block, imap):
    return pl.BlockSpec(block, imap)


def _rev_imap(imap, n):
    def r(*idx):
        return imap(*idx[:-1], n - 1 - idx[-1])
    return r


def p_fwd(name, f, grid, ins, outs, carry=None, save_carry=None):
    n_in, n_out = len(ins), len(outs)

    def body(*refs):
        in_refs = refs[:n_in]
        out_refs = refs[n_in:n_in + n_out]
        rest = refs[n_in + n_out:]
        vals = [r[...] for r in in_refs]
        if carry is None:
            res = f(*vals)
        else:
            if save_carry is not None:
                sv_ref, c_ref = rest
            else:
                (c_ref,) = rest

            @pl.when(pl.program_id(len(grid) - 1) == 0)
            def _():
                c_ref[...] = jnp.zeros(c_ref.shape, c_ref.dtype)

            c = c_ref[...]
            if save_carry is not None:
                sv_ref[...] = c
            res = f(c, *vals)
            c_ref[...] = res[0]
            res = res[1:]
        if not isinstance(res, (tuple, list)):
            res = (res,)
        for r, v in zip(out_refs, res):
            r[...] = v.astype(r.dtype)

    out_shape = [jax.ShapeDtypeStruct(s, d) for (s, d, _, _) in outs]
    out_specs = [_bs(b, m) for (_, _, b, m) in outs]
    scratch = []
    if carry is not None:
        if save_carry is not None:
            out_shape.append(jax.ShapeDtypeStruct(save_carry[0], carry[1]))
            out_specs.append(_bs(save_carry[1], save_carry[2]))
        scratch.append(pltpu.VMEM(carry[0], carry[1]))
    return _pcall(
        body, name=name, grid=grid,
        in_specs=[_bs(b, m) for (_, b, m) in ins],
        out_specs=out_specs, out_shape=out_shape, scratch_shapes=scratch,
        compiler_params=_cparams(),
    )(*[a for (a, _, _) in ins])


def p_bwd(name, f, grid, ins, cts, wrt, carry=None, saved=None):
    n_in, n_ct, n_w = len(ins), len(cts), len(wrt)
    rev = carry is not None
    n_last = grid[-1]

    def fix(imap):
        return _rev_imap(imap, n_last) if rev else imap

    def body(*refs):
        in_refs = refs[:n_in]
        ct_refs = refs[n_in:n_in + n_ct]
        k = n_in + n_ct
        if rev:
            sv_ref = refs[k]
            k += 1
        out_refs = refs[k:k + n_w]
        rest = refs[k + n_w:]
        vals = [r[...] for r in in_refs]
        ct_vals = [r[...].astype(F32) for r in ct_refs]
        widx = [w[0] for w in wrt]

        if rev:
            (dc_ref,) = rest

            @pl.when(pl.program_id(len(grid) - 1) == 0)
            def _():
                dc_ref[...] = jnp.zeros(dc_ref.shape, dc_ref.dtype)

            c_in = sv_ref[...]

            def g(c, *dv):
                full = list(vals)
                for i, d in zip(widx, dv):
                    full[i] = d
                return tuple(f(c, *full))

            _, vjp = jax.vjp(g, c_in, *[vals[i] for i in widx])
            grads = vjp((dc_ref[...],) + tuple(ct_vals))
            dc_ref[...] = grads[0]
            grads = grads[1:]
        else:
            def g(*dv):
                full = list(vals)
                for i, d in zip(widx, dv):
                    full[i] = d
                r = f(*full)
                return tuple(r) if isinstance(r, (tuple, list)) else (r,)

            _, vjp = jax.vjp(g, *[vals[i] for i in widx])
            grads = vjp(tuple(ct_vals))

        for w, o_ref, gr in zip(wrt, out_refs, grads):
            acc = w[1]
            if acc is None:
                o_ref[...] = gr.astype(o_ref.dtype)
            else:
                first = None
                for ax in acc:
                    c0 = pl.program_id(ax) == 0
                    first = c0 if first is None else jnp.logical_and(first, c0)

                @pl.when(first)
                def _():
                    o_ref[...] = jnp.zeros(o_ref.shape, o_ref.dtype)

                o_ref[...] += gr.astype(o_ref.dtype)

    in_specs = [_bs(b, fix(m)) for (_, b, m) in ins] + [_bs(b, fix(m)) for (_, b, m) in cts]
    args = [a for (a, _, _) in ins] + [a for (a, _, _) in cts]
    if rev:
        in_specs.append(_bs(saved[1], fix(saved[2])))
        args.append(saved[0])
    out_shape, out_specs = [], []
    for w in wrt:
        a, b, m = ins[w[0]]
        if len(w) > 2 and w[2] is not None:
            m = w[2]
        out_shape.append(jax.ShapeDtypeStruct(a.shape, w[3] if len(w) > 3 else F32))
        out_specs.append(_bs(b, fix(m)))
    scratch = [pltpu.VMEM(carry[0], carry[1])] if rev else []
    return _pcall(
        body, name=name, grid=grid, in_specs=in_specs, out_specs=out_specs,
        out_shape=out_shape, scratch_shapes=scratch, compiler_params=_cparams(),
    )(*args)


def _rows(name, f, row_ins, params, out_widths, tm, fwd=True, cts=None, wrt_rows=(), wrt_params=(), dtypes=None):
    t = row_ins[0].shape[0]
    width = sum(a.shape[1] for a in row_ins)
    width += sum(out_widths) if fwd else sum(c.shape[1] for c in cts) + sum(row_ins[i].shape[1] for i in wrt_rows)
    tm = min(tm, t)
    while tm > 8 and 2 * 4 * tm * width > ROWS_VMEM:
        tm //= 2
    rmap = lambda i: (i, 0)
    pmap = lambda i: (0, 0)
    ins = [(a, (tm, a.shape[1]), rmap) for a in row_ins] + [(p, p.shape, pmap) for p in params]
    if fwd:
        dtypes = dtypes or [F32] * len(out_widths)
        outs = [((t, w), dt, (tm, w), rmap) for w, dt in zip(out_widths, dtypes)]
        return p_fwd(name, f, (t // tm,), ins, outs)
    ct_specs = [(c, (tm, c.shape[1]), rmap) for c in cts]
    dtypes = dtypes or [F32] * len(wrt_rows)
    wrt = [(i, None, None, dt) for i, dt in zip(wrt_rows, dtypes)] + [(len(row_ins) + i, (0,)) for i in wrt_params]
    return p_bwd(name, f, (t // tm,), ins, ct_specs, wrt)


def _pick(n, cands):
    for c in cands:
        if n % c == 0:
            return c
    return n


def _wide(n, cap=1664):
    best = None
    for w in range(128, min(n, cap) + 1, 128):
        if n % w == 0:
            best = w
    return best or n


MM_VMEM = 40 * 1024 * 1024


def _mm_tiles(m, n, k, sa, sb, so, has_add):
    bm, bn = _pick(m, (512, 256, 128)), _wide(n)
    bk = k if k <= 2048 else _wide(k, 3328)

    def vmem(bm, bn, bk):
        acc = 0 if bk == k else 4 * bm * bn
        return 2 * (bm * bk * sa + bk * bn * sb + bm * bn * so * (2 if has_add else 1)) + acc

    while vmem(bm, bn, bk) > MM_VMEM and bk % 256 == 0:
        bk //= 2
    while vmem(bm, bn, bk) > MM_VMEM and bn % 256 == 0:
        bn //= 2
    return bm, bn, bk


def matmul(name, a, b, ta=False, tb=False, out_dtype=F32, add=None):
    m = a.shape[1] if ta else a.shape[0]
    k = a.shape[0] if ta else a.shape[1]
    n = b.shape[0] if tb else b.shape[1]
    assert (b.shape[1] if tb else b.shape[0]) == k, (a.shape, b.shape, ta, tb)
    bm, bn, bk = _mm_tiles(m, n, k, a.dtype.itemsize, b.dtype.itemsize, jnp.dtype(out_dtype).itemsize, add is not None)
    nk = k // bk

    def body(a_ref, b_ref, *rest):
        o_ref, acc_ref = rest[-2:]
        if nk == 1:
            r = mm(a_ref[...], b_ref[...], ta, tb)
            o_ref[...] = (r if add is None else r + rest[0][...].astype(F32)).astype(o_ref.dtype)
            return

        @pl.when(pl.program_id(2) == 0)
        def _():
            acc_ref[...] = jnp.zeros(acc_ref.shape, F32) if add is None else rest[0][...].astype(F32)

        acc_ref[...] += mm(a_ref[...], b_ref[...], ta, tb)

        @pl.when(pl.program_id(2) == nk - 1)
        def _():
            o_ref[...] = acc_ref[...].astype(o_ref.dtype)

    a_spec = _bs((bk, bm), lambda i, j, l: (l, i)) if ta else _bs((bm, bk), lambda i, j, l: (i, l))
    b_spec = _bs((bn, bk), lambda i, j, l: (j, l)) if tb else _bs((bk, bn), lambda i, j, l: (l, j))
    o_spec = _bs((bm, bn), lambda i, j, l: (i, j))
    return _pcall(
        body, name=name, grid=(m // bm, n // bn, nk),
        in_specs=[a_spec, b_spec] + ([] if add is None else [o_spec]), out_specs=o_spec,
        out_shape=jax.ShapeDtypeStruct((m, n), out_dtype),
        scratch_shapes=[pltpu.VMEM((bm, bn) if nk > 1 else (8, 128), F32)],
        compiler_params=_cparams(dimension_semantics=("parallel", "parallel", "arbitrary")),
    )(*((a, b) if add is None else (a, b, add)))


def _to_heads(x, b, h):
    t, w = x.shape
    return x.reshape(b, t // b, h, w // h).transpose(0, 2, 1, 3)


def _from_heads(x):
    b, h, s, d = x.shape
    return x.transpose(0, 2, 1, 3).reshape(b * s, h * d)


RW_STEPS = 8
RW_G = 16


def rwkv_group(st, w8, kk8, ka8, kh8, r8, vc):
    ys = []
    for t in range(RW_STEPS):
        row = lambda x: x[:, t:t + 1, :]
        sa = jnp.sum(st * row(kk8), axis=2, keepdims=True)
        st = st * row(w8) - sa * row(ka8) + vc[:, :, t:t + 1] * row(kh8)
        ys.append(jnp.sum(st * row(r8), axis=2, keepdims=True))
    return st, jnp.concatenate(ys, axis=2)


def rwkv_prehead_f(kkraw, a):
    nrm = jnp.sqrt(jnp.sum(kkraw * kkraw, axis=-1, keepdims=True))
    kk = kkraw / jnp.maximum(nrm, 1e-12)
    return kk, kk * a


def rwkv_pre_f(kd, w_lr, a_lr, g_lr, w0, wd2, a0, wa2, wg2, k_k, k_a):
    wpre = w0 + mm(jnp.tanh(w_lr), wd2)
    w = -jax.nn.softplus(-wpre) - 0.5
    decay = jnp.exp(-jnp.exp(w))
    a = jax.nn.sigmoid(a0 + mm(a_lr, wa2))
    g = mm(jax.nn.sigmoid(g_lr), wg2)
    kkraw = kd * k_k
    kh = kd * (1.0 + (a - 1.0) * k_a)
    return decay, a, g, kkraw, kh


def rwkv_post_f(y, r, kh, v, g, gn_g, gn_b, r_k):
    mu = jnp.mean(y, axis=-1, keepdims=True)
    var = jnp.mean(jnp.square(y - mu), axis=-1, keepdims=True)
    yn = (y - mu) * lax.rsqrt(var + RWKV_GN_EPS) * gn_g + gn_b
    bonus = jnp.sum(r * kh * r_k, axis=-1, keepdims=True) * v
    return (yn + bonus) * g


def _to_cols(xh):
    b, h, s, d = xh.shape
    return xh.reshape(b, h, s // RW_STEPS, RW_STEPS, d).transpose(0, 1, 2, 4, 3)


def _from_cols(xc):
    b, h, n, d, k = xc.shape
    return xc.transpose(0, 1, 2, 4, 3).reshape(b, h, n * k, d)


def rwkv_block(bsz, r, kd, vd, w_lr, a_lr, g_lr, prm, cts=None):
    t = r.shape[0]
    hh, n = 16, 64
    tm = 256
    pre_rows = [kd, w_lr, a_lr, g_lr]
    pre_prm = [prm[k] for k in ("w0", "wd2", "a0", "wa2", "wg2", "k_k", "k_a")]
    decay, a, g, kkraw, kh = _rows("rwkv_pre", rwkv_pre_f, pre_rows, pre_prm, [1024] * 5, tm)
    heads = lambda x: _to_heads(x, bsz, hh)
    rh, khh, vh, gh, dech, kkrawh, ah = (heads(x) for x in (r, kh, vd, g, decay, kkraw, a))
    s = t // bsz
    ng = s // RW_STEPS
    ts = 512
    hb = (None, None, ts, n)
    hm = lambda h, b, i: (b, h, i, 0)
    pgrid = (hh, bsz, s // ts)
    ph_ins = [(kkrawh, hb, hm), (ah, hb, hm)]
    kkh, kah = p_fwd("rwkv_prehead", rwkv_prehead_f, pgrid, ph_ins, [((bsz, hh, s, n), F32, hb, hm)] * 2)
    grp = lambda x: x.reshape(bsz, hh, ng, RW_STEPS, n)
    cb, rb, sb = (None, RW_G, None, n, RW_STEPS), (None, RW_G, None, RW_STEPS, n), (None, RW_G, None, n, n)
    cm = lambda b, h, i: (b, h, i, 0, 0)
    sc_ins = [(grp(x), rb, cm) for x in (dech, kkh, kah, khh, rh)] + [(_to_cols(vh), cb, cm)]
    grid = (bsz, hh // RW_G, ng)
    yc, sv = p_fwd("rwkv_scan", rwkv_group, grid, sc_ins, [((bsz, hh, ng, n, RW_STEPS), F32, cb, cm)],
                   carry=((RW_G, n, n), F32), save_carry=((bsz, hh, ng, n, n), sb, cm))
    yh = _from_cols(yc)
    pb = (None, 1, n)
    pm = lambda h, b, i: (h, 0, 0)
    gn_g, gn_b, r_k = (prm[k].reshape(hh, 1, n) for k in ("gn_g", "gn_b", "r_k"))
    post_ins = [(x, hb, hm) for x in (yh, rh, khh, vh, gh)] + [(p, pb, pm) for p in (gn_g, gn_b, r_k)]
    if cts is None:
        (oh,) = p_fwd("rwkv_post", rwkv_post_f, pgrid, post_ins, [((bsz, hh, s, n), BF16, hb, hm)])
        return _from_heads(oh)
    doh = _to_heads(cts, bsz, hh)
    dyh, drh1, dkhh1, dvh1, dgh, dgn_g, dgn_b, dr_k = p_bwd(
        "rwkv_post_bwd", rwkv_post_f, pgrid, post_ins, [(doh, hb, hm)],
        [(i, None) for i in range(5)] + [(5 + i, (1, 2)) for i in range(3)])
    drows_v = p_bwd("rwkv_scan_bwd", rwkv_group, grid, sc_ins, [(_to_cols(dyh), cb, cm)], [(i, None) for i in range(6)],
                    carry=((RW_G, n, n), F32), saved=(sv, sb, cm))
    ddech, dkkh, dkah, dkhh2, drh2 = (x.reshape(bsz, hh, s, n) for x in drows_v[:5])
    dkkrawh, dah = p_bwd("rwkv_prehead_bwd", rwkv_prehead_f, pgrid, ph_ins, [(dkkh, hb, hm), (dkah, hb, hm)],
                         [(0, None), (1, None)])
    ddecay, dkkraw, da = _from_heads(ddech), _from_heads(dkkrawh), _from_heads(dah)
    dv = _from_heads(_from_cols(drows_v[5]) + dvh1)
    dr = _from_heads(drh2 + drh1)
    dkh = _from_heads(dkhh2 + dkhh1)
    dg = _from_heads(dgh)
    res = _rows("rwkv_pre_bwd", rwkv_pre_f, pre_rows, pre_prm, None, tm, fwd=False,
                cts=[ddecay, da, dg, dkkraw, dkh], wrt_rows=(0, 1, 2, 3), wrt_params=tuple(range(7)))
    dkd, dw_lr, da_lr, dg_lr = res[:4]
    dprm = dict(zip(("w0", "wd2", "a0", "wa2", "wg2", "k_k", "k_a"), res[4:]))
    dprm.update(gn_g=dgn_g.reshape(1, -1), gn_b=dgn_b.reshape(1, -1), r_k=dr_k.reshape(hh, n))
    return (dr, dkd, dv, dw_lr, da_lr, dg_lr), dprm


GLA_C, GLA_DK, GLA_DV, GLA_H, GLA_TAU = 64, 128, 256, 4, 16.0


def gla_chunk_f(st, q, k, v, lr, r, w2, bg, ng, nb):
    la = jax.nn.log_sigmoid(mm(lr, w2) + bg) / GLA_TAU
    ri = lax.broadcasted_iota(jnp.int32, (GLA_C, GLA_C), 0)
    ci = lax.broadcasted_iota(jnp.int32, (GLA_C, GLA_C), 1)
    causal = ci <= ri
    b = jnp.dot(causal.astype(F32), la, precision=lax.Precision.HIGHEST, preferred_element_type=F32)
    b_last = jnp.sum(la, axis=0, keepdims=True)
    q_dec = (q * (GLA_DK ** -0.5)) * jnp.exp(b)
    k_inv = k * jnp.exp(-b)
    k_end = k * jnp.exp(b_last - b)
    att = jnp.where(causal, mm(q_dec, k_inv, tb=True), 0.0)
    o = mm(att, v) + mm(q_dec, st, tb=True)
    st_new = st * jnp.exp(b_last) + mm(v, k_end, ta=True)
    mu = jnp.mean(o, axis=-1, keepdims=True)
    var = jnp.mean(jnp.square(o - mu), axis=-1, keepdims=True)
    on = (o - mu) * lax.rsqrt(var + LN_EPS) * ng + nb
    return st_new, on * jax.nn.silu(r)


def gla_block(bsz, q, k, v, r, lr, w2, bg, ng, nb, cts=None):
    t = q.shape[0]
    nc = t // bsz // GLA_C
    grid = (GLA_H, bsz, nc)
    rm = lambda h, b, c: (b * nc + c, h)
    ins = [(q, (GLA_C, GLA_DK), rm), (k, (GLA_C, GLA_DK), rm), (v, (GLA_C, GLA_DV), rm),
           (jnp.broadcast_to(lr[None], (GLA_H,) + lr.shape), (None, GLA_C, lr.shape[1]), lambda h, b, c: (h, b * nc + c, 0)),
           (r, (GLA_C, GLA_DV), rm),
           (w2, (w2.shape[0], GLA_DK), lambda h, b, c: (0, h)), (bg, (1, GLA_DK), lambda h, b, c: (0, h)),
           (ng, (1, GLA_DV), lambda h, b, c: (0, 0)), (nb, (1, GLA_DV), lambda h, b, c: (0, 0))]
    ob = (GLA_C, GLA_DV)
    sshape, sblock = (GLA_H, bsz, nc, GLA_DV, GLA_DK), (None, None, None, GLA_DV, GLA_DK)
    sm = lambda h, b, c: (h, b, c, 0, 0)
    carry = ((GLA_DV, GLA_DK), F32)
    if cts is None:
        out, sv = p_fwd("gla_scan", gla_chunk_f, grid, ins, [((t, GLA_H * GLA_DV), BF16, ob, rm)],
                        carry=carry, save_carry=(sshape, sblock, sm))
        return out, sv
    dout, sv = cts
    return p_bwd("gla_scan_bwd", gla_chunk_f, grid, ins, [(dout, ob, rm)],
                 [(0, None), (1, None), (2, None), (3, None), (4, None), (5, (1, 2)), (6, (1, 2)), (7, (0, 1, 2)), (8, (0, 1, 2))],
                 carry=carry, saved=(sv, sblock, sm))


def _softmax_rows(sc):
    m = lax.stop_gradient(jnp.max(sc, axis=-1, keepdims=True))
    e = jnp.exp(sc - m)
    return e / jnp.sum(e, axis=-1, keepdims=True)


def mla_attn_f(qn, r1, r2, kn, kr1, kr2, v):
    tq, s = qn.shape[0], kn.shape[0]
    sc = (mm(qn, kn, tb=True) + mm(r1, kr1, tb=True) + mm(r2, kr2, tb=True)) * (192.0 ** -0.5)
    qpos = pl.program_id(2) * tq + lax.broadcasted_iota(jnp.int32, (tq, s), 0)
    kpos = lax.broadcasted_iota(jnp.int32, (tq, s), 1)
    sc = jnp.where(kpos <= qpos, sc, NEG_INF)
    return mm(_softmax_rows(sc), v)


def mla_attn(bsz, qn, r1, r2, kn, kr1, kr2, v, cts=None, tq=256):
    t = qn.shape[0]
    s = t // bsz
    nq = s // tq
    hh = 8
    grid = (bsz, hh, nq)
    qm = lambda b, h, i: (b * nq + i, h)
    km_ = lambda b, h, i: (b, h)
    ins = [(qn, (tq, 128), qm),
           (r1, (None, None, tq, 32), lambda b, h, i: (b, h, i, 0)), (r2, (None, None, tq, 32), lambda b, h, i: (b, h, i, 0)),
           (kn, (s, 128), km_), (kr1, (s, 32), lambda b, h, i: (b, 0)), (kr2, (s, 32), lambda b, h, i: (b, 0)),
           (v, (s, 128), km_)]
    if cts is None:
        return p_fwd("mla_attn", mla_attn_f, grid, ins, [((t, 1024), BF16, (tq, 128), qm)])[0]
    return p_bwd("mla_attn_bwd", mla_attn_f, grid, ins, [(cts, (tq, 128), qm)],
                 [(0, None), (1, None), (2, None), (3, (2,)), (4, (1, 2)), (5, (1, 2)), (6, (2,))])


def xattn_f(q, k, v):
    sc = mm(q, k, tb=True) * (512.0 ** -0.5)
    return mm(_softmax_rows(sc), v)


def xattn(bsz, q, k, v, cts=None, tq=512):
    t = q.shape[0]
    nq = t // bsz // tq
    mlen = k.shape[0] // bsz
    grid = (bsz, 4, nq)
    qm = lambda b, h, i: (b * nq + i, h)
    km_ = lambda b, h, i: (b, h)
    ins = [(q, (tq, 512), qm), (k, (mlen, 512), km_), (v, (mlen, 512), km_)]
    if cts is None:
        return p_fwd("xattn", xattn_f, grid, ins, [((t, 2048), BF16, (tq, 512), qm)])[0]
    return p_bwd("xattn_bwd", xattn_f, grid, ins, [(cts, (tq, 512), qm)], [(0, None, None, BF16), (1, (2,)), (2, (2,))])


DIL_SPAN = 128
DIL_BRANCHES = ((128, 1), (512, 4), (2048, 16))


def dil_attn_f(q1, q2, k1c, k2c, vc, k1p, k2p, vp):
    gb, sp = q1.shape[0], DIL_SPAN
    scale = 128.0 ** -0.5
    sc_c = (mm(q1, k1c, tb=True) + mm(q2, k2c, tb=True)) * scale
    sc_p = (mm(q1, k1p, tb=True) + mm(q2, k2p, tb=True)) * scale
    ql = lax.broadcasted_iota(jnp.int32, (gb, sp, sp), 1)
    kl = lax.broadcasted_iota(jnp.int32, (gb, sp, sp), 2)
    has_prev = pl.program_id(1) > 0
    sc_c = jnp.where(kl <= ql, sc_c, NEG_INF)
    sc_p = jnp.where(jnp.logical_and(kl >= ql, has_prev), sc_p, NEG_INF)
    m = lax.stop_gradient(jnp.maximum(jnp.max(sc_c, axis=-1, keepdims=True), jnp.max(sc_p, axis=-1, keepdims=True)))
    e_c, e_p = jnp.exp(sc_c - m), jnp.exp(sc_p - m)
    den = jnp.sum(e_c, axis=-1, keepdims=True) + jnp.sum(e_p, axis=-1, keepdims=True)
    o = mm(e_c / den, vc) + mm(e_p / den, vp)
    return o, m + jnp.log(den)


def dil_branch(q1, q2, k1, k2, v, cts=None, gb=8):
    g, l, _ = q1.shape
    nb = l // DIL_SPAN
    grid = (g // gb, nb)
    cm = lambda i, n: (i, n, 0)
    pm = lambda i, n: (i, jnp.maximum(n - 1, 0), 0)
    b64, b128, b1 = (gb, DIL_SPAN, 64), (gb, DIL_SPAN, 128), (gb, DIL_SPAN, 1)
    ins = [(q1, b64, cm), (q2, b64, cm), (k1, b64, cm), (k2, b64, cm), (v, b128, cm),
           (k1, b64, pm), (k2, b64, pm), (v, b128, pm)]
    if cts is None:
        return p_fwd("dil_attn", dil_attn_f, grid, ins, [((g, l, 128), F32, b128, cm), ((g, l, 1), F32, b1, cm)])
    do, dlse = cts
    dq1, dq2, dk1c, dk2c, dvc, dk1p, dk2p, dvp = p_bwd(
        "dil_attn_bwd", dil_attn_f, grid, ins, [(do, b128, cm), (dlse, b1, cm)],
        [(i, None) for i in range(5)] + [(i, None, cm) for i in (5, 6, 7)])

    def fold(dc, dp):
        return dc + jnp.pad(dp[:, DIL_SPAN:], ((0, 0), (0, DIL_SPAN), (0, 0)))

    return dq1, dq2, fold(dk1c, dk1p), fold(dk2c, dk2p), fold(dvc, dvp)


def dil_mix_f(o1, o2, o3, l1, l2, l3):
    m = lax.stop_gradient(jnp.maximum(jnp.maximum(l1, l2), l3))
    e1, e2, e3 = jnp.exp(l1 - m), jnp.exp(l2 - m), jnp.exp(l3 - m)
    den = e1 + e2 + e3
    return (e1 / den) * o1 + (e2 / den) * o2 + (e3 / den) * o3


def _to_res(xh, dil):
    b, h, s, d = xh.shape
    return xh.reshape(b, h, s // dil, dil, d).transpose(0, 1, 3, 2, 4).reshape(b * h * dil, s // dil, d)


def _from_res(xr, b, h, dil):
    g, l, d = xr.shape
    return xr.reshape(b, h, dil, l, d).transpose(0, 1, 3, 2, 4).reshape(b, h, l * dil, d)


def dil_block(bsz, q1, q2, k1, k2, v, cts=None):
    hh = 8
    heads = [_to_heads(x, bsz, hh) for x in (q1, q2, k1, k2, v)]
    s = heads[0].shape[2]
    outs, res_in = [], []
    for window, dil in DIL_BRANCHES:
        assert window // dil == DIL_SPAN and (s // dil) % DIL_SPAN == 0
        rin = [_to_res(x, dil) for x in heads]
        o, lse = dil_branch(*rin)
        res_in.append(rin)
        outs.append((_from_res(o, bsz, hh, dil), _from_res(lse, bsz, hh, dil)))
    tq = 512
    ob, lb = (None, None, tq, 128), (None, None, tq, 1)
    hm = lambda b, h, i: (b, h, i, 0)
    mix_ins = [(o, ob, hm) for (o, _) in outs] + [(l, lb, hm) for (_, l) in outs]
    grid = (bsz, hh, s // tq)
    if cts is None:
        (mix,) = p_fwd("dil_mix", dil_mix_f, grid, mix_ins, [((bsz, hh, s, 128), BF16, ob, hm)])
        return _from_heads(mix)
    dmix = _to_heads(cts, bsz, hh)
    dml = p_bwd("dil_mix_bwd", dil_mix_f, grid, mix_ins, [(dmix, ob, hm)], [(i, None) for i in range(6)])
    tot = None
    for j, (window, dil) in enumerate(DIL_BRANCHES):
        do, dl = _to_res(dml[j], dil), _to_res(dml[3 + j], dil)
        gr = dil_branch(*res_in[j], cts=(do, dl))
        gr = [_from_res(x, bsz, hh, dil) for x in gr]
        tot = gr if tot is None else [a + b for a, b in zip(tot, gr)]
    return tuple(_from_heads(x) for x in tot)


_ANY = pl.BlockSpec(memory_space=pl.ANY)


def _me_and_peers():
    x, y, c = lax.axis_index("x"), lax.axis_index("y"), lax.axis_index("c")
    me = 4 * x + 2 * y + c
    peers = []
    for k in range(1, N_DEV):
        px = 1 - x if k & 4 else x
        py = 1 - y if k & 2 else y
        pc = 1 - c if k & 1 else c
        peers.append(((px, py, pc), 4 * px + 2 * py + pc))
    return me, peers


def _exchange(name, x, scatter):
    shape = x.shape[1:] if scatter else x.shape

    def body(x_ref, out_ref, send_sems, recv_sems, local_sem):
        me, peers = _me_and_peers()
        src_me = x_ref.at[me] if scatter else x_ref
        local = pltpu.make_async_copy(src_me, out_ref.at[me], local_sem)
        local.start()
        sends = []
        for k, (dev, idx) in enumerate(peers):
            cp = pltpu.make_async_remote_copy(
                src_ref=x_ref.at[idx] if scatter else x_ref, dst_ref=out_ref.at[me],
                send_sem=send_sems.at[k], recv_sem=recv_sems.at[k],
                device_id=dev, device_id_type=pl.DeviceIdType.MESH)
            cp.start()
            sends.append(cp)
        for k, (dev, idx) in enumerate(peers):
            pltpu.make_async_remote_copy(
                src_ref=src_me, dst_ref=out_ref.at[idx], send_sem=send_sems.at[k], recv_sem=recv_sems.at[k],
                device_id=dev, device_id_type=pl.DeviceIdType.MESH).wait_recv()
        for cp in sends:
            cp.wait_send()
        local.wait()

    return _pcall(
        body, name=name, in_specs=[_ANY], out_specs=_ANY,
        out_shape=jax.ShapeDtypeStruct((N_DEV,) + tuple(shape), x.dtype),
        scratch_shapes=[pltpu.SemaphoreType.DMA((N_DEV - 1,)), pltpu.SemaphoreType.DMA((N_DEV - 1,)),
                        pltpu.SemaphoreType.DMA],
        compiler_params=pltpu.CompilerParams(has_side_effects=True),
    )(x)


def all_gather(name, x):
    return _exchange(name, x, False)


def reduce_scatter_exchange(name, x):
    return _exchange(name, x, True)


def all_gather_2level(name, x):
    def body(x_ref, out_ref, send_sems, recv_sems, local_sem):
        x_, y_, c_ = lax.axis_index("x"), lax.axis_index("y"), lax.axis_index("c")
        sibling = (x_, y_, 1 - c_)
        chips = [(1 - x_, y_), (x_, 1 - y_), (1 - x_, 1 - y_)]

        def slot(px, py, pc):
            return out_ref.at[4 * px + 2 * py + pc]

        def copy(k, block, to, src=None):
            return pltpu.make_async_remote_copy(
                src_ref=slot(*block) if src is None else src, dst_ref=slot(*block),
                send_sem=send_sems.at[k], recv_sem=recv_sems.at[k], device_id=to, device_id_type=pl.DeviceIdType.MESH)

        me = (x_, y_, c_)
        mine = pltpu.make_async_copy(x_ref, slot(*me), local_sem)
        mine.start()
        first = [copy(0, me, sibling, src=x_ref)]
        first += [copy(1 + j, me, (*chip, c_), src=x_ref) for j, chip in enumerate(chips)]
        for cp in first:
            cp.start()
        passed = [copy(4 + j, (*chip, c_), sibling) for j, chip in enumerate(chips)]
        for j, chip in enumerate(chips):
            copy(1 + j, (*chip, c_), me).wait_recv()
            passed[j].start()
        copy(0, sibling, me).wait_recv()
        for j, chip in enumerate(chips):
            copy(4 + j, (*chip, 1 - c_), me).wait_recv()
        for cp in first + passed:
            cp.wait_send()
        mine.wait()

    return _pcall(
        body, name=name, in_specs=[_ANY], out_specs=_ANY,
        out_shape=jax.ShapeDtypeStruct((N_DEV,) + tuple(x.shape), x.dtype),
        scratch_shapes=[pltpu.SemaphoreType.DMA((7,)), pltpu.SemaphoreType.DMA((7,)), pltpu.SemaphoreType.DMA],
        compiler_params=pltpu.CompilerParams(has_side_effects=True),
    )(x)


def sibling_swap(name, x):
    shape = (4,) + tuple(x.shape[1:])

    def body(x_ref, theirs_ref, send_sems, recv_sems):
        x_, y_, c_ = lax.axis_index("x"), lax.axis_index("y"), lax.axis_index("c")
        sends = []
        for j in range(4):
            cp = pltpu.make_async_remote_copy(
                src_ref=x_ref.at[2 * j + 1 - c_], dst_ref=theirs_ref.at[j], send_sem=send_sems.at[j], recv_sem=recv_sems.at[j],
                device_id=(x_, y_, 1 - c_), device_id_type=pl.DeviceIdType.MESH)
            cp.start()
            sends.append(cp)
        for cp in sends:
            cp.wait()

    return _pcall(
        body, name=name, in_specs=[_ANY], out_specs=_ANY,
        out_shape=jax.ShapeDtypeStruct(shape, x.dtype),
        scratch_shapes=[pltpu.SemaphoreType.DMA((4,)), pltpu.SemaphoreType.DMA((4,))],
        compiler_params=pltpu.CompilerParams(has_side_effects=True),
    )(x)


def pair_add(name, stack, theirs, out_dtype):
    shp = theirs.shape
    c = shp[-1]
    r = math.prod(shp[1:-1])
    br = r
    for cand in (1024, 512, 256, 128, 64, 32, 16, 8):
        if r % cand == 0 and cand * c <= 256 * 1024:
            br = cand
            break

    def body(s0_ref, s1_ref, t_ref, o_ref):
        mine = jnp.where(lax.axis_index("c") == 0, s0_ref[...], s1_ref[...])
        o_ref[...] = (mine + t_ref[...]).astype(o_ref.dtype)

    s4 = stack.reshape(4, 2, r, c)
    out = _pcall(
        body, name=name, grid=(4, r // br),
        in_specs=[_bs((None, None, br, c), lambda j, i: (j, 0, i, 0)), _bs((None, None, br, c), lambda j, i: (j, 1, i, 0)),
                  _bs((None, br, c), lambda j, i: (j, i, 0))],
        out_specs=_bs((None, br, c), lambda j, i: (j, i, 0)),
        out_shape=jax.ShapeDtypeStruct((4, r, c), out_dtype), compiler_params=_cparams(),
    )(s4, s4, theirs.reshape(4, r, c))
    return out.reshape(shp)


def chip_exchange(name, p):
    def body(p_ref, out_ref, send_sems, recv_sems, local_sem):
        x_, y_, c_ = lax.axis_index("x"), lax.axis_index("y"), lax.axis_index("c")
        my_chip = 2 * x_ + y_
        chips = [(1 - x_, y_), (x_, 1 - y_), (1 - x_, 1 - y_)]
        local = pltpu.make_async_copy(p_ref.at[my_chip], out_ref.at[my_chip], local_sem)
        local.start()
        sends = []
        for k, (px, py) in enumerate(chips):
            cp = pltpu.make_async_remote_copy(
                src_ref=p_ref.at[2 * px + py], dst_ref=out_ref.at[my_chip], send_sem=send_sems.at[k], recv_sem=recv_sems.at[k],
                device_id=(px, py, c_), device_id_type=pl.DeviceIdType.MESH)
            cp.start()
            sends.append(cp)
        for k, (px, py) in enumerate(chips):
            pltpu.make_async_remote_copy(
                src_ref=p_ref.at[my_chip], dst_ref=out_ref.at[2 * px + py], send_sem=send_sems.at[k], recv_sem=recv_sems.at[k],
                device_id=(px, py, c_), device_id_type=pl.DeviceIdType.MESH).wait_recv()
        for cp in sends:
            cp.wait_send()
        local.wait()

    return _pcall(
        body, name=name, in_specs=[_ANY], out_specs=_ANY,
        out_shape=jax.ShapeDtypeStruct(p.shape, p.dtype),
        scratch_shapes=[pltpu.SemaphoreType.DMA((3,)), pltpu.SemaphoreType.DMA((3,)), pltpu.SemaphoreType.DMA],
        compiler_params=pltpu.CompilerParams(has_side_effects=True),
    )(p)


def reduce_scatter_2level(name, stack):
    theirs = sibling_swap(name + "_d2d", stack)
    part = pair_add(name + "_add", stack, theirs, BF16)
    return chip_exchange(name + "_ici", part)


def ln_res_f(h, r, g, b):
    x = ALPHA * h + r
    mu = jnp.mean(x, axis=-1, keepdims=True)
    var = jnp.mean(jnp.square(x - mu), axis=-1, keepdims=True)
    return (x - mu) * lax.rsqrt(var + LN_EPS) * g + b


def rms_f(x, g):
    return x * lax.rsqrt(jnp.mean(x * x, axis=-1, keepdims=True) + RMS_EPS) * g


def rope_f(x1, x2, c, s):
    return x1 * c - x2 * s, x1 * s + x2 * c


def swiglu_f(g, u):
    return jax.nn.silu(g) * u


def tshift_f(d, dprev, mu):
    return d + (dprev - d) * mu


def loss_f(y, tgt):
    e = y - tgt
    return e / y.shape[-1], 0.5 * jnp.mean(e * e, axis=-1, keepdims=True)


def adamw(name, w, m, v, gstack):
    shp = w.shape
    c = shp[-1]
    r = math.prod(shp[:-1])
    br = r
    for cand in (512, 256, 128, 64, 32, 16, 8):
        if r % cand == 0 and cand * c <= 128 * 1024:
            br = cand
            break
    k = gstack.shape[0]

    def body(w_ref, m_ref, v_ref, g_ref, go_ref, d_ref, mo_ref, vo_ref):
        g = g_ref[0].astype(F32)
        for j in range(1, k):
            g = g + g_ref[j].astype(F32)
        m_new = ADAM_B1 * m_ref[...] + (1.0 - ADAM_B1) * g
        v_new = ADAM_B2 * v_ref[...] + (1.0 - ADAM_B2) * jnp.square(g)
        m_hat = m_new / (1.0 - ADAM_B1 ** ADAM_STEP)
        v_hat = v_new / (1.0 - ADAM_B2 ** ADAM_STEP)
        go_ref[...] = g
        d_ref[...] = -ADAM_LR * (m_hat / (jnp.sqrt(v_hat) + ADAM_EPS) + ADAM_WD * w_ref[...])
        mo_ref[...] = m_new
        vo_ref[...] = v_new

    spec = _bs((br, c), lambda i: (i, 0))
    outs = _pcall(
        body, name=name, grid=(r // br,),
        in_specs=[spec, spec, spec, _bs((k, br, c), lambda i: (0, i, 0))],
        out_specs=[spec] * 4, out_shape=[jax.ShapeDtypeStruct((r, c), F32)] * 4,
        compiler_params=_cparams(),
    )(w.reshape(r, c), m.reshape(r, c), v.reshape(r, c), gstack.reshape(k, r, c))
    return tuple(o.reshape(shp) for o in outs)


EV_W = (512, 256, 64, 512, 512, 1024, 1024, 16)
EV_IN, EV_PAD = 3920, 4096
OD_IN, OD_PAD = 6592, 6656
TM = 256


def _offsets(widths):
    offs, acc = [], 0
    for w in widths:
        offs.append((acc, acc + w))
        acc += w
    return offs


def _rope_tables(seq, dim):
    inv = 10000.0 ** (-jnp.arange(0, dim, 2, dtype=F32) / dim)
    ang = jnp.arange(seq, dtype=F32)[:, None] * inv[None, :]
    return jnp.cos(ang), jnp.sin(ang)


def _halves(x, nh):
    t, w = x.shape
    x3 = x.reshape(t, nh, w // nh)
    hd = w // nh // 2
    return x3[:, :, :hd].reshape(t, nh * hd), x3[:, :, hd:].reshape(t, nh * hd)


def _unhalves(x1, x2, nh):
    t = x1.shape[0]
    return jnp.concatenate([x1.reshape(t, nh, -1), x2.reshape(t, nh, -1)], axis=2).reshape(t, -1)


def ln_res2_f(h, r, g, b):
    y = ln_res_f(h, r, g, b)
    return y, y


def _ln(name, h, r, g, b, cts=None):
    if cts is None:
        return _rows(name, ln_res2_f, [h, r], [g, b], [h.shape[1]] * 2, TM, dtypes=[F32, BF16])
    return _rows(name + "_bwd", ln_res_f, [h, r], [g, b], None, TM, fwd=False, cts=[cts], wrt_rows=(0, 1), wrt_params=(0, 1),
                 dtypes=[F32, BF16])


def _tail_fwd(l, h, hb, mem2b, bsz, p):
    qx = matmul(f"xa_q{l}", hb, p["xa_w_q"][l])
    kx = matmul(f"xa_k{l}", mem2b, p["xa_w_k"][l])
    vx = matmul(f"xa_v{l}", mem2b, p["xa_w_v"][l])
    ox = xattn(bsz, qx, kx, vx)
    xa = matmul(f"xa_o{l}", ox, p["xa_w_o"][l])
    h2, h2b = _ln(f"ln_xa{l}", h, xa, p["ln_xa_g"][l:l + 1], p["ln_xa_b"][l:l + 1])
    gg = matmul(f"ffn_g{l}", h2b, p["ffn_w_gate"][l])
    uu = matmul(f"ffn_u{l}", h2b, p["ffn_w_up"][l])
    act = _rows(f"swiglu{l}", swiglu_f, [gg, uu], [], [gg.shape[1]], TM, dtypes=[BF16])[0]
    ff = matmul(f"ffn_d{l}", act, p["ffn_w_down"][l])
    h3, h3b = _ln(f"ln_ffn{l}", h2, ff, p["ln_ffn_g"][l:l + 1], p["ln_ffn_b"][l:l + 1])
    return h3, h3b, (h, hb, qx, kx, vx, ox, xa, h2, h2b, gg, uu, act, ff)


def _tail_bwd(l, dh3, saved, mem2b, bsz, p, gr):
    h, hb, qx, kx, vx, ox, xa, h2, h2b, gg, uu, act, ff = saved
    dh2, dff, gr["ln_ffn_g"][l], gr["ln_ffn_b"][l] = _ln(f"ln_ffn{l}", h2, ff, p["ln_ffn_g"][l:l + 1], p["ln_ffn_b"][l:l + 1], cts=dh3)
    dact = matmul(f"ffn_d_dx{l}", dff, p["ffn_w_down"][l], tb=True)
    gr["ffn_w_down"][l] = matmul(f"ffn_d_dw{l}", act, dff, ta=True)
    dgg, duu = _rows(f"swiglu_bwd{l}", swiglu_f, [gg, uu], [], None, TM, fwd=False, cts=[dact], wrt_rows=(0, 1),
                     dtypes=[BF16, BF16])
    gr["ffn_w_gate"][l] = matmul(f"ffn_g_dw{l}", h2b, dgg, ta=True)
    gr["ffn_w_up"][l] = matmul(f"ffn_u_dw{l}", h2b, duu, ta=True)
    dh2 = matmul(f"ffn_g_dx{l}", dgg, p["ffn_w_gate"][l], tb=True, add=dh2)
    dh2 = matmul(f"ffn_u_dx{l}", duu, p["ffn_w_up"][l], tb=True, add=dh2)
    dh, dxa, gr["ln_xa_g"][l], gr["ln_xa_b"][l] = _ln(f"ln_xa{l}", h, xa, p["ln_xa_g"][l:l + 1], p["ln_xa_b"][l:l + 1], cts=dh2)
    dox = matmul(f"xa_o_dx{l}", dxa, p["xa_w_o"][l], tb=True)
    gr["xa_w_o"][l] = matmul(f"xa_o_dw{l}", ox, dxa, ta=True)
    dqx, dkx, dvx = xattn(bsz, qx, kx, vx, cts=dox)
    gr["xa_w_q"][l] = matmul(f"xa_q_dw{l}", hb, dqx, ta=True)
    gr["xa_w_k"][l] = matmul(f"xa_k_dw{l}", mem2b, dkx, ta=True)
    gr["xa_w_v"][l] = matmul(f"xa_v_dw{l}", mem2b, dvx, ta=True)
    return matmul(f"xa_q_dx{l}", dqx, p["xa_w_q"][l], tb=True, add=dh)


def _uq_perm(w):
    w3 = w.reshape(w.shape[0], 8, 192)
    return jnp.concatenate([w3[:, :, :128].reshape(-1, 1024), w3[:, :, 128:160].reshape(-1, 256),
                            w3[:, :, 160:].reshape(-1, 256)], axis=1)


def _uq_unperm(g):
    r = g.shape[0]
    return jnp.concatenate([g[:, :1024].reshape(r, 8, 128), g[:, 1024:1280].reshape(r, 8, 32),
                            g[:, 1280:].reshape(r, 8, 32)], axis=2).reshape(r, 1536)


def _ukv_perm(w):
    w3 = w.reshape(w.shape[0], 8, 256)
    return jnp.concatenate([w3[:, :, :128].reshape(-1, 1024), w3[:, :, 128:].reshape(-1, 1024)], axis=1)


def _ukv_unperm(g):
    r = g.shape[0]
    return jnp.concatenate([g[:, :1024].reshape(r, 8, 128), g[:, 1024:].reshape(r, 8, 128)], axis=2).reshape(r, 2048)


def _pad_cols(w, n):
    return jnp.pad(w, ((0, 0), (0, n - w.shape[1])))


def _shift_prev(x, bsz):
    t, w = x.shape
    x3 = x.reshape(bsz, t // bsz, w)
    return jnp.pad(x3, ((0, 0), (1, 0), (0, 0)))[:, :-1].reshape(t, w)


def _shift_next(x, bsz):
    t, w = x.shape
    x3 = x.reshape(bsz, t // bsz, w)
    return jnp.pad(x3[:, 1:], ((0, 0), (0, 1), (0, 0))).reshape(t, w)


def device_step(x, mem, tgt, p):
    bsz, seq, d = x.shape
    t = bsz * seq
    x2, mem2, tgt2 = x.reshape(t, d), mem.reshape(bsz * mem.shape[1], d), tgt.reshape(t, d)
    x2b, mem2b = x2.astype(BF16), mem2.astype(BF16)
    gr = {k: [None] * DEPTH for k in ("ln_mix_g", "ln_mix_b", "xa_w_q", "xa_w_k", "xa_w_v", "xa_w_o", "ln_xa_g", "ln_xa_b",
                                      "ffn_w_gate", "ffn_w_up", "ffn_w_down", "ln_ffn_g", "ln_ffn_b")}
    cos_pe, sin_pe = _rope_tables(seq, 64)
    cos_c, sin_c = _rope_tables(seq, 128)
    cq, sq = jnp.tile(cos_pe, (bsz, 8)), jnp.tile(sin_pe, (bsz, 8))
    ck, sk = jnp.tile(cos_pe, (bsz, 1)), jnp.tile(sin_pe, (bsz, 1))
    cd, sd = jnp.tile(cos_c, (bsz, 8)), jnp.tile(sin_c, (bsz, 8))

    w_in0 = _pad_cols(p["ev_w_in"][0], EV_PAD)
    w_uq, w_ukv = _uq_perm(p["ev_mla_w_uq"][0]), _ukv_perm(p["ev_mla_w_ukv"][0])
    z0 = matmul("ev_in", x2b, w_in0)
    c_q, c_kv, k_pe, q_g, k_g, v_g, r_g, lr_g = (z0[:, a:b] for a, b in _offsets(EV_W))
    qr = _rows("q_rms", rms_f, [c_q], [p["ev_mla_q_norm"]], [512], TM, dtypes=[BF16])[0]
    kvr = _rows("kv_rms", rms_f, [c_kv], [p["ev_mla_kv_norm"]], [256], TM, dtypes=[BF16])[0]
    q = matmul("mla_uq", qr, w_uq)
    kv = matmul("mla_ukv", kvr, w_ukv)
    qn, qp1, qp2 = q[:, :1024], q[:, 1024:1280], q[:, 1280:]
    kn, vv = kv[:, :1024], kv[:, 1024:]
    kp1, kp2 = k_pe[:, :32], k_pe[:, 32:]
    r1, r2 = _rows("rope_q", rope_f, [qp1, qp2, cq, sq], [], [256, 256], TM)
    kr1, kr2 = _rows("rope_k", rope_f, [kp1, kp2, ck, sk], [], [32, 32], TM)
    r1h, r2h = _to_heads(r1, bsz, 8), _to_heads(r2, bsz, 8)
    a_out = mla_attn(bsz, qn, r1h, r2h, kn, kr1, kr2, vv)
    gla_prm = (p["ev_gla_w_gate2"][0], p["ev_gla_b_gate"], p["ev_gla_norm_g"], p["ev_gla_norm_b"])
    b_out, gla_sv = gla_block(bsz, q_g, k_g, v_g, r_g, lr_g, *gla_prm)
    mixin0 = jnp.concatenate([a_out, b_out], axis=1)
    mix0 = matmul("ev_out", mixin0, p["ev_w_out"][0])
    h1, h1b = _ln("ln_mix0", x2, mix0, p["ln_mix_g"][0:1], p["ln_mix_b"][0:1])
    h3, h3b, tail0 = _tail_fwd(0, h1, h1b, mem2b, bsz, p)

    w_in1 = _pad_cols(p["od_w_in"][0], OD_PAD)
    z1 = matmul("od_in", h3b, w_in1)
    dq_, dk_, dv_ = z1[:, :1024], z1[:, 1024:2048], z1[:, 2048:3072]
    d_in = z1[:, 3072:OD_IN]
    q1, q2 = _halves(dq_, 8)
    k1, k2 = _halves(dk_, 8)
    qd1, qd2 = _rows("rope_dq", rope_f, [q1, q2, cd, sd], [], [512, 512], TM)
    kd1, kd2 = _rows("rope_dk", rope_f, [k1, k2, cd, sd], [], [512, 512], TM)
    c_out = dil_block(bsz, qd1, qd2, kd1, kd2, dv_)
    d_prev = _shift_prev(d_in, bsz)
    mu = p["od_rwkv_mu"]
    ds = _rows("tshift", tshift_f, [d_in, d_prev], [mu], [d_in.shape[1]], TM)[0]
    rw_in = tuple(ds[:, a:b] for a, b in _offsets((1024, 1024, 1024, 96, 96, 256)))
    rw_prm = dict(w0=p["od_rwkv_w0"], wd2=p["od_rwkv_w_decay2"][0], a0=p["od_rwkv_a0"], wa2=p["od_rwkv_w_a2"][0],
                  wg2=p["od_rwkv_w_gate2"][0], k_k=p["od_rwkv_k_k"], k_a=p["od_rwkv_k_a"], r_k=p["od_rwkv_r_k"][0],
                  gn_g=p["od_rwkv_gn_g"], gn_b=p["od_rwkv_gn_b"])
    d_out = rwkv_block(bsz, *rw_in, rw_prm)
    mixin1 = jnp.concatenate([c_out, d_out], axis=1)
    mix1 = matmul("od_out", mixin1, p["od_w_out"][0])
    h4, h4b = _ln("ln_mix1", h3, mix1, p["ln_mix_g"][1:2], p["ln_mix_b"][1:2])
    y, _, tail1 = _tail_fwd(1, h4, h4b, mem2b, bsz, p)

    dy, row_loss = _rows("loss", loss_f, [y, tgt2], [], [d, 1], TM)
    loss = jnp.sum(row_loss)

    dh4 = _tail_bwd(1, dy, tail1, mem2b, bsz, p, gr)
    dh3, dmix1, gr["ln_mix_g"][1], gr["ln_mix_b"][1] = _ln("ln_mix1", h3, mix1, p["ln_mix_g"][1:2], p["ln_mix_b"][1:2], cts=dh4)
    dmixin1 = matmul("od_out_dx", dmix1, p["od_w_out"][0], tb=True)
    gr["od_w_out"] = matmul("od_out_dw", mixin1, dmix1, ta=True)[None]
    dc_out, dd_out = dmixin1[:, :1024], dmixin1[:, 1024:]
    drw_in, drw_prm = rwkv_block(bsz, *rw_in, rw_prm, cts=dd_out)
    dds = jnp.concatenate(drw_in, axis=1)
    dd_in, dd_prev, dmu = _rows("tshift_bwd", tshift_f, [d_in, d_prev], [mu], None, TM, fwd=False, cts=[dds],
                                wrt_rows=(0, 1), wrt_params=(0,))
    dd_in = dd_in + _shift_next(dd_prev, bsz)
    dqd1, dqd2, dkd1, dkd2, ddv = dil_block(bsz, qd1, qd2, kd1, kd2, dv_, cts=dc_out)
    dq1, dq2 = _rows("rope_dq_bwd", rope_f, [q1, q2, cd, sd], [], None, TM, fwd=False, cts=[dqd1, dqd2], wrt_rows=(0, 1))
    dk1, dk2 = _rows("rope_dk_bwd", rope_f, [k1, k2, cd, sd], [], None, TM, fwd=False, cts=[dkd1, dkd2], wrt_rows=(0, 1))
    dz1 = jnp.concatenate([_unhalves(dq1, dq2, 8), _unhalves(dk1, dk2, 8), ddv, dd_in,
                           jnp.zeros((t, OD_PAD - OD_IN), F32)], axis=1).astype(BF16)
    gr["od_w_in"] = matmul("od_in_dw", h3b, dz1, ta=True)[:, :OD_IN][None]
    dh3 = matmul("od_in_dx", dz1, w_in1, tb=True, add=dh3)
    gr["od_rwkv_mu"] = dmu
    gr["od_rwkv_w0"], gr["od_rwkv_w_decay2"], gr["od_rwkv_a0"] = drw_prm["w0"], drw_prm["wd2"][None], drw_prm["a0"]
    gr["od_rwkv_w_a2"], gr["od_rwkv_w_gate2"] = drw_prm["wa2"][None], drw_prm["wg2"][None]
    gr["od_rwkv_k_k"], gr["od_rwkv_k_a"], gr["od_rwkv_r_k"] = drw_prm["k_k"], drw_prm["k_a"], drw_prm["r_k"][None]
    gr["od_rwkv_gn_g"], gr["od_rwkv_gn_b"] = drw_prm["gn_g"], drw_prm["gn_b"]

    dh1 = _tail_bwd(0, dh3, tail0, mem2b, bsz, p, gr)
    dx2, dmix0, gr["ln_mix_g"][0], gr["ln_mix_b"][0] = _ln("ln_mix0", x2, mix0, p["ln_mix_g"][0:1], p["ln_mix_b"][0:1], cts=dh1)
    dmixin0 = matmul("ev_out_dx", dmix0, p["ev_w_out"][0], tb=True)
    gr["ev_w_out"] = matmul("ev_out_dw", mixin0, dmix0, ta=True)[None]
    da_out, db_out = dmixin0[:, :1024], dmixin0[:, 1024:]
    dq_g, dk_g, dv_g, dlr4, dr_g, dw2, dbg, dng, dnb = gla_block(bsz, q_g, k_g, v_g, r_g, lr_g, *gla_prm, cts=(db_out, gla_sv))
    dlr_g = jnp.sum(dlr4, axis=0)
    dqn, dr1h, dr2h, dkn, dkr1, dkr2, dvv = mla_attn(bsz, qn, r1h, r2h, kn, kr1, kr2, vv, cts=da_out)
    dqp1, dqp2 = _rows("rope_q_bwd", rope_f, [qp1, qp2, cq, sq], [], None, TM, fwd=False,
                       cts=[_from_heads(dr1h), _from_heads(dr2h)], wrt_rows=(0, 1))
    dkp1, dkp2 = _rows("rope_k_bwd", rope_f, [kp1, kp2, ck, sk], [], None, TM, fwd=False, cts=[dkr1, dkr2], wrt_rows=(0, 1))
    dq = jnp.concatenate([dqn, dqp1, dqp2], axis=1).astype(BF16)
    dkv = jnp.concatenate([dkn, dvv], axis=1).astype(BF16)
    dqr = matmul("mla_uq_dx", dq, w_uq, tb=True)
    gr["ev_mla_w_uq"] = _uq_unperm(matmul("mla_uq_dw", qr, dq, ta=True))[None]
    dkvr = matmul("mla_ukv_dx", dkv, w_ukv, tb=True)
    gr["ev_mla_w_ukv"] = _ukv_unperm(matmul("mla_ukv_dw", kvr, dkv, ta=True))[None]
    dc_q, gr["ev_mla_q_norm"] = _rows("q_rms_bwd", rms_f, [c_q], [p["ev_mla_q_norm"]], None, TM, fwd=False, cts=[dqr],
                                      wrt_rows=(0,), wrt_params=(0,))
    dc_kv, gr["ev_mla_kv_norm"] = _rows("kv_rms_bwd", rms_f, [c_kv], [p["ev_mla_kv_norm"]], None, TM, fwd=False, cts=[dkvr],
                                        wrt_rows=(0,), wrt_params=(0,))
    dz0 = jnp.concatenate([dc_q, dc_kv, dkp1, dkp2, dq_g, dk_g, dv_g, dr_g, dlr_g,
                           jnp.zeros((t, EV_PAD - EV_IN), F32)], axis=1).astype(BF16)
    gr["ev_w_in"] = matmul("ev_in_dw", x2b, dz0, ta=True)[:, :EV_IN][None]
    dx2 = matmul("ev_in_dx", dz0, w_in0, tb=True, add=dx2)
    gr["ev_gla_w_gate2"], gr["ev_gla_b_gate"] = dw2[None], dbg
    gr["ev_gla_norm_g"], gr["ev_gla_norm_b"] = dng, dnb
    for k in list(gr):
        if isinstance(gr[k], list):
            gr[k] = jnp.stack([g[0] if k.startswith("ln_") else g for g in gr[k]])
    return loss, dx2.reshape(bsz, seq, d), gr


WEIGHTS = ['ev_w_in', 'ev_mla_q_norm', 'ev_mla_w_uq', 'ev_mla_kv_norm', 'ev_mla_w_ukv', 'ev_gla_w_gate2', 'ev_gla_b_gate',
           'ev_gla_norm_g', 'ev_gla_norm_b', 'ev_w_out', 'od_w_in', 'od_rwkv_mu', 'od_rwkv_w0', 'od_rwkv_w_decay2',
           'od_rwkv_a0', 'od_rwkv_w_a2', 'od_rwkv_w_gate2', 'od_rwkv_k_k', 'od_rwkv_k_a', 'od_rwkv_r_k', 'od_rwkv_gn_g',
           'od_rwkv_gn_b', 'od_w_out', 'ln_mix_g', 'ln_mix_b', 'xa_w_q', 'xa_w_k', 'xa_w_v', 'xa_w_o', 'ln_xa_g', 'ln_xa_b',
           'ffn_w_gate', 'ffn_w_up', 'ffn_w_down', 'ln_ffn_g', 'ln_ffn_b']
BIG = {'ev_w_in': -1, 'ev_mla_w_uq': -1, 'ev_mla_w_ukv': -1, 'ev_w_out': -2, 'od_w_in': -1, 'od_w_out': -2,
       'xa_w_q': -2, 'xa_w_k': -2, 'xa_w_v': -2, 'xa_w_o': -2, 'ffn_w_gate': -1, 'ffn_w_up': -1, 'ffn_w_down': -2}
SMALL = ['ev_gla_w_gate2', 'od_rwkv_mu', 'od_rwkv_w0', 'od_rwkv_w_decay2', 'od_rwkv_a0', 'od_rwkv_w_a2', 'od_rwkv_w_gate2',
         'od_rwkv_k_k', 'od_rwkv_k_a', 'od_rwkv_gn_g', 'od_rwkv_gn_b']
REPL = ['ev_mla_q_norm', 'ev_mla_kv_norm', 'ev_gla_b_gate', 'ev_gla_norm_g', 'ev_gla_norm_b', 'od_rwkv_r_k',
        'ln_mix_g', 'ln_mix_b', 'ln_xa_g', 'ln_xa_b', 'ln_ffn_g', 'ln_ffn_b']
PACK_COLS = 128


def _unshard(g, shape, axis):
    axis %= len(shape)
    full = list(shape)
    full[axis] *= N_DEV
    return jnp.moveaxis(g, 0, axis).reshape(full)


def _shard_stack(gfull, axis):
    axis %= gfull.ndim
    shp = list(gfull.shape)
    shp[axis:axis + 1] = [N_DEV, shp[axis] // N_DEV]
    return jnp.moveaxis(gfull.reshape(shp), axis, 0)


def _pack(arrs):
    lead = arrs[0].shape[0]
    flat = jnp.concatenate([a.reshape(lead, -1) for a in arrs], axis=1)
    n = flat.shape[1]
    rows = -(-n // (8 * PACK_COLS)) * 8
    return jnp.pad(flat, ((0, 0), (0, rows * PACK_COLS - n))).reshape(lead, rows, PACK_COLS)


def _unpack(buf, shapes):
    lead = buf.shape[0]
    flat = buf.reshape(lead, -1)
    out, off = [], 0
    for shp in shapes:
        n = math.prod(shp)
        out.append(flat[:, off:off + n].reshape((lead,) + tuple(shp)))
        off += n
    return out


def train_step(x, mem, loss_target, w, m, v):
    p = {}
    for k, ax in BIG.items():
        p[k] = _unshard(all_gather_2level("ag_" + k, w[k].astype(BF16)), w[k].shape, ax)
    small_loc = _pack([w[k][None] for k in SMALL])[0]
    small_all = _unpack(all_gather("ag_small", small_loc), [w[k].shape for k in SMALL])
    for k, g in zip(SMALL, small_all):
        p[k] = _unshard(g, w[k].shape, -1)
    for k in REPL:
        p[k] = w[k]
    loss, dx, gr = device_step(x, mem, loss_target, p)
    loss = lax.psum(loss, ("x", "y", "c"))

    stacks = {}
    for k, ax in BIG.items():
        stacks[k] = reduce_scatter_2level("rs_" + k, _shard_stack(gr[k], ax))
    small_send = _pack([_shard_stack(gr[k], -1) for k in SMALL])
    small_recv = _unpack(reduce_scatter_exchange("rs_small", small_send), [w[k].shape for k in SMALL])
    stacks.update(zip(SMALL, small_recv))
    repl_loc = _pack([gr[k][None] for k in REPL])[0]
    repl_all = _unpack(all_gather("ag_repl_grads", repl_loc), [w[k].shape for k in REPL])
    stacks.update(zip(REPL, repl_all))

    grads, deltas, new_m, new_v = [], [], [], []
    for k in WEIGHTS:
        g, dl, mn, vn = adamw("adamw_" + k, w[k], m[k], v[k], stacks[k])
        grads.append(g), deltas.append(dl), new_m.append(mn), new_v.append(vn)
    return (loss, dx, *grads, *deltas, *new_m, *new_v)


def kernel(x, mem, ev_w_in, ev_mla_q_norm, ev_mla_w_uq, ev_mla_kv_norm, ev_mla_w_ukv, ev_gla_w_gate2, ev_gla_b_gate, ev_gla_norm_g, ev_gla_norm_b, ev_w_out, od_w_in, od_rwkv_mu, od_rwkv_w0, od_rwkv_w_decay2, od_rwkv_a0, od_rwkv_w_a2, od_rwkv_w_gate2, od_rwkv_k_k, od_rwkv_k_a, od_rwkv_r_k, od_rwkv_gn_g, od_rwkv_gn_b, od_w_out, ln_mix_g, ln_mix_b, xa_w_q, xa_w_k, xa_w_v, xa_w_o, ln_xa_g, ln_xa_b, ffn_w_gate, ffn_w_up, ffn_w_down, ln_ffn_g, ln_ffn_b, loss_target, m_ev_w_in, m_ev_mla_q_norm, m_ev_mla_w_uq, m_ev_mla_kv_norm, m_ev_mla_w_ukv, m_ev_gla_w_gate2, m_ev_gla_b_gate, m_ev_gla_norm_g, m_ev_gla_norm_b, m_ev_w_out, m_od_w_in, m_od_rwkv_mu, m_od_rwkv_w0, m_od_rwkv_w_decay2, m_od_rwkv_a0, m_od_rwkv_w_a2, m_od_rwkv_w_gate2, m_od_rwkv_k_k, m_od_rwkv_k_a, m_od_rwkv_r_k, m_od_rwkv_gn_g, m_od_rwkv_gn_b, m_od_w_out, m_ln_mix_g, m_ln_mix_b, m_xa_w_q, m_xa_w_k, m_xa_w_v, m_xa_w_o, m_ln_xa_g, m_ln_xa_b, m_ffn_w_gate, m_ffn_w_up, m_ffn_w_down, m_ln_ffn_g, m_ln_ffn_b, v_ev_w_in, v_ev_mla_q_norm, v_ev_mla_w_uq, v_ev_mla_kv_norm, v_ev_mla_w_ukv, v_ev_gla_w_gate2, v_ev_gla_b_gate, v_ev_gla_norm_g, v_ev_gla_norm_b, v_ev_w_out, v_od_w_in, v_od_rwkv_mu, v_od_rwkv_w0, v_od_rwkv_w_decay2, v_od_rwkv_a0, v_od_rwkv_w_a2, v_od_rwkv_w_gate2, v_od_rwkv_k_k, v_od_rwkv_k_a, v_od_rwkv_r_k, v_od_rwkv_gn_g, v_od_rwkv_gn_b, v_od_w_out, v_ln_mix_g, v_ln_mix_b, v_xa_w_q, v_xa_w_k, v_xa_w_v, v_xa_w_o, v_ln_xa_g, v_ln_xa_b, v_ffn_w_gate, v_ffn_w_up, v_ffn_w_down, v_ln_ffn_g, v_ln_ffn_b):
    given = dict(locals())
    w = {k: given[k] for k in WEIGHTS}
    m = {k: given["m_" + k] for k in WEIGHTS}
    v = {k: given["v_" + k] for k in WEIGHTS}
    return train_step(given["x"], given["mem"], given["loss_target"], w, m, v)
```

```python
import functools
import math

import jax
import jax.numpy as jnp
from jax import lax
from jax.experimental import pallas as pl
from jax.experimental.pallas import tpu as pltpu

F32 = jnp.float32
BF16 = jnp.bfloat16
VMEM_LIMIT = 56 * 1024 * 1024
ROWS_VMEM = 20 * 1024 * 1024

N_DEV = 8
DEPTH = 2
ALPHA = (2.0 * DEPTH) ** 0.25
LN_EPS = 1e-5
RMS_EPS = 1e-6
RWKV_GN_EPS = 64e-5
ADAM_LR, ADAM_B1, ADAM_B2, ADAM_EPS, ADAM_WD, ADAM_STEP = 0.001, 0.9, 0.999, 1e-08, 0.01, 10
NEG_INF = float("-inf")


def _pcall(body, **kw):
    return pl.pallas_call(body, **kw)


def _cparams(**kw):
    return pltpu.CompilerParams(vmem_limit_bytes=VMEM_LIMIT, **kw)


def _dg(a, b, ca, cb, batch):
    nb = 1 if batch else 0
    dims = (((ca + nb,), (cb + nb,)), ((0,), (0,)) if batch else ((), ()))
    return lax.dot_general(a.astype(BF16), b.astype(BF16), dims, preferred_element_type=F32)


@functools.partial(jax.custom_vjp, nondiff_argnums=(2, 3, 4))
def _mm(a, b, ta, tb, batch):
    return _dg(a, b, 0 if ta else 1, 1 if tb else 0, batch)


def _mm_fwd(a, b, ta, tb, batch):
    return _mm(a, b, ta, tb, batch), (a, b)


def _mm_bwd(ta, tb, batch, res, g):
    a, b = res
    if not ta and not tb:
        da, db = _mm(g, b, False, True, batch), _mm(a, g, True, False, batch)
    elif not ta and tb:
        da, db = _mm(g, b, False, False, batch), _mm(g, a, True, False, batch)
    elif ta and not tb:
        da, db = _mm(b, g, False, True, batch), _mm(a, g, False, False, batch)
    else:
        da, db = _mm(b, g, True, True, batch), _mm(g, a, True, True, batch)
    return da.astype(a.dtype), db.astype(b.dtype)


_mm.defvjp(_mm_fwd, _mm_bwd)


def mm(a, b, ta=False, tb=False):
    return _mm(a, b, ta, tb, a.ndim == 3)


def _bs(block, imap):
    return pl.BlockSpec(block, imap)


def _rev_imap(imap, n):
    def r(*idx):
        return imap(*idx[:-1], n - 1 - idx[-1])
    return r


def _gcall(body, name, grid, in_specs, out_specs, out_shape, scratch_shapes, args, comm=None):
    comm = comm or []
    n_in, n_out, n_scr = len(in_specs), len(out_specs), len(scratch_shapes)
    c_in = [a for c in comm for a in c["ins"]]
    c_out = [o for c in comm for o in c["outs"]]
    c_sem = [s for c in comm for s in c["sems"]]

    def body2(*refs):
        i = 0
        r_in, i = refs[i:i + n_in], i + n_in
        k_in, i = refs[i:i + len(c_in)], i + len(c_in)
        r_out, i = refs[i:i + n_out], i + n_out
        k_out, i = refs[i:i + len(c_out)], i + len(c_out)
        r_scr, k_sem = refs[i:i + n_scr], refs[i + n_scr:]

        def each(which):
            a = b = s = 0
            for c in comm:
                na, nb, ns = len(c["ins"]), len(c["outs"]), len(c["sems"])
                c[which](k_in[a:a + na], k_out[b:b + nb], k_sem[s:s + ns])
                a, b, s = a + na, b + nb, s + ns

        if comm:
            first = last = None
            for ax, n in enumerate(grid):
                f0, l0 = pl.program_id(ax) == 0, pl.program_id(ax) == n - 1
                first = f0 if first is None else jnp.logical_and(first, f0)
                last = l0 if last is None else jnp.logical_and(last, l0)
            pl.when(first)(lambda: each("start"))
        body(*r_in, *r_out, *r_scr)
        if comm:
            pl.when(last)(lambda: each("finish"))

    return _pcall(
        body2, name=name, grid=grid,
        in_specs=list(in_specs) + [_ANY] * len(c_in), out_specs=list(out_specs) + [_ANY] * len(c_out),
        out_shape=list(out_shape) + c_out, scratch_shapes=list(scratch_shapes) + c_sem,
        compiler_params=_cparams(has_side_effects=True) if comm else _cparams(),
    )(*args, *c_in)


def p_fwd(name, f, grid, ins, outs, carry=None, save_carry=None, comm=None):
    n_in, n_out = len(ins), len(outs)

    def body(*refs):
        in_refs = refs[:n_in]
        out_refs = refs[n_in:n_in + n_out]
        rest = refs[n_in + n_out:]
        vals = [r[...] for r in in_refs]
        if carry is None:
            res = f(*vals)
        else:
            if save_carry is not None:
                sv_ref, c_ref = rest
            else:
                (c_ref,) = rest

            @pl.when(pl.program_id(len(grid) - 1) == 0)
            def _():
                c_ref[...] = jnp.zeros(c_ref.shape, c_ref.dtype)

            c = c_ref[...]
            if save_carry is not None:
                sv_ref[...] = c
            res = f(c, *vals)
            c_ref[...] = res[0]
            res = res[1:]
        if not isinstance(res, (tuple, list)):
            res = (res,)
        for r, v in zip(out_refs, res):
            r[...] = v.astype(r.dtype)

    out_shape = [jax.ShapeDtypeStruct(s, d) for (s, d, _, _) in outs]
    out_specs = [_bs(b, m) for (_, _, b, m) in outs]
    scratch = []
    if carry is not None:
        if save_carry is not None:
            out_shape.append(jax.ShapeDtypeStruct(save_carry[0], carry[1]))
            out_specs.append(_bs(save_carry[1], save_carry[2]))
        scratch.append(pltpu.VMEM(carry[0], carry[1]))
    return _gcall(body, name, grid, [_bs(b, m) for (_, b, m) in ins], out_specs, out_shape, scratch,
                  [a for (a, _, _) in ins], comm)


def p_bwd(name, f, grid, ins, cts, wrt, carry=None, saved=None, comm=None):
    n_in, n_ct, n_w = len(ins), len(cts), len(wrt)
    rev = carry is not None
    n_last = grid[-1]

    def fix(imap):
        return _rev_imap(imap, n_last) if rev else imap

    def body(*refs):
        in_refs = refs[:n_in]
        ct_refs = refs[n_in:n_in + n_ct]
        k = n_in + n_ct
        if rev:
            sv_ref = refs[k]
            k += 1
        out_refs = refs[k:k + n_w]
        rest = refs[k + n_w:]
        vals = [r[...] for r in in_refs]
        ct_vals = [r[...].astype(F32) for r in ct_refs]
        widx = [w[0] for w in wrt]

        if rev:
            (dc_ref,) = rest

            @pl.when(pl.program_id(len(grid) - 1) == 0)
            def _():
                dc_ref[...] = jnp.zeros(dc_ref.shape, dc_ref.dtype)

            c_in = sv_ref[...]

            def g(c, *dv):
                full = list(vals)
                for i, d in zip(widx, dv):
                    full[i] = d
                return tuple(f(c, *full))

            _, vjp = jax.vjp(g, c_in, *[vals[i] for i in widx])
            grads = vjp((dc_ref[...],) + tuple(ct_vals))
            dc_ref[...] = grads[0]
            grads = grads[1:]
        else:
            def g(*dv):
                full = list(vals)
                for i, d in zip(widx, dv):
                    full[i] = d
                r = f(*full)
                return tuple(r) if isinstance(r, (tuple, list)) else (r,)

            _, vjp = jax.vjp(g, *[vals[i] for i in widx])
            grads = vjp(tuple(ct_vals))

        for w, o_ref, gr in zip(wrt, out_refs, grads):
            acc = w[1]
            if acc is None:
                o_ref[...] = gr.astype(o_ref.dtype)
            else:
                first = None
                for ax in acc:
                    c0 = pl.program_id(ax) == 0
                    first = c0 if first is None else jnp.logical_and(first, c0)

                @pl.when(first)
                def _():
                    o_ref[...] = jnp.zeros(o_ref.shape, o_ref.dtype)

                o_ref[...] += gr.astype(o_ref.dtype)

    in_specs = [_bs(b, fix(m)) for (_, b, m) in ins] + [_bs(b, fix(m)) for (_, b, m) in cts]
    args = [a for (a, _, _) in ins] + [a for (a, _, _) in cts]
    if rev:
        in_specs.append(_bs(saved[1], fix(saved[2])))
        args.append(saved[0])
    out_shape, out_specs = [], []
    for w in wrt:
        a, b, m = ins[w[0]]
        if len(w) > 2 and w[2] is not None:
            m = w[2]
        out_shape.append(jax.ShapeDtypeStruct(a.shape, w[3] if len(w) > 3 else F32))
        out_specs.append(_bs(b, fix(m)))
    scratch = [pltpu.VMEM(carry[0], carry[1])] if rev else []
    return _gcall(body, name, grid, in_specs, out_specs, out_shape, scratch, args, comm)


def _rows(name, f, row_ins, params, out_widths, tm, fwd=True, cts=None, wrt_rows=(), wrt_params=(), dtypes=None):
    t = row_ins[0].shape[0]
    width = sum(a.shape[1] for a in row_ins)
    width += sum(out_widths) if fwd else sum(c.shape[1] for c in cts) + sum(row_ins[i].shape[1] for i in wrt_rows)
    tm = min(tm, t)
    while tm > 8 and 2 * 4 * tm * width > ROWS_VMEM:
        tm //= 2
    rmap = lambda i: (i, 0)
    pmap = lambda i: (0, 0)
    ins = [(a, (tm, a.shape[1]), rmap) for a in row_ins] + [(p, p.shape, pmap) for p in params]
    if fwd:
        dtypes = dtypes or [F32] * len(out_widths)
        outs = [((t, w), dt, (tm, w), rmap) for w, dt in zip(out_widths, dtypes)]
        return p_fwd(name, f, (t // tm,), ins, outs)
    ct_specs = [(c, (tm, c.shape[1]), rmap) for c in cts]
    dtypes = dtypes or [F32] * len(wrt_rows)
    wrt = [(i, None, None, dt) for i, dt in zip(wrt_rows, dtypes)] + [(len(row_ins) + i, (0,)) for i in wrt_params]
    return p_bwd(name, f, (t // tm,), ins, ct_specs, wrt)


def _pick(n, cands):
    for c in cands:
        if n % c == 0:
            return c
    return n


def _wide(n, cap=1664):
    best = None
    for w in range(128, min(n, cap) + 1, 128):
        if n % w == 0:
            best = w
    return best or n


MM_VMEM = 40 * 1024 * 1024


def _mm_tiles(m, n, k, sa, sb, so, has_add):
    bm, bn = _pick(m, (512, 256, 128)), _wide(n)
    bk = k if k <= 2048 else _wide(k, 3328)

    def vmem(bm, bn, bk):
        acc = 0 if bk == k else 4 * bm * bn
        return 2 * (bm * bk * sa + bk * bn * sb + bm * bn * so * (2 if has_add else 1)) + acc

    while vmem(bm, bn, bk) > MM_VMEM and bk % 256 == 0:
        bk //= 2
    while vmem(bm, bn, bk) > MM_VMEM and bn % 256 == 0:
        bn //= 2
    return bm, bn, bk


def matmul(name, a, b, ta=False, tb=False, out_dtype=F32, add=None):
    m = a.shape[1] if ta else a.shape[0]
    k = a.shape[0] if ta else a.shape[1]
    n = b.shape[0] if tb else b.shape[1]
    assert (b.shape[1] if tb else b.shape[0]) == k, (a.shape, b.shape, ta, tb)
    bm, bn, bk = _mm_tiles(m, n, k, a.dtype.itemsize, b.dtype.itemsize, jnp.dtype(out_dtype).itemsize, add is not None)
    nk = k // bk

    def body(a_ref, b_ref, *rest):
        o_ref, acc_ref = rest[-2:]
        if nk == 1:
            r = mm(a_ref[...], b_ref[...], ta, tb)
            o_ref[...] = (r if add is None else r + rest[0][...].astype(F32)).astype(o_ref.dtype)
            return

        @pl.when(pl.program_id(2) == 0)
        def _():
            acc_ref[...] = jnp.zeros(acc_ref.shape, F32) if add is None else rest[0][...].astype(F32)

        acc_ref[...] += mm(a_ref[...], b_ref[...], ta, tb)

        @pl.when(pl.program_id(2) == nk - 1)
        def _():
            o_ref[...] = acc_ref[...].astype(o_ref.dtype)

    a_spec = _bs((bk, bm), lambda i, j, l: (l, i)) if ta else _bs((bm, bk), lambda i, j, l: (i, l))
    b_spec = _bs((bn, bk), lambda i, j, l: (j, l)) if tb else _bs((bk, bn), lambda i, j, l: (l, j))
    o_spec = _bs((bm, bn), lambda i, j, l: (i, j))
    return _pcall(
        body, name=name, grid=(m // bm, n // bn, nk),
        in_specs=[a_spec, b_spec] + ([] if add is None else [o_spec]), out_specs=o_spec,
        out_shape=jax.ShapeDtypeStruct((m, n), out_dtype),
        scratch_shapes=[pltpu.VMEM((bm, bn) if nk > 1 else (8, 128), F32)],
        compiler_params=_cparams(dimension_semantics=("parallel", "parallel", "arbitrary")),
    )(*((a, b) if add is None else (a, b, add)))


def _to_heads(x, b, h):
    t, w = x.shape
    return x.reshape(b, t // b, h, w // h).transpose(0, 2, 1, 3)


def _from_heads(x):
    b, h, s, d = x.shape
    return x.transpose(0, 2, 1, 3).reshape(b * s, h * d)


RW_STEPS = 8
RW_G = 16


def rwkv_group(st, w8, kk8, ka8, kh8, r8, vc):
    ys = []
    for t in range(RW_STEPS):
        row = lambda x: x[:, t:t + 1, :]
        sa = jnp.sum(st * row(kk8), axis=2, keepdims=True)
        st = st * row(w8) - sa * row(ka8) + vc[:, :, t:t + 1] * row(kh8)
        ys.append(jnp.sum(st * row(r8), axis=2, keepdims=True))
    return st, jnp.concatenate(ys, axis=2)


def rwkv_prehead_f(kkraw, a):
    nrm = jnp.sqrt(jnp.sum(kkraw * kkraw, axis=-1, keepdims=True))
    kk = kkraw / jnp.maximum(nrm, 1e-12)
    return kk, kk * a


def rwkv_pre_f(kd, w_lr, a_lr, g_lr, w0, wd2, a0, wa2, wg2, k_k, k_a):
    wpre = w0 + mm(jnp.tanh(w_lr), wd2)
    w = -jax.nn.softplus(-wpre) - 0.5
    decay = jnp.exp(-jnp.exp(w))
    a = jax.nn.sigmoid(a0 + mm(a_lr, wa2))
    g = mm(jax.nn.sigmoid(g_lr), wg2)
    kkraw = kd * k_k
    kh = kd * (1.0 + (a - 1.0) * k_a)
    return decay, a, g, kkraw, kh


def rwkv_post_f(y, r, kh, v, g, gn_g, gn_b, r_k):
    mu = jnp.mean(y, axis=-1, keepdims=True)
    var = jnp.mean(jnp.square(y - mu), axis=-1, keepdims=True)
    yn = (y - mu) * lax.rsqrt(var + RWKV_GN_EPS) * gn_g + gn_b
    bonus = jnp.sum(r * kh * r_k, axis=-1, keepdims=True) * v
    return (yn + bonus) * g


def _to_cols(xh):
    b, h, s, d = xh.shape
    return xh.reshape(b, h, s // RW_STEPS, RW_STEPS, d).transpose(0, 1, 2, 4, 3)


def _from_cols(xc):
    b, h, n, d, k = xc.shape
    return xc.transpose(0, 1, 2, 4, 3).reshape(b, h, n * k, d)


def rwkv_block(bsz, r, kd, vd, w_lr, a_lr, g_lr, prm, cts=None, scan=None, comm=None):
    t = r.shape[0]
    hh, n = 16, 64
    tm = 256
    pre_rows = [kd, w_lr, a_lr, g_lr]
    pre_prm = [prm[k] for k in ("w0", "wd2", "a0", "wa2", "wg2", "k_k", "k_a")]
    decay, a, g, kkraw, kh = _rows("rwkv_pre", rwkv_pre_f, pre_rows, pre_prm, [1024] * 5, tm)
    heads = lambda x: _to_heads(x, bsz, hh)
    rh, khh, vh, gh, dech, kkrawh, ah = (heads(x) for x in (r, kh, vd, g, decay, kkraw, a))
    s = t // bsz
    ng = s // RW_STEPS
    ts = 512
    hb = (None, None, ts, n)
    hm = lambda h, b, i: (b, h, i, 0)
    pgrid = (hh, bsz, s // ts)
    ph_ins = [(kkrawh, hb, hm), (ah, hb, hm)]
    kkh, kah = p_fwd("rwkv_prehead", rwkv_prehead_f, pgrid, ph_ins, [((bsz, hh, s, n), F32, hb, hm)] * 2)
    grp = lambda x: x.reshape(bsz, hh, ng, RW_STEPS, n)
    cb, rb, sb = (None, RW_G, None, n, RW_STEPS), (None, RW_G, None, RW_STEPS, n), (None, RW_G, None, n, n)
    cm = lambda b, h, i: (b, h, i, 0, 0)
    sc_ins = [(grp(x), rb, cm) for x in (dech, kkh, kah, khh, rh)] + [(_to_cols(vh), cb, cm)]
    grid = (bsz, hh // RW_G, ng)
    comm_out = []
    if scan is None:
        yc, sv, *comm_out = p_fwd("rwkv_scan", rwkv_group, grid, sc_ins, [((bsz, hh, ng, n, RW_STEPS), F32, cb, cm)],
                                  carry=((RW_G, n, n), F32), save_carry=((bsz, hh, ng, n, n), sb, cm), comm=comm)
    else:
        yc, sv = scan
    yh = _from_cols(yc)
    pb = (None, 1, n)
    pm = lambda h, b, i: (h, 0, 0)
    gn_g, gn_b, r_k = (prm[k].reshape(hh, 1, n) for k in ("gn_g", "gn_b", "r_k"))
    post_ins = [(x, hb, hm) for x in (yh, rh, khh, vh, gh)] + [(p, pb, pm) for p in (gn_g, gn_b, r_k)]
    if cts is None:
        (oh,) = p_fwd("rwkv_post", rwkv_post_f, pgrid, post_ins, [((bsz, hh, s, n), BF16, hb, hm)])
        return _from_heads(oh), (yc, sv), comm_out
    doh = _to_heads(cts, bsz, hh)
    dyh, drh1, dkhh1, dvh1, dgh, dgn_g, dgn_b, dr_k = p_bwd(
        "rwkv_post_bwd", rwkv_post_f, pgrid, post_ins, [(doh, hb, hm)],
        [(i, None) for i in range(5)] + [(5 + i, (1, 2)) for i in range(3)])
    drows_v = p_bwd("rwkv_scan_bwd", rwkv_group, grid, sc_ins, [(_to_cols(dyh), cb, cm)], [(i, None) for i in range(6)],
                    carry=((RW_G, n, n), F32), saved=(sv, sb, cm), comm=comm)
    drows_v, comm_out = drows_v[:6], list(drows_v[6:])
    ddech, dkkh, dkah, dkhh2, drh2 = (x.reshape(bsz, hh, s, n) for x in drows_v[:5])
    dkkrawh, dah = p_bwd("rwkv_prehead_bwd", rwkv_prehead_f, pgrid, ph_ins, [(dkkh, hb, hm), (dkah, hb, hm)],
                         [(0, None), (1, None)])
    ddecay, dkkraw, da = _from_heads(ddech), _from_heads(dkkrawh), _from_heads(dah)
    dv = _from_heads(_from_cols(drows_v[5]) + dvh1)
    dr = _from_heads(drh2 + drh1)
    dkh = _from_heads(dkhh2 + dkhh1)
    dg = _from_heads(dgh)
    res = _rows("rwkv_pre_bwd", rwkv_pre_f, pre_rows, pre_prm, None, tm, fwd=False,
                cts=[ddecay, da, dg, dkkraw, dkh], wrt_rows=(0, 1, 2, 3), wrt_params=tuple(range(7)))
    dkd, dw_lr, da_lr, dg_lr = res[:4]
    dprm = dict(zip(("w0", "wd2", "a0", "wa2", "wg2", "k_k", "k_a"), res[4:]))
    dprm.update(gn_g=dgn_g.reshape(1, -1), gn_b=dgn_b.reshape(1, -1), r_k=dr_k.reshape(hh, n))
    return (dr, dkd, dv, dw_lr, da_lr, dg_lr), dprm, comm_out


GLA_C, GLA_DK, GLA_DV, GLA_H, GLA_TAU = 64, 128, 256, 4, 16.0


def gla_chunk_f(st, q, k, v, lr, r, w2, bg, ng, nb):
    la = jax.nn.log_sigmoid(mm(lr, w2) + bg) / GLA_TAU
    ri = lax.broadcasted_iota(jnp.int32, (GLA_C, GLA_C), 0)
    ci = lax.broadcasted_iota(jnp.int32, (GLA_C, GLA_C), 1)
    causal = ci <= ri
    b = jnp.dot(causal.astype(F32), la, precision=lax.Precision.HIGHEST, preferred_element_type=F32)
    b_last = jnp.sum(la, axis=0, keepdims=True)
    q_dec = (q * (GLA_DK ** -0.5)) * jnp.exp(b)
    k_inv = k * jnp.exp(-b)
    k_end = k * jnp.exp(b_last - b)
    att = jnp.where(causal, mm(q_dec, k_inv, tb=True), 0.0)
    o = mm(att, v) + mm(q_dec, st, tb=True)
    st_new = st * jnp.exp(b_last) + mm(v, k_end, ta=True)
    mu = jnp.mean(o, axis=-1, keepdims=True)
    var = jnp.mean(jnp.square(o - mu), axis=-1, keepdims=True)
    on = (o - mu) * lax.rsqrt(var + LN_EPS) * ng + nb
    return st_new, on * jax.nn.silu(r)


def gla_block(bsz, q, k, v, r, lr, w2, bg, ng, nb, cts=None):
    t = q.shape[0]
    nc = t // bsz // GLA_C
    grid = (GLA_H, bsz, nc)
    rm = lambda h, b, c: (b * nc + c, h)
    ins = [(q, (GLA_C, GLA_DK), rm), (k, (GLA_C, GLA_DK), rm), (v, (GLA_C, GLA_DV), rm),
           (jnp.broadcast_to(lr[None], (GLA_H,) + lr.shape), (None, GLA_C, lr.shape[1]), lambda h, b, c: (h, b * nc + c, 0)),
           (r, (GLA_C, GLA_DV), rm),
           (w2, (w2.shape[0], GLA_DK), lambda h, b, c: (0, h)), (bg, (1, GLA_DK), lambda h, b, c: (0, h)),
           (ng, (1, GLA_DV), lambda h, b, c: (0, 0)), (nb, (1, GLA_DV), lambda h, b, c: (0, 0))]
    ob = (GLA_C, GLA_DV)
    sshape, sblock = (GLA_H, bsz, nc, GLA_DV, GLA_DK), (None, None, None, GLA_DV, GLA_DK)
    sm = lambda h, b, c: (h, b, c, 0, 0)
    carry = ((GLA_DV, GLA_DK), F32)
    if cts is None:
        out, sv = p_fwd("gla_scan", gla_chunk_f, grid, ins, [((t, GLA_H * GLA_DV), BF16, ob, rm)],
                        carry=carry, save_carry=(sshape, sblock, sm))
        return out, sv
    dout, sv = cts
    return p_bwd("gla_scan_bwd", gla_chunk_f, grid, ins, [(dout, ob, rm)],
                 [(0, None), (1, None), (2, None), (3, None), (4, None), (5, (1, 2)), (6, (1, 2)), (7, (0, 1, 2)), (8, (0, 1, 2))],
                 carry=carry, saved=(sv, sblock, sm))


def _softmax_rows(sc):
    m = lax.stop_gradient(jnp.max(sc, axis=-1, keepdims=True))
    e = jnp.exp(sc - m)
    return e / jnp.sum(e, axis=-1, keepdims=True)


def mla_attn_f(qn, r1, r2, kn, kr1, kr2, v):
    tq, s = qn.shape[0], kn.shape[0]
    sc = (mm(qn, kn, tb=True) + mm(r1, kr1, tb=True) + mm(r2, kr2, tb=True)) * (192.0 ** -0.5)
    qpos = pl.program_id(2) * tq + lax.broadcasted_iota(jnp.int32, (tq, s), 0)
    kpos = lax.broadcasted_iota(jnp.int32, (tq, s), 1)
    sc = jnp.where(kpos <= qpos, sc, NEG_INF)
    return mm(_softmax_rows(sc), v)


def mla_attn(bsz, qn, r1, r2, kn, kr1, kr2, v, cts=None, tq=256, comm=None):
    t = qn.shape[0]
    s = t // bsz
    nq = s // tq
    hh = 8
    grid = (bsz, hh, nq)
    qm = lambda b, h, i: (b * nq + i, h)
    km_ = lambda b, h, i: (b, h)
    ins = [(qn, (tq, 128), qm),
           (r1, (None, None, tq, 32), lambda b, h, i: (b, h, i, 0)), (r2, (None, None, tq, 32), lambda b, h, i: (b, h, i, 0)),
           (kn, (s, 128), km_), (kr1, (s, 32), lambda b, h, i: (b, 0)), (kr2, (s, 32), lambda b, h, i: (b, 0)),
           (v, (s, 128), km_)]
    if cts is None:
        return p_fwd("mla_attn", mla_attn_f, grid, ins, [((t, 1024), BF16, (tq, 128), qm)])[0]
    return p_bwd("mla_attn_bwd", mla_attn_f, grid, ins, [(cts, (tq, 128), qm)],
                 [(0, None), (1, None), (2, None), (3, (2,)), (4, (1, 2)), (5, (1, 2)), (6, (2,))], comm=comm)


def xattn_f(q, k, v):
    sc = mm(q, k, tb=True) * (512.0 ** -0.5)
    return mm(_softmax_rows(sc), v)


def xattn(bsz, q, k, v, cts=None, tq=512):
    t = q.shape[0]
    nq = t // bsz // tq
    mlen = k.shape[0] // bsz
    grid = (bsz, 4, nq)
    qm = lambda b, h, i: (b * nq + i, h)
    km_ = lambda b, h, i: (b, h)
    ins = [(q, (tq, 512), qm), (k, (mlen, 512), km_), (v, (mlen, 512), km_)]
    if cts is None:
        return p_fwd("xattn", xattn_f, grid, ins, [((t, 2048), BF16, (tq, 512), qm)])[0]
    return p_bwd("xattn_bwd", xattn_f, grid, ins, [(cts, (tq, 512), qm)], [(0, None, None, BF16), (1, (2,)), (2, (2,))])


DIL_SPAN = 128
DIL_BRANCHES = ((128, 1), (512, 4), (2048, 16))


def dil_attn_f(q1, q2, k1c, k2c, vc, k1p, k2p, vp):
    gb, sp = q1.shape[0], DIL_SPAN
    scale = 128.0 ** -0.5
    sc_c = (mm(q1, k1c, tb=True) + mm(q2, k2c, tb=True)) * scale
    sc_p = (mm(q1, k1p, tb=True) + mm(q2, k2p, tb=True)) * scale
    ql = lax.broadcasted_iota(jnp.int32, (gb, sp, sp), 1)
    kl = lax.broadcasted_iota(jnp.int32, (gb, sp, sp), 2)
    has_prev = pl.program_id(1) > 0
    sc_c = jnp.where(kl <= ql, sc_c, NEG_INF)
    sc_p = jnp.where(jnp.logical_and(kl >= ql, has_prev), sc_p, NEG_INF)
    m = lax.stop_gradient(jnp.maximum(jnp.max(sc_c, axis=-1, keepdims=True), jnp.max(sc_p, axis=-1, keepdims=True)))
    e_c, e_p = jnp.exp(sc_c - m), jnp.exp(sc_p - m)
    den = jnp.sum(e_c, axis=-1, keepdims=True) + jnp.sum(e_p, axis=-1, keepdims=True)
    o = mm(e_c / den, vc) + mm(e_p / den, vp)
    return o, m + jnp.log(den)


def dil_branch(q1, q2, k1, k2, v, cts=None, gb=8):
    g, l, _ = q1.shape
    nb = l // DIL_SPAN
    grid = (g // gb, nb)
    cm = lambda i, n: (i, n, 0)
    pm = lambda i, n: (i, jnp.maximum(n - 1, 0), 0)
    b64, b128, b1 = (gb, DIL_SPAN, 64), (gb, DIL_SPAN, 128), (gb, DIL_SPAN, 1)
    ins = [(q1, b64, cm), (q2, b64, cm), (k1, b64, cm), (k2, b64, cm), (v, b128, cm),
           (k1, b64, pm), (k2, b64, pm), (v, b128, pm)]
    if cts is None:
        return p_fwd("dil_attn", dil_attn_f, grid, ins, [((g, l, 128), F32, b128, cm), ((g, l, 1), F32, b1, cm)])
    do, dlse = cts
    dq1, dq2, dk1c, dk2c, dvc, dk1p, dk2p, dvp = p_bwd(
        "dil_attn_bwd", dil_attn_f, grid, ins, [(do, b128, cm), (dlse, b1, cm)],
        [(i, None) for i in range(5)] + [(i, None, cm) for i in (5, 6, 7)])

    def fold(dc, dp):
        return dc + jnp.pad(dp[:, DIL_SPAN:], ((0, 0), (0, DIL_SPAN), (0, 0)))

    return dq1, dq2, fold(dk1c, dk1p), fold(dk2c, dk2p), fold(dvc, dvp)


def dil_mix_f(o1, o2, o3, l1, l2, l3):
    m = lax.stop_gradient(jnp.maximum(jnp.maximum(l1, l2), l3))
    e1, e2, e3 = jnp.exp(l1 - m), jnp.exp(l2 - m), jnp.exp(l3 - m)
    den = e1 + e2 + e3
    return (e1 / den) * o1 + (e2 / den) * o2 + (e3 / den) * o3


def _to_res(xh, dil):
    b, h, s, d = xh.shape
    return xh.reshape(b, h, s // dil, dil, d).transpose(0, 1, 3, 2, 4).reshape(b * h * dil, s // dil, d)


def _from_res(xr, b, h, dil):
    g, l, d = xr.shape
    return xr.reshape(b, h, dil, l, d).transpose(0, 1, 3, 2, 4).reshape(b, h, l * dil, d)


def dil_block(bsz, q1, q2, k1, k2, v, cts=None):
    hh = 8
    heads = [_to_heads(x, bsz, hh) for x in (q1, q2, k1, k2, v)]
    s = heads[0].shape[2]
    outs, res_in = [], []
    for window, dil in DIL_BRANCHES:
        assert window // dil == DIL_SPAN and (s // dil) % DIL_SPAN == 0
        rin = [_to_res(x, dil) for x in heads]
        o, lse = dil_branch(*rin)
        res_in.append(rin)
        outs.append((_from_res(o, bsz, hh, dil), _from_res(lse, bsz, hh, dil)))
    tq = 512
    ob, lb = (None, None, tq, 128), (None, None, tq, 1)
    hm = lambda b, h, i: (b, h, i, 0)
    mix_ins = [(o, ob, hm) for (o, _) in outs] + [(l, lb, hm) for (_, l) in outs]
    grid = (bsz, hh, s // tq)
    if cts is None:
        (mix,) = p_fwd("dil_mix", dil_mix_f, grid, mix_ins, [((bsz, hh, s, 128), BF16, ob, hm)])
        return _from_heads(mix)
    dmix = _to_heads(cts, bsz, hh)
    dml = p_bwd("dil_mix_bwd", dil_mix_f, grid, mix_ins, [(dmix, ob, hm)], [(i, None) for i in range(6)])
    tot = None
    for j, (window, dil) in enumerate(DIL_BRANCHES):
        do, dl = _to_res(dml[j], dil), _to_res(dml[3 + j], dil)
        gr = dil_branch(*res_in[j], cts=(do, dl))
        gr = [_from_res(x, bsz, hh, dil) for x in gr]
        tot = gr if tot is None else [a + b for a, b in zip(tot, gr)]
    return tuple(_from_heads(x) for x in tot)


_ANY = pl.BlockSpec(memory_space=pl.ANY)


def _me_and_peers():
    x, y, c = lax.axis_index("x"), lax.axis_index("y"), lax.axis_index("c")
    me = 4 * x + 2 * y + c
    peers = []
    for k in range(1, N_DEV):
        px = 1 - x if k & 4 else x
        py = 1 - y if k & 2 else y
        pc = 1 - c if k & 1 else c
        peers.append(((px, py, pc), 4 * px + 2 * py + pc))
    return me, peers


def _exchange(name, x, scatter):
    shape = x.shape[1:] if scatter else x.shape

    def body(x_ref, out_ref, send_sems, recv_sems, local_sem):
        me, peers = _me_and_peers()
        src_me = x_ref.at[me] if scatter else x_ref
        local = pltpu.make_async_copy(src_me, out_ref.at[me], local_sem)
        local.start()
        sends = []
        for k, (dev, idx) in enumerate(peers):
            cp = pltpu.make_async_remote_copy(
                src_ref=x_ref.at[idx] if scatter else x_ref, dst_ref=out_ref.at[me],
                send_sem=send_sems.at[k], recv_sem=recv_sems.at[k],
                device_id=dev, device_id_type=pl.DeviceIdType.MESH)
            cp.start()
            sends.append(cp)
        for k, (dev, idx) in enumerate(peers):
            pltpu.make_async_remote_copy(
                src_ref=src_me, dst_ref=out_ref.at[idx], send_sem=send_sems.at[k], recv_sem=recv_sems.at[k],
                device_id=dev, device_id_type=pl.DeviceIdType.MESH).wait_recv()
        for cp in sends:
            cp.wait_send()
        local.wait()

    return _pcall(
        body, name=name, in_specs=[_ANY], out_specs=_ANY,
        out_shape=jax.ShapeDtypeStruct((N_DEV,) + tuple(shape), x.dtype),
        scratch_shapes=[pltpu.SemaphoreType.DMA((N_DEV - 1,)), pltpu.SemaphoreType.DMA((N_DEV - 1,)),
                        pltpu.SemaphoreType.DMA],
        compiler_params=pltpu.CompilerParams(has_side_effects=True),
    )(x)


def all_gather(name, x):
    return _exchange(name, x, False)


def reduce_scatter_exchange(name, x):
    return _exchange(name, x, True)


def run_exchange(name, spec):
    n_in, n_out = len(spec["ins"]), len(spec["outs"])

    def body(*refs):
        parts = refs[:n_in], refs[n_in:n_in + n_out], refs[n_in + n_out:]
        spec["start"](*parts)
        spec["finish"](*parts)

    return _pcall(
        body, name=name, in_specs=[_ANY] * n_in, out_specs=[_ANY] * n_out, out_shape=list(spec["outs"]),
        scratch_shapes=list(spec["sems"]), compiler_params=pltpu.CompilerParams(has_side_effects=True),
    )(*spec["ins"])


def all_gather_2level_spec(x):
    def parts(ins, outs, sems):
        (x_ref,), (out_ref,), (send_sems, recv_sems, local_sem) = ins, outs, sems
        x_, y_, c_ = lax.axis_index("x"), lax.axis_index("y"), lax.axis_index("c")
        sibling = (x_, y_, 1 - c_)
        chips = [(1 - x_, y_), (x_, 1 - y_), (1 - x_, 1 - y_)]

        def slot(px, py, pc):
            return out_ref.at[4 * px + 2 * py + pc]

        def copy(k, block, to, src=None):
            return pltpu.make_async_remote_copy(
                src_ref=slot(*block) if src is None else src, dst_ref=slot(*block),
                send_sem=send_sems.at[k], recv_sem=recv_sems.at[k], device_id=to, device_id_type=pl.DeviceIdType.MESH)

        me = (x_, y_, c_)
        mine = pltpu.make_async_copy(x_ref, slot(*me), local_sem)
        first = [copy(0, me, sibling, src=x_ref)]
        first += [copy(1 + j, me, (*chip, c_), src=x_ref) for j, chip in enumerate(chips)]
        return copy, me, sibling, chips, c_, mine, first

    def start(ins, outs, sems):
        *_, mine, first = parts(ins, outs, sems)
        mine.start()
        for cp in first:
            cp.start()

    def finish(ins, outs, sems):
        copy, me, sibling, chips, c_, mine, first = parts(ins, outs, sems)
        passed = [copy(4 + j, (*chip, c_), sibling) for j, chip in enumerate(chips)]
        for j, chip in enumerate(chips):
            copy(1 + j, (*chip, c_), me).wait_recv()
            passed[j].start()
        copy(0, sibling, me).wait_recv()
        for j, chip in enumerate(chips):
            copy(4 + j, (*chip, 1 - c_), me).wait_recv()
        for cp in first + passed:
            cp.wait_send()
        mine.wait()

    return dict(ins=[x], outs=[jax.ShapeDtypeStruct((N_DEV,) + tuple(x.shape), x.dtype)],
                sems=[pltpu.SemaphoreType.DMA((7,)), pltpu.SemaphoreType.DMA((7,)), pltpu.SemaphoreType.DMA],
                start=start, finish=finish)


def all_gather_2level(name, x):
    return run_exchange(name, all_gather_2level_spec(x))[0]


def sibling_swap(name, x):
    shape = (4,) + tuple(x.shape[1:])

    def body(x_ref, theirs_ref, send_sems, recv_sems):
        x_, y_, c_ = lax.axis_index("x"), lax.axis_index("y"), lax.axis_index("c")
        sends = []
        for j in range(4):
            cp = pltpu.make_async_remote_copy(
                src_ref=x_ref.at[2 * j + 1 - c_], dst_ref=theirs_ref.at[j], send_sem=send_sems.at[j], recv_sem=recv_sems.at[j],
                device_id=(x_, y_, 1 - c_), device_id_type=pl.DeviceIdType.MESH)
            cp.start()
            sends.append(cp)
        for cp in sends:
            cp.wait()

    return _pcall(
        body, name=name, in_specs=[_ANY], out_specs=_ANY,
        out_shape=jax.ShapeDtypeStruct(shape, x.dtype),
        scratch_shapes=[pltpu.SemaphoreType.DMA((4,)), pltpu.SemaphoreType.DMA((4,))],
        compiler_params=pltpu.CompilerParams(has_side_effects=True),
    )(x)


def pair_add(name, stack, theirs, out_dtype):
    shp = theirs.shape
    c = shp[-1]
    r = math.prod(shp[1:-1])
    br = r
    for cand in (1024, 512, 256, 128, 64, 32, 16, 8):
        if r % cand == 0 and cand * c <= 256 * 1024:
            br = cand
            break

    def body(s0_ref, s1_ref, t_ref, o_ref):
        mine = jnp.where(lax.axis_index("c") == 0, s0_ref[...], s1_ref[...])
        o_ref[...] = (mine + t_ref[...]).astype(o_ref.dtype)

    s4 = stack.reshape(4, 2, r, c)
    out = _pcall(
        body, name=name, grid=(4, r // br),
        in_specs=[_bs((None, None, br, c), lambda j, i: (j, 0, i, 0)), _bs((None, None, br, c), lambda j, i: (j, 1, i, 0)),
                  _bs((None, br, c), lambda j, i: (j, i, 0))],
        out_specs=_bs((None, br, c), lambda j, i: (j, i, 0)),
        out_shape=jax.ShapeDtypeStruct((4, r, c), out_dtype), compiler_params=_cparams(),
    )(s4, s4, theirs.reshape(4, r, c))
    return out.reshape(shp)


def chip_exchange_spec(p):
    def parts(ins, outs, sems, sending=False):
        (p_ref,), (out_ref,), (send_sems, recv_sems, local_sem) = ins, outs, sems
        x_, y_, c_ = lax.axis_index("x"), lax.axis_index("y"), lax.axis_index("c")
        my_chip = 2 * x_ + y_
        chips = [(1 - x_, y_), (x_, 1 - y_), (1 - x_, 1 - y_)]
        local = pltpu.make_async_copy(p_ref.at[my_chip], out_ref.at[my_chip], local_sem)
        sends = [pltpu.make_async_remote_copy(
            src_ref=p_ref.at[2 * px + py], dst_ref=out_ref.at[my_chip], send_sem=send_sems.at[k], recv_sem=recv_sems.at[k],
            device_id=(px, py, c_), device_id_type=pl.DeviceIdType.MESH) for k, (px, py) in enumerate(chips)]
        recvs = [] if sending else [pltpu.make_async_remote_copy(
            src_ref=p_ref.at[my_chip], dst_ref=out_ref.at[2 * px + py], send_sem=send_sems.at[k], recv_sem=recv_sems.at[k],
            device_id=(px, py, c_), device_id_type=pl.DeviceIdType.MESH) for k, (px, py) in enumerate(chips)]
        return local, sends, recvs

    def start(ins, outs, sems):
        local, sends, _ = parts(ins, outs, sems, sending=True)
        local.start()
        for cp in sends:
            cp.start()

    def finish(ins, outs, sems):
        local, sends, recvs = parts(ins, outs, sems)
        for cp in recvs:
            cp.wait_recv()
        for cp in sends:
            cp.wait_send()
        local.wait()

    return dict(ins=[p], outs=[jax.ShapeDtypeStruct(p.shape, p.dtype)],
                sems=[pltpu.SemaphoreType.DMA((3,)), pltpu.SemaphoreType.DMA((3,)), pltpu.SemaphoreType.DMA],
                start=start, finish=finish)


def chip_exchange(name, p):
    return run_exchange(name, chip_exchange_spec(p))[0]


def chip_partials(name, stack):
    theirs = sibling_swap(name + "_d2d", stack)
    return pair_add(name + "_add", stack, theirs, BF16)


def reduce_scatter_2level(name, stack):
    return chip_exchange(name + "_ici", chip_partials(name, stack))


def ln_res_f(h, r, g, b):
    x = ALPHA * h + r
    mu = jnp.mean(x, axis=-1, keepdims=True)
    var = jnp.mean(jnp.square(x - mu), axis=-1, keepdims=True)
    return (x - mu) * lax.rsqrt(var + LN_EPS) * g + b


def rms_f(x, g):
    return x * lax.rsqrt(jnp.mean(x * x, axis=-1, keepdims=True) + RMS_EPS) * g


def rope_f(x1, x2, c, s):
    return x1 * c - x2 * s, x1 * s + x2 * c


def swiglu_f(g, u):
    return jax.nn.silu(g) * u


def tshift_f(d, dprev, mu):
    return d + (dprev - d) * mu


def loss_f(y, tgt):
    e = y - tgt
    return e / y.shape[-1], 0.5 * jnp.mean(e * e, axis=-1, keepdims=True)


def adamw(name, w, m, v, gstack):
    shp = w.shape
    c = shp[-1]
    r = math.prod(shp[:-1])
    br = r
    for cand in (512, 256, 128, 64, 32, 16, 8):
        if r % cand == 0 and cand * c <= 128 * 1024:
            br = cand
            break
    k = gstack.shape[0]

    def body(w_ref, m_ref, v_ref, g_ref, go_ref, d_ref, mo_ref, vo_ref):
        g = g_ref[0].astype(F32)
        for j in range(1, k):
            g = g + g_ref[j].astype(F32)
        m_new = ADAM_B1 * m_ref[...] + (1.0 - ADAM_B1) * g
        v_new = ADAM_B2 * v_ref[...] + (1.0 - ADAM_B2) * jnp.square(g)
        m_hat = m_new / (1.0 - ADAM_B1 ** ADAM_STEP)
        v_hat = v_new / (1.0 - ADAM_B2 ** ADAM_STEP)
        go_ref[...] = g
        d_ref[...] = -ADAM_LR * (m_hat / (jnp.sqrt(v_hat) + ADAM_EPS) + ADAM_WD * w_ref[...])
        mo_ref[...] = m_new
        vo_ref[...] = v_new

    spec = _bs((br, c), lambda i: (i, 0))
    outs = _pcall(
        body, name=name, grid=(r // br,),
        in_specs=[spec, spec, spec, _bs((k, br, c), lambda i: (0, i, 0))],
        out_specs=[spec] * 4, out_shape=[jax.ShapeDtypeStruct((r, c), F32)] * 4,
        compiler_params=_cparams(),
    )(w.reshape(r, c), m.reshape(r, c), v.reshape(r, c), gstack.reshape(k, r, c))
    return tuple(o.reshape(shp) for o in outs)


EV_W = (512, 256, 64, 512, 512, 1024, 1024, 16)
EV_IN, EV_PAD = 3920, 4096
OD_IN, OD_PAD = 6592, 6656
TM = 256


def _offsets(widths):
    offs, acc = [], 0
    for w in widths:
        offs.append((acc, acc + w))
        acc += w
    return offs


def _rope_tables(seq, dim):
    inv = 10000.0 ** (-jnp.arange(0, dim, 2, dtype=F32) / dim)
    ang = jnp.arange(seq, dtype=F32)[:, None] * inv[None, :]
    return jnp.cos(ang), jnp.sin(ang)


def _halves(x, nh):
    t, w = x.shape
    x3 = x.reshape(t, nh, w // nh)
    hd = w // nh // 2
    return x3[:, :, :hd].reshape(t, nh * hd), x3[:, :, hd:].reshape(t, nh * hd)


def _unhalves(x1, x2, nh):
    t = x1.shape[0]
    return jnp.concatenate([x1.reshape(t, nh, -1), x2.reshape(t, nh, -1)], axis=2).reshape(t, -1)


def ln_res2_f(h, r, g, b):
    y = ln_res_f(h, r, g, b)
    return y, y


def _ln(name, h, r, g, b, cts=None):
    if cts is None:
        return _rows(name, ln_res2_f, [h, r], [g, b], [h.shape[1]] * 2, TM, dtypes=[F32, BF16])
    return _rows(name + "_bwd", ln_res_f, [h, r], [g, b], None, TM, fwd=False, cts=[cts], wrt_rows=(0, 1), wrt_params=(0, 1),
                 dtypes=[F32, BF16])


def _tail_fwd(l, h, hb, mem2b, bsz, p):
    qx = matmul(f"xa_q{l}", hb, p["xa_w_q"][l])
    kx = matmul(f"xa_k{l}", mem2b, p["xa_w_k"][l])
    vx = matmul(f"xa_v{l}", mem2b, p["xa_w_v"][l])
    ox = xattn(bsz, qx, kx, vx)
    xa = matmul(f"xa_o{l}", ox, p["xa_w_o"][l])
    h2, h2b = _ln(f"ln_xa{l}", h, xa, p["ln_xa_g"][l:l + 1], p["ln_xa_b"][l:l + 1])
    gg = matmul(f"ffn_g{l}", h2b, p["ffn_w_gate"][l])
    uu = matmul(f"ffn_u{l}", h2b, p["ffn_w_up"][l])
    act = _rows(f"swiglu{l}", swiglu_f, [gg, uu], [], [gg.shape[1]], TM, dtypes=[BF16])[0]
    ff = matmul(f"ffn_d{l}", act, p["ffn_w_down"][l])
    h3, h3b = _ln(f"ln_ffn{l}", h2, ff, p["ln_ffn_g"][l:l + 1], p["ln_ffn_b"][l:l + 1])
    return h3, h3b, (h, hb, qx, kx, vx, ox, xa, h2, h2b, gg, uu, act, ff)


def _tail_bwd(l, dh3, saved, mem2b, bsz, p, gr):
    h, hb, qx, kx, vx, ox, xa, h2, h2b, gg, uu, act, ff = saved
    dh2, dff, gr["ln_ffn_g"][l], gr["ln_ffn_b"][l] = _ln(f"ln_ffn{l}", h2, ff, p["ln_ffn_g"][l:l + 1], p["ln_ffn_b"][l:l + 1], cts=dh3)
    dact = matmul(f"ffn_d_dx{l}", dff, p["ffn_w_down"][l], tb=True)
    gr["ffn_w_down"][l] = matmul(f"ffn_d_dw{l}", act, dff, ta=True)
    dgg, duu = _rows(f"swiglu_bwd{l}", swiglu_f, [gg, uu], [], None, TM, fwd=False, cts=[dact], wrt_rows=(0, 1),
                     dtypes=[BF16, BF16])
    gr["ffn_w_gate"][l] = matmul(f"ffn_g_dw{l}", h2b, dgg, ta=True)
    gr["ffn_w_up"][l] = matmul(f"ffn_u_dw{l}", h2b, duu, ta=True)
    dh2 = matmul(f"ffn_g_dx{l}", dgg, p["ffn_w_gate"][l], tb=True, add=dh2)
    dh2 = matmul(f"ffn_u_dx{l}", duu, p["ffn_w_up"][l], tb=True, add=dh2)
    dh, dxa, gr["ln_xa_g"][l], gr["ln_xa_b"][l] = _ln(f"ln_xa{l}", h, xa, p["ln_xa_g"][l:l + 1], p["ln_xa_b"][l:l + 1], cts=dh2)
    dox = matmul(f"xa_o_dx{l}", dxa, p["xa_w_o"][l], tb=True)
    gr["xa_w_o"][l] = matmul(f"xa_o_dw{l}", ox, dxa, ta=True)
    dqx, dkx, dvx = xattn(bsz, qx, kx, vx, cts=dox)
    gr["xa_w_q"][l] = matmul(f"xa_q_dw{l}", hb, dqx, ta=True)
    gr["xa_w_k"][l] = matmul(f"xa_k_dw{l}", mem2b, dkx, ta=True)
    gr["xa_w_v"][l] = matmul(f"xa_v_dw{l}", mem2b, dvx, ta=True)
    return matmul(f"xa_q_dx{l}", dqx, p["xa_w_q"][l], tb=True, add=dh)


def _uq_perm(w):
    w3 = w.reshape(w.shape[0], 8, 192)
    return jnp.concatenate([w3[:, :, :128].reshape(-1, 1024), w3[:, :, 128:160].reshape(-1, 256),
                            w3[:, :, 160:].reshape(-1, 256)], axis=1)


def _uq_unperm(g):
    r = g.shape[0]
    return jnp.concatenate([g[:, :1024].reshape(r, 8, 128), g[:, 1024:1280].reshape(r, 8, 32),
                            g[:, 1280:].reshape(r, 8, 32)], axis=2).reshape(r, 1536)


def _ukv_perm(w):
    w3 = w.reshape(w.shape[0], 8, 256)
    return jnp.concatenate([w3[:, :, :128].reshape(-1, 1024), w3[:, :, 128:].reshape(-1, 1024)], axis=1)


def _ukv_unperm(g):
    r = g.shape[0]
    return jnp.concatenate([g[:, :1024].reshape(r, 8, 128), g[:, 1024:].reshape(r, 8, 128)], axis=2).reshape(r, 2048)


def _pad_cols(w, n):
    return jnp.pad(w, ((0, 0), (0, n - w.shape[1])))


def _shift_prev(x, bsz):
    t, w = x.shape
    x3 = x.reshape(bsz, t // bsz, w)
    return jnp.pad(x3, ((0, 0), (1, 0), (0, 0)))[:, :-1].reshape(t, w)


def _shift_next(x, bsz):
    t, w = x.shape
    x3 = x.reshape(bsz, t // bsz, w)
    return jnp.pad(x3[:, 1:], ((0, 0), (0, 1), (0, 0))).reshape(t, w)


LATE = ("xa_w_q", "xa_w_k", "xa_w_v", "xa_w_o", "ffn_w_gate", "ffn_w_up", "ffn_w_down")


def _late_partials(layer, gr):
    return [chip_partials(f"rs{layer}_{k}", _shard_stack(gr[k][layer], BIG[k])) for k in LATE]


def device_step(x, mem, tgt, p, late_local):
    bsz, seq, d = x.shape
    t = bsz * seq
    x2, mem2, tgt2 = x.reshape(t, d), mem.reshape(bsz * mem.shape[1], d), tgt.reshape(t, d)
    x2b, mem2b = x2.astype(BF16), mem2.astype(BF16)
    gr = {k: [None] * DEPTH for k in ("ln_mix_g", "ln_mix_b", "xa_w_q", "xa_w_k", "xa_w_v", "xa_w_o", "ln_xa_g", "ln_xa_b",
                                      "ffn_w_gate", "ffn_w_up", "ffn_w_down", "ln_ffn_g", "ln_ffn_b")}
    cos_pe, sin_pe = _rope_tables(seq, 64)
    cos_c, sin_c = _rope_tables(seq, 128)
    cq, sq = jnp.tile(cos_pe, (bsz, 8)), jnp.tile(sin_pe, (bsz, 8))
    ck, sk = jnp.tile(cos_pe, (bsz, 1)), jnp.tile(sin_pe, (bsz, 1))
    cd, sd = jnp.tile(cos_c, (bsz, 8)), jnp.tile(sin_c, (bsz, 8))

    w_in0 = _pad_cols(p["ev_w_in"][0], EV_PAD)
    w_uq, w_ukv = _uq_perm(p["ev_mla_w_uq"][0]), _ukv_perm(p["ev_mla_w_ukv"][0])
    z0 = matmul("ev_in", x2b, w_in0)
    c_q, c_kv, k_pe, q_g, k_g, v_g, r_g, lr_g = (z0[:, a:b] for a, b in _offsets(EV_W))
    qr = _rows("q_rms", rms_f, [c_q], [p["ev_mla_q_norm"]], [512], TM, dtypes=[BF16])[0]
    kvr = _rows("kv_rms", rms_f, [c_kv], [p["ev_mla_kv_norm"]], [256], TM, dtypes=[BF16])[0]
    q = matmul("mla_uq", qr, w_uq)
    kv = matmul("mla_ukv", kvr, w_ukv)
    qn, qp1, qp2 = q[:, :1024], q[:, 1024:1280], q[:, 1280:]
    kn, vv = kv[:, :1024], kv[:, 1024:]
    kp1, kp2 = k_pe[:, :32], k_pe[:, 32:]
    r1, r2 = _rows("rope_q", rope_f, [qp1, qp2, cq, sq], [], [256, 256], TM)
    kr1, kr2 = _rows("rope_k", rope_f, [kp1, kp2, ck, sk], [], [32, 32], TM)
    r1h, r2h = _to_heads(r1, bsz, 8), _to_heads(r2, bsz, 8)
    a_out = mla_attn(bsz, qn, r1h, r2h, kn, kr1, kr2, vv)
    gla_prm = (p["ev_gla_w_gate2"][0], p["ev_gla_b_gate"], p["ev_gla_norm_g"], p["ev_gla_norm_b"])
    b_out, gla_sv = gla_block(bsz, q_g, k_g, v_g, r_g, lr_g, *gla_prm)
    mixin0 = jnp.concatenate([a_out, b_out], axis=1)
    mix0 = matmul("ev_out", mixin0, p["ev_w_out"][0])
    h1, h1b = _ln("ln_mix0", x2, mix0, p["ln_mix_g"][0:1], p["ln_mix_b"][0:1])
    h3, h3b, tail0 = _tail_fwd(0, h1, h1b, mem2b, bsz, p)

    w_in1 = _pad_cols(p["od_w_in"][0], OD_PAD)
    z1 = matmul("od_in", h3b, w_in1)
    dq_, dk_, dv_ = z1[:, :1024], z1[:, 1024:2048], z1[:, 2048:3072]
    d_in = z1[:, 3072:OD_IN]
    q1, q2 = _halves(dq_, 8)
    k1, k2 = _halves(dk_, 8)
    qd1, qd2 = _rows("rope_dq", rope_f, [q1, q2, cd, sd], [], [512, 512], TM)
    kd1, kd2 = _rows("rope_dk", rope_f, [k1, k2, cd, sd], [], [512, 512], TM)
    c_out = dil_block(bsz, qd1, qd2, kd1, kd2, dv_)
    d_prev = _shift_prev(d_in, bsz)
    mu = p["od_rwkv_mu"]
    ds = _rows("tshift", tshift_f, [d_in, d_prev], [mu], [d_in.shape[1]], TM)[0]
    rw_in = tuple(ds[:, a:b] for a, b in _offsets((1024, 1024, 1024, 96, 96, 256)))
    rw_prm = dict(w0=p["od_rwkv_w0"], wd2=p["od_rwkv_w_decay2"][0], a0=p["od_rwkv_a0"], wa2=p["od_rwkv_w_a2"][0],
                  wg2=p["od_rwkv_w_gate2"][0], k_k=p["od_rwkv_k_k"], k_a=p["od_rwkv_k_a"], r_k=p["od_rwkv_r_k"][0],
                  gn_g=p["od_rwkv_gn_g"], gn_b=p["od_rwkv_gn_b"])
    d_out, rw_scan, gathered = rwkv_block(bsz, *rw_in, rw_prm, comm=[all_gather_2level_spec(late_local[k]) for k in LATE])
    for k, g in zip(LATE, gathered):
        p[k][1] = _unshard(g, late_local[k].shape, BIG[k])
    mixin1 = jnp.concatenate([c_out, d_out], axis=1)
    mix1 = matmul("od_out", mixin1, p["od_w_out"][0])
    h4, h4b = _ln("ln_mix1", h3, mix1, p["ln_mix_g"][1:2], p["ln_mix_b"][1:2])
    y, _, tail1 = _tail_fwd(1, h4, h4b, mem2b, bsz, p)

    dy, row_loss = _rows("loss", loss_f, [y, tgt2], [], [d, 1], TM)
    loss = jnp.sum(row_loss)

    dh4 = _tail_bwd(1, dy, tail1, mem2b, bsz, p, gr)
    dh3, dmix1, gr["ln_mix_g"][1], gr["ln_mix_b"][1] = _ln("ln_mix1", h3, mix1, p["ln_mix_g"][1:2], p["ln_mix_b"][1:2], cts=dh4)
    dmixin1 = matmul("od_out_dx", dmix1, p["od_w_out"][0], tb=True)
    gr["od_w_out"] = matmul("od_out_dw", mixin1, dmix1, ta=True)[None]
    dc_out, dd_out = dmixin1[:, :1024], dmixin1[:, 1024:]
    drw_in, drw_prm, done1 = rwkv_block(bsz, *rw_in, rw_prm, cts=dd_out, scan=rw_scan,
                                        comm=[chip_exchange_spec(x_) for x_ in _late_partials(1, gr)])
    dds = jnp.concatenate(drw_in, axis=1)
    dd_in, dd_prev, dmu = _rows("tshift_bwd", tshift_f, [d_in, d_prev], [mu], None, TM, fwd=False, cts=[dds],
                                wrt_rows=(0, 1), wrt_params=(0,))
    dd_in = dd_in + _shift_next(dd_prev, bsz)
    dqd1, dqd2, dkd1, dkd2, ddv = dil_block(bsz, qd1, qd2, kd1, kd2, dv_, cts=dc_out)
    dq1, dq2 = _rows("rope_dq_bwd", rope_f, [q1, q2, cd, sd], [], None, TM, fwd=False, cts=[dqd1, dqd2], wrt_rows=(0, 1))
    dk1, dk2 = _rows("rope_dk_bwd", rope_f, [k1, k2, cd, sd], [], None, TM, fwd=False, cts=[dkd1, dkd2], wrt_rows=(0, 1))
    dz1 = jnp.concatenate([_unhalves(dq1, dq2, 8), _unhalves(dk1, dk2, 8), ddv, dd_in,
                           jnp.zeros((t, OD_PAD - OD_IN), F32)], axis=1).astype(BF16)
    gr["od_w_in"] = matmul("od_in_dw", h3b, dz1, ta=True)[:, :OD_IN][None]
    dh3 = matmul("od_in_dx", dz1, w_in1, tb=True, add=dh3)
    gr["od_rwkv_mu"] = dmu
    gr["od_rwkv_w0"], gr["od_rwkv_w_decay2"], gr["od_rwkv_a0"] = drw_prm["w0"], drw_prm["wd2"][None], drw_prm["a0"]
    gr["od_rwkv_w_a2"], gr["od_rwkv_w_gate2"] = drw_prm["wa2"][None], drw_prm["wg2"][None]
    gr["od_rwkv_k_k"], gr["od_rwkv_k_a"], gr["od_rwkv_r_k"] = drw_prm["k_k"], drw_prm["k_a"], drw_prm["r_k"][None]
    gr["od_rwkv_gn_g"], gr["od_rwkv_gn_b"] = drw_prm["gn_g"], drw_prm["gn_b"]

    dh1 = _tail_bwd(0, dh3, tail0, mem2b, bsz, p, gr)
    dx2, dmix0, gr["ln_mix_g"][0], gr["ln_mix_b"][0] = _ln("ln_mix0", x2, mix0, p["ln_mix_g"][0:1], p["ln_mix_b"][0:1], cts=dh1)
    dmixin0 = matmul("ev_out_dx", dmix0, p["ev_w_out"][0], tb=True)
    gr["ev_w_out"] = matmul("ev_out_dw", mixin0, dmix0, ta=True)[None]
    da_out, db_out = dmixin0[:, :1024], dmixin0[:, 1024:]
    dq_g, dk_g, dv_g, dlr4, dr_g, dw2, dbg, dng, dnb = gla_block(bsz, q_g, k_g, v_g, r_g, lr_g, *gla_prm, cts=(db_out, gla_sv))
    dlr_g = jnp.sum(dlr4, axis=0)
    dqn, dr1h, dr2h, dkn, dkr1, dkr2, dvv, *done0 = mla_attn(bsz, qn, r1h, r2h, kn, kr1, kr2, vv, cts=da_out,
                                                             comm=[chip_exchange_spec(x_) for x_ in _late_partials(0, gr)])
    dqp1, dqp2 = _rows("rope_q_bwd", rope_f, [qp1, qp2, cq, sq], [], None, TM, fwd=False,
                       cts=[_from_heads(dr1h), _from_heads(dr2h)], wrt_rows=(0, 1))
    dkp1, dkp2 = _rows("rope_k_bwd", rope_f, [kp1, kp2, ck, sk], [], None, TM, fwd=False, cts=[dkr1, dkr2], wrt_rows=(0, 1))
    dq = jnp.concatenate([dqn, dqp1, dqp2], axis=1).astype(BF16)
    dkv = jnp.concatenate([dkn, dvv], axis=1).astype(BF16)
    dqr = matmul("mla_uq_dx", dq, w_uq, tb=True)
    gr["ev_mla_w_uq"] = _uq_unperm(matmul("mla_uq_dw", qr, dq, ta=True))[None]
    dkvr = matmul("mla_ukv_dx", dkv, w_ukv, tb=True)
    gr["ev_mla_w_ukv"] = _ukv_unperm(matmul("mla_ukv_dw", kvr, dkv, ta=True))[None]
    dc_q, gr["ev_mla_q_norm"] = _rows("q_rms_bwd", rms_f, [c_q], [p["ev_mla_q_norm"]], None, TM, fwd=False, cts=[dqr],
                                      wrt_rows=(0,), wrt_params=(0,))
    dc_kv, gr["ev_mla_kv_norm"] = _rows("kv_rms_bwd", rms_f, [c_kv], [p["ev_mla_kv_norm"]], None, TM, fwd=False, cts=[dkvr],
                                        wrt_rows=(0,), wrt_params=(0,))
    dz0 = jnp.concatenate([dc_q, dc_kv, dkp1, dkp2, dq_g, dk_g, dv_g, dr_g, dlr_g,
                           jnp.zeros((t, EV_PAD - EV_IN), F32)], axis=1).astype(BF16)
    gr["ev_w_in"] = matmul("ev_in_dw", x2b, dz0, ta=True)[:, :EV_IN][None]
    dx2 = matmul("ev_in_dx", dz0, w_in0, tb=True, add=dx2)
    gr["ev_gla_w_gate2"], gr["ev_gla_b_gate"] = dw2[None], dbg
    gr["ev_gla_norm_g"], gr["ev_gla_norm_b"] = dng, dnb
    stacks = {k: jnp.stack([s0, s1], axis=1) for k, s0, s1 in zip(LATE, done0, done1)}
    for k in LATE:
        del gr[k]
    for k in list(gr):
        if isinstance(gr[k], list):
            gr[k] = jnp.stack([g[0] if k.startswith("ln_") else g for g in gr[k]])
    return loss, dx2.reshape(bsz, seq, d), gr, stacks


WEIGHTS = ['ev_w_in', 'ev_mla_q_norm', 'ev_mla_w_uq', 'ev_mla_kv_norm', 'ev_mla_w_ukv', 'ev_gla_w_gate2', 'ev_gla_b_gate',
           'ev_gla_norm_g', 'ev_gla_norm_b', 'ev_w_out', 'od_w_in', 'od_rwkv_mu', 'od_rwkv_w0', 'od_rwkv_w_decay2',
           'od_rwkv_a0', 'od_rwkv_w_a2', 'od_rwkv_w_gate2', 'od_rwkv_k_k', 'od_rwkv_k_a', 'od_rwkv_r_k', 'od_rwkv_gn_g',
           'od_rwkv_gn_b', 'od_w_out', 'ln_mix_g', 'ln_mix_b', 'xa_w_q', 'xa_w_k', 'xa_w_v', 'xa_w_o', 'ln_xa_g', 'ln_xa_b',
           'ffn_w_gate', 'ffn_w_up', 'ffn_w_down', 'ln_ffn_g', 'ln_ffn_b']
BIG = {'ev_w_in': -1, 'ev_mla_w_uq': -1, 'ev_mla_w_ukv': -1, 'ev_w_out': -2, 'od_w_in': -1, 'od_w_out': -2,
       'xa_w_q': -2, 'xa_w_k': -2, 'xa_w_v': -2, 'xa_w_o': -2, 'ffn_w_gate': -1, 'ffn_w_up': -1, 'ffn_w_down': -2}
SMALL = ['ev_gla_w_gate2', 'od_rwkv_mu', 'od_rwkv_w0', 'od_rwkv_w_decay2', 'od_rwkv_a0', 'od_rwkv_w_a2', 'od_rwkv_w_gate2',
         'od_rwkv_k_k', 'od_rwkv_k_a', 'od_rwkv_gn_g', 'od_rwkv_gn_b']
REPL = ['ev_mla_q_norm', 'ev_mla_kv_norm', 'ev_gla_b_gate', 'ev_gla_norm_g', 'ev_gla_norm_b', 'od_rwkv_r_k',
        'ln_mix_g', 'ln_mix_b', 'ln_xa_g', 'ln_xa_b', 'ln_ffn_g', 'ln_ffn_b']
PACK_COLS = 128


def _unshard(g, shape, axis):
    axis %= len(shape)
    full = list(shape)
    full[axis] *= N_DEV
    return jnp.moveaxis(g, 0, axis).reshape(full)


def _shard_stack(gfull, axis):
    axis %= gfull.ndim
    shp = list(gfull.shape)
    shp[axis:axis + 1] = [N_DEV, shp[axis] // N_DEV]
    return jnp.moveaxis(gfull.reshape(shp), axis, 0)


def _pack(arrs):
    lead = arrs[0].shape[0]
    flat = jnp.concatenate([a.reshape(lead, -1) for a in arrs], axis=1)
    n = flat.shape[1]
    rows = -(-n // (8 * PACK_COLS)) * 8
    return jnp.pad(flat, ((0, 0), (0, rows * PACK_COLS - n))).reshape(lead, rows, PACK_COLS)


def _unpack(buf, shapes):
    lead = buf.shape[0]
    flat = buf.reshape(lead, -1)
    out, off = [], 0
    for shp in shapes:
        n = math.prod(shp)
        out.append(flat[:, off:off + n].reshape((lead,) + tuple(shp)))
        off += n
    return out


def train_step(x, mem, loss_target, w, m, v):
    p, late_local = {}, {}
    for k, ax in BIG.items():
        if k in LATE:
            w0 = w[k][0].astype(BF16)
            p[k] = [_unshard(all_gather_2level("ag0_" + k, w0), w0.shape, ax), None]
            late_local[k] = w[k][1].astype(BF16)
        else:
            p[k] = _unshard(all_gather_2level("ag_" + k, w[k].astype(BF16)), w[k].shape, ax)
    small_loc = _pack([w[k][None] for k in SMALL])[0]
    small_all = _unpack(all_gather("ag_small", small_loc), [w[k].shape for k in SMALL])
    for k, g in zip(SMALL, small_all):
        p[k] = _unshard(g, w[k].shape, -1)
    for k in REPL:
        p[k] = w[k]
    loss, dx, gr, stacks = device_step(x, mem, loss_target, p, late_local)
    loss = lax.psum(loss, ("x", "y", "c"))

    for k, ax in BIG.items():
        if k not in LATE:
            stacks[k] = reduce_scatter_2level("rs_" + k, _shard_stack(gr[k], ax))
    small_send = _pack([_shard_stack(gr[k], -1) for k in SMALL])
    small_recv = _unpack(reduce_scatter_exchange("rs_small", small_send), [w[k].shape for k in SMALL])
    stacks.update(zip(SMALL, small_recv))
    repl_loc = _pack([gr[k][None] for k in REPL])[0]
    repl_all = _unpack(all_gather("ag_repl_grads", repl_loc), [w[k].shape for k in REPL])
    stacks.update(zip(REPL, repl_all))

    grads, deltas, new_m, new_v = [], [], [], []
    for k in WEIGHTS:
        g, dl, mn, vn = adamw("adamw_" + k, w[k], m[k], v[k], stacks[k])
        grads.append(g), deltas.append(dl), new_m.append(mn), new_v.append(vn)
    return (loss, dx, *grads, *deltas, *new_m, *new_v)


def kernel(x, mem, ev_w_in, ev_mla_q_norm, ev_mla_w_uq, ev_mla_kv_norm, ev_mla_w_ukv, ev_gla_w_gate2, ev_gla_b_gate, ev_gla_norm_g, ev_gla_norm_b, ev_w_out, od_w_in, od_rwkv_mu, od_rwkv_w0, od_rwkv_w_decay2, od_rwkv_a0, od_rwkv_w_a2, od_rwkv_w_gate2, od_rwkv_k_k, od_rwkv_k_a, od_rwkv_r_k, od_rwkv_gn_g, od_rwkv_gn_b, od_w_out, ln_mix_g, ln_mix_b, xa_w_q, xa_w_k, xa_w_v, xa_w_o, ln_xa_g, ln_xa_b, ffn_w_gate, ffn_w_up, ffn_w_down, ln_ffn_g, ln_ffn_b, loss_target, m_ev_w_in, m_ev_mla_q_norm, m_ev_mla_w_uq, m_ev_mla_kv_norm, m_ev_mla_w_ukv, m_ev_gla_w_gate2, m_ev_gla_b_gate, m_ev_gla_norm_g, m_ev_gla_norm_b, m_ev_w_out, m_od_w_in, m_od_rwkv_mu, m_od_rwkv_w0, m_od_rwkv_w_decay2, m_od_rwkv_a0, m_od_rwkv_w_a2, m_od_rwkv_w_gate2, m_od_rwkv_k_k, m_od_rwkv_k_a, m_od_rwkv_r_k, m_od_rwkv_gn_g, m_od_rwkv_gn_b, m_od_w_out, m_ln_mix_g, m_ln_mix_b, m_xa_w_q, m_xa_w_k, m_xa_w_v, m_xa_w_o, m_ln_xa_g, m_ln_xa_b, m_ffn_w_gate, m_ffn_w_up, m_ffn_w_down, m_ln_ffn_g, m_ln_ffn_b, v_ev_w_in, v_ev_mla_q_norm, v_ev_mla_w_uq, v_ev_mla_kv_norm, v_ev_mla_w_ukv, v_ev_gla_w_gate2, v_ev_gla_b_gate, v_ev_gla_norm_g, v_ev_gla_norm_b, v_ev_w_out, v_od_w_in, v_od_rwkv_mu, v_od_rwkv_w0, v_od_rwkv_w_decay2, v_od_rwkv_a0, v_od_rwkv_w_a2, v_od_rwkv_w_gate2, v_od_rwkv_k_k, v_od_rwkv_k_a, v_od_rwkv_r_k, v_od_rwkv_gn_g, v_od_rwkv_gn_b, v_od_w_out, v_ln_mix_g, v_ln_mix_b, v_xa_w_q, v_xa_w_k, v_xa_w_v, v_xa_w_o, v_ln_xa_g, v_ln_xa_b, v_ffn_w_gate, v_ffn_w_up, v_ffn_w_down, v_ln_ffn_g, v_ln_ffn_b):
    given = dict(locals())
    w = {k: given[k] for k in WEIGHTS}
    m = {k: given["m_" + k] for k in WEIGHTS}
    v = {k: given["v_" + k] for k in WEIGHTS}
    return train_step(given["x"], given["mem"], given["loss_target"], w, m, v)
```

```python
import functools
import math

import jax
import jax.numpy as jnp
from jax import lax
from jax.experimental import pallas as pl
from jax.experimental.pallas import tpu as pltpu

F32 = jnp.float32
BF16 = jnp.bfloat16
VMEM_LIMIT = 56 * 1024 * 1024
ROWS_VMEM = 20 * 1024 * 1024

N_DEV = 8
DEPTH = 2
ALPHA = (2.0 * DEPTH) ** 0.25
LN_EPS = 1e-5
RMS_EPS = 1e-6
RWKV_GN_EPS = 64e-5
ADAM_LR, ADAM_B1, ADAM_B2, ADAM_EPS, ADAM_WD, ADAM_STEP = 0.001, 0.9, 0.999, 1e-08, 0.01, 10
NEG_INF = float("-inf")


def _pcall(body, **kw):
    return pl.pallas_call(body, **kw)


def _cparams(**kw):
    return pltpu.CompilerParams(vmem_limit_bytes=VMEM_LIMIT, **kw)


def _dg(a, b, ca, cb, batch):
    nb = 1 if batch else 0
    dims = (((ca + nb,), (cb + nb,)), ((0,), (0,)) if batch else ((), ()))
    return lax.dot_general(a.astype(BF16), b.astype(BF16), dims, preferred_element_type=F32)


@functools.partial(jax.custom_vjp, nondiff_argnums=(2, 3, 4))
def _mm(a, b, ta, tb, batch):
    return _dg(a, b, 0 if ta else 1, 1 if tb else 0, batch)


def _mm_fwd(a, b, ta, tb, batch):
    return _mm(a, b, ta, tb, batch), (a, b)


def _mm_bwd(ta, tb, batch, res, g):
    a, b = res
    if not ta and not tb:
        da, db = _mm(g, b, False, True, batch), _mm(a, g, True, False, batch)
    elif not ta and tb:
        da, db = _mm(g, b, False, False, batch), _mm(g, a, True, False, batch)
    elif ta and not tb:
        da, db = _mm(b, g, False, True, batch), _mm(a, g, False, False, batch)
    else:
        da, db = _mm(b, g, True, True, batch), _mm(g, a, True, True, batch)
    return da.astype(a.dtype), db.astype(b.dtype)


_mm.defvjp(_mm_fwd, _mm_bwd)


def mm(a, b, ta=False, tb=False):
    return _mm(a, b, ta, tb, a.ndim == 3)


def _bs(block, imap):
    return pl.BlockSpec(block, imap)


def _rev_imap(imap, n):
    def r(*idx):
        return imap(*idx[:-1], n - 1 - idx[-1])
    return r


def _gcall(body, name, grid, in_specs, out_specs, out_shape, scratch_shapes, args, comm=None):
    comm = comm or []
    n_in, n_out, n_scr = len(in_specs), len(out_specs), len(scratch_shapes)
    c_in = [a for c in comm for a in c["ins"]]
    c_out = [o for c in comm for o in c["outs"]]
    c_sem = [s for c in comm for s in c["sems"]]

    def body2(*refs):
        i = 0
        r_in, i = refs[i:i + n_in], i + n_in
        k_in, i = refs[i:i + len(c_in)], i + len(c_in)
        r_out, i = refs[i:i + n_out], i + n_out
        k_out, i = refs[i:i + len(c_out)], i + len(c_out)
        r_scr, k_sem = refs[i:i + n_scr], refs[i + n_scr:]

        def each(which):
            a = b = s = 0
            for c in comm:
                na, nb, ns = len(c["ins"]), len(c["outs"]), len(c["sems"])
                c[which](k_in[a:a + na], k_out[b:b + nb], k_sem[s:s + ns])
                a, b, s = a + na, b + nb, s + ns

        if comm:
            first = last = None
            for ax, n in enumerate(grid):
                f0, l0 = pl.program_id(ax) == 0, pl.program_id(ax) == n - 1
                first = f0 if first is None else jnp.logical_and(first, f0)
                last = l0 if last is None else jnp.logical_and(last, l0)
            pl.when(first)(lambda: each("start"))
        body(*r_in, *r_out, *r_scr)
        if comm:
            pl.when(last)(lambda: each("finish"))

    return _pcall(
        body2, name=name, grid=grid,
        in_specs=list(in_specs) + [_ANY] * len(c_in), out_specs=list(out_specs) + [_ANY] * len(c_out),
        out_shape=list(out_shape) + c_out, scratch_shapes=list(scratch_shapes) + c_sem,
        compiler_params=_cparams(has_side_effects=True) if comm else _cparams(),
    )(*args, *c_in)


def p_fwd(name, f, grid, ins, outs, carry=None, save_carry=None, comm=None):
    n_in, n_out = len(ins), len(outs)

    def body(*refs):
        in_refs = refs[:n_in]
        out_refs = refs[n_in:n_in + n_out]
        rest = refs[n_in + n_out:]
        vals = [r[...] for r in in_refs]
        if carry is None:
            res = f(*vals)
        else:
            if save_carry is not None:
                sv_ref, c_ref = rest
            else:
                (c_ref,) = rest

            @pl.when(pl.program_id(len(grid) - 1) == 0)
            def _():
                c_ref[...] = jnp.zeros(c_ref.shape, c_ref.dtype)

            c = c_ref[...]
            if save_carry is not None:
                sv_ref[...] = c
            res = f(c, *vals)
            c_ref[...] = res[0]
            res = res[1:]
        if not isinstance(res, (tuple, list)):
            res = (res,)
        for r, v in zip(out_refs, res):
            r[...] = v.astype(r.dtype)

    out_shape = [jax.ShapeDtypeStruct(s, d) for (s, d, _, _) in outs]
    out_specs = [_bs(b, m) for (_, _, b, m) in outs]
    scratch = []
    if carry is not None:
        if save_carry is not None:
            out_shape.append(jax.ShapeDtypeStruct(save_carry[0], carry[1]))
            out_specs.append(_bs(save_carry[1], save_carry[2]))
        scratch.append(pltpu.VMEM(carry[0], carry[1]))
    return _gcall(body, name, grid, [_bs(b, m) for (_, b, m) in ins], out_specs, out_shape, scratch,
                  [a for (a, _, _) in ins], comm)


def p_bwd(name, f, grid, ins, cts, wrt, carry=None, saved=None, comm=None):
    n_in, n_ct, n_w = len(ins), len(cts), len(wrt)
    rev = carry is not None
    n_last = grid[-1]

    def fix(imap):
        return _rev_imap(imap, n_last) if rev else imap

    def body(*refs):
        in_refs = refs[:n_in]
        ct_refs = refs[n_in:n_in + n_ct]
        k = n_in + n_ct
        if rev:
            sv_ref = refs[k]
            k += 1
        out_refs = refs[k:k + n_w]
        rest = refs[k + n_w:]
        vals = [r[...] for r in in_refs]
        ct_vals = [r[...].astype(F32) for r in ct_refs]
        widx = [w[0] for w in wrt]

        if rev:
            (dc_ref,) = rest

            @pl.when(pl.program_id(len(grid) - 1) == 0)
            def _():
                dc_ref[...] = jnp.zeros(dc_ref.shape, dc_ref.dtype)

            c_in = sv_ref[...]

            def g(c, *dv):
                full = list(vals)
                for i, d in zip(widx, dv):
                    full[i] = d
                return tuple(f(c, *full))

            _, vjp = jax.vjp(g, c_in, *[vals[i] for i in widx])
            grads = vjp((dc_ref[...],) + tuple(ct_vals))
            dc_ref[...] = grads[0]
            grads = grads[1:]
        else:
            def g(*dv):
                full = list(vals)
                for i, d in zip(widx, dv):
                    full[i] = d
                r = f(*full)
                return tuple(r) if isinstance(r, (tuple, list)) else (r,)

            _, vjp = jax.vjp(g, *[vals[i] for i in widx])
            grads = vjp(tuple(ct_vals))

        for w, o_ref, gr in zip(wrt, out_refs, grads):
            acc = w[1]
            if acc is None:
                o_ref[...] = gr.astype(o_ref.dtype)
            else:
                first = None
                for ax in acc:
                    c0 = pl.program_id(ax) == 0
                    first = c0 if first is None else jnp.logical_and(first, c0)

                @pl.when(first)
                def _():
                    o_ref[...] = jnp.zeros(o_ref.shape, o_ref.dtype)

                o_ref[...] += gr.astype(o_ref.dtype)

    in_specs = [_bs(b, fix(m)) for (_, b, m) in ins] + [_bs(b, fix(m)) for (_, b, m) in cts]
    args = [a for (a, _, _) in ins] + [a for (a, _, _) in cts]
    if rev:
        in_specs.append(_bs(saved[1], fix(saved[2])))
        args.append(saved[0])
    out_shape, out_specs = [], []
    for w in wrt:
        a, b, m = ins[w[0]]
        if len(w) > 2 and w[2] is not None:
            m = w[2]
        out_shape.append(jax.ShapeDtypeStruct(a.shape, w[3] if len(w) > 3 else F32))
        out_specs.append(_bs(b, fix(m)))
    scratch = [pltpu.VMEM(carry[0], carry[1])] if rev else []
    return _gcall(body, name, grid, in_specs, out_specs, out_shape, scratch, args, comm)


def _rows(name, f, row_ins, params, out_widths, tm, fwd=True, cts=None, wrt_rows=(), wrt_params=(), dtypes=None):
    t = row_ins[0].shape[0]
    width = sum(a.shape[1] for a in row_ins)
    width += sum(out_widths) if fwd else sum(c.shape[1] for c in cts) + sum(row_ins[i].shape[1] for i in wrt_rows)
    tm = min(tm, t)
    while tm > 8 and 2 * 4 * tm * width > ROWS_VMEM:
        tm //= 2
    rmap = lambda i: (i, 0)
    pmap = lambda i: (0, 0)
    ins = [(a, (tm, a.shape[1]), rmap) for a in row_ins] + [(p, p.shape, pmap) for p in params]
    if fwd:
        dtypes = dtypes or [F32] * len(out_widths)
        outs = [((t, w), dt, (tm, w), rmap) for w, dt in zip(out_widths, dtypes)]
        return p_fwd(name, f, (t // tm,), ins, outs)
    ct_specs = [(c, (tm, c.shape[1]), rmap) for c in cts]
    dtypes = dtypes or [F32] * len(wrt_rows)
    wrt = [(i, None, None, dt) for i, dt in zip(wrt_rows, dtypes)] + [(len(row_ins) + i, (0,)) for i in wrt_params]
    return p_bwd(name, f, (t // tm,), ins, ct_specs, wrt)


def _pick(n, cands):
    for c in cands:
        if n % c == 0:
            return c
    return n


def _wide(n, cap=1664):
    best = None
    for w in range(128, min(n, cap) + 1, 128):
        if n % w == 0:
            best = w
    return best or n


MM_VMEM = 40 * 1024 * 1024


def _mm_tiles(m, n, k, sa, sb, so, has_add):
    bm, bn = _pick(m, (512, 256, 128)), _wide(n)
    bk = k if k <= 2048 else _wide(k, 3328)

    def vmem(bm, bn, bk):
        acc = 0 if bk == k else 4 * bm * bn
        return 2 * (bm * bk * sa + bk * bn * sb + bm * bn * so * (2 if has_add else 1)) + acc

    while vmem(bm, bn, bk) > MM_VMEM and bk % 256 == 0:
        bk //= 2
    while vmem(bm, bn, bk) > MM_VMEM and bn % 256 == 0:
        bn //= 2
    return bm, bn, bk


def matmul(name, a, b, ta=False, tb=False, out_dtype=F32, add=None, comm=None):
    m = a.shape[1] if ta else a.shape[0]
    k = a.shape[0] if ta else a.shape[1]
    n = b.shape[0] if tb else b.shape[1]
    assert (b.shape[1] if tb else b.shape[0]) == k, (a.shape, b.shape, ta, tb)
    bm, bn, bk = _mm_tiles(m, n, k, a.dtype.itemsize, b.dtype.itemsize, jnp.dtype(out_dtype).itemsize, add is not None)
    nk = k // bk

    def body(a_ref, b_ref, *rest):
        o_ref, acc_ref = rest[-2:]
        if nk == 1:
            r = mm(a_ref[...], b_ref[...], ta, tb)
            o_ref[...] = (r if add is None else r + rest[0][...].astype(F32)).astype(o_ref.dtype)
            return

        @pl.when(pl.program_id(2) == 0)
        def _():
            acc_ref[...] = jnp.zeros(acc_ref.shape, F32) if add is None else rest[0][...].astype(F32)

        acc_ref[...] += mm(a_ref[...], b_ref[...], ta, tb)

        @pl.when(pl.program_id(2) == nk - 1)
        def _():
            o_ref[...] = acc_ref[...].astype(o_ref.dtype)

    a_spec = _bs((bk, bm), lambda i, j, l: (l, i)) if ta else _bs((bm, bk), lambda i, j, l: (i, l))
    b_spec = _bs((bn, bk), lambda i, j, l: (j, l)) if tb else _bs((bk, bn), lambda i, j, l: (l, j))
    o_spec = _bs((bm, bn), lambda i, j, l: (i, j))
    if comm:
        out, *comm_out = _gcall(
            body, name, (m // bm, n // bn, nk), [a_spec, b_spec] + ([] if add is None else [o_spec]), [o_spec],
            [jax.ShapeDtypeStruct((m, n), out_dtype)], [pltpu.VMEM((bm, bn) if nk > 1 else (8, 128), F32)],
            (a, b) if add is None else (a, b, add), comm)
        return out, comm_out
    return _pcall(
        body, name=name, grid=(m // bm, n // bn, nk),
        in_specs=[a_spec, b_spec] + ([] if add is None else [o_spec]), out_specs=o_spec,
        out_shape=jax.ShapeDtypeStruct((m, n), out_dtype),
        scratch_shapes=[pltpu.VMEM((bm, bn) if nk > 1 else (8, 128), F32)],
        compiler_params=_cparams(dimension_semantics=("parallel", "parallel", "arbitrary")),
    )(*((a, b) if add is None else (a, b, add)))


def _to_heads(x, b, h):
    t, w = x.shape
    return x.reshape(b, t // b, h, w // h).transpose(0, 2, 1, 3)


def _from_heads(x):
    b, h, s, d = x.shape
    return x.transpose(0, 2, 1, 3).reshape(b * s, h * d)


RW_STEPS = 8
RW_G = 16


def rwkv_group(st, w8, kk8, ka8, kh8, r8, vc):
    ys = []
    for t in range(RW_STEPS):
        row = lambda x: x[:, t:t + 1, :]
        sa = jnp.sum(st * row(kk8), axis=2, keepdims=True)
        st = st * row(w8) - sa * row(ka8) + vc[:, :, t:t + 1] * row(kh8)
        ys.append(jnp.sum(st * row(r8), axis=2, keepdims=True))
    return st, jnp.concatenate(ys, axis=2)


def rwkv_prehead_f(kkraw, a):
    nrm = jnp.sqrt(jnp.sum(kkraw * kkraw, axis=-1, keepdims=True))
    kk = kkraw / jnp.maximum(nrm, 1e-12)
    return kk, kk * a


def rwkv_pre_f(kd, w_lr, a_lr, g_lr, w0, wd2, a0, wa2, wg2, k_k, k_a):
    wpre = w0 + mm(jnp.tanh(w_lr), wd2)
    w = -jax.nn.softplus(-wpre) - 0.5
    decay = jnp.exp(-jnp.exp(w))
    a = jax.nn.sigmoid(a0 + mm(a_lr, wa2))
    g = mm(jax.nn.sigmoid(g_lr), wg2)
    kkraw = kd * k_k
    kh = kd * (1.0 + (a - 1.0) * k_a)
    return decay, a, g, kkraw, kh


def rwkv_post_f(y, r, kh, v, g, gn_g, gn_b, r_k):
    mu = jnp.mean(y, axis=-1, keepdims=True)
    var = jnp.mean(jnp.square(y - mu), axis=-1, keepdims=True)
    yn = (y - mu) * lax.rsqrt(var + RWKV_GN_EPS) * gn_g + gn_b
    bonus = jnp.sum(r * kh * r_k, axis=-1, keepdims=True) * v
    return (yn + bonus) * g


def _to_cols(xh):
    b, h, s, d = xh.shape
    return xh.reshape(b, h, s // RW_STEPS, RW_STEPS, d).transpose(0, 1, 2, 4, 3)


def _from_cols(xc):
    b, h, n, d, k = xc.shape
    return xc.transpose(0, 1, 2, 4, 3).reshape(b, h, n * k, d)


def rwkv_block(bsz, r, kd, vd, w_lr, a_lr, g_lr, prm, cts=None, scan=None, comm=None):
    t = r.shape[0]
    hh, n = 16, 64
    tm = 256
    pre_rows = [kd, w_lr, a_lr, g_lr]
    pre_prm = [prm[k] for k in ("w0", "wd2", "a0", "wa2", "wg2", "k_k", "k_a")]
    decay, a, g, kkraw, kh = _rows("rwkv_pre", rwkv_pre_f, pre_rows, pre_prm, [1024] * 5, tm)
    heads = lambda x: _to_heads(x, bsz, hh)
    rh, khh, vh, gh, dech, kkrawh, ah = (heads(x) for x in (r, kh, vd, g, decay, kkraw, a))
    s = t // bsz
    ng = s // RW_STEPS
    ts = 512
    hb = (None, None, ts, n)
    hm = lambda h, b, i: (b, h, i, 0)
    pgrid = (hh, bsz, s // ts)
    ph_ins = [(kkrawh, hb, hm), (ah, hb, hm)]
    kkh, kah = p_fwd("rwkv_prehead", rwkv_prehead_f, pgrid, ph_ins, [((bsz, hh, s, n), F32, hb, hm)] * 2)
    grp = lambda x: x.reshape(bsz, hh, ng, RW_STEPS, n)
    cb, rb, sb = (None, RW_G, None, n, RW_STEPS), (None, RW_G, None, RW_STEPS, n), (None, RW_G, None, n, n)
    cm = lambda b, h, i: (b, h, i, 0, 0)
    sc_ins = [(grp(x), rb, cm) for x in (dech, kkh, kah, khh, rh)] + [(_to_cols(vh), cb, cm)]
    grid = (bsz, hh // RW_G, ng)
    comm_out = []
    if scan is None:
        yc, sv, *comm_out = p_fwd("rwkv_scan", rwkv_group, grid, sc_ins, [((bsz, hh, ng, n, RW_STEPS), F32, cb, cm)],
                                  carry=((RW_G, n, n), F32), save_carry=((bsz, hh, ng, n, n), sb, cm), comm=comm)
    else:
        yc, sv = scan
    yh = _from_cols(yc)
    pb = (None, 1, n)
    pm = lambda h, b, i: (h, 0, 0)
    gn_g, gn_b, r_k = (prm[k].reshape(hh, 1, n) for k in ("gn_g", "gn_b", "r_k"))
    post_ins = [(x, hb, hm) for x in (yh, rh, khh, vh, gh)] + [(p, pb, pm) for p in (gn_g, gn_b, r_k)]
    if cts is None:
        (oh,) = p_fwd("rwkv_post", rwkv_post_f, pgrid, post_ins, [((bsz, hh, s, n), BF16, hb, hm)])
        return _from_heads(oh), (yc, sv), comm_out
    doh = _to_heads(cts, bsz, hh)
    dyh, drh1, dkhh1, dvh1, dgh, dgn_g, dgn_b, dr_k = p_bwd(
        "rwkv_post_bwd", rwkv_post_f, pgrid, post_ins, [(doh, hb, hm)],
        [(i, None) for i in range(5)] + [(5 + i, (1, 2)) for i in range(3)])
    drows_v = p_bwd("rwkv_scan_bwd", rwkv_group, grid, sc_ins, [(_to_cols(dyh), cb, cm)], [(i, None) for i in range(6)],
                    carry=((RW_G, n, n), F32), saved=(sv, sb, cm), comm=comm)
    drows_v, comm_out = drows_v[:6], list(drows_v[6:])
    ddech, dkkh, dkah, dkhh2, drh2 = (x.reshape(bsz, hh, s, n) for x in drows_v[:5])
    dkkrawh, dah = p_bwd("rwkv_prehead_bwd", rwkv_prehead_f, pgrid, ph_ins, [(dkkh, hb, hm), (dkah, hb, hm)],
                         [(0, None), (1, None)])
    ddecay, dkkraw, da = _from_heads(ddech), _from_heads(dkkrawh), _from_heads(dah)
    dv = _from_heads(_from_cols(drows_v[5]) + dvh1)
    dr = _from_heads(drh2 + drh1)
    dkh = _from_heads(dkhh2 + dkhh1)
    dg = _from_heads(dgh)
    res = _rows("rwkv_pre_bwd", rwkv_pre_f, pre_rows, pre_prm, None, tm, fwd=False,
                cts=[ddecay, da, dg, dkkraw, dkh], wrt_rows=(0, 1, 2, 3), wrt_params=tuple(range(7)))
    dkd, dw_lr, da_lr, dg_lr = res[:4]
    dprm = dict(zip(("w0", "wd2", "a0", "wa2", "wg2", "k_k", "k_a"), res[4:]))
    dprm.update(gn_g=dgn_g.reshape(1, -1), gn_b=dgn_b.reshape(1, -1), r_k=dr_k.reshape(hh, n))
    return (dr, dkd, dv, dw_lr, da_lr, dg_lr), dprm, comm_out


GLA_C, GLA_DK, GLA_DV, GLA_H, GLA_TAU = 64, 128, 256, 4, 16.0


def gla_chunk_f(st, q, k, v, lr, r, w2, bg, ng, nb):
    la = jax.nn.log_sigmoid(mm(lr, w2) + bg) / GLA_TAU
    ri = lax.broadcasted_iota(jnp.int32, (GLA_C, GLA_C), 0)
    ci = lax.broadcasted_iota(jnp.int32, (GLA_C, GLA_C), 1)
    causal = ci <= ri
    b = jnp.dot(causal.astype(F32), la, precision=lax.Precision.HIGHEST, preferred_element_type=F32)
    b_last = jnp.sum(la, axis=0, keepdims=True)
    q_dec = (q * (GLA_DK ** -0.5)) * jnp.exp(b)
    k_inv = k * jnp.exp(-b)
    k_end = k * jnp.exp(b_last - b)
    att = jnp.where(causal, mm(q_dec, k_inv, tb=True), 0.0)
    o = mm(att, v) + mm(q_dec, st, tb=True)
    st_new = st * jnp.exp(b_last) + mm(v, k_end, ta=True)
    mu = jnp.mean(o, axis=-1, keepdims=True)
    var = jnp.mean(jnp.square(o - mu), axis=-1, keepdims=True)
    on = (o - mu) * lax.rsqrt(var + LN_EPS) * ng + nb
    return st_new, on * jax.nn.silu(r)


def gla_block(bsz, q, k, v, r, lr, w2, bg, ng, nb, cts=None, comm=None):
    t = q.shape[0]
    nc = t // bsz // GLA_C
    grid = (GLA_H, bsz, nc)
    rm = lambda h, b, c: (b * nc + c, h)
    ins = [(q, (GLA_C, GLA_DK), rm), (k, (GLA_C, GLA_DK), rm), (v, (GLA_C, GLA_DV), rm),
           (jnp.broadcast_to(lr[None], (GLA_H,) + lr.shape), (None, GLA_C, lr.shape[1]), lambda h, b, c: (h, b * nc + c, 0)),
           (r, (GLA_C, GLA_DV), rm),
           (w2, (w2.shape[0], GLA_DK), lambda h, b, c: (0, h)), (bg, (1, GLA_DK), lambda h, b, c: (0, h)),
           (ng, (1, GLA_DV), lambda h, b, c: (0, 0)), (nb, (1, GLA_DV), lambda h, b, c: (0, 0))]
    ob = (GLA_C, GLA_DV)
    sshape, sblock = (GLA_H, bsz, nc, GLA_DV, GLA_DK), (None, None, None, GLA_DV, GLA_DK)
    sm = lambda h, b, c: (h, b, c, 0, 0)
    carry = ((GLA_DV, GLA_DK), F32)
    if cts is None:
        out, sv, *comm_out = p_fwd("gla_scan", gla_chunk_f, grid, ins, [((t, GLA_H * GLA_DV), BF16, ob, rm)],
                                   carry=carry, save_carry=(sshape, sblock, sm), comm=comm)
        return out, sv, comm_out
    dout, sv = cts
    return p_bwd("gla_scan_bwd", gla_chunk_f, grid, ins, [(dout, ob, rm)],
                 [(0, None), (1, None), (2, None), (3, None), (4, None), (5, (1, 2)), (6, (1, 2)), (7, (0, 1, 2)), (8, (0, 1, 2))],
                 carry=carry, saved=(sv, sblock, sm), comm=comm)


def _softmax_rows(sc):
    m = lax.stop_gradient(jnp.max(sc, axis=-1, keepdims=True))
    e = jnp.exp(sc - m)
    return e / jnp.sum(e, axis=-1, keepdims=True)


def mla_attn_f(qn, r1, r2, kn, kr1, kr2, v):
    tq, s = qn.shape[0], kn.shape[0]
    sc = (mm(qn, kn, tb=True) + mm(r1, kr1, tb=True) + mm(r2, kr2, tb=True)) * (192.0 ** -0.5)
    qpos = pl.program_id(2) * tq + lax.broadcasted_iota(jnp.int32, (tq, s), 0)
    kpos = lax.broadcasted_iota(jnp.int32, (tq, s), 1)
    sc = jnp.where(kpos <= qpos, sc, NEG_INF)
    return mm(_softmax_rows(sc), v)


def mla_attn(bsz, qn, r1, r2, kn, kr1, kr2, v, cts=None, tq=256, comm=None):
    t = qn.shape[0]
    s = t // bsz
    nq = s // tq
    hh = 8
    grid = (bsz, hh, nq)
    qm = lambda b, h, i: (b * nq + i, h)
    km_ = lambda b, h, i: (b, h)
    ins = [(qn, (tq, 128), qm),
           (r1, (None, None, tq, 32), lambda b, h, i: (b, h, i, 0)), (r2, (None, None, tq, 32), lambda b, h, i: (b, h, i, 0)),
           (kn, (s, 128), km_), (kr1, (s, 32), lambda b, h, i: (b, 0)), (kr2, (s, 32), lambda b, h, i: (b, 0)),
           (v, (s, 128), km_)]
    if cts is None:
        out, *comm_out = p_fwd("mla_attn", mla_attn_f, grid, ins, [((t, 1024), BF16, (tq, 128), qm)], comm=comm)
        return out, comm_out
    return p_bwd("mla_attn_bwd", mla_attn_f, grid, ins, [(cts, (tq, 128), qm)],
                 [(0, None), (1, None), (2, None), (3, (2,)), (4, (1, 2)), (5, (1, 2)), (6, (2,))], comm=comm)


def xattn_f(q, k, v):
    sc = mm(q, k, tb=True) * (512.0 ** -0.5)
    return mm(_softmax_rows(sc), v)


def xattn(bsz, q, k, v, cts=None, tq=512):
    t = q.shape[0]
    nq = t // bsz // tq
    mlen = k.shape[0] // bsz
    grid = (bsz, 4, nq)
    qm = lambda b, h, i: (b * nq + i, h)
    km_ = lambda b, h, i: (b, h)
    ins = [(q, (tq, 512), qm), (k, (mlen, 512), km_), (v, (mlen, 512), km_)]
    if cts is None:
        return p_fwd("xattn", xattn_f, grid, ins, [((t, 2048), BF16, (tq, 512), qm)])[0]
    return p_bwd("xattn_bwd", xattn_f, grid, ins, [(cts, (tq, 512), qm)], [(0, None, None, BF16), (1, (2,)), (2, (2,))])


DIL_SPAN = 128
DIL_BRANCHES = ((128, 1), (512, 4), (2048, 16))


def dil_attn_f(q1, q2, k1c, k2c, vc, k1p, k2p, vp):
    gb, sp = q1.shape[0], DIL_SPAN
    scale = 128.0 ** -0.5
    sc_c = (mm(q1, k1c, tb=True) + mm(q2, k2c, tb=True)) * scale
    sc_p = (mm(q1, k1p, tb=True) + mm(q2, k2p, tb=True)) * scale
    ql = lax.broadcasted_iota(jnp.int32, (gb, sp, sp), 1)
    kl = lax.broadcasted_iota(jnp.int32, (gb, sp, sp), 2)
    has_prev = pl.program_id(1) > 0
    sc_c = jnp.where(kl <= ql, sc_c, NEG_INF)
    sc_p = jnp.where(jnp.logical_and(kl >= ql, has_prev), sc_p, NEG_INF)
    m = lax.stop_gradient(jnp.maximum(jnp.max(sc_c, axis=-1, keepdims=True), jnp.max(sc_p, axis=-1, keepdims=True)))
    e_c, e_p = jnp.exp(sc_c - m), jnp.exp(sc_p - m)
    den = jnp.sum(e_c, axis=-1, keepdims=True) + jnp.sum(e_p, axis=-1, keepdims=True)
    o = mm(e_c / den, vc) + mm(e_p / den, vp)
    return o, m + jnp.log(den)


def dil_branch(q1, q2, k1, k2, v, cts=None, gb=8):
    g, l, _ = q1.shape
    nb = l // DIL_SPAN
    grid = (g // gb, nb)
    cm = lambda i, n: (i, n, 0)
    pm = lambda i, n: (i, jnp.maximum(n - 1, 0), 0)
    b64, b128, b1 = (gb, DIL_SPAN, 64), (gb, DIL_SPAN, 128), (gb, DIL_SPAN, 1)
    ins = [(q1, b64, cm), (q2, b64, cm), (k1, b64, cm), (k2, b64, cm), (v, b128, cm),
           (k1, b64, pm), (k2, b64, pm), (v, b128, pm)]
    if cts is None:
        return p_fwd("dil_attn", dil_attn_f, grid, ins, [((g, l, 128), F32, b128, cm), ((g, l, 1), F32, b1, cm)])
    do, dlse = cts
    dq1, dq2, dk1c, dk2c, dvc, dk1p, dk2p, dvp = p_bwd(
        "dil_attn_bwd", dil_attn_f, grid, ins, [(do, b128, cm), (dlse, b1, cm)],
        [(i, None) for i in range(5)] + [(i, None, cm) for i in (5, 6, 7)])

    def fold(dc, dp):
        return dc + jnp.pad(dp[:, DIL_SPAN:], ((0, 0), (0, DIL_SPAN), (0, 0)))

    return dq1, dq2, fold(dk1c, dk1p), fold(dk2c, dk2p), fold(dvc, dvp)


def dil_mix_f(o1, o2, o3, l1, l2, l3):
    m = lax.stop_gradient(jnp.maximum(jnp.maximum(l1, l2), l3))
    e1, e2, e3 = jnp.exp(l1 - m), jnp.exp(l2 - m), jnp.exp(l3 - m)
    den = e1 + e2 + e3
    return (e1 / den) * o1 + (e2 / den) * o2 + (e3 / den) * o3


def _to_res(xh, dil):
    b, h, s, d = xh.shape
    return xh.reshape(b, h, s // dil, dil, d).transpose(0, 1, 3, 2, 4).reshape(b * h * dil, s // dil, d)


def _from_res(xr, b, h, dil):
    g, l, d = xr.shape
    return xr.reshape(b, h, dil, l, d).transpose(0, 1, 3, 2, 4).reshape(b, h, l * dil, d)


def dil_block(bsz, q1, q2, k1, k2, v, cts=None):
    hh = 8
    heads = [_to_heads(x, bsz, hh) for x in (q1, q2, k1, k2, v)]
    s = heads[0].shape[2]
    outs, res_in = [], []
    for window, dil in DIL_BRANCHES:
        assert window // dil == DIL_SPAN and (s // dil) % DIL_SPAN == 0
        rin = [_to_res(x, dil) for x in heads]
        o, lse = dil_branch(*rin)
        res_in.append(rin)
        outs.append((_from_res(o, bsz, hh, dil), _from_res(lse, bsz, hh, dil)))
    tq = 512
    ob, lb = (None, None, tq, 128), (None, None, tq, 1)
    hm = lambda b, h, i: (b, h, i, 0)
    mix_ins = [(o, ob, hm) for (o, _) in outs] + [(l, lb, hm) for (_, l) in outs]
    grid = (bsz, hh, s // tq)
    if cts is None:
        (mix,) = p_fwd("dil_mix", dil_mix_f, grid, mix_ins, [((bsz, hh, s, 128), BF16, ob, hm)])
        return _from_heads(mix)
    dmix = _to_heads(cts, bsz, hh)
    dml = p_bwd("dil_mix_bwd", dil_mix_f, grid, mix_ins, [(dmix, ob, hm)], [(i, None) for i in range(6)])
    tot = None
    for j, (window, dil) in enumerate(DIL_BRANCHES):
        do, dl = _to_res(dml[j], dil), _to_res(dml[3 + j], dil)
        gr = dil_branch(*res_in[j], cts=(do, dl))
        gr = [_from_res(x, bsz, hh, dil) for x in gr]
        tot = gr if tot is None else [a + b for a, b in zip(tot, gr)]
    return tuple(_from_heads(x) for x in tot)


_ANY = pl.BlockSpec(memory_space=pl.ANY)


def _me_and_peers():
    x, y, c = lax.axis_index("x"), lax.axis_index("y"), lax.axis_index("c")
    me = 4 * x + 2 * y + c
    peers = []
    for k in range(1, N_DEV):
        px = 1 - x if k & 4 else x
        py = 1 - y if k & 2 else y
        pc = 1 - c if k & 1 else c
        peers.append(((px, py, pc), 4 * px + 2 * py + pc))
    return me, peers


def _exchange(name, x, scatter):
    shape = x.shape[1:] if scatter else x.shape

    def body(x_ref, out_ref, send_sems, recv_sems, local_sem):
        me, peers = _me_and_peers()
        src_me = x_ref.at[me] if scatter else x_ref
        local = pltpu.make_async_copy(src_me, out_ref.at[me], local_sem)
        local.start()
        sends = []
        for k, (dev, idx) in enumerate(peers):
            cp = pltpu.make_async_remote_copy(
                src_ref=x_ref.at[idx] if scatter else x_ref, dst_ref=out_ref.at[me],
                send_sem=send_sems.at[k], recv_sem=recv_sems.at[k],
                device_id=dev, device_id_type=pl.DeviceIdType.MESH)
            cp.start()
            sends.append(cp)
        for k, (dev, idx) in enumerate(peers):
            pltpu.make_async_remote_copy(
                src_ref=src_me, dst_ref=out_ref.at[idx], send_sem=send_sems.at[k], recv_sem=recv_sems.at[k],
                device_id=dev, device_id_type=pl.DeviceIdType.MESH).wait_recv()
        for cp in sends:
            cp.wait_send()
        local.wait()

    return _pcall(
        body, name=name, in_specs=[_ANY], out_specs=_ANY,
        out_shape=jax.ShapeDtypeStruct((N_DEV,) + tuple(shape), x.dtype),
        scratch_shapes=[pltpu.SemaphoreType.DMA((N_DEV - 1,)), pltpu.SemaphoreType.DMA((N_DEV - 1,)),
                        pltpu.SemaphoreType.DMA],
        compiler_params=pltpu.CompilerParams(has_side_effects=True),
    )(x)


def all_gather(name, x):
    return _exchange(name, x, False)


def reduce_scatter_exchange(name, x):
    return _exchange(name, x, True)


def run_exchange(name, spec):
    n_in, n_out = len(spec["ins"]), len(spec["outs"])

    def body(*refs):
        parts = refs[:n_in], refs[n_in:n_in + n_out], refs[n_in + n_out:]
        spec["start"](*parts)
        spec["finish"](*parts)

    return _pcall(
        body, name=name, in_specs=[_ANY] * n_in, out_specs=[_ANY] * n_out, out_shape=list(spec["outs"]),
        scratch_shapes=list(spec["sems"]), compiler_params=pltpu.CompilerParams(has_side_effects=True),
    )(*spec["ins"])


def all_gather_2level_spec(x):
    def parts(ins, outs, sems):
        (x_ref,), (out_ref,), (send_sems, recv_sems, local_sem) = ins, outs, sems
        x_, y_, c_ = lax.axis_index("x"), lax.axis_index("y"), lax.axis_index("c")
        sibling = (x_, y_, 1 - c_)
        chips = [(1 - x_, y_), (x_, 1 - y_), (1 - x_, 1 - y_)]

        def slot(px, py, pc):
            return out_ref.at[4 * px + 2 * py + pc]

        def copy(k, block, to, src=None):
            return pltpu.make_async_remote_copy(
                src_ref=slot(*block) if src is None else src, dst_ref=slot(*block),
                send_sem=send_sems.at[k], recv_sem=recv_sems.at[k], device_id=to, device_id_type=pl.DeviceIdType.MESH)

        me = (x_, y_, c_)
        mine = pltpu.make_async_copy(x_ref, slot(*me), local_sem)
        first = [copy(0, me, sibling, src=x_ref)]
        first += [copy(1 + j, me, (*chip, c_), src=x_ref) for j, chip in enumerate(chips)]
        return copy, me, sibling, chips, c_, mine, first

    def start(ins, outs, sems):
        *_, mine, first = parts(ins, outs, sems)
        mine.start()
        for cp in first:
            cp.start()

    def finish(ins, outs, sems):
        copy, me, sibling, chips, c_, mine, first = parts(ins, outs, sems)
        passed = [copy(4 + j, (*chip, c_), sibling) for j, chip in enumerate(chips)]
        for j, chip in enumerate(chips):
            copy(1 + j, (*chip, c_), me).wait_recv()
            passed[j].start()
        copy(0, sibling, me).wait_recv()
        for j, chip in enumerate(chips):
            copy(4 + j, (*chip, 1 - c_), me).wait_recv()
        for cp in first + passed:
            cp.wait_send()
        mine.wait()

    return dict(ins=[x], outs=[jax.ShapeDtypeStruct((N_DEV,) + tuple(x.shape), x.dtype)],
                sems=[pltpu.SemaphoreType.DMA((7,)), pltpu.SemaphoreType.DMA((7,)), pltpu.SemaphoreType.DMA],
                start=start, finish=finish)


def all_gather_2level(name, x):
    return run_exchange(name, all_gather_2level_spec(x))[0]


def sibling_swap(name, x):
    shape = (4,) + tuple(x.shape[1:])

    def body(x_ref, theirs_ref, send_sems, recv_sems):
        x_, y_, c_ = lax.axis_index("x"), lax.axis_index("y"), lax.axis_index("c")
        sends = []
        for j in range(4):
            cp = pltpu.make_async_remote_copy(
                src_ref=x_ref.at[2 * j + 1 - c_], dst_ref=theirs_ref.at[j], send_sem=send_sems.at[j], recv_sem=recv_sems.at[j],
                device_id=(x_, y_, 1 - c_), device_id_type=pl.DeviceIdType.MESH)
            cp.start()
            sends.append(cp)
        for cp in sends:
            cp.wait()

    return _pcall(
        body, name=name, in_specs=[_ANY], out_specs=_ANY,
        out_shape=jax.ShapeDtypeStruct(shape, x.dtype),
        scratch_shapes=[pltpu.SemaphoreType.DMA((4,)), pltpu.SemaphoreType.DMA((4,))],
        compiler_params=pltpu.CompilerParams(has_side_effects=True),
    )(x)


def pair_add(name, stack, theirs, out_dtype):
    shp = theirs.shape
    c = shp[-1]
    r = math.prod(shp[1:-1])
    br = r
    for cand in (1024, 512, 256, 128, 64, 32, 16, 8):
        if r % cand == 0 and cand * c <= 256 * 1024:
            br = cand
            break

    def body(s0_ref, s1_ref, t_ref, o_ref):
        mine = jnp.where(lax.axis_index("c") == 0, s0_ref[...], s1_ref[...])
        o_ref[...] = (mine + t_ref[...]).astype(o_ref.dtype)

    s4 = stack.reshape(4, 2, r, c)
    out = _pcall(
        body, name=name, grid=(4, r // br),
        in_specs=[_bs((None, None, br, c), lambda j, i: (j, 0, i, 0)), _bs((None, None, br, c), lambda j, i: (j, 1, i, 0)),
                  _bs((None, br, c), lambda j, i: (j, i, 0))],
        out_specs=_bs((None, br, c), lambda j, i: (j, i, 0)),
        out_shape=jax.ShapeDtypeStruct((4, r, c), out_dtype), compiler_params=_cparams(),
    )(s4, s4, theirs.reshape(4, r, c))
    return out.reshape(shp)


def chip_exchange_spec(p):
    def parts(ins, outs, sems, sending=False):
        (p_ref,), (out_ref,), (send_sems, recv_sems, local_sem) = ins, outs, sems
        x_, y_, c_ = lax.axis_index("x"), lax.axis_index("y"), lax.axis_index("c")
        my_chip = 2 * x_ + y_
        chips = [(1 - x_, y_), (x_, 1 - y_), (1 - x_, 1 - y_)]
        local = pltpu.make_async_copy(p_ref.at[my_chip], out_ref.at[my_chip], local_sem)
        sends = [pltpu.make_async_remote_copy(
            src_ref=p_ref.at[2 * px + py], dst_ref=out_ref.at[my_chip], send_sem=send_sems.at[k], recv_sem=recv_sems.at[k],
            device_id=(px, py, c_), device_id_type=pl.DeviceIdType.MESH) for k, (px, py) in enumerate(chips)]
        recvs = [] if sending else [pltpu.make_async_remote_copy(
            src_ref=p_ref.at[my_chip], dst_ref=out_ref.at[2 * px + py], send_sem=send_sems.at[k], recv_sem=recv_sems.at[k],
            device_id=(px, py, c_), device_id_type=pl.DeviceIdType.MESH) for k, (px, py) in enumerate(chips)]
        return local, sends, recvs

    def start(ins, outs, sems):
        local, sends, _ = parts(ins, outs, sems, sending=True)
        local.start()
        for cp in sends:
            cp.start()

    def finish(ins, outs, sems):
        local, sends, recvs = parts(ins, outs, sems)
        for cp in recvs:
            cp.wait_recv()
        for cp in sends:
            cp.wait_send()
        local.wait()

    return dict(ins=[p], outs=[jax.ShapeDtypeStruct(p.shape, p.dtype)],
                sems=[pltpu.SemaphoreType.DMA((3,)), pltpu.SemaphoreType.DMA((3,)), pltpu.SemaphoreType.DMA],
                start=start, finish=finish)


def chip_exchange(name, p):
    return run_exchange(name, chip_exchange_spec(p))[0]


def chip_partials(name, stack):
    theirs = sibling_swap(name + "_d2d", stack)
    return pair_add(name + "_add", stack, theirs, BF16)


def reduce_scatter_2level(name, stack):
    return chip_exchange(name + "_ici", chip_partials(name, stack))


def ln_res_f(h, r, g, b):
    x = ALPHA * h + r
    mu = jnp.mean(x, axis=-1, keepdims=True)
    var = jnp.mean(jnp.square(x - mu), axis=-1, keepdims=True)
    return (x - mu) * lax.rsqrt(var + LN_EPS) * g + b


def rms_f(x, g):
    return x * lax.rsqrt(jnp.mean(x * x, axis=-1, keepdims=True) + RMS_EPS) * g


def rope_f(x1, x2, c, s):
    return x1 * c - x2 * s, x1 * s + x2 * c


def swiglu_f(g, u):
    return jax.nn.silu(g) * u


def tshift_f(d, dprev, mu):
    return d + (dprev - d) * mu


def loss_f(y, tgt):
    e = y - tgt
    return e / y.shape[-1], 0.5 * jnp.mean(e * e, axis=-1, keepdims=True)


def adamw(name, w, m, v, gstack):
    shp = w.shape
    c = shp[-1]
    r = math.prod(shp[:-1])
    br = r
    for cand in (512, 256, 128, 64, 32, 16, 8):
        if r % cand == 0 and cand * c <= 128 * 1024:
            br = cand
            break
    k = gstack.shape[0]

    def body(w_ref, m_ref, v_ref, g_ref, go_ref, d_ref, mo_ref, vo_ref):
        g = g_ref[0].astype(F32)
        for j in range(1, k):
            g = g + g_ref[j].astype(F32)
        m_new = ADAM_B1 * m_ref[...] + (1.0 - ADAM_B1) * g
        v_new = ADAM_B2 * v_ref[...] + (1.0 - ADAM_B2) * jnp.square(g)
        m_hat = m_new / (1.0 - ADAM_B1 ** ADAM_STEP)
        v_hat = v_new / (1.0 - ADAM_B2 ** ADAM_STEP)
        go_ref[...] = g
        d_ref[...] = -ADAM_LR * (m_hat / (jnp.sqrt(v_hat) + ADAM_EPS) + ADAM_WD * w_ref[...])
        mo_ref[...] = m_new
        vo_ref[...] = v_new

    spec = _bs((br, c), lambda i: (i, 0))
    outs = _pcall(
        body, name=name, grid=(r // br,),
        in_specs=[spec, spec, spec, _bs((k, br, c), lambda i: (0, i, 0))],
        out_specs=[spec] * 4, out_shape=[jax.ShapeDtypeStruct((r, c), F32)] * 4,
        compiler_params=_cparams(),
    )(w.reshape(r, c), m.reshape(r, c), v.reshape(r, c), gstack.reshape(k, r, c))
    return tuple(o.reshape(shp) for o in outs)


EV_W = (512, 256, 64, 512, 512, 1024, 1024, 16)
EV_IN, EV_PAD = 3920, 4096
OD_IN, OD_PAD = 6592, 6656
TM = 256


def _offsets(widths):
    offs, acc = [], 0
    for w in widths:
        offs.append((acc, acc + w))
        acc += w
    return offs


def _rope_tables(seq, dim):
    inv = 10000.0 ** (-jnp.arange(0, dim, 2, dtype=F32) / dim)
    ang = jnp.arange(seq, dtype=F32)[:, None] * inv[None, :]
    return jnp.cos(ang), jnp.sin(ang)


def _halves(x, nh):
    t, w = x.shape
    x3 = x.reshape(t, nh, w // nh)
    hd = w // nh // 2
    return x3[:, :, :hd].reshape(t, nh * hd), x3[:, :, hd:].reshape(t, nh * hd)


def _unhalves(x1, x2, nh):
    t = x1.shape[0]
    return jnp.concatenate([x1.reshape(t, nh, -1), x2.reshape(t, nh, -1)], axis=2).reshape(t, -1)


def ln_res2_f(h, r, g, b):
    y = ln_res_f(h, r, g, b)
    return y, y


def _ln(name, h, r, g, b, cts=None):
    if cts is None:
        return _rows(name, ln_res2_f, [h, r], [g, b], [h.shape[1]] * 2, TM, dtypes=[F32, BF16])
    return _rows(name + "_bwd", ln_res_f, [h, r], [g, b], None, TM, fwd=False, cts=[cts], wrt_rows=(0, 1), wrt_params=(0, 1),
                 dtypes=[F32, BF16])


def _gather_specs(late_local, keys):
    return [all_gather_2level_spec(late_local[k]) for k in keys]


def _put_gathered(p, late_local, keys, gathered):
    for k, g in zip(keys, gathered):
        p[k[0]][k[1]] = _unshard(g, late_local[k].shape, BIG[k[0]])


def _tail_fwd(l, h, hb, mem2b, bsz, p, late_local=None):
    qx = matmul(f"xa_q{l}", hb, p["xa_w_q"][l])
    kx = matmul(f"xa_k{l}", mem2b, p["xa_w_k"][l])
    vx = matmul(f"xa_v{l}", mem2b, p["xa_w_v"][l])
    ox = xattn(bsz, qx, kx, vx)
    xa = matmul(f"xa_o{l}", ox, p["xa_w_o"][l])
    h2, h2b = _ln(f"ln_xa{l}", h, xa, p["ln_xa_g"][l:l + 1], p["ln_xa_b"][l:l + 1])
    if late_local is None:
        gg = matmul(f"ffn_g{l}", h2b, p["ffn_w_gate"][l])
        uu = matmul(f"ffn_u{l}", h2b, p["ffn_w_up"][l])
    else:
        k_down, k_in = [("ffn_w_down", l)], [("od_w_in", 0)]
        gg, got = matmul(f"ffn_g{l}", h2b, p["ffn_w_gate"][l], comm=_gather_specs(late_local, k_down))
        _put_gathered(p, late_local, k_down, got)
        uu, got = matmul(f"ffn_u{l}", h2b, p["ffn_w_up"][l], comm=_gather_specs(late_local, k_in))
        _put_gathered(p, late_local, k_in, got)
    act = _rows(f"swiglu{l}", swiglu_f, [gg, uu], [], [gg.shape[1]], TM, dtypes=[BF16])[0]
    if late_local is None:
        ff = matmul(f"ffn_d{l}", act, p["ffn_w_down"][l])
    else:
        keys = [("od_w_out", 0)]
        ff, got = matmul(f"ffn_d{l}", act, p["ffn_w_down"][l], comm=_gather_specs(late_local, keys))
        _put_gathered(p, late_local, keys, got)
    h3, h3b = _ln(f"ln_ffn{l}", h2, ff, p["ln_ffn_g"][l:l + 1], p["ln_ffn_b"][l:l + 1])
    return h3, h3b, (h, hb, qx, kx, vx, ox, xa, h2, h2b, gg, uu, act, ff)


def _tail_bwd(l, dh3, saved, mem2b, bsz, p, gr):
    h, hb, qx, kx, vx, ox, xa, h2, h2b, gg, uu, act, ff = saved
    dh2, dff, gr["ln_ffn_g"][l], gr["ln_ffn_b"][l] = _ln(f"ln_ffn{l}", h2, ff, p["ln_ffn_g"][l:l + 1], p["ln_ffn_b"][l:l + 1], cts=dh3)
    dact = matmul(f"ffn_d_dx{l}", dff, p["ffn_w_down"][l], tb=True)
    gr["ffn_w_down"][l] = matmul(f"ffn_d_dw{l}", act, dff, ta=True)
    dgg, duu = _rows(f"swiglu_bwd{l}", swiglu_f, [gg, uu], [], None, TM, fwd=False, cts=[dact], wrt_rows=(0, 1),
                     dtypes=[BF16, BF16])
    gr["ffn_w_gate"][l] = matmul(f"ffn_g_dw{l}", h2b, dgg, ta=True)
    gr["ffn_w_up"][l] = matmul(f"ffn_u_dw{l}", h2b, duu, ta=True)
    dh2 = matmul(f"ffn_g_dx{l}", dgg, p["ffn_w_gate"][l], tb=True, add=dh2)
    dh2 = matmul(f"ffn_u_dx{l}", duu, p["ffn_w_up"][l], tb=True, add=dh2)
    dh, dxa, gr["ln_xa_g"][l], gr["ln_xa_b"][l] = _ln(f"ln_xa{l}", h, xa, p["ln_xa_g"][l:l + 1], p["ln_xa_b"][l:l + 1], cts=dh2)
    dox = matmul(f"xa_o_dx{l}", dxa, p["xa_w_o"][l], tb=True)
    gr["xa_w_o"][l] = matmul(f"xa_o_dw{l}", ox, dxa, ta=True)
    dqx, dkx, dvx = xattn(bsz, qx, kx, vx, cts=dox)
    gr["xa_w_q"][l] = matmul(f"xa_q_dw{l}", hb, dqx, ta=True)
    gr["xa_w_k"][l] = matmul(f"xa_k_dw{l}", mem2b, dkx, ta=True)
    gr["xa_w_v"][l] = matmul(f"xa_v_dw{l}", mem2b, dvx, ta=True)
    return matmul(f"xa_q_dx{l}", dqx, p["xa_w_q"][l], tb=True, add=dh)


def _uq_perm(w):
    w3 = w.reshape(w.shape[0], 8, 192)
    return jnp.concatenate([w3[:, :, :128].reshape(-1, 1024), w3[:, :, 128:160].reshape(-1, 256),
                            w3[:, :, 160:].reshape(-1, 256)], axis=1)


def _uq_unperm(g):
    r = g.shape[0]
    return jnp.concatenate([g[:, :1024].reshape(r, 8, 128), g[:, 1024:1280].reshape(r, 8, 32),
                            g[:, 1280:].reshape(r, 8, 32)], axis=2).reshape(r, 1536)


def _ukv_perm(w):
    w3 = w.reshape(w.shape[0], 8, 256)
    return jnp.concatenate([w3[:, :, :128].reshape(-1, 1024), w3[:, :, 128:].reshape(-1, 1024)], axis=1)


def _ukv_unperm(g):
    r = g.shape[0]
    return jnp.concatenate([g[:, :1024].reshape(r, 8, 128), g[:, 1024:].reshape(r, 8, 128)], axis=2).reshape(r, 2048)


def _pad_cols(w, n):
    return jnp.pad(w, ((0, 0), (0, n - w.shape[1])))


def _shift_prev(x, bsz):
    t, w = x.shape
    x3 = x.reshape(bsz, t // bsz, w)
    return jnp.pad(x3, ((0, 0), (1, 0), (0, 0)))[:, :-1].reshape(t, w)


def _shift_next(x, bsz):
    t, w = x.shape
    x3 = x.reshape(bsz, t // bsz, w)
    return jnp.pad(x3[:, 1:], ((0, 0), (0, 1), (0, 0))).reshape(t, w)


LATE = ("xa_w_q", "xa_w_k", "xa_w_v", "xa_w_o", "ffn_w_gate", "ffn_w_up", "ffn_w_down")


def _late_partials(layer, gr):
    return [chip_partials(f"rs{layer}_{k}", _shard_stack(gr[k][layer], BIG[k])) for k in LATE]


def device_step(x, mem, tgt, p, late_local):
    bsz, seq, d = x.shape
    t = bsz * seq
    x2, mem2, tgt2 = x.reshape(t, d), mem.reshape(bsz * mem.shape[1], d), tgt.reshape(t, d)
    x2b, mem2b = x2.astype(BF16), mem2.astype(BF16)
    gr = {k: [None] * DEPTH for k in ("ln_mix_g", "ln_mix_b", "xa_w_q", "xa_w_k", "xa_w_v", "xa_w_o", "ln_xa_g", "ln_xa_b",
                                      "ffn_w_gate", "ffn_w_up", "ffn_w_down", "ln_ffn_g", "ln_ffn_b")}
    cos_pe, sin_pe = _rope_tables(seq, 64)
    cos_c, sin_c = _rope_tables(seq, 128)
    cq, sq = jnp.tile(cos_pe, (bsz, 8)), jnp.tile(sin_pe, (bsz, 8))
    ck, sk = jnp.tile(cos_pe, (bsz, 1)), jnp.tile(sin_pe, (bsz, 1))
    cd, sd = jnp.tile(cos_c, (bsz, 8)), jnp.tile(sin_c, (bsz, 8))

    w_in0 = _pad_cols(p["ev_w_in"][0], EV_PAD)
    w_uq, w_ukv = _uq_perm(p["ev_mla_w_uq"][0]), _ukv_perm(p["ev_mla_w_ukv"][0])
    z0 = matmul("ev_in", x2b, w_in0)
    c_q, c_kv, k_pe, q_g, k_g, v_g, r_g, lr_g = (z0[:, a:b] for a, b in _offsets(EV_W))
    qr = _rows("q_rms", rms_f, [c_q], [p["ev_mla_q_norm"]], [512], TM, dtypes=[BF16])[0]
    kvr = _rows("kv_rms", rms_f, [c_kv], [p["ev_mla_kv_norm"]], [256], TM, dtypes=[BF16])[0]
    q = matmul("mla_uq", qr, w_uq)
    kv = matmul("mla_ukv", kvr, w_ukv)
    qn, qp1, qp2 = q[:, :1024], q[:, 1024:1280], q[:, 1280:]
    kn, vv = kv[:, :1024], kv[:, 1024:]
    kp1, kp2 = k_pe[:, :32], k_pe[:, 32:]
    r1, r2 = _rows("rope_q", rope_f, [qp1, qp2, cq, sq], [], [256, 256], TM)
    kr1, kr2 = _rows("rope_k", rope_f, [kp1, kp2, ck, sk], [], [32, 32], TM)
    r1h, r2h = _to_heads(r1, bsz, 8), _to_heads(r2, bsz, 8)
    keys = [(k, 0) for k in ("xa_w_q", "xa_w_k", "xa_w_v", "xa_w_o")]
    a_out, got = mla_attn(bsz, qn, r1h, r2h, kn, kr1, kr2, vv, comm=_gather_specs(late_local, keys))
    _put_gathered(p, late_local, keys, got)
    gla_prm = (p["ev_gla_w_gate2"][0], p["ev_gla_b_gate"], p["ev_gla_norm_g"], p["ev_gla_norm_b"])
    keys = [("ffn_w_gate", 0), ("ffn_w_up", 0)]
    b_out, gla_sv, got = gla_block(bsz, q_g, k_g, v_g, r_g, lr_g, *gla_prm, comm=_gather_specs(late_local, keys))
    _put_gathered(p, late_local, keys, got)
    mixin0 = jnp.concatenate([a_out, b_out], axis=1)
    mix0 = matmul("ev_out", mixin0, p["ev_w_out"][0])
    h1, h1b = _ln("ln_mix0", x2, mix0, p["ln_mix_g"][0:1], p["ln_mix_b"][0:1])
    h3, h3b, tail0 = _tail_fwd(0, h1, h1b, mem2b, bsz, p, late_local)

    w_in1 = _pad_cols(p["od_w_in"][0], OD_PAD)
    z1 = matmul("od_in", h3b, w_in1)
    dq_, dk_, dv_ = z1[:, :1024], z1[:, 1024:2048], z1[:, 2048:3072]
    d_in = z1[:, 3072:OD_IN]
    q1, q2 = _halves(dq_, 8)
    k1, k2 = _halves(dk_, 8)
    qd1, qd2 = _rows("rope_dq", rope_f, [q1, q2, cd, sd], [], [512, 512], TM)
    kd1, kd2 = _rows("rope_dk", rope_f, [k1, k2, cd, sd], [], [512, 512], TM)
    c_out = dil_block(bsz, qd1, qd2, kd1, kd2, dv_)
    d_prev = _shift_prev(d_in, bsz)
    mu = p["od_rwkv_mu"]
    ds = _rows("tshift", tshift_f, [d_in, d_prev], [mu], [d_in.shape[1]], TM)[0]
    rw_in = tuple(ds[:, a:b] for a, b in _offsets((1024, 1024, 1024, 96, 96, 256)))
    rw_prm = dict(w0=p["od_rwkv_w0"], wd2=p["od_rwkv_w_decay2"][0], a0=p["od_rwkv_a0"], wa2=p["od_rwkv_w_a2"][0],
                  wg2=p["od_rwkv_w_gate2"][0], k_k=p["od_rwkv_k_k"], k_a=p["od_rwkv_k_a"], r_k=p["od_rwkv_r_k"][0],
                  gn_g=p["od_rwkv_gn_g"], gn_b=p["od_rwkv_gn_b"])
    keys = [(k, 1) for k in LATE]
    d_out, rw_scan, got = rwkv_block(bsz, *rw_in, rw_prm, comm=_gather_specs(late_local, keys))
    _put_gathered(p, late_local, keys, got)
    mixin1 = jnp.concatenate([c_out, d_out], axis=1)
    mix1 = matmul("od_out", mixin1, p["od_w_out"][0])
    h4, h4b = _ln("ln_mix1", h3, mix1, p["ln_mix_g"][1:2], p["ln_mix_b"][1:2])
    y, _, tail1 = _tail_fwd(1, h4, h4b, mem2b, bsz, p)

    dy, row_loss = _rows("loss", loss_f, [y, tgt2], [], [d, 1], TM)
    loss = jnp.sum(row_loss)

    dh4 = _tail_bwd(1, dy, tail1, mem2b, bsz, p, gr)
    dh3, dmix1, gr["ln_mix_g"][1], gr["ln_mix_b"][1] = _ln("ln_mix1", h3, mix1, p["ln_mix_g"][1:2], p["ln_mix_b"][1:2], cts=dh4)
    dmixin1 = matmul("od_out_dx", dmix1, p["od_w_out"][0], tb=True)
    gr["od_w_out"] = matmul("od_out_dw", mixin1, dmix1, ta=True)[None]
    dc_out, dd_out = dmixin1[:, :1024], dmixin1[:, 1024:]
    drw_in, drw_prm, done1 = rwkv_block(bsz, *rw_in, rw_prm, cts=dd_out, scan=rw_scan,
                                        comm=[chip_exchange_spec(x_) for x_ in _late_partials(1, gr)])
    dds = jnp.concatenate(drw_in, axis=1)
    dd_in, dd_prev, dmu = _rows("tshift_bwd", tshift_f, [d_in, d_prev], [mu], None, TM, fwd=False, cts=[dds],
                                wrt_rows=(0, 1), wrt_params=(0,))
    dd_in = dd_in + _shift_next(dd_prev, bsz)
    dqd1, dqd2, dkd1, dkd2, ddv = dil_block(bsz, qd1, qd2, kd1, kd2, dv_, cts=dc_out)
    dq1, dq2 = _rows("rope_dq_bwd", rope_f, [q1, q2, cd, sd], [], None, TM, fwd=False, cts=[dqd1, dqd2], wrt_rows=(0, 1))
    dk1, dk2 = _rows("rope_dk_bwd", rope_f, [k1, k2, cd, sd], [], None, TM, fwd=False, cts=[dkd1, dkd2], wrt_rows=(0, 1))
    dz1 = jnp.concatenate([_unhalves(dq1, dq2, 8), _unhalves(dk1, dk2, 8), ddv, dd_in,
                           jnp.zeros((t, OD_PAD - OD_IN), F32)], axis=1).astype(BF16)
    gr["od_w_in"] = matmul("od_in_dw", h3b, dz1, ta=True)[:, :OD_IN][None]
    dh3 = matmul("od_in_dx", dz1, w_in1, tb=True, add=dh3)
    gr["od_rwkv_mu"] = dmu
    gr["od_rwkv_w0"], gr["od_rwkv_w_decay2"], gr["od_rwkv_a0"] = drw_prm["w0"], drw_prm["wd2"][None], drw_prm["a0"]
    gr["od_rwkv_w_a2"], gr["od_rwkv_w_gate2"] = drw_prm["wa2"][None], drw_prm["wg2"][None]
    gr["od_rwkv_k_k"], gr["od_rwkv_k_a"], gr["od_rwkv_r_k"] = drw_prm["k_k"], drw_prm["k_a"], drw_prm["r_k"][None]
    gr["od_rwkv_gn_g"], gr["od_rwkv_gn_b"] = drw_prm["gn_g"], drw_prm["gn_b"]

    dh1 = _tail_bwd(0, dh3, tail0, mem2b, bsz, p, gr)
    dx2, dmix0, gr["ln_mix_g"][0], gr["ln_mix_b"][0] = _ln("ln_mix0", x2, mix0, p["ln_mix_g"][0:1], p["ln_mix_b"][0:1], cts=dh1)
    dmixin0 = matmul("ev_out_dx", dmix0, p["ev_w_out"][0], tb=True)
    gr["ev_w_out"] = matmul("ev_out_dw", mixin0, dmix0, ta=True)[None]
    da_out, db_out = dmixin0[:, :1024], dmixin0[:, 1024:]
    od_parts = [chip_partials("rs_" + k, _shard_stack(gr[k], BIG[k])) for k in ("od_w_in", "od_w_out")]
    dq_g, dk_g, dv_g, dlr4, dr_g, dw2, dbg, dng, dnb, done_in, done_out = gla_block(
        bsz, q_g, k_g, v_g, r_g, lr_g, *gla_prm, cts=(db_out, gla_sv), comm=[chip_exchange_spec(x_) for x_ in od_parts])
    dlr_g = jnp.sum(dlr4, axis=0)
    dqn, dr1h, dr2h, dkn, dkr1, dkr2, dvv, *done0 = mla_attn(bsz, qn, r1h, r2h, kn, kr1, kr2, vv, cts=da_out,
                                                             comm=[chip_exchange_spec(x_) for x_ in _late_partials(0, gr)])
    dqp1, dqp2 = _rows("rope_q_bwd", rope_f, [qp1, qp2, cq, sq], [], None, TM, fwd=False,
                       cts=[_from_heads(dr1h), _from_heads(dr2h)], wrt_rows=(0, 1))
    dkp1, dkp2 = _rows("rope_k_bwd", rope_f, [kp1, kp2, ck, sk], [], None, TM, fwd=False, cts=[dkr1, dkr2], wrt_rows=(0, 1))
    dq = jnp.concatenate([dqn, dqp1, dqp2], axis=1).astype(BF16)
    dkv = jnp.concatenate([dkn, dvv], axis=1).astype(BF16)
    dqr = matmul("mla_uq_dx", dq, w_uq, tb=True)
    gr["ev_mla_w_uq"] = _uq_unperm(matmul("mla_uq_dw", qr, dq, ta=True))[None]
    dkvr = matmul("mla_ukv_dx", dkv, w_ukv, tb=True)
    gr["ev_mla_w_ukv"] = _ukv_unperm(matmul("mla_ukv_dw", kvr, dkv, ta=True))[None]
    dc_q, gr["ev_mla_q_norm"] = _rows("q_rms_bwd", rms_f, [c_q], [p["ev_mla_q_norm"]], None, TM, fwd=False, cts=[dqr],
                                      wrt_rows=(0,), wrt_params=(0,))
    dc_kv, gr["ev_mla_kv_norm"] = _rows("kv_rms_bwd", rms_f, [c_kv], [p["ev_mla_kv_norm"]], None, TM, fwd=False, cts=[dkvr],
                                        wrt_rows=(0,), wrt_params=(0,))
    dz0 = jnp.concatenate([dc_q, dc_kv, dkp1, dkp2, dq_g, dk_g, dv_g, dr_g, dlr_g,
                           jnp.zeros((t, EV_PAD - EV_IN), F32)], axis=1).astype(BF16)
    gr["ev_w_in"] = matmul("ev_in_dw", x2b, dz0, ta=True)[:, :EV_IN][None]
    dx2 = matmul("ev_in_dx", dz0, w_in0, tb=True, add=dx2)
    gr["ev_gla_w_gate2"], gr["ev_gla_b_gate"] = dw2[None], dbg
    gr["ev_gla_norm_g"], gr["ev_gla_norm_b"] = dng, dnb
    stacks = {k: jnp.stack([s0, s1], axis=1) for k, s0, s1 in zip(LATE, done0, done1)}
    stacks.update(od_w_in=done_in, od_w_out=done_out)
    for k in stacks:
        del gr[k]
    for k in list(gr):
        if isinstance(gr[k], list):
            gr[k] = jnp.stack([g[0] if k.startswith("ln_") else g for g in gr[k]])
    return loss, dx2.reshape(bsz, seq, d), gr, stacks


WEIGHTS = ['ev_w_in', 'ev_mla_q_norm', 'ev_mla_w_uq', 'ev_mla_kv_norm', 'ev_mla_w_ukv', 'ev_gla_w_gate2', 'ev_gla_b_gate',
           'ev_gla_norm_g', 'ev_gla_norm_b', 'ev_w_out', 'od_w_in', 'od_rwkv_mu', 'od_rwkv_w0', 'od_rwkv_w_decay2',
           'od_rwkv_a0', 'od_rwkv_w_a2', 'od_rwkv_w_gate2', 'od_rwkv_k_k', 'od_rwkv_k_a', 'od_rwkv_r_k', 'od_rwkv_gn_g',
           'od_rwkv_gn_b', 'od_w_out', 'ln_mix_g', 'ln_mix_b', 'xa_w_q', 'xa_w_k', 'xa_w_v', 'xa_w_o', 'ln_xa_g', 'ln_xa_b',
           'ffn_w_gate', 'ffn_w_up', 'ffn_w_down', 'ln_ffn_g', 'ln_ffn_b']
BIG = {'ev_w_in': -1, 'ev_mla_w_uq': -1, 'ev_mla_w_ukv': -1, 'ev_w_out': -2, 'od_w_in': -1, 'od_w_out': -2,
       'xa_w_q': -2, 'xa_w_k': -2, 'xa_w_v': -2, 'xa_w_o': -2, 'ffn_w_gate': -1, 'ffn_w_up': -1, 'ffn_w_down': -2}
SMALL = ['ev_gla_w_gate2', 'od_rwkv_mu', 'od_rwkv_w0', 'od_rwkv_w_decay2', 'od_rwkv_a0', 'od_rwkv_w_a2', 'od_rwkv_w_gate2',
         'od_rwkv_k_k', 'od_rwkv_k_a', 'od_rwkv_gn_g', 'od_rwkv_gn_b']
REPL = ['ev_mla_q_norm', 'ev_mla_kv_norm', 'ev_gla_b_gate', 'ev_gla_norm_g', 'ev_gla_norm_b', 'od_rwkv_r_k',
        'ln_mix_g', 'ln_mix_b', 'ln_xa_g', 'ln_xa_b', 'ln_ffn_g', 'ln_ffn_b']
PACK_COLS = 128


def _unshard(g, shape, axis):
    axis %= len(shape)
    full = list(shape)
    full[axis] *= N_DEV
    return jnp.moveaxis(g, 0, axis).reshape(full)


def _shard_stack(gfull, axis):
    axis %= gfull.ndim
    shp = list(gfull.shape)
    shp[axis:axis + 1] = [N_DEV, shp[axis] // N_DEV]
    return jnp.moveaxis(gfull.reshape(shp), axis, 0)


def _pack(arrs):
    lead = arrs[0].shape[0]
    flat = jnp.concatenate([a.reshape(lead, -1) for a in arrs], axis=1)
    n = flat.shape[1]
    rows = -(-n // (8 * PACK_COLS)) * 8
    return jnp.pad(flat, ((0, 0), (0, rows * PACK_COLS - n))).reshape(lead, rows, PACK_COLS)


def _unpack(buf, shapes):
    lead = buf.shape[0]
    flat = buf.reshape(lead, -1)
    out, off = [], 0
    for shp in shapes:
        n = math.prod(shp)
        out.append(flat[:, off:off + n].reshape((lead,) + tuple(shp)))
        off += n
    return out


def train_step(x, mem, loss_target, w, m, v):
    p, late_local = {}, {}
    for k, ax in BIG.items():
        if k in LATE or k in ("od_w_in", "od_w_out"):
            p[k] = [None] * w[k].shape[0]
            for l in range(w[k].shape[0]):
                late_local[(k, l)] = w[k][l].astype(BF16)
        else:
            p[k] = _unshard(all_gather_2level("ag_" + k, w[k].astype(BF16)), w[k].shape, ax)
    small_loc = _pack([w[k][None] for k in SMALL])[0]
    small_all = _unpack(all_gather("ag_small", small_loc), [w[k].shape for k in SMALL])
    for k, g in zip(SMALL, small_all):
        p[k] = _unshard(g, w[k].shape, -1)
    for k in REPL:
        p[k] = w[k]
    loss, dx, gr, stacks = device_step(x, mem, loss_target, p, late_local)
    loss = lax.psum(loss, ("x", "y", "c"))

    for k, ax in BIG.items():
        if k not in stacks:
            stacks[k] = reduce_scatter_2level("rs_" + k, _shard_stack(gr[k], ax))
    small_send = _pack([_shard_stack(gr[k], -1) for k in SMALL])
    small_recv = _unpack(reduce_scatter_exchange("rs_small", small_send), [w[k].shape for k in SMALL])
    stacks.update(zip(SMALL, small_recv))
    repl_loc = _pack([gr[k][None] for k in REPL])[0]
    repl_all = _unpack(all_gather("ag_repl_grads", repl_loc), [w[k].shape for k in REPL])
    stacks.update(zip(REPL, repl_all))

    grads, deltas, new_m, new_v = [], [], [], []
    for k in WEIGHTS:
        g, dl, mn, vn = adamw("adamw_" + k, w[k], m[k], v[k], stacks[k])
        grads.append(g), deltas.append(dl), new_m.append(mn), new_v.append(vn)
    return (loss, dx, *grads, *deltas, *new_m, *new_v)


def kernel(x, mem, ev_w_in, ev_mla_q_norm, ev_mla_w_uq, ev_mla_kv_norm, ev_mla_w_ukv, ev_gla_w_gate2, ev_gla_b_gate, ev_gla_norm_g, ev_gla_norm_b, ev_w_out, od_w_in, od_rwkv_mu, od_rwkv_w0, od_rwkv_w_decay2, od_rwkv_a0, od_rwkv_w_a2, od_rwkv_w_gate2, od_rwkv_k_k, od_rwkv_k_a, od_rwkv_r_k, od_rwkv_gn_g, od_rwkv_gn_b, od_w_out, ln_mix_g, ln_mix_b, xa_w_q, xa_w_k, xa_w_v, xa_w_o, ln_xa_g, ln_xa_b, ffn_w_gate, ffn_w_up, ffn_w_down, ln_ffn_g, ln_ffn_b, loss_target, m_ev_w_in, m_ev_mla_q_norm, m_ev_mla_w_uq, m_ev_mla_kv_norm, m_ev_mla_w_ukv, m_ev_gla_w_gate2, m_ev_gla_b_gate, m_ev_gla_norm_g, m_ev_gla_norm_b, m_ev_w_out, m_od_w_in, m_od_rwkv_mu, m_od_rwkv_w0, m_od_rwkv_w_decay2, m_od_rwkv_a0, m_od_rwkv_w_a2, m_od_rwkv_w_gate2, m_od_rwkv_k_k, m_od_rwkv_k_a, m_od_rwkv_r_k, m_od_rwkv_gn_g, m_od_rwkv_gn_b, m_od_w_out, m_ln_mix_g, m_ln_mix_b, m_xa_w_q, m_xa_w_k, m_xa_w_v, m_xa_w_o, m_ln_xa_g, m_ln_xa_b, m_ffn_w_gate, m_ffn_w_up, m_ffn_w_down, m_ln_ffn_g, m_ln_ffn_b, v_ev_w_in, v_ev_mla_q_norm, v_ev_mla_w_uq, v_ev_mla_kv_norm, v_ev_mla_w_ukv, v_ev_gla_w_gate2, v_ev_gla_b_gate, v_ev_gla_norm_g, v_ev_gla_norm_b, v_ev_w_out, v_od_w_in, v_od_rwkv_mu, v_od_rwkv_w0, v_od_rwkv_w_decay2, v_od_rwkv_a0, v_od_rwkv_w_a2, v_od_rwkv_w_gate2, v_od_rwkv_k_k, v_od_rwkv_k_a, v_od_rwkv_r_k, v_od_rwkv_gn_g, v_od_rwkv_gn_b, v_od_w_out, v_ln_mix_g, v_ln_mix_b, v_xa_w_q, v_xa_w_k, v_xa_w_v, v_xa_w_o, v_ln_xa_g, v_ln_xa_b, v_ffn_w_gate, v_ffn_w_up, v_ffn_w_down, v_ln_ffn_g, v_ln_ffn_b):
    given = dict(locals())
    w = {k: given[k] for k in WEIGHTS}
    m = {k: given["m_" + k] for k in WEIGHTS}
    v = {k: given["v_" + k] for k in WEIGHTS}
    return train_step(given["x"], given["mem"], given["loss_target"], w, m, v)
```

```python
import functools
import math

import jax
import jax.numpy as jnp
from jax import lax
from jax.experimental import pallas as pl
from jax.experimental.pallas import tpu as pltpu

F32 = jnp.float32
BF16 = jnp.bfloat16
VMEM_LIMIT = 56 * 1024 * 1024
ROWS_VMEM = 20 * 1024 * 1024

N_DEV = 8
DEPTH = 2
ALPHA = (2.0 * DEPTH) ** 0.25
LN_EPS = 1e-5
RMS_EPS = 1e-6
RWKV_GN_EPS = 64e-5
ADAM_LR, ADAM_B1, ADAM_B2, ADAM_EPS, ADAM_WD, ADAM_STEP = 0.001, 0.9, 0.999, 1e-08, 0.01, 10
NEG_INF = float("-inf")


def _pcall(body, **kw):
    return pl.pallas_call(body, **kw)


def _cparams(**kw):
    return pltpu.CompilerParams(vmem_limit_bytes=VMEM_LIMIT, **kw)


def _dg(a, b, ca, cb, batch):
    nb = 1 if batch else 0
    dims = (((ca + nb,), (cb + nb,)), ((0,), (0,)) if batch else ((), ()))
    return lax.dot_general(a.astype(BF16), b.astype(BF16), dims, preferred_element_type=F32)


@functools.partial(jax.custom_vjp, nondiff_argnums=(2, 3, 4))
def _mm(a, b, ta, tb, batch):
    return _dg(a, b, 0 if ta else 1, 1 if tb else 0, batch)


def _mm_fwd(a, b, ta, tb, batch):
    return _mm(a, b, ta, tb, batch), (a, b)


def _mm_bwd(ta, tb, batch, res, g):
    a, b = res
    if not ta and not tb:
        da, db = _mm(g, b, False, True, batch), _mm(a, g, True, False, batch)
    elif not ta and tb:
        da, db = _mm(g, b, False, False, batch), _mm(g, a, True, False, batch)
    elif ta and not tb:
        da, db = _mm(b, g, False, True, batch), _mm(a, g, False, False, batch)
    else:
        da, db = _mm(b, g, True, True, batch), _mm(g, a, True, True, batch)
    return da.astype(a.dtype), db.astype(b.dtype)


_mm.defvjp(_mm_fwd, _mm_bwd)


def mm(a, b, ta=False, tb=False):
    return _mm(a, b, ta, tb, a.ndim == 3)


def _bs(block, imap):
    return pl.BlockSpec(block, imap)


def _rev_imap(imap, n):
    def r(*idx):
        return imap(*idx[:-1], n - 1 - idx[-1])
    return r


def _gcall(body, name, grid, in_specs, out_specs, out_shape, scratch_shapes, args, comm=None):
    comm = comm or []
    n_in, n_out, n_scr = len(in_specs), len(out_specs), len(scratch_shapes)
    c_in = [a for c in comm for a in c["ins"]]
    c_out = [o for c in comm for o in c["outs"]]
    c_sem = [s for c in comm for s in c["sems"]]

    def body2(*refs):
        i = 0
        r_in, i = refs[i:i + n_in], i + n_in
        k_in, i = refs[i:i + len(c_in)], i + len(c_in)
        r_out, i = refs[i:i + n_out], i + n_out
        k_out, i = refs[i:i + len(c_out)], i + len(c_out)
        r_scr, k_sem = refs[i:i + n_scr], refs[i + n_scr:]

        def each(which):
            a = b = s = 0
            for c in comm:
                na, nb, ns = len(c["ins"]), len(c["outs"]), len(c["sems"])
                c[which](k_in[a:a + na], k_out[b:b + nb], k_sem[s:s + ns])
                a, b, s = a + na, b + nb, s + ns

        if comm:
            first = last = None
            for ax, n in enumerate(grid):
                f0, l0 = pl.program_id(ax) == 0, pl.program_id(ax) == n - 1
                first = f0 if first is None else jnp.logical_and(first, f0)
                last = l0 if last is None else jnp.logical_and(last, l0)
            pl.when(first)(lambda: each("start"))
        body(*r_in, *r_out, *r_scr)
        if comm:
            pl.when(last)(lambda: each("finish"))

    return _pcall(
        body2, name=name, grid=grid,
        in_specs=list(in_specs) + [_ANY] * len(c_in), out_specs=list(out_specs) + [_ANY] * len(c_out),
        out_shape=list(out_shape) + c_out, scratch_shapes=list(scratch_shapes) + c_sem,
        compiler_params=_cparams(has_side_effects=True) if comm else _cparams(),
    )(*args, *c_in)


def p_fwd(name, f, grid, ins, outs, carry=None, save_carry=None, comm=None):
    n_in, n_out = len(ins), len(outs)

    def body(*refs):
        in_refs = refs[:n_in]
        out_refs = refs[n_in:n_in + n_out]
        rest = refs[n_in + n_out:]
        vals = [r[...] for r in in_refs]
        if carry is None:
            res = f(*vals)
        else:
            if save_carry is not None:
                sv_ref, c_ref = rest
            else:
                (c_ref,) = rest

            @pl.when(pl.program_id(len(grid) - 1) == 0)
            def _():
                c_ref[...] = jnp.zeros(c_ref.shape, c_ref.dtype)

            c = c_ref[...]
            if save_carry is not None:
                sv_ref[...] = c
            res = f(c, *vals)
            c_ref[...] = res[0]
            res = res[1:]
        if not isinstance(res, (tuple, list)):
            res = (res,)
        for r, v in zip(out_refs, res):
            r[...] = v.astype(r.dtype)

    out_shape = [jax.ShapeDtypeStruct(s, d) for (s, d, _, _) in outs]
    out_specs = [_bs(b, m) for (_, _, b, m) in outs]
    scratch = []
    if carry is not None:
        if save_carry is not None:
            out_shape.append(jax.ShapeDtypeStruct(save_carry[0], carry[1]))
            out_specs.append(_bs(save_carry[1], save_carry[2]))
        scratch.append(pltpu.VMEM(carry[0], carry[1]))
    return _gcall(body, name, grid, [_bs(b, m) for (_, b, m) in ins], out_specs, out_shape, scratch,
                  [a for (a, _, _) in ins], comm)


def p_bwd(name, f, grid, ins, cts, wrt, carry=None, saved=None, comm=None):
    n_in, n_ct, n_w = len(ins), len(cts), len(wrt)
    rev = carry is not None
    n_last = grid[-1]

    def fix(imap):
        return _rev_imap(imap, n_last) if rev else imap

    def body(*refs):
        in_refs = refs[:n_in]
        ct_refs = refs[n_in:n_in + n_ct]
        k = n_in + n_ct
        if rev:
            sv_ref = refs[k]
            k += 1
        out_refs = refs[k:k + n_w]
        rest = refs[k + n_w:]
        vals = [r[...] for r in in_refs]
        ct_vals = [r[...].astype(F32) for r in ct_refs]
        widx = [w[0] for w in wrt]

        if rev:
            (dc_ref,) = rest

            @pl.when(pl.program_id(len(grid) - 1) == 0)
            def _():
                dc_ref[...] = jnp.zeros(dc_ref.shape, dc_ref.dtype)

            c_in = sv_ref[...]

            def g(c, *dv):
                full = list(vals)
                for i, d in zip(widx, dv):
                    full[i] = d
                return tuple(f(c, *full))

            _, vjp = jax.vjp(g, c_in, *[vals[i] for i in widx])
            grads = vjp((dc_ref[...],) + tuple(ct_vals))
            dc_ref[...] = grads[0]
            grads = grads[1:]
        else:
            def g(*dv):
                full = list(vals)
                for i, d in zip(widx, dv):
                    full[i] = d
                r = f(*full)
                return tuple(r) if isinstance(r, (tuple, list)) else (r,)

            _, vjp = jax.vjp(g, *[vals[i] for i in widx])
            grads = vjp(tuple(ct_vals))

        for w, o_ref, gr in zip(wrt, out_refs, grads):
            acc = w[1]
            if acc is None:
                o_ref[...] = gr.astype(o_ref.dtype)
            else:
                first = None
                for ax in acc:
                    c0 = pl.program_id(ax) == 0
                    first = c0 if first is None else jnp.logical_and(first, c0)

                @pl.when(first)
                def _():
                    o_ref[...] = jnp.zeros(o_ref.shape, o_ref.dtype)

                o_ref[...] += gr.astype(o_ref.dtype)

    in_specs = [_bs(b, fix(m)) for (_, b, m) in ins] + [_bs(b, fix(m)) for (_, b, m) in cts]
    args = [a for (a, _, _) in ins] + [a for (a, _, _) in cts]
    if rev:
        in_specs.append(_bs(saved[1], fix(saved[2])))
        args.append(saved[0])
    out_shape, out_specs = [], []
    for w in wrt:
        a, b, m = ins[w[0]]
        if len(w) > 2 and w[2] is not None:
            m = w[2]
        out_shape.append(jax.ShapeDtypeStruct(a.shape, w[3] if len(w) > 3 else F32))
        out_specs.append(_bs(b, fix(m)))
    scratch = [pltpu.VMEM(carry[0], carry[1])] if rev else []
    return _gcall(body, name, grid, in_specs, out_specs, out_shape, scratch, args, comm)


def _rows(name, f, row_ins, params, out_widths, tm, fwd=True, cts=None, wrt_rows=(), wrt_params=(), dtypes=None):
    t = row_ins[0].shape[0]
    width = sum(a.shape[1] for a in row_ins)
    width += sum(out_widths) if fwd else sum(c.shape[1] for c in cts) + sum(row_ins[i].shape[1] for i in wrt_rows)
    tm = min(tm, t)
    while tm > 8 and 2 * 4 * tm * width > ROWS_VMEM:
        tm //= 2
    rmap = lambda i: (i, 0)
    pmap = lambda i: (0, 0)
    ins = [(a, (tm, a.shape[1]), rmap) for a in row_ins] + [(p, p.shape, pmap) for p in params]
    if fwd:
        dtypes = dtypes or [F32] * len(out_widths)
        outs = [((t, w), dt, (tm, w), rmap) for w, dt in zip(out_widths, dtypes)]
        return p_fwd(name, f, (t // tm,), ins, outs)
    ct_specs = [(c, (tm, c.shape[1]), rmap) for c in cts]
    dtypes = dtypes or [F32] * len(wrt_rows)
    wrt = [(i, None, None, dt) for i, dt in zip(wrt_rows, dtypes)] + [(len(row_ins) + i, (0,)) for i in wrt_params]
    return p_bwd(name, f, (t // tm,), ins, ct_specs, wrt)


def _pick(n, cands):
    for c in cands:
        if n % c == 0:
            return c
    return n


def _wide(n, cap=1664):
    best = None
    for w in range(128, min(n, cap) + 1, 128):
        if n % w == 0:
            best = w
    return best or n


MM_VMEM = 40 * 1024 * 1024


def _mm_tiles(m, n, k, sa, sb, so, has_add):
    bm, bn = _pick(m, (512, 256, 128)), _wide(n)
    bk = k if k <= 2048 else _wide(k, 3328)

    def vmem(bm, bn, bk):
        acc = 0 if bk == k else 4 * bm * bn
        return 2 * (bm * bk * sa + bk * bn * sb + bm * bn * so * (2 if has_add else 1)) + acc

    while vmem(bm, bn, bk) > MM_VMEM and bk % 256 == 0:
        bk //= 2
    while vmem(bm, bn, bk) > MM_VMEM and bn % 256 == 0:
        bn //= 2
    return bm, bn, bk


def matmul(name, a, b, ta=False, tb=False, out_dtype=F32, add=None, comm=None):
    m = a.shape[1] if ta else a.shape[0]
    k = a.shape[0] if ta else a.shape[1]
    n = b.shape[0] if tb else b.shape[1]
    assert (b.shape[1] if tb else b.shape[0]) == k, (a.shape, b.shape, ta, tb)
    bm, bn, bk = _mm_tiles(m, n, k, a.dtype.itemsize, b.dtype.itemsize, jnp.dtype(out_dtype).itemsize, add is not None)
    nk = k // bk

    def body(a_ref, b_ref, *rest):
        o_ref, acc_ref = rest[-2:]
        if nk == 1:
            r = mm(a_ref[...], b_ref[...], ta, tb)
            o_ref[...] = (r if add is None else r + rest[0][...].astype(F32)).astype(o_ref.dtype)
            return

        @pl.when(pl.program_id(2) == 0)
        def _():
            acc_ref[...] = jnp.zeros(acc_ref.shape, F32) if add is None else rest[0][...].astype(F32)

        acc_ref[...] += mm(a_ref[...], b_ref[...], ta, tb)

        @pl.when(pl.program_id(2) == nk - 1)
        def _():
            o_ref[...] = acc_ref[...].astype(o_ref.dtype)

    a_spec = _bs((bk, bm), lambda i, j, l: (l, i)) if ta else _bs((bm, bk), lambda i, j, l: (i, l))
    b_spec = _bs((bn, bk), lambda i, j, l: (j, l)) if tb else _bs((bk, bn), lambda i, j, l: (l, j))
    o_spec = _bs((bm, bn), lambda i, j, l: (i, j))
    if comm:
        out, *comm_out = _gcall(
            body, name, (m // bm, n // bn, nk), [a_spec, b_spec] + ([] if add is None else [o_spec]), [o_spec],
            [jax.ShapeDtypeStruct((m, n), out_dtype)], [pltpu.VMEM((bm, bn) if nk > 1 else (8, 128), F32)],
            (a, b) if add is None else (a, b, add), comm)
        return out, comm_out
    return _pcall(
        body, name=name, grid=(m // bm, n // bn, nk),
        in_specs=[a_spec, b_spec] + ([] if add is None else [o_spec]), out_specs=o_spec,
        out_shape=jax.ShapeDtypeStruct((m, n), out_dtype),
        scratch_shapes=[pltpu.VMEM((bm, bn) if nk > 1 else (8, 128), F32)],
        compiler_params=_cparams(dimension_semantics=("parallel", "parallel", "arbitrary")),
    )(*((a, b) if add is None else (a, b, add)))


def _to_heads(x, b, h):
    t, w = x.shape
    return x.reshape(b, t // b, h, w // h).transpose(0, 2, 1, 3)


def _from_heads(x):
    b, h, s, d = x.shape
    return x.transpose(0, 2, 1, 3).reshape(b * s, h * d)


RW_STEPS = 16
RW_G = 16


def rwkv_group(st, w8, kk8, ka8, kh8, r8, vc):
    ys = []
    for t in range(RW_STEPS):
        row = lambda x: x[:, t:t + 1, :]
        sa = jnp.sum(st * row(kk8), axis=2, keepdims=True)
        st = st * row(w8) - sa * row(ka8) + vc[:, :, t:t + 1] * row(kh8)
        ys.append(jnp.sum(st * row(r8), axis=2, keepdims=True))
    return st, jnp.concatenate(ys, axis=2)


def rwkv_prehead_f(kkraw, a):
    nrm = jnp.sqrt(jnp.sum(kkraw * kkraw, axis=-1, keepdims=True))
    kk = kkraw / jnp.maximum(nrm, 1e-12)
    return kk, kk * a


def rwkv_pre_f(kd, w_lr, a_lr, g_lr, w0, wd2, a0, wa2, wg2, k_k, k_a):
    wpre = w0 + mm(jnp.tanh(w_lr), wd2)
    w = -jax.nn.softplus(-wpre) - 0.5
    decay = jnp.exp(-jnp.exp(w))
    a = jax.nn.sigmoid(a0 + mm(a_lr, wa2))
    g = mm(jax.nn.sigmoid(g_lr), wg2)
    kkraw = kd * k_k
    kh = kd * (1.0 + (a - 1.0) * k_a)
    return decay, a, g, kkraw, kh


def rwkv_post_f(y, r, kh, v, g, gn_g, gn_b, r_k):
    mu = jnp.mean(y, axis=-1, keepdims=True)
    var = jnp.mean(jnp.square(y - mu), axis=-1, keepdims=True)
    yn = (y - mu) * lax.rsqrt(var + RWKV_GN_EPS) * gn_g + gn_b
    bonus = jnp.sum(r * kh * r_k, axis=-1, keepdims=True) * v
    return (yn + bonus) * g


def _to_cols(xh):
    b, h, s, d = xh.shape
    return xh.reshape(b, h, s // RW_STEPS, RW_STEPS, d).transpose(0, 1, 2, 4, 3)


def _from_cols(xc):
    b, h, n, d, k = xc.shape
    return xc.transpose(0, 1, 2, 4, 3).reshape(b, h, n * k, d)


def rwkv_block(bsz, r, kd, vd, w_lr, a_lr, g_lr, prm, cts=None, scan=None, comm=None):
    t = r.shape[0]
    hh, n = 16, 64
    tm = 256
    pre_rows = [kd, w_lr, a_lr, g_lr]
    pre_prm = [prm[k] for k in ("w0", "wd2", "a0", "wa2", "wg2", "k_k", "k_a")]
    decay, a, g, kkraw, kh = _rows("rwkv_pre", rwkv_pre_f, pre_rows, pre_prm, [1024] * 5, tm)
    heads = lambda x: _to_heads(x, bsz, hh)
    rh, khh, vh, gh, dech, kkrawh, ah = (heads(x) for x in (r, kh, vd, g, decay, kkraw, a))
    s = t // bsz
    ng = s // RW_STEPS
    ts = 512
    hb = (None, None, ts, n)
    hm = lambda h, b, i: (b, h, i, 0)
    pgrid = (hh, bsz, s // ts)
    ph_ins = [(kkrawh, hb, hm), (ah, hb, hm)]
    kkh, kah = p_fwd("rwkv_prehead", rwkv_prehead_f, pgrid, ph_ins, [((bsz, hh, s, n), F32, hb, hm)] * 2)
    grp = lambda x: x.reshape(bsz, hh, ng, RW_STEPS, n)
    cb, rb, sb = (None, RW_G, None, n, RW_STEPS), (None, RW_G, None, RW_STEPS, n), (None, RW_G, None, n, n)
    cm = lambda b, h, i: (b, h, i, 0, 0)
    sc_ins = [(grp(x), rb, cm) for x in (dech, kkh, kah, khh, rh)] + [(_to_cols(vh), cb, cm)]
    grid = (bsz, hh // RW_G, ng)
    comm_out = []
    if scan is None:
        yc, sv, *comm_out = p_fwd("rwkv_scan", rwkv_group, grid, sc_ins, [((bsz, hh, ng, n, RW_STEPS), F32, cb, cm)],
                                  carry=((RW_G, n, n), F32), save_carry=((bsz, hh, ng, n, n), sb, cm), comm=comm)
    else:
        yc, sv = scan
    yh = _from_cols(yc)
    pb = (None, 1, n)
    pm = lambda h, b, i: (h, 0, 0)
    gn_g, gn_b, r_k = (prm[k].reshape(hh, 1, n) for k in ("gn_g", "gn_b", "r_k"))
    post_ins = [(x, hb, hm) for x in (yh, rh, khh, vh, gh)] + [(p, pb, pm) for p in (gn_g, gn_b, r_k)]
    if cts is None:
        (oh,) = p_fwd("rwkv_post", rwkv_post_f, pgrid, post_ins, [((bsz, hh, s, n), BF16, hb, hm)])
        return _from_heads(oh), (yc, sv), comm_out
    doh = _to_heads(cts, bsz, hh)
    dyh, drh1, dkhh1, dvh1, dgh, dgn_g, dgn_b, dr_k = p_bwd(
        "rwkv_post_bwd", rwkv_post_f, pgrid, post_ins, [(doh, hb, hm)],
        [(i, None) for i in range(5)] + [(5 + i, (1, 2)) for i in range(3)])
    drows_v = p_bwd("rwkv_scan_bwd", rwkv_group, grid, sc_ins, [(_to_cols(dyh), cb, cm)], [(i, None) for i in range(6)],
                    carry=((RW_G, n, n), F32), saved=(sv, sb, cm), comm=comm)
    drows_v, comm_out = drows_v[:6], list(drows_v[6:])
    ddech, dkkh, dkah, dkhh2, drh2 = (x.reshape(bsz, hh, s, n) for x in drows_v[:5])
    dkkrawh, dah = p_bwd("rwkv_prehead_bwd", rwkv_prehead_f, pgrid, ph_ins, [(dkkh, hb, hm), (dkah, hb, hm)],
                         [(0, None), (1, None)])
    ddecay, dkkraw, da = _from_heads(ddech), _from_heads(dkkrawh), _from_heads(dah)
    dv = _from_heads(_from_cols(drows_v[5]) + dvh1)
    dr = _from_heads(drh2 + drh1)
    dkh = _from_heads(dkhh2 + dkhh1)
    dg = _from_heads(dgh)
    res = _rows("rwkv_pre_bwd", rwkv_pre_f, pre_rows, pre_prm, None, tm, fwd=False,
                cts=[ddecay, da, dg, dkkraw, dkh], wrt_rows=(0, 1, 2, 3), wrt_params=tuple(range(7)))
    dkd, dw_lr, da_lr, dg_lr = res[:4]
    dprm = dict(zip(("w0", "wd2", "a0", "wa2", "wg2", "k_k", "k_a"), res[4:]))
    dprm.update(gn_g=dgn_g.reshape(1, -1), gn_b=dgn_b.reshape(1, -1), r_k=dr_k.reshape(hh, n))
    return (dr, dkd, dv, dw_lr, da_lr, dg_lr), dprm, comm_out


GLA_C, GLA_DK, GLA_DV, GLA_H, GLA_TAU = 64, 128, 256, 4, 16.0


def gla_chunk_f(st, q, k, v, lr, r, w2, bg, ng, nb):
    la = jax.nn.log_sigmoid(mm(lr, w2) + bg) / GLA_TAU
    ri = lax.broadcasted_iota(jnp.int32, (GLA_C, GLA_C), 0)
    ci = lax.broadcasted_iota(jnp.int32, (GLA_C, GLA_C), 1)
    causal = ci <= ri
    b = jnp.dot(causal.astype(F32), la, precision=lax.Precision.HIGHEST, preferred_element_type=F32)
    b_last = jnp.sum(la, axis=0, keepdims=True)
    q_dec = (q * (GLA_DK ** -0.5)) * jnp.exp(b)
    k_inv = k * jnp.exp(-b)
    k_end = k * jnp.exp(b_last - b)
    att = jnp.where(causal, mm(q_dec, k_inv, tb=True), 0.0)
    o = mm(att, v) + mm(q_dec, st, tb=True)
    st_new = st * jnp.exp(b_last) + mm(v, k_end, ta=True)
    mu = jnp.mean(o, axis=-1, keepdims=True)
    var = jnp.mean(jnp.square(o - mu), axis=-1, keepdims=True)
    on = (o - mu) * lax.rsqrt(var + LN_EPS) * ng + nb
    return st_new, on * jax.nn.silu(r)


def gla_block(bsz, q, k, v, r, lr, w2, bg, ng, nb, cts=None, comm=None):
    t = q.shape[0]
    nc = t // bsz // GLA_C
    grid = (GLA_H, bsz, nc)
    rm = lambda h, b, c: (b * nc + c, h)
    ins = [(q, (GLA_C, GLA_DK), rm), (k, (GLA_C, GLA_DK), rm), (v, (GLA_C, GLA_DV), rm),
           (jnp.broadcast_to(lr[None], (GLA_H,) + lr.shape), (None, GLA_C, lr.shape[1]), lambda h, b, c: (h, b * nc + c, 0)),
           (r, (GLA_C, GLA_DV), rm),
           (w2, (w2.shape[0], GLA_DK), lambda h, b, c: (0, h)), (bg, (1, GLA_DK), lambda h, b, c: (0, h)),
           (ng, (1, GLA_DV), lambda h, b, c: (0, 0)), (nb, (1, GLA_DV), lambda h, b, c: (0, 0))]
    ob = (GLA_C, GLA_DV)
    sshape, sblock = (GLA_H, bsz, nc, GLA_DV, GLA_DK), (None, None, None, GLA_DV, GLA_DK)
    sm = lambda h, b, c: (h, b, c, 0, 0)
    carry = ((GLA_DV, GLA_DK), F32)
    if cts is None:
        out, sv, *comm_out = p_fwd("gla_scan", gla_chunk_f, grid, ins, [((t, GLA_H * GLA_DV), BF16, ob, rm)],
                                   carry=carry, save_carry=(sshape, sblock, sm), comm=comm)
        return out, sv, comm_out
    dout, sv = cts
    return p_bwd("gla_scan_bwd", gla_chunk_f, grid, ins, [(dout, ob, rm)],
                 [(0, None), (1, None), (2, None), (3, None), (4, None), (5, (1, 2)), (6, (1, 2)), (7, (0, 1, 2)), (8, (0, 1, 2))],
                 carry=carry, saved=(sv, sblock, sm), comm=comm)


def _softmax_rows(sc):
    m = lax.stop_gradient(jnp.max(sc, axis=-1, keepdims=True))
    e = jnp.exp(sc - m)
    return e / jnp.sum(e, axis=-1, keepdims=True)


def mla_attn_f(qn, r1, r2, kn, kr1, kr2, v):
    tq, s = qn.shape[0], kn.shape[0]
    sc = (mm(qn, kn, tb=True) + mm(r1, kr1, tb=True) + mm(r2, kr2, tb=True)) * (192.0 ** -0.5)
    qpos = pl.program_id(2) * tq + lax.broadcasted_iota(jnp.int32, (tq, s), 0)
    kpos = lax.broadcasted_iota(jnp.int32, (tq, s), 1)
    sc = jnp.where(kpos <= qpos, sc, NEG_INF)
    return mm(_softmax_rows(sc), v)


def mla_attn(bsz, qn, r1, r2, kn, kr1, kr2, v, cts=None, tq=256, comm=None):
    t = qn.shape[0]
    s = t // bsz
    nq = s // tq
    hh = 8
    grid = (bsz, hh, nq)
    qm = lambda b, h, i: (b * nq + i, h)
    km_ = lambda b, h, i: (b, h)
    ins = [(qn, (tq, 128), qm),
           (r1, (None, None, tq, 32), lambda b, h, i: (b, h, i, 0)), (r2, (None, None, tq, 32), lambda b, h, i: (b, h, i, 0)),
           (kn, (s, 128), km_), (kr1, (s, 32), lambda b, h, i: (b, 0)), (kr2, (s, 32), lambda b, h, i: (b, 0)),
           (v, (s, 128), km_)]
    if cts is None:
        out, *comm_out = p_fwd("mla_attn", mla_attn_f, grid, ins, [((t, 1024), BF16, (tq, 128), qm)], comm=comm)
        return out, comm_out
    return p_bwd("mla_attn_bwd", mla_attn_f, grid, ins, [(cts, (tq, 128), qm)],
                 [(0, None), (1, None), (2, None), (3, (2,)), (4, (1, 2)), (5, (1, 2)), (6, (2,))], comm=comm)


def xattn_f(q, k, v):
    sc = mm(q, k, tb=True) * (512.0 ** -0.5)
    return mm(_softmax_rows(sc), v)


def xattn(bsz, q, k, v, cts=None, tq=512):
    t = q.shape[0]
    nq = t // bsz // tq
    mlen = k.shape[0] // bsz
    grid = (bsz, 4, nq)
    qm = lambda b, h, i: (b * nq + i, h)
    km_ = lambda b, h, i: (b, h)
    ins = [(q, (tq, 512), qm), (k, (mlen, 512), km_), (v, (mlen, 512), km_)]
    if cts is None:
        return p_fwd("xattn", xattn_f, grid, ins, [((t, 2048), BF16, (tq, 512), qm)])[0]
    return p_bwd("xattn_bwd", xattn_f, grid, ins, [(cts, (tq, 512), qm)], [(0, None, None, BF16), (1, (2,)), (2, (2,))])


DIL_SPAN = 128
DIL_BRANCHES = ((128, 1), (512, 4), (2048, 16))


def dil_attn_f(q1, q2, k1c, k2c, vc, k1p, k2p, vp):
    gb, sp = q1.shape[0], DIL_SPAN
    scale = 128.0 ** -0.5
    sc_c = (mm(q1, k1c, tb=True) + mm(q2, k2c, tb=True)) * scale
    sc_p = (mm(q1, k1p, tb=True) + mm(q2, k2p, tb=True)) * scale
    ql = lax.broadcasted_iota(jnp.int32, (gb, sp, sp), 1)
    kl = lax.broadcasted_iota(jnp.int32, (gb, sp, sp), 2)
    has_prev = pl.program_id(1) > 0
    sc_c = jnp.where(kl <= ql, sc_c, NEG_INF)
    sc_p = jnp.where(jnp.logical_and(kl >= ql, has_prev), sc_p, NEG_INF)
    m = lax.stop_gradient(jnp.maximum(jnp.max(sc_c, axis=-1, keepdims=True), jnp.max(sc_p, axis=-1, keepdims=True)))
    e_c, e_p = jnp.exp(sc_c - m), jnp.exp(sc_p - m)
    den = jnp.sum(e_c, axis=-1, keepdims=True) + jnp.sum(e_p, axis=-1, keepdims=True)
    o = mm(e_c / den, vc) + mm(e_p / den, vp)
    return o, m + jnp.log(den)


def dil_branch(q1, q2, k1, k2, v, cts=None, gb=8):
    g, l, _ = q1.shape
    nb = l // DIL_SPAN
    grid = (g // gb, nb)
    cm = lambda i, n: (i, n, 0)
    pm = lambda i, n: (i, jnp.maximum(n - 1, 0), 0)
    b64, b128, b1 = (gb, DIL_SPAN, 64), (gb, DIL_SPAN, 128), (gb, DIL_SPAN, 1)
    ins = [(q1, b64, cm), (q2, b64, cm), (k1, b64, cm), (k2, b64, cm), (v, b128, cm),
           (k1, b64, pm), (k2, b64, pm), (v, b128, pm)]
    if cts is None:
        return p_fwd("dil_attn", dil_attn_f, grid, ins, [((g, l, 128), F32, b128, cm), ((g, l, 1), F32, b1, cm)])
    do, dlse = cts
    dq1, dq2, dk1c, dk2c, dvc, dk1p, dk2p, dvp = p_bwd(
        "dil_attn_bwd", dil_attn_f, grid, ins, [(do, b128, cm), (dlse, b1, cm)],
        [(i, None) for i in range(5)] + [(i, None, cm) for i in (5, 6, 7)])

    def fold(dc, dp):
        return dc + jnp.pad(dp[:, DIL_SPAN:], ((0, 0), (0, DIL_SPAN), (0, 0)))

    return dq1, dq2, fold(dk1c, dk1p), fold(dk2c, dk2p), fold(dvc, dvp)


def dil_mix_f(o1, o2, o3, l1, l2, l3):
    m = lax.stop_gradient(jnp.maximum(jnp.maximum(l1, l2), l3))
    e1, e2, e3 = jnp.exp(l1 - m), jnp.exp(l2 - m), jnp.exp(l3 - m)
    den = e1 + e2 + e3
    return (e1 / den) * o1 + (e2 / den) * o2 + (e3 / den) * o3


def _to_res(xh, dil):
    b, h, s, d = xh.shape
    return xh.reshape(b, h, s // dil, dil, d).transpose(0, 1, 3, 2, 4).reshape(b * h * dil, s // dil, d)


def _from_res(xr, b, h, dil):
    g, l, d = xr.shape
    return xr.reshape(b, h, dil, l, d).transpose(0, 1, 3, 2, 4).reshape(b, h, l * dil, d)


def dil_block(bsz, q1, q2, k1, k2, v, cts=None):
    hh = 8
    heads = [_to_heads(x, bsz, hh) for x in (q1, q2, k1, k2, v)]
    s = heads[0].shape[2]
    outs, res_in = [], []
    for window, dil in DIL_BRANCHES:
        assert window // dil == DIL_SPAN and (s // dil) % DIL_SPAN == 0
        rin = [_to_res(x, dil) for x in heads]
        o, lse = dil_branch(*rin)
        res_in.append(rin)
        outs.append((_from_res(o, bsz, hh, dil), _from_res(lse, bsz, hh, dil)))
    tq = 512
    ob, lb = (None, None, tq, 128), (None, None, tq, 1)
    hm = lambda b, h, i: (b, h, i, 0)
    mix_ins = [(o, ob, hm) for (o, _) in outs] + [(l, lb, hm) for (_, l) in outs]
    grid = (bsz, hh, s // tq)
    if cts is None:
        (mix,) = p_fwd("dil_mix", dil_mix_f, grid, mix_ins, [((bsz, hh, s, 128), BF16, ob, hm)])
        return _from_heads(mix)
    dmix = _to_heads(cts, bsz, hh)
    dml = p_bwd("dil_mix_bwd", dil_mix_f, grid, mix_ins, [(dmix, ob, hm)], [(i, None) for i in range(6)])
    tot = None
    for j, (window, dil) in enumerate(DIL_BRANCHES):
        do, dl = _to_res(dml[j], dil), _to_res(dml[3 + j], dil)
        gr = dil_branch(*res_in[j], cts=(do, dl))
        gr = [_from_res(x, bsz, hh, dil) for x in gr]
        tot = gr if tot is None else [a + b for a, b in zip(tot, gr)]
    return tuple(_from_heads(x) for x in tot)


_ANY = pl.BlockSpec(memory_space=pl.ANY)


def _me_and_peers():
    x, y, c = lax.axis_index("x"), lax.axis_index("y"), lax.axis_index("c")
    me = 4 * x + 2 * y + c
    peers = []
    for k in range(1, N_DEV):
        px = 1 - x if k & 4 else x
        py = 1 - y if k & 2 else y
        pc = 1 - c if k & 1 else c
        peers.append(((px, py, pc), 4 * px + 2 * py + pc))
    return me, peers


def _exchange(name, x, scatter):
    shape = x.shape[1:] if scatter else x.shape

    def body(x_ref, out_ref, send_sems, recv_sems, local_sem):
        me, peers = _me_and_peers()
        src_me = x_ref.at[me] if scatter else x_ref
        local = pltpu.make_async_copy(src_me, out_ref.at[me], local_sem)
        local.start()
        sends = []
        for k, (dev, idx) in enumerate(peers):
            cp = pltpu.make_async_remote_copy(
                src_ref=x_ref.at[idx] if scatter else x_ref, dst_ref=out_ref.at[me],
                send_sem=send_sems.at[k], recv_sem=recv_sems.at[k],
                device_id=dev, device_id_type=pl.DeviceIdType.MESH)
            cp.start()
            sends.append(cp)
        for k, (dev, idx) in enumerate(peers):
            pltpu.make_async_remote_copy(
                src_ref=src_me, dst_ref=out_ref.at[idx], send_sem=send_sems.at[k], recv_sem=recv_sems.at[k],
                device_id=dev, device_id_type=pl.DeviceIdType.MESH).wait_recv()
        for cp in sends:
            cp.wait_send()
        local.wait()

    return _pcall(
        body, name=name, in_specs=[_ANY], out_specs=_ANY,
        out_shape=jax.ShapeDtypeStruct((N_DEV,) + tuple(shape), x.dtype),
        scratch_shapes=[pltpu.SemaphoreType.DMA((N_DEV - 1,)), pltpu.SemaphoreType.DMA((N_DEV - 1,)),
                        pltpu.SemaphoreType.DMA],
        compiler_params=pltpu.CompilerParams(has_side_effects=True),
    )(x)


def all_gather(name, x):
    return _exchange(name, x, False)


def reduce_scatter_exchange(name, x):
    return _exchange(name, x, True)


def run_exchange(name, spec):
    n_in, n_out = len(spec["ins"]), len(spec["outs"])

    def body(*refs):
        parts = refs[:n_in], refs[n_in:n_in + n_out], refs[n_in + n_out:]
        spec["start"](*parts)
        spec["finish"](*parts)

    return _pcall(
        body, name=name, in_specs=[_ANY] * n_in, out_specs=[_ANY] * n_out, out_shape=list(spec["outs"]),
        scratch_shapes=list(spec["sems"]), compiler_params=pltpu.CompilerParams(has_side_effects=True),
    )(*spec["ins"])


def all_gather_2level_spec(x):
    def parts(ins, outs, sems):
        (x_ref,), (out_ref,), (send_sems, recv_sems, local_sem) = ins, outs, sems
        x_, y_, c_ = lax.axis_index("x"), lax.axis_index("y"), lax.axis_index("c")
        sibling = (x_, y_, 1 - c_)
        chips = [(1 - x_, y_), (x_, 1 - y_), (1 - x_, 1 - y_)]

        def slot(px, py, pc):
            return out_ref.at[4 * px + 2 * py + pc]

        def copy(k, block, to, src=None):
            return pltpu.make_async_remote_copy(
                src_ref=slot(*block) if src is None else src, dst_ref=slot(*block),
                send_sem=send_sems.at[k], recv_sem=recv_sems.at[k], device_id=to, device_id_type=pl.DeviceIdType.MESH)

        me = (x_, y_, c_)
        mine = pltpu.make_async_copy(x_ref, slot(*me), local_sem)
        first = [copy(0, me, sibling, src=x_ref)]
        first += [copy(1 + j, me, (*chip, c_), src=x_ref) for j, chip in enumerate(chips)]
        return copy, me, sibling, chips, c_, mine, first

    def start(ins, outs, sems):
        *_, mine, first = parts(ins, outs, sems)
        mine.start()
        for cp in first:
            cp.start()

    def finish(ins, outs, sems):
        copy, me, sibling, chips, c_, mine, first = parts(ins, outs, sems)
        passed = [copy(4 + j, (*chip, c_), sibling) for j, chip in enumerate(chips)]
        for j, chip in enumerate(chips):
            copy(1 + j, (*chip, c_), me).wait_recv()
            passed[j].start()
        copy(0, sibling, me).wait_recv()
        for j, chip in enumerate(chips):
            copy(4 + j, (*chip, 1 - c_), me).wait_recv()
        for cp in first + passed:
            cp.wait_send()
        mine.wait()

    return dict(ins=[x], outs=[jax.ShapeDtypeStruct((N_DEV,) + tuple(x.shape), x.dtype)],
                sems=[pltpu.SemaphoreType.DMA((7,)), pltpu.SemaphoreType.DMA((7,)), pltpu.SemaphoreType.DMA],
                start=start, finish=finish)


def all_gather_2level(name, x):
    return run_exchange(name, all_gather_2level_spec(x))[0]


def sibling_swap(name, x):
    shape = (4,) + tuple(x.shape[1:])

    def body(x_ref, theirs_ref, send_sems, recv_sems):
        x_, y_, c_ = lax.axis_index("x"), lax.axis_index("y"), lax.axis_index("c")
        sends = []
        for j in range(4):
            cp = pltpu.make_async_remote_copy(
                src_ref=x_ref.at[2 * j + 1 - c_], dst_ref=theirs_ref.at[j], send_sem=send_sems.at[j], recv_sem=recv_sems.at[j],
                device_id=(x_, y_, 1 - c_), device_id_type=pl.DeviceIdType.MESH)
            cp.start()
            sends.append(cp)
        for cp in sends:
            cp.wait()

    return _pcall(
        body, name=name, in_specs=[_ANY], out_specs=_ANY,
        out_shape=jax.ShapeDtypeStruct(shape, x.dtype),
        scratch_shapes=[pltpu.SemaphoreType.DMA((4,)), pltpu.SemaphoreType.DMA((4,))],
        compiler_params=pltpu.CompilerParams(has_side_effects=True),
    )(x)


def pair_add(name, stack, theirs, out_dtype):
    shp = theirs.shape
    c = shp[-1]
    r = math.prod(shp[1:-1])
    br = r
    for cand in (1024, 512, 256, 128, 64, 32, 16, 8):
        if r % cand == 0 and cand * c <= 256 * 1024:
            br = cand
            break

    def body(s0_ref, s1_ref, t_ref, o_ref):
        mine = jnp.where(lax.axis_index("c") == 0, s0_ref[...], s1_ref[...])
        o_ref[...] = (mine + t_ref[...]).astype(o_ref.dtype)

    s4 = stack.reshape(4, 2, r, c)
    out = _pcall(
        body, name=name, grid=(4, r // br),
        in_specs=[_bs((None, None, br, c), lambda j, i: (j, 0, i, 0)), _bs((None, None, br, c), lambda j, i: (j, 1, i, 0)),
                  _bs((None, br, c), lambda j, i: (j, i, 0))],
        out_specs=_bs((None, br, c), lambda j, i: (j, i, 0)),
        out_shape=jax.ShapeDtypeStruct((4, r, c), out_dtype), compiler_params=_cparams(),
    )(s4, s4, theirs.reshape(4, r, c))
    return out.reshape(shp)


def chip_exchange_spec(p):
    def parts(ins, outs, sems, sending=False):
        (p_ref,), (out_ref,), (send_sems, recv_sems, local_sem) = ins, outs, sems
        x_, y_, c_ = lax.axis_index("x"), lax.axis_index("y"), lax.axis_index("c")
        my_chip = 2 * x_ + y_
        chips = [(1 - x_, y_), (x_, 1 - y_), (1 - x_, 1 - y_)]
        local = pltpu.make_async_copy(p_ref.at[my_chip], out_ref.at[my_chip], local_sem)
        sends = [pltpu.make_async_remote_copy(
            src_ref=p_ref.at[2 * px + py], dst_ref=out_ref.at[my_chip], send_sem=send_sems.at[k], recv_sem=recv_sems.at[k],
            device_id=(px, py, c_), device_id_type=pl.DeviceIdType.MESH) for k, (px, py) in enumerate(chips)]
        recvs = [] if sending else [pltpu.make_async_remote_copy(
            src_ref=p_ref.at[my_chip], dst_ref=out_ref.at[2 * px + py], send_sem=send_sems.at[k], recv_sem=recv_sems.at[k],
            device_id=(px, py, c_), device_id_type=pl.DeviceIdType.MESH) for k, (px, py) in enumerate(chips)]
        return local, sends, recvs

    def start(ins, outs, sems):
        local, sends, _ = parts(ins, outs, sems, sending=True)
        local.start()
        for cp in sends:
            cp.start()

    def finish(ins, outs, sems):
        local, sends, recvs = parts(ins, outs, sems)
        for cp in recvs:
            cp.wait_recv()
        for cp in sends:
            cp.wait_send()
        local.wait()

    return dict(ins=[p], outs=[jax.ShapeDtypeStruct(p.shape, p.dtype)],
                sems=[pltpu.SemaphoreType.DMA((3,)), pltpu.SemaphoreType.DMA((3,)), pltpu.SemaphoreType.DMA],
                start=start, finish=finish)


def chip_exchange(name, p):
    return run_exchange(name, chip_exchange_spec(p))[0]


def chip_partials(name, stack):
    theirs = sibling_swap(name + "_d2d", stack)
    return pair_add(name + "_add", stack, theirs, BF16)


def reduce_scatter_2level(name, stack):
    return chip_exchange(name + "_ici", chip_partials(name, stack))


def ln_res_f(h, r, g, b):
    x = ALPHA * h + r
    mu = jnp.mean(x, axis=-1, keepdims=True)
    var = jnp.mean(jnp.square(x - mu), axis=-1, keepdims=True)
    return (x - mu) * lax.rsqrt(var + LN_EPS) * g + b


def rms_f(x, g):
    return x * lax.rsqrt(jnp.mean(x * x, axis=-1, keepdims=True) + RMS_EPS) * g


def rope_f(x1, x2, c, s):
    return x1 * c - x2 * s, x1 * s + x2 * c


def swiglu_f(g, u):
    return jax.nn.silu(g) * u


def tshift_f(d, dprev, mu):
    return d + (dprev - d) * mu


def loss_f(y, tgt):
    e = y - tgt
    return e / y.shape[-1], 0.5 * jnp.mean(e * e, axis=-1, keepdims=True)


def adamw(name, w, m, v, gstack):
    shp = w.shape
    c = shp[-1]
    r = math.prod(shp[:-1])
    br = r
    for cand in (512, 256, 128, 64, 32, 16, 8):
        if r % cand == 0 and cand * c <= 128 * 1024:
            br = cand
            break
    k = gstack.shape[0]

    def body(w_ref, m_ref, v_ref, g_ref, go_ref, d_ref, mo_ref, vo_ref):
        g = g_ref[0].astype(F32)
        for j in range(1, k):
            g = g + g_ref[j].astype(F32)
        m_new = ADAM_B1 * m_ref[...] + (1.0 - ADAM_B1) * g
        v_new = ADAM_B2 * v_ref[...] + (1.0 - ADAM_B2) * jnp.square(g)
        m_hat = m_new / (1.0 - ADAM_B1 ** ADAM_STEP)
        v_hat = v_new / (1.0 - ADAM_B2 ** ADAM_STEP)
        go_ref[...] = g
        d_ref[...] = -ADAM_LR * (m_hat / (jnp.sqrt(v_hat) + ADAM_EPS) + ADAM_WD * w_ref[...])
        mo_ref[...] = m_new
        vo_ref[...] = v_new

    spec = _bs((br, c), lambda i: (i, 0))
    outs = _pcall(
        body, name=name, grid=(r // br,),
        in_specs=[spec, spec, spec, _bs((k, br, c), lambda i: (0, i, 0))],
        out_specs=[spec] * 4, out_shape=[jax.ShapeDtypeStruct((r, c), F32)] * 4,
        compiler_params=_cparams(),
    )(w.reshape(r, c), m.reshape(r, c), v.reshape(r, c), gstack.reshape(k, r, c))
    return tuple(o.reshape(shp) for o in outs)


EV_W = (512, 256, 64, 512, 512, 1024, 1024, 16)
EV_IN, EV_PAD = 3920, 4096
OD_IN, OD_PAD = 6592, 6656
TM = 256


def _offsets(widths):
    offs, acc = [], 0
    for w in widths:
        offs.append((acc, acc + w))
        acc += w
    return offs


def _rope_tables(seq, dim):
    inv = 10000.0 ** (-jnp.arange(0, dim, 2, dtype=F32) / dim)
    ang = jnp.arange(seq, dtype=F32)[:, None] * inv[None, :]
    return jnp.cos(ang), jnp.sin(ang)


def _halves(x, nh):
    t, w = x.shape
    x3 = x.reshape(t, nh, w // nh)
    hd = w // nh // 2
    return x3[:, :, :hd].reshape(t, nh * hd), x3[:, :, hd:].reshape(t, nh * hd)


def _unhalves(x1, x2, nh):
    t = x1.shape[0]
    return jnp.concatenate([x1.reshape(t, nh, -1), x2.reshape(t, nh, -1)], axis=2).reshape(t, -1)


def ln_res2_f(h, r, g, b):
    y = ln_res_f(h, r, g, b)
    return y, y


def _ln(name, h, r, g, b, cts=None):
    if cts is None:
        return _rows(name, ln_res2_f, [h, r], [g, b], [h.shape[1]] * 2, TM, dtypes=[F32, BF16])
    return _rows(name + "_bwd", ln_res_f, [h, r], [g, b], None, TM, fwd=False, cts=[cts], wrt_rows=(0, 1), wrt_params=(0, 1),
                 dtypes=[F32, BF16])


def _gather_specs(late_local, keys):
    return [all_gather_2level_spec(late_local[k]) for k in keys]


def _put_gathered(p, late_local, keys, gathered):
    for k, g in zip(keys, gathered):
        p[k[0]][k[1]] = _unshard(g, late_local[k].shape, BIG[k[0]])


def _tail_fwd(l, h, hb, mem2b, bsz, p, late_local=None):
    qx = matmul(f"xa_q{l}", hb, p["xa_w_q"][l])
    kx = matmul(f"xa_k{l}", mem2b, p["xa_w_k"][l])
    vx = matmul(f"xa_v{l}", mem2b, p["xa_w_v"][l])
    ox = xattn(bsz, qx, kx, vx)
    xa = matmul(f"xa_o{l}", ox, p["xa_w_o"][l])
    h2, h2b = _ln(f"ln_xa{l}", h, xa, p["ln_xa_g"][l:l + 1], p["ln_xa_b"][l:l + 1])
    if late_local is None:
        gg = matmul(f"ffn_g{l}", h2b, p["ffn_w_gate"][l])
        uu = matmul(f"ffn_u{l}", h2b, p["ffn_w_up"][l])
    else:
        k_down, k_in = [("ffn_w_down", l)], [("od_w_in", 0)]
        gg, got = matmul(f"ffn_g{l}", h2b, p["ffn_w_gate"][l], comm=_gather_specs(late_local, k_down))
        _put_gathered(p, late_local, k_down, got)
        uu, got = matmul(f"ffn_u{l}", h2b, p["ffn_w_up"][l], comm=_gather_specs(late_local, k_in))
        _put_gathered(p, late_local, k_in, got)
    act = _rows(f"swiglu{l}", swiglu_f, [gg, uu], [], [gg.shape[1]], TM, dtypes=[BF16])[0]
    if late_local is None:
        ff = matmul(f"ffn_d{l}", act, p["ffn_w_down"][l])
    else:
        keys = [("od_w_out", 0)]
        ff, got = matmul(f"ffn_d{l}", act, p["ffn_w_down"][l], comm=_gather_specs(late_local, keys))
        _put_gathered(p, late_local, keys, got)
    h3, h3b = _ln(f"ln_ffn{l}", h2, ff, p["ln_ffn_g"][l:l + 1], p["ln_ffn_b"][l:l + 1])
    return h3, h3b, (h, hb, qx, kx, vx, ox, xa, h2, h2b, gg, uu, act, ff)


def _tail_bwd(l, dh3, saved, mem2b, bsz, p, gr):
    h, hb, qx, kx, vx, ox, xa, h2, h2b, gg, uu, act, ff = saved
    dh2, dff, gr["ln_ffn_g"][l], gr["ln_ffn_b"][l] = _ln(f"ln_ffn{l}", h2, ff, p["ln_ffn_g"][l:l + 1], p["ln_ffn_b"][l:l + 1], cts=dh3)
    dact = matmul(f"ffn_d_dx{l}", dff, p["ffn_w_down"][l], tb=True)
    gr["ffn_w_down"][l] = matmul(f"ffn_d_dw{l}", act, dff, ta=True)
    dgg, duu = _rows(f"swiglu_bwd{l}", swiglu_f, [gg, uu], [], None, TM, fwd=False, cts=[dact], wrt_rows=(0, 1),
                     dtypes=[BF16, BF16])
    gr["ffn_w_gate"][l] = matmul(f"ffn_g_dw{l}", h2b, dgg, ta=True)
    gr["ffn_w_up"][l] = matmul(f"ffn_u_dw{l}", h2b, duu, ta=True)
    dh2 = matmul(f"ffn_g_dx{l}", dgg, p["ffn_w_gate"][l], tb=True, add=dh2)
    dh2 = matmul(f"ffn_u_dx{l}", duu, p["ffn_w_up"][l], tb=True, add=dh2)
    dh, dxa, gr["ln_xa_g"][l], gr["ln_xa_b"][l] = _ln(f"ln_xa{l}", h, xa, p["ln_xa_g"][l:l + 1], p["ln_xa_b"][l:l + 1], cts=dh2)
    dox = matmul(f"xa_o_dx{l}", dxa, p["xa_w_o"][l], tb=True)
    gr["xa_w_o"][l] = matmul(f"xa_o_dw{l}", ox, dxa, ta=True)
    dqx, dkx, dvx = xattn(bsz, qx, kx, vx, cts=dox)
    gr["xa_w_q"][l] = matmul(f"xa_q_dw{l}", hb, dqx, ta=True)
    gr["xa_w_k"][l] = matmul(f"xa_k_dw{l}", mem2b, dkx, ta=True)
    gr["xa_w_v"][l] = matmul(f"xa_v_dw{l}", mem2b, dvx, ta=True)
    return matmul(f"xa_q_dx{l}", dqx, p["xa_w_q"][l], tb=True, add=dh)


def _uq_perm(w):
    w3 = w.reshape(w.shape[0], 8, 192)
    return jnp.concatenate([w3[:, :, :128].reshape(-1, 1024), w3[:, :, 128:160].reshape(-1, 256),
                            w3[:, :, 160:].reshape(-1, 256)], axis=1)


def _uq_unperm(g):
    r = g.shape[0]
    return jnp.concatenate([g[:, :1024].reshape(r, 8, 128), g[:, 1024:1280].reshape(r, 8, 32),
                            g[:, 1280:].reshape(r, 8, 32)], axis=2).reshape(r, 1536)


def _ukv_perm(w):
    w3 = w.reshape(w.shape[0], 8, 256)
    return jnp.concatenate([w3[:, :, :128].reshape(-1, 1024), w3[:, :, 128:].reshape(-1, 1024)], axis=1)


def _ukv_unperm(g):
    r = g.shape[0]
    return jnp.concatenate([g[:, :1024].reshape(r, 8, 128), g[:, 1024:].reshape(r, 8, 128)], axis=2).reshape(r, 2048)


def _pad_cols(w, n):
    return jnp.pad(w, ((0, 0), (0, n - w.shape[1])))


def _shift_prev(x, bsz):
    t, w = x.shape
    x3 = x.reshape(bsz, t // bsz, w)
    return jnp.pad(x3, ((0, 0), (1, 0), (0, 0)))[:, :-1].reshape(t, w)


def _shift_next(x, bsz):
    t, w = x.shape
    x3 = x.reshape(bsz, t // bsz, w)
    return jnp.pad(x3[:, 1:], ((0, 0), (0, 1), (0, 0))).reshape(t, w)


LATE = ("xa_w_q", "xa_w_k", "xa_w_v", "xa_w_o", "ffn_w_gate", "ffn_w_up", "ffn_w_down")


def _late_partials(layer, gr):
    return [chip_partials(f"rs{layer}_{k}", _shard_stack(gr[k][layer], BIG[k])) for k in LATE]


def device_step(x, mem, tgt, p, late_local):
    bsz, seq, d = x.shape
    t = bsz * seq
    x2, mem2, tgt2 = x.reshape(t, d), mem.reshape(bsz * mem.shape[1], d), tgt.reshape(t, d)
    x2b, mem2b = x2.astype(BF16), mem2.astype(BF16)
    gr = {k: [None] * DEPTH for k in ("ln_mix_g", "ln_mix_b", "xa_w_q", "xa_w_k", "xa_w_v", "xa_w_o", "ln_xa_g", "ln_xa_b",
                                      "ffn_w_gate", "ffn_w_up", "ffn_w_down", "ln_ffn_g", "ln_ffn_b")}
    cos_pe, sin_pe = _rope_tables(seq, 64)
    cos_c, sin_c = _rope_tables(seq, 128)
    cq, sq = jnp.tile(cos_pe, (bsz, 8)), jnp.tile(sin_pe, (bsz, 8))
    ck, sk = jnp.tile(cos_pe, (bsz, 1)), jnp.tile(sin_pe, (bsz, 1))
    cd, sd = jnp.tile(cos_c, (bsz, 8)), jnp.tile(sin_c, (bsz, 8))

    w_in0 = _pad_cols(p["ev_w_in"][0], EV_PAD)
    keys = [(k, 0) for k in ("ev_mla_w_uq", "ev_mla_w_ukv", "ev_w_out")]
    z0, got = matmul("ev_in", x2b, w_in0, comm=_gather_specs(late_local, keys))
    _put_gathered(p, late_local, keys, got)
    w_uq, w_ukv = _uq_perm(p["ev_mla_w_uq"][0]), _ukv_perm(p["ev_mla_w_ukv"][0])
    c_q, c_kv, k_pe, q_g, k_g, v_g, r_g, lr_g = (z0[:, a:b] for a, b in _offsets(EV_W))
    qr = _rows("q_rms", rms_f, [c_q], [p["ev_mla_q_norm"]], [512], TM, dtypes=[BF16])[0]
    kvr = _rows("kv_rms", rms_f, [c_kv], [p["ev_mla_kv_norm"]], [256], TM, dtypes=[BF16])[0]
    q = matmul("mla_uq", qr, w_uq)
    kv = matmul("mla_ukv", kvr, w_ukv)
    qn, qp1, qp2 = q[:, :1024], q[:, 1024:1280], q[:, 1280:]
    kn, vv = kv[:, :1024], kv[:, 1024:]
    kp1, kp2 = k_pe[:, :32], k_pe[:, 32:]
    r1, r2 = _rows("rope_q", rope_f, [qp1, qp2, cq, sq], [], [256, 256], TM)
    kr1, kr2 = _rows("rope_k", rope_f, [kp1, kp2, ck, sk], [], [32, 32], TM)
    r1h, r2h = _to_heads(r1, bsz, 8), _to_heads(r2, bsz, 8)
    keys = [(k, 0) for k in ("xa_w_q", "xa_w_k", "xa_w_v", "xa_w_o")]
    a_out, got = mla_attn(bsz, qn, r1h, r2h, kn, kr1, kr2, vv, comm=_gather_specs(late_local, keys))
    _put_gathered(p, late_local, keys, got)
    gla_prm = (p["ev_gla_w_gate2"][0], p["ev_gla_b_gate"], p["ev_gla_norm_g"], p["ev_gla_norm_b"])
    keys = [("ffn_w_gate", 0), ("ffn_w_up", 0)]
    b_out, gla_sv, got = gla_block(bsz, q_g, k_g, v_g, r_g, lr_g, *gla_prm, comm=_gather_specs(late_local, keys))
    _put_gathered(p, late_local, keys, got)
    mixin0 = jnp.concatenate([a_out, b_out], axis=1)
    mix0 = matmul("ev_out", mixin0, p["ev_w_out"][0])
    h1, h1b = _ln("ln_mix0", x2, mix0, p["ln_mix_g"][0:1], p["ln_mix_b"][0:1])
    h3, h3b, tail0 = _tail_fwd(0, h1, h1b, mem2b, bsz, p, late_local)

    w_in1 = _pad_cols(p["od_w_in"][0], OD_PAD)
    z1 = matmul("od_in", h3b, w_in1)
    dq_, dk_, dv_ = z1[:, :1024], z1[:, 1024:2048], z1[:, 2048:3072]
    d_in = z1[:, 3072:OD_IN]
    q1, q2 = _halves(dq_, 8)
    k1, k2 = _halves(dk_, 8)
    qd1, qd2 = _rows("rope_dq", rope_f, [q1, q2, cd, sd], [], [512, 512], TM)
    kd1, kd2 = _rows("rope_dk", rope_f, [k1, k2, cd, sd], [], [512, 512], TM)
    c_out = dil_block(bsz, qd1, qd2, kd1, kd2, dv_)
    d_prev = _shift_prev(d_in, bsz)
    mu = p["od_rwkv_mu"]
    ds = _rows("tshift", tshift_f, [d_in, d_prev], [mu], [d_in.shape[1]], TM)[0]
    rw_in = tuple(ds[:, a:b] for a, b in _offsets((1024, 1024, 1024, 96, 96, 256)))
    rw_prm = dict(w0=p["od_rwkv_w0"], wd2=p["od_rwkv_w_decay2"][0], a0=p["od_rwkv_a0"], wa2=p["od_rwkv_w_a2"][0],
                  wg2=p["od_rwkv_w_gate2"][0], k_k=p["od_rwkv_k_k"], k_a=p["od_rwkv_k_a"], r_k=p["od_rwkv_r_k"][0],
                  gn_g=p["od_rwkv_gn_g"], gn_b=p["od_rwkv_gn_b"])
    keys = [(k, 1) for k in LATE]
    d_out, rw_scan, got = rwkv_block(bsz, *rw_in, rw_prm, comm=_gather_specs(late_local, keys))
    _put_gathered(p, late_local, keys, got)
    mixin1 = jnp.concatenate([c_out, d_out], axis=1)
    mix1 = matmul("od_out", mixin1, p["od_w_out"][0])
    h4, h4b = _ln("ln_mix1", h3, mix1, p["ln_mix_g"][1:2], p["ln_mix_b"][1:2])
    y, _, tail1 = _tail_fwd(1, h4, h4b, mem2b, bsz, p)

    dy, row_loss = _rows("loss", loss_f, [y, tgt2], [], [d, 1], TM)
    loss = jnp.sum(row_loss)

    dh4 = _tail_bwd(1, dy, tail1, mem2b, bsz, p, gr)
    dh3, dmix1, gr["ln_mix_g"][1], gr["ln_mix_b"][1] = _ln("ln_mix1", h3, mix1, p["ln_mix_g"][1:2], p["ln_mix_b"][1:2], cts=dh4)
    dmixin1 = matmul("od_out_dx", dmix1, p["od_w_out"][0], tb=True)
    gr["od_w_out"] = matmul("od_out_dw", mixin1, dmix1, ta=True)[None]
    dc_out, dd_out = dmixin1[:, :1024], dmixin1[:, 1024:]
    drw_in, drw_prm, done1 = rwkv_block(bsz, *rw_in, rw_prm, cts=dd_out, scan=rw_scan,
                                        comm=[chip_exchange_spec(x_) for x_ in _late_partials(1, gr)])
    dds = jnp.concatenate(drw_in, axis=1)
    dd_in, dd_prev, dmu = _rows("tshift_bwd", tshift_f, [d_in, d_prev], [mu], None, TM, fwd=False, cts=[dds],
                                wrt_rows=(0, 1), wrt_params=(0,))
    dd_in = dd_in + _shift_next(dd_prev, bsz)
    dqd1, dqd2, dkd1, dkd2, ddv = dil_block(bsz, qd1, qd2, kd1, kd2, dv_, cts=dc_out)
    dq1, dq2 = _rows("rope_dq_bwd", rope_f, [q1, q2, cd, sd], [], None, TM, fwd=False, cts=[dqd1, dqd2], wrt_rows=(0, 1))
    dk1, dk2 = _rows("rope_dk_bwd", rope_f, [k1, k2, cd, sd], [], None, TM, fwd=False, cts=[dkd1, dkd2], wrt_rows=(0, 1))
    dz1 = jnp.concatenate([_unhalves(dq1, dq2, 8), _unhalves(dk1, dk2, 8), ddv, dd_in,
                           jnp.zeros((t, OD_PAD - OD_IN), F32)], axis=1).astype(BF16)
    gr["od_w_in"] = matmul("od_in_dw", h3b, dz1, ta=True)[:, :OD_IN][None]
    dh3 = matmul("od_in_dx", dz1, w_in1, tb=True, add=dh3)
    gr["od_rwkv_mu"] = dmu
    gr["od_rwkv_w0"], gr["od_rwkv_w_decay2"], gr["od_rwkv_a0"] = drw_prm["w0"], drw_prm["wd2"][None], drw_prm["a0"]
    gr["od_rwkv_w_a2"], gr["od_rwkv_w_gate2"] = drw_prm["wa2"][None], drw_prm["wg2"][None]
    gr["od_rwkv_k_k"], gr["od_rwkv_k_a"], gr["od_rwkv_r_k"] = drw_prm["k_k"], drw_prm["k_a"], drw_prm["r_k"][None]
    gr["od_rwkv_gn_g"], gr["od_rwkv_gn_b"] = drw_prm["gn_g"], drw_prm["gn_b"]

    dh1 = _tail_bwd(0, dh3, tail0, mem2b, bsz, p, gr)
    dx2, dmix0, gr["ln_mix_g"][0], gr["ln_mix_b"][0] = _ln("ln_mix0", x2, mix0, p["ln_mix_g"][0:1], p["ln_mix_b"][0:1], cts=dh1)
    dmixin0 = matmul("ev_out_dx", dmix0, p["ev_w_out"][0], tb=True)
    gr["ev_w_out"] = matmul("ev_out_dw", mixin0, dmix0, ta=True)[None]
    da_out, db_out = dmixin0[:, :1024], dmixin0[:, 1024:]
    od_parts = [chip_partials("rs_" + k, _shard_stack(gr[k], BIG[k])) for k in ("od_w_in", "od_w_out", "ev_w_out")]
    dq_g, dk_g, dv_g, dlr4, dr_g, dw2, dbg, dng, dnb, done_in, done_out, done_ev_out = gla_block(
        bsz, q_g, k_g, v_g, r_g, lr_g, *gla_prm, cts=(db_out, gla_sv), comm=[chip_exchange_spec(x_) for x_ in od_parts])
    dlr_g = jnp.sum(dlr4, axis=0)
    dqn, dr1h, dr2h, dkn, dkr1, dkr2, dvv, *done0 = mla_attn(bsz, qn, r1h, r2h, kn, kr1, kr2, vv, cts=da_out, tq=512,
                                                             comm=[chip_exchange_spec(x_) for x_ in _late_partials(0, gr)])
    dqp1, dqp2 = _rows("rope_q_bwd", rope_f, [qp1, qp2, cq, sq], [], None, TM, fwd=False,
                       cts=[_from_heads(dr1h), _from_heads(dr2h)], wrt_rows=(0, 1))
    dkp1, dkp2 = _rows("rope_k_bwd", rope_f, [kp1, kp2, ck, sk], [], None, TM, fwd=False, cts=[dkr1, dkr2], wrt_rows=(0, 1))
    dq = jnp.concatenate([dqn, dqp1, dqp2], axis=1).astype(BF16)
    dkv = jnp.concatenate([dkn, dvv], axis=1).astype(BF16)
    dqr = matmul("mla_uq_dx", dq, w_uq, tb=True)
    gr["ev_mla_w_uq"] = _uq_unperm(matmul("mla_uq_dw", qr, dq, ta=True))[None]
    dkvr = matmul("mla_ukv_dx", dkv, w_ukv, tb=True)
    gr["ev_mla_w_ukv"] = _ukv_unperm(matmul("mla_ukv_dw", kvr, dkv, ta=True))[None]
    dc_q, gr["ev_mla_q_norm"] = _rows("q_rms_bwd", rms_f, [c_q], [p["ev_mla_q_norm"]], None, TM, fwd=False, cts=[dqr],
                                      wrt_rows=(0,), wrt_params=(0,))
    dc_kv, gr["ev_mla_kv_norm"] = _rows("kv_rms_bwd", rms_f, [c_kv], [p["ev_mla_kv_norm"]], None, TM, fwd=False, cts=[dkvr],
                                        wrt_rows=(0,), wrt_params=(0,))
    dz0 = jnp.concatenate([dc_q, dc_kv, dkp1, dkp2, dq_g, dk_g, dv_g, dr_g, dlr_g,
                           jnp.zeros((t, EV_PAD - EV_IN), F32)], axis=1).astype(BF16)
    gr["ev_w_in"] = matmul("ev_in_dw", x2b, dz0, ta=True)[:, :EV_IN][None]
    dx2 = matmul("ev_in_dx", dz0, w_in0, tb=True, add=dx2)
    gr["ev_gla_w_gate2"], gr["ev_gla_b_gate"] = dw2[None], dbg
    gr["ev_gla_norm_g"], gr["ev_gla_norm_b"] = dng, dnb
    stacks = {k: jnp.stack([s0, s1], axis=1) for k, s0, s1 in zip(LATE, done0, done1)}
    stacks.update(od_w_in=done_in, od_w_out=done_out, ev_w_out=done_ev_out)
    for k in stacks:
        del gr[k]
    for k in list(gr):
        if isinstance(gr[k], list):
            gr[k] = jnp.stack([g[0] if k.startswith("ln_") else g for g in gr[k]])
    return loss, dx2.reshape(bsz, seq, d), gr, stacks


WEIGHTS = ['ev_w_in', 'ev_mla_q_norm', 'ev_mla_w_uq', 'ev_mla_kv_norm', 'ev_mla_w_ukv', 'ev_gla_w_gate2', 'ev_gla_b_gate',
           'ev_gla_norm_g', 'ev_gla_norm_b', 'ev_w_out', 'od_w_in', 'od_rwkv_mu', 'od_rwkv_w0', 'od_rwkv_w_decay2',
           'od_rwkv_a0', 'od_rwkv_w_a2', 'od_rwkv_w_gate2', 'od_rwkv_k_k', 'od_rwkv_k_a', 'od_rwkv_r_k', 'od_rwkv_gn_g',
           'od_rwkv_gn_b', 'od_w_out', 'ln_mix_g', 'ln_mix_b', 'xa_w_q', 'xa_w_k', 'xa_w_v', 'xa_w_o', 'ln_xa_g', 'ln_xa_b',
           'ffn_w_gate', 'ffn_w_up', 'ffn_w_down', 'ln_ffn_g', 'ln_ffn_b']
BIG = {'ev_w_in': -1, 'ev_mla_w_uq': -1, 'ev_mla_w_ukv': -1, 'ev_w_out': -2, 'od_w_in': -1, 'od_w_out': -2,
       'xa_w_q': -2, 'xa_w_k': -2, 'xa_w_v': -2, 'xa_w_o': -2, 'ffn_w_gate': -1, 'ffn_w_up': -1, 'ffn_w_down': -2}
SMALL = ['ev_gla_w_gate2', 'od_rwkv_mu', 'od_rwkv_w0', 'od_rwkv_w_decay2', 'od_rwkv_a0', 'od_rwkv_w_a2', 'od_rwkv_w_gate2',
         'od_rwkv_k_k', 'od_rwkv_k_a', 'od_rwkv_gn_g', 'od_rwkv_gn_b']
REPL = ['ev_mla_q_norm', 'ev_mla_kv_norm', 'ev_gla_b_gate', 'ev_gla_norm_g', 'ev_gla_norm_b', 'od_rwkv_r_k',
        'ln_mix_g', 'ln_mix_b', 'ln_xa_g', 'ln_xa_b', 'ln_ffn_g', 'ln_ffn_b']
PACK_COLS = 128


def _unshard(g, shape, axis):
    axis %= len(shape)
    full = list(shape)
    full[axis] *= N_DEV
    return jnp.moveaxis(g, 0, axis).reshape(full)


def _shard_stack(gfull, axis):
    axis %= gfull.ndim
    shp = list(gfull.shape)
    shp[axis:axis + 1] = [N_DEV, shp[axis] // N_DEV]
    return jnp.moveaxis(gfull.reshape(shp), axis, 0)


def _pack(arrs):
    lead = arrs[0].shape[0]
    flat = jnp.concatenate([a.reshape(lead, -1) for a in arrs], axis=1)
    n = flat.shape[1]
    rows = -(-n // (8 * PACK_COLS)) * 8
    return jnp.pad(flat, ((0, 0), (0, rows * PACK_COLS - n))).reshape(lead, rows, PACK_COLS)


def _unpack(buf, shapes):
    lead = buf.shape[0]
    flat = buf.reshape(lead, -1)
    out, off = [], 0
    for shp in shapes:
        n = math.prod(shp)
        out.append(flat[:, off:off + n].reshape((lead,) + tuple(shp)))
        off += n
    return out


def train_step(x, mem, loss_target, w, m, v):
    p, late_local = {}, {}
    for k, ax in BIG.items():
        if k != "ev_w_in":
            p[k] = [None] * w[k].shape[0]
            for l in range(w[k].shape[0]):
                late_local[(k, l)] = w[k][l].astype(BF16)
        else:
            p[k] = _unshard(all_gather_2level("ag_" + k, w[k].astype(BF16)), w[k].shape, ax)
    small_loc = _pack([w[k][None] for k in SMALL])[0]
    small_all = _unpack(all_gather("ag_small", small_loc), [w[k].shape for k in SMALL])
    for k, g in zip(SMALL, small_all):
        p[k] = _unshard(g, w[k].shape, -1)
    for k in REPL:
        p[k] = w[k]
    loss, dx, gr, stacks = device_step(x, mem, loss_target, p, late_local)
    loss = lax.psum(loss, ("x", "y", "c"))

    for k, ax in BIG.items():
        if k not in stacks:
            stacks[k] = reduce_scatter_2level("rs_" + k, _shard_stack(gr[k], ax))
    small_send = _pack([_shard_stack(gr[k], -1) for k in SMALL])
    small_recv = _unpack(reduce_scatter_exchange("rs_small", small_send), [w[k].shape for k in SMALL])
    stacks.update(zip(SMALL, small_recv))
    repl_loc = _pack([gr[k][None] for k in REPL])[0]
    repl_all = _unpack(all_gather("ag_repl_grads", repl_loc), [w[k].shape for k in REPL])
    stacks.update(zip(REPL, repl_all))

    grads, deltas, new_m, new_v = [], [], [], []
    for k in WEIGHTS:
        g, dl, mn, vn = adamw("adamw_" + k, w[k], m[k], v[k], stacks[k])
        grads.append(g), deltas.append(dl), new_m.append(mn), new_v.append(vn)
    return (loss, dx, *grads, *deltas, *new_m, *new_v)


def kernel(x, mem, ev_w_in, ev_mla_q_norm, ev_mla_w_uq, ev_mla_kv_norm, ev_mla_w_ukv, ev_gla_w_gate2, ev_gla_b_gate, ev_gla_norm_g, ev_gla_norm_b, ev_w_out, od_w_in, od_rwkv_mu, od_rwkv_w0, od_rwkv_w_decay2, od_rwkv_a0, od_rwkv_w_a2, od_rwkv_w_gate2, od_rwkv_k_k, od_rwkv_k_a, od_rwkv_r_k, od_rwkv_gn_g, od_rwkv_gn_b, od_w_out, ln_mix_g, ln_mix_b, xa_w_q, xa_w_k, xa_w_v, xa_w_o, ln_xa_g, ln_xa_b, ffn_w_gate, ffn_w_up, ffn_w_down, ln_ffn_g, ln_ffn_b, loss_target, m_ev_w_in, m_ev_mla_q_norm, m_ev_mla_w_uq, m_ev_mla_kv_norm, m_ev_mla_w_ukv, m_ev_gla_w_gate2, m_ev_gla_b_gate, m_ev_gla_norm_g, m_ev_gla_norm_b, m_ev_w_out, m_od_w_in, m_od_rwkv_mu, m_od_rwkv_w0, m_od_rwkv_w_decay2, m_od_rwkv_a0, m_od_rwkv_w_a2, m_od_rwkv_w_gate2, m_od_rwkv_k_k, m_od_rwkv_k_a, m_od_rwkv_r_k, m_od_rwkv_gn_g, m_od_rwkv_gn_b, m_od_w_out, m_ln_mix_g, m_ln_mix_b, m_xa_w_q, m_xa_w_k, m_xa_w_v, m_xa_w_o, m_ln_xa_g, m_ln_xa_b, m_ffn_w_gate, m_ffn_w_up, m_ffn_w_down, m_ln_ffn_g, m_ln_ffn_b, v_ev_w_in, v_ev_mla_q_norm, v_ev_mla_w_uq, v_ev_mla_kv_norm, v_ev_mla_w_ukv, v_ev_gla_w_gate2, v_ev_gla_b_gate, v_ev_gla_norm_g, v_ev_gla_norm_b, v_ev_w_out, v_od_w_in, v_od_rwkv_mu, v_od_rwkv_w0, v_od_rwkv_w_decay2, v_od_rwkv_a0, v_od_rwkv_w_a2, v_od_rwkv_w_gate2, v_od_rwkv_k_k, v_od_rwkv_k_a, v_od_rwkv_r_k, v_od_rwkv_gn_g, v_od_rwkv_gn_b, v_od_w_out, v_ln_mix_g, v_ln_mix_b, v_xa_w_q, v_xa_w_k, v_xa_w_v, v_xa_w_o, v_ln_xa_g, v_ln_xa_b, v_ffn_w_gate, v_ffn_w_up, v_ffn_w_down, v_ln_ffn_g, v_ln_ffn_b):
    given = dict(locals())
    w = {k: given[k] for k in WEIGHTS}
    m = {k: given["m_" + k] for k in WEIGHTS}
    v = {k: given["v_" + k] for k in WEIGHTS}
    return train_step(given["x"], given["mem"], given["loss_target"], w, m, v)
```

```python
import functools
import math

import jax
import jax.numpy as jnp
from jax import lax
from jax.experimental import pallas as pl
from jax.experimental.pallas import tpu as pltpu

F32 = jnp.float32
BF16 = jnp.bfloat16
VMEM_LIMIT = 56 * 1024 * 1024
ROWS_VMEM = 20 * 1024 * 1024

N_DEV = 8
DEPTH = 2
ALPHA = (2.0 * DEPTH) ** 0.25
LN_EPS = 1e-5
RMS_EPS = 1e-6
RWKV_GN_EPS = 64e-5
ADAM_LR, ADAM_B1, ADAM_B2, ADAM_EPS, ADAM_WD, ADAM_STEP = 0.001, 0.9, 0.999, 1e-08, 0.01, 10
NEG_INF = float("-inf")


def _pcall(body, **kw):
    return pl.pallas_call(body, **kw)


def _cparams(**kw):
    return pltpu.CompilerParams(vmem_limit_bytes=VMEM_LIMIT, **kw)


def _dg(a, b, ca, cb, batch):
    nb = 1 if batch else 0
    dims = (((ca + nb,), (cb + nb,)), ((0,), (0,)) if batch else ((), ()))
    return lax.dot_general(a.astype(BF16), b.astype(BF16), dims, preferred_element_type=F32)


@functools.partial(jax.custom_vjp, nondiff_argnums=(2, 3, 4))
def _mm(a, b, ta, tb, batch):
    return _dg(a, b, 0 if ta else 1, 1 if tb else 0, batch)


def _mm_fwd(a, b, ta, tb, batch):
    return _mm(a, b, ta, tb, batch), (a, b)


def _mm_bwd(ta, tb, batch, res, g):
    a, b = res
    if not ta and not tb:
        da, db = _mm(g, b, False, True, batch), _mm(a, g, True, False, batch)
    elif not ta and tb:
        da, db = _mm(g, b, False, False, batch), _mm(g, a, True, False, batch)
    elif ta and not tb:
        da, db = _mm(b, g, False, True, batch), _mm(a, g, False, False, batch)
    else:
        da, db = _mm(b, g, True, True, batch), _mm(g, a, True, True, batch)
    return da.astype(a.dtype), db.astype(b.dtype)


_mm.defvjp(_mm_fwd, _mm_bwd)


def mm(a, b, ta=False, tb=False):
    return _mm(a, b, ta, tb, a.ndim == 3)


def _bs(block, imap):
    return pl.BlockSpec(block, imap)


def _rev_imap(imap, n):
    def r(*idx):
        return imap(*idx[:-1], n - 1 - idx[-1])
    return r


def _gcall(body, name, grid, in_specs, out_specs, out_shape, scratch_shapes, args, comm=None):
    comm = comm or []
    n_in, n_out, n_scr = len(in_specs), len(out_specs), len(scratch_shapes)
    c_in = [a for c in comm for a in c["ins"]]
    c_out = [o for c in comm for o in c["outs"]]
    c_sem = [s for c in comm for s in c["sems"]]

    def body2(*refs):
        i = 0
        r_in, i = refs[i:i + n_in], i + n_in
        k_in, i = refs[i:i + len(c_in)], i + len(c_in)
        r_out, i = refs[i:i + n_out], i + n_out
        k_out, i = refs[i:i + len(c_out)], i + len(c_out)
        r_scr, k_sem = refs[i:i + n_scr], refs[i + n_scr:]

        def each(which):
            a = b = s = 0
            for c in comm:
                na, nb, ns = len(c["ins"]), len(c["outs"]), len(c["sems"])
                c[which](k_in[a:a + na], k_out[b:b + nb], k_sem[s:s + ns])
                a, b, s = a + na, b + nb, s + ns

        if comm:
            first = last = None
            for ax, n in enumerate(grid):
                f0, l0 = pl.program_id(ax) == 0, pl.program_id(ax) == n - 1
                first = f0 if first is None else jnp.logical_and(first, f0)
                last = l0 if last is None else jnp.logical_and(last, l0)
            pl.when(first)(lambda: each("start"))
        body(*r_in, *r_out, *r_scr)
        if comm:
            pl.when(last)(lambda: each("finish"))

    return _pcall(
        body2, name=name, grid=grid,
        in_specs=list(in_specs) + [_ANY] * len(c_in), out_specs=list(out_specs) + [_ANY] * len(c_out),
        out_shape=list(out_shape) + c_out, scratch_shapes=list(scratch_shapes) + c_sem,
        compiler_params=_cparams(has_side_effects=True) if comm else _cparams(),
    )(*args, *c_in)


def p_fwd(name, f, grid, ins, outs, carry=None, save_carry=None, comm=None):
    n_in, n_out = len(ins), len(outs)

    def body(*refs):
        in_refs = refs[:n_in]
        out_refs = refs[n_in:n_in + n_out]
        rest = refs[n_in + n_out:]
        vals = [r[...] for r in in_refs]
        if carry is None:
            res = f(*vals)
        else:
            if save_carry is not None:
                sv_ref, c_ref = rest
            else:
                (c_ref,) = rest

            @pl.when(pl.program_id(len(grid) - 1) == 0)
            def _():
                c_ref[...] = jnp.zeros(c_ref.shape, c_ref.dtype)

            c = c_ref[...]
            if save_carry is not None:
                sv_ref[...] = c
            res = f(c, *vals)
            c_ref[...] = res[0]
            res = res[1:]
        if not isinstance(res, (tuple, list)):
            res = (res,)
        for r, v in zip(out_refs, res):
            r[...] = v.astype(r.dtype)

    out_shape = [jax.ShapeDtypeStruct(s, d) for (s, d, _, _) in outs]
    out_specs = [_bs(b, m) for (_, _, b, m) in outs]
    scratch = []
    if carry is not None:
        if save_carry is not None:
            out_shape.append(jax.ShapeDtypeStruct(save_carry[0], carry[1]))
            out_specs.append(_bs(save_carry[1], save_carry[2]))
        scratch.append(pltpu.VMEM(carry[0], carry[1]))
    return _gcall(body, name, grid, [_bs(b, m) for (_, b, m) in ins], out_specs, out_shape, scratch,
                  [a for (a, _, _) in ins], comm)


def p_bwd(name, f, grid, ins, cts, wrt, carry=None, saved=None, comm=None):
    n_in, n_ct, n_w = len(ins), len(cts), len(wrt)
    rev = carry is not None
    n_last = grid[-1]

    def fix(imap):
        return _rev_imap(imap, n_last) if rev else imap

    def body(*refs):
        in_refs = refs[:n_in]
        ct_refs = refs[n_in:n_in + n_ct]
        k = n_in + n_ct
        if rev:
            sv_ref = refs[k]
            k += 1
        out_refs = refs[k:k + n_w]
        rest = refs[k + n_w:]
        vals = [r[...] for r in in_refs]
        ct_vals = [r[...].astype(F32) for r in ct_refs]
        widx = [w[0] for w in wrt]

        if rev:
            (dc_ref,) = rest

            @pl.when(pl.program_id(len(grid) - 1) == 0)
            def _():
                dc_ref[...] = jnp.zeros(dc_ref.shape, dc_ref.dtype)

            c_in = sv_ref[...]

            def g(c, *dv):
                full = list(vals)
                for i, d in zip(widx, dv):
                    full[i] = d
                return tuple(f(c, *full))

            _, vjp = jax.vjp(g, c_in, *[vals[i] for i in widx])
            grads = vjp((dc_ref[...],) + tuple(ct_vals))
            dc_ref[...] = grads[0]
            grads = grads[1:]
        else:
            def g(*dv):
                full = list(vals)
                for i, d in zip(widx, dv):
                    full[i] = d
                r = f(*full)
                return tuple(r) if isinstance(r, (tuple, list)) else (r,)

            _, vjp = jax.vjp(g, *[vals[i] for i in widx])
            grads = vjp(tuple(ct_vals))

        for w, o_ref, gr in zip(wrt, out_refs, grads):
            acc = w[1]
            if acc is None:
                o_ref[...] = gr.astype(o_ref.dtype)
            else:
                first = None
                for ax in acc:
                    c0 = pl.program_id(ax) == 0
                    first = c0 if first is None else jnp.logical_and(first, c0)

                @pl.when(first)
                def _():
                    o_ref[...] = jnp.zeros(o_ref.shape, o_ref.dtype)

                o_ref[...] += gr.astype(o_ref.dtype)

    in_specs = [_bs(b, fix(m)) for (_, b, m) in ins] + [_bs(b, fix(m)) for (_, b, m) in cts]
    args = [a for (a, _, _) in ins] + [a for (a, _, _) in cts]
    if rev:
        in_specs.append(_bs(saved[1], fix(saved[2])))
        args.append(saved[0])
    out_shape, out_specs = [], []
    for w in wrt:
        a, b, m = ins[w[0]]
        if len(w) > 2 and w[2] is not None:
            m = w[2]
        out_shape.append(jax.ShapeDtypeStruct(a.shape, w[3] if len(w) > 3 else F32))
        out_specs.append(_bs(b, fix(m)))
    scratch = [pltpu.VMEM(carry[0], carry[1])] if rev else []
    return _gcall(body, name, grid, in_specs, out_specs, out_shape, scratch, args, comm)


def _rows(name, f, row_ins, params, out_widths, tm, fwd=True, cts=None, wrt_rows=(), wrt_params=(), dtypes=None):
    t = row_ins[0].shape[0]
    width = sum(a.shape[1] for a in row_ins)
    width += sum(out_widths) if fwd else sum(c.shape[1] for c in cts) + sum(row_ins[i].shape[1] for i in wrt_rows)
    tm = min(tm, t)
    while tm > 8 and 2 * 4 * tm * width > ROWS_VMEM:
        tm //= 2
    rmap = lambda i: (i, 0)
    pmap = lambda i: (0, 0)
    ins = [(a, (tm, a.shape[1]), rmap) for a in row_ins] + [(p, p.shape, pmap) for p in params]
    if fwd:
        dtypes = dtypes or [F32] * len(out_widths)
        outs = [((t, w), dt, (tm, w), rmap) for w, dt in zip(out_widths, dtypes)]
        return p_fwd(name, f, (t // tm,), ins, outs)
    ct_specs = [(c, (tm, c.shape[1]), rmap) for c in cts]
    dtypes = dtypes or [F32] * len(wrt_rows)
    wrt = [(i, None, None, dt) for i, dt in zip(wrt_rows, dtypes)] + [(len(row_ins) + i, (0,)) for i in wrt_params]
    return p_bwd(name, f, (t // tm,), ins, ct_specs, wrt)


def _pick(n, cands):
    for c in cands:
        if n % c == 0:
            return c
    return n


def _wide(n, cap=1664):
    best = None
    for w in range(128, min(n, cap) + 1, 128):
        if n % w == 0:
            best = w
    return best or n


MM_VMEM = 40 * 1024 * 1024


def _mm_tiles(m, n, k, sa, sb, so, has_add):
    bm, bn = _pick(m, (512, 256, 128)), _wide(n)
    bk = k if k <= 2048 else _wide(k, 3328)

    def vmem(bm, bn, bk):
        acc = 0 if bk == k else 4 * bm * bn
        return 2 * (bm * bk * sa + bk * bn * sb + bm * bn * so * (2 if has_add else 1)) + acc

    while vmem(bm, bn, bk) > MM_VMEM and bk % 256 == 0:
        bk //= 2
    while vmem(bm, bn, bk) > MM_VMEM and bn % 256 == 0:
        bn //= 2
    return bm, bn, bk


def matmul(name, a, b, ta=False, tb=False, out_dtype=F32, add=None, comm=None):
    m = a.shape[1] if ta else a.shape[0]
    k = a.shape[0] if ta else a.shape[1]
    n = b.shape[0] if tb else b.shape[1]
    assert (b.shape[1] if tb else b.shape[0]) == k, (a.shape, b.shape, ta, tb)
    bm, bn, bk = _mm_tiles(m, n, k, a.dtype.itemsize, b.dtype.itemsize, jnp.dtype(out_dtype).itemsize, add is not None)
    nk = k // bk

    def body(a_ref, b_ref, *rest):
        o_ref, acc_ref = rest[-2:]
        if nk == 1:
            r = mm(a_ref[...], b_ref[...], ta, tb)
            o_ref[...] = (r if add is None else r + rest[0][...].astype(F32)).astype(o_ref.dtype)
            return

        @pl.when(pl.program_id(2) == 0)
        def _():
            acc_ref[...] = jnp.zeros(acc_ref.shape, F32) if add is None else rest[0][...].astype(F32)

        acc_ref[...] += mm(a_ref[...], b_ref[...], ta, tb)

        @pl.when(pl.program_id(2) == nk - 1)
        def _():
            o_ref[...] = acc_ref[...].astype(o_ref.dtype)

    a_spec = _bs((bk, bm), lambda i, j, l: (l, i)) if ta else _bs((bm, bk), lambda i, j, l: (i, l))
    b_spec = _bs((bn, bk), lambda i, j, l: (j, l)) if tb else _bs((bk, bn), lambda i, j, l: (l, j))
    o_spec = _bs((bm, bn), lambda i, j, l: (i, j))
    if comm:
        out, *comm_out = _gcall(
            body, name, (m // bm, n // bn, nk), [a_spec, b_spec] + ([] if add is None else [o_spec]), [o_spec],
            [jax.ShapeDtypeStruct((m, n), out_dtype)], [pltpu.VMEM((bm, bn) if nk > 1 else (8, 128), F32)],
            (a, b) if add is None else (a, b, add), comm)
        return out, comm_out
    return _pcall(
        body, name=name, grid=(m // bm, n // bn, nk),
        in_specs=[a_spec, b_spec] + ([] if add is None else [o_spec]), out_specs=o_spec,
        out_shape=jax.ShapeDtypeStruct((m, n), out_dtype),
        scratch_shapes=[pltpu.VMEM((bm, bn) if nk > 1 else (8, 128), F32)],
        compiler_params=_cparams(dimension_semantics=("parallel", "parallel", "arbitrary")),
    )(*((a, b) if add is None else (a, b, add)))


def _to_heads(x, b, h):
    t, w = x.shape
    return x.reshape(b, t // b, h, w // h).transpose(0, 2, 1, 3)


def _from_heads(x):
    b, h, s, d = x.shape
    return x.transpose(0, 2, 1, 3).reshape(b * s, h * d)


RW_STEPS = 16
RW_G = 16


def rwkv_group(st, w8, kk8, ka8, kh8, r8, vc):
    ys = []
    for t in range(RW_STEPS):
        row = lambda x: x[:, t:t + 1, :]
        sa = jnp.sum(st * row(kk8), axis=2, keepdims=True)
        st = st * row(w8) - sa * row(ka8) + vc[:, :, t:t + 1] * row(kh8)
        ys.append(jnp.sum(st * row(r8), axis=2, keepdims=True))
    return st, jnp.concatenate(ys, axis=2)


def rwkv_prehead_f(kkraw, a):
    nrm = jnp.sqrt(jnp.sum(kkraw * kkraw, axis=-1, keepdims=True))
    kk = kkraw / jnp.maximum(nrm, 1e-12)
    return kk, kk * a


def rwkv_pre_f(kd, w_lr, a_lr, g_lr, w0, wd2, a0, wa2, wg2, k_k, k_a):
    wpre = w0 + mm(jnp.tanh(w_lr), wd2)
    w = -jax.nn.softplus(-wpre) - 0.5
    decay = jnp.exp(-jnp.exp(w))
    a = jax.nn.sigmoid(a0 + mm(a_lr, wa2))
    g = mm(jax.nn.sigmoid(g_lr), wg2)
    kkraw = kd * k_k
    kh = kd * (1.0 + (a - 1.0) * k_a)
    return decay, a, g, kkraw, kh


def rwkv_post_f(y, r, kh, v, g, gn_g, gn_b, r_k):
    mu = jnp.mean(y, axis=-1, keepdims=True)
    var = jnp.mean(jnp.square(y - mu), axis=-1, keepdims=True)
    yn = (y - mu) * lax.rsqrt(var + RWKV_GN_EPS) * gn_g + gn_b
    bonus = jnp.sum(r * kh * r_k, axis=-1, keepdims=True) * v
    return (yn + bonus) * g


def _to_cols(xh):
    b, h, s, d = xh.shape
    return xh.reshape(b, h, s // RW_STEPS, RW_STEPS, d).transpose(0, 1, 2, 4, 3)


def _from_cols(xc):
    b, h, n, d, k = xc.shape
    return xc.transpose(0, 1, 2, 4, 3).reshape(b, h, n * k, d)


def rwkv_block(bsz, r, kd, vd, w_lr, a_lr, g_lr, prm, cts=None, scan=None, comm=None):
    t = r.shape[0]
    hh, n = 16, 64
    tm = 256
    pre_rows = [kd, w_lr, a_lr, g_lr]
    pre_prm = [prm[k] for k in ("w0", "wd2", "a0", "wa2", "wg2", "k_k", "k_a")]
    decay, a, g, kkraw, kh = _rows("rwkv_pre", rwkv_pre_f, pre_rows, pre_prm, [1024] * 5, tm)
    heads = lambda x: _to_heads(x, bsz, hh)
    rh, khh, vh, gh, dech, kkrawh, ah = (heads(x) for x in (r, kh, vd, g, decay, kkraw, a))
    s = t // bsz
    ng = s // RW_STEPS
    ts = 512
    hb = (None, None, ts, n)
    hm = lambda h, b, i: (b, h, i, 0)
    pgrid = (hh, bsz, s // ts)
    ph_ins = [(kkrawh, hb, hm), (ah, hb, hm)]
    kkh, kah = p_fwd("rwkv_prehead", rwkv_prehead_f, pgrid, ph_ins, [((bsz, hh, s, n), F32, hb, hm)] * 2)
    grp = lambda x: x.reshape(bsz, hh, ng, RW_STEPS, n)
    cb, rb, sb = (None, RW_G, None, n, RW_STEPS), (None, RW_G, None, RW_STEPS, n), (None, RW_G, None, n, n)
    cm = lambda b, h, i: (b, h, i, 0, 0)
    sc_ins = [(grp(x), rb, cm) for x in (dech, kkh, kah, khh, rh)] + [(_to_cols(vh), cb, cm)]
    grid = (bsz, hh // RW_G, ng)
    comm_out = []
    if scan is None:
        yc, sv, *comm_out = p_fwd("rwkv_scan", rwkv_group, grid, sc_ins, [((bsz, hh, ng, n, RW_STEPS), F32, cb, cm)],
                                  carry=((RW_G, n, n), F32), save_carry=((bsz, hh, ng, n, n), sb, cm), comm=comm)
    else:
        yc, sv = scan
    yh = _from_cols(yc)
    pb = (None, 1, n)
    pm = lambda h, b, i: (h, 0, 0)
    gn_g, gn_b, r_k = (prm[k].reshape(hh, 1, n) for k in ("gn_g", "gn_b", "r_k"))
    post_ins = [(x, hb, hm) for x in (yh, rh, khh, vh, gh)] + [(p, pb, pm) for p in (gn_g, gn_b, r_k)]
    if cts is None:
        (oh,) = p_fwd("rwkv_post", rwkv_post_f, pgrid, post_ins, [((bsz, hh, s, n), BF16, hb, hm)])
        return _from_heads(oh), (yc, sv), comm_out
    doh = _to_heads(cts, bsz, hh)
    dyh, drh1, dkhh1, dvh1, dgh, dgn_g, dgn_b, dr_k = p_bwd(
        "rwkv_post_bwd", rwkv_post_f, pgrid, post_ins, [(doh, hb, hm)],
        [(i, None) for i in range(5)] + [(5 + i, (1, 2)) for i in range(3)])
    drows_v = p_bwd("rwkv_scan_bwd", rwkv_group, grid, sc_ins, [(_to_cols(dyh), cb, cm)], [(i, None) for i in range(6)],
                    carry=((RW_G, n, n), F32), saved=(sv, sb, cm), comm=comm)
    drows_v, comm_out = drows_v[:6], list(drows_v[6:])
    ddech, dkkh, dkah, dkhh2, drh2 = (x.reshape(bsz, hh, s, n) for x in drows_v[:5])
    dkkrawh, dah = p_bwd("rwkv_prehead_bwd", rwkv_prehead_f, pgrid, ph_ins, [(dkkh, hb, hm), (dkah, hb, hm)],
                         [(0, None), (1, None)])
    ddecay, dkkraw, da = _from_heads(ddech), _from_heads(dkkrawh), _from_heads(dah)
    dv = _from_heads(_from_cols(drows_v[5]) + dvh1)
    dr = _from_heads(drh2 + drh1)
    dkh = _from_heads(dkhh2 + dkhh1)
    dg = _from_heads(dgh)
    res = _rows("rwkv_pre_bwd", rwkv_pre_f, pre_rows, pre_prm, None, tm, fwd=False,
                cts=[ddecay, da, dg, dkkraw, dkh], wrt_rows=(0, 1, 2, 3), wrt_params=tuple(range(7)))
    dkd, dw_lr, da_lr, dg_lr = res[:4]
    dprm = dict(zip(("w0", "wd2", "a0", "wa2", "wg2", "k_k", "k_a"), res[4:]))
    dprm.update(gn_g=dgn_g.reshape(1, -1), gn_b=dgn_b.reshape(1, -1), r_k=dr_k.reshape(hh, n))
    return (dr, dkd, dv, dw_lr, da_lr, dg_lr), dprm, comm_out


GLA_C, GLA_DK, GLA_DV, GLA_H, GLA_TAU = 64, 128, 256, 4, 16.0


def gla_chunk_f(st, q, k, v, lr, r, w2, bg, ng, nb):
    la = jax.nn.log_sigmoid(mm(lr, w2) + bg) / GLA_TAU
    ri = lax.broadcasted_iota(jnp.int32, (GLA_C, GLA_C), 0)
    ci = lax.broadcasted_iota(jnp.int32, (GLA_C, GLA_C), 1)
    causal = ci <= ri
    b = jnp.dot(causal.astype(F32), la, precision=lax.Precision.HIGHEST, preferred_element_type=F32)
    b_last = jnp.sum(la, axis=0, keepdims=True)
    q_dec = (q * (GLA_DK ** -0.5)) * jnp.exp(b)
    k_inv = k * jnp.exp(-b)
    k_end = k * jnp.exp(b_last - b)
    att = jnp.where(causal, mm(q_dec, k_inv, tb=True), 0.0)
    o = mm(att, v) + mm(q_dec, st, tb=True)
    st_new = st * jnp.exp(b_last) + mm(v, k_end, ta=True)
    mu = jnp.mean(o, axis=-1, keepdims=True)
    var = jnp.mean(jnp.square(o - mu), axis=-1, keepdims=True)
    on = (o - mu) * lax.rsqrt(var + LN_EPS) * ng + nb
    return st_new, on * jax.nn.silu(r)


def gla_block(bsz, q, k, v, r, lr, w2, bg, ng, nb, cts=None, comm=None):
    t = q.shape[0]
    nc = t // bsz // GLA_C
    grid = (GLA_H, bsz, nc)
    rm = lambda h, b, c: (b * nc + c, h)
    ins = [(q, (GLA_C, GLA_DK), rm), (k, (GLA_C, GLA_DK), rm), (v, (GLA_C, GLA_DV), rm),
           (jnp.broadcast_to(lr[None], (GLA_H,) + lr.shape), (None, GLA_C, lr.shape[1]), lambda h, b, c: (h, b * nc + c, 0)),
           (r, (GLA_C, GLA_DV), rm),
           (w2, (w2.shape[0], GLA_DK), lambda h, b, c: (0, h)), (bg, (1, GLA_DK), lambda h, b, c: (0, h)),
           (ng, (1, GLA_DV), lambda h, b, c: (0, 0)), (nb, (1, GLA_DV), lambda h, b, c: (0, 0))]
    ob = (GLA_C, GLA_DV)
    sshape, sblock = (GLA_H, bsz, nc, GLA_DV, GLA_DK), (None, None, None, GLA_DV, GLA_DK)
    sm = lambda h, b, c: (h, b, c, 0, 0)
    carry = ((GLA_DV, GLA_DK), F32)
    if cts is None:
        out, sv, *comm_out = p_fwd("gla_scan", gla_chunk_f, grid, ins, [((t, GLA_H * GLA_DV), BF16, ob, rm)],
                                   carry=carry, save_carry=(sshape, sblock, sm), comm=comm)
        return out, sv, comm_out
    dout, sv = cts
    return p_bwd("gla_scan_bwd", gla_chunk_f, grid, ins, [(dout, ob, rm)],
                 [(0, None), (1, None), (2, None), (3, None), (4, None), (5, (1, 2)), (6, (1, 2)), (7, (0, 1, 2)), (8, (0, 1, 2))],
                 carry=carry, saved=(sv, sblock, sm), comm=comm)


def _softmax_rows(sc):
    m = lax.stop_gradient(jnp.max(sc, axis=-1, keepdims=True))
    e = jnp.exp(sc - m)
    return e / jnp.sum(e, axis=-1, keepdims=True)


def mla_attn_f(qn, r1, r2, kn, kr1, kr2, v):
    tq, s = qn.shape[0], kn.shape[0]
    sc = (mm(qn, kn, tb=True) + mm(r1, kr1, tb=True) + mm(r2, kr2, tb=True)) * (192.0 ** -0.5)
    qpos = pl.program_id(2) * tq + lax.broadcasted_iota(jnp.int32, (tq, s), 0)
    kpos = lax.broadcasted_iota(jnp.int32, (tq, s), 1)
    sc = jnp.where(kpos <= qpos, sc, NEG_INF)
    return mm(_softmax_rows(sc), v)


def mla_attn(bsz, qn, r1, r2, kn, kr1, kr2, v, cts=None, tq=256, comm=None):
    t = qn.shape[0]
    s = t // bsz
    nq = s // tq
    hh = 8
    grid = (bsz, hh, nq)
    qm = lambda b, h, i: (b * nq + i, h)
    km_ = lambda b, h, i: (b, h)
    ins = [(qn, (tq, 128), qm),
           (r1, (None, None, tq, 32), lambda b, h, i: (b, h, i, 0)), (r2, (None, None, tq, 32), lambda b, h, i: (b, h, i, 0)),
           (kn, (s, 128), km_), (kr1, (s, 32), lambda b, h, i: (b, 0)), (kr2, (s, 32), lambda b, h, i: (b, 0)),
           (v, (s, 128), km_)]
    if cts is None:
        out, *comm_out = p_fwd("mla_attn", mla_attn_f, grid, ins, [((t, 1024), BF16, (tq, 128), qm)], comm=comm)
        return out, comm_out
    return p_bwd("mla_attn_bwd", mla_attn_f, grid, ins, [(cts, (tq, 128), qm)],
                 [(0, None), (1, None), (2, None), (3, (2,)), (4, (1, 2)), (5, (1, 2)), (6, (2,))], comm=comm)


def xattn_f(q, k, v):
    sc = mm(q, k, tb=True) * (512.0 ** -0.5)
    return mm(_softmax_rows(sc), v)


def xattn(bsz, q, k, v, cts=None, tq=512):
    t = q.shape[0]
    nq = t // bsz // tq
    mlen = k.shape[0] // bsz
    grid = (bsz, 4, nq)
    qm = lambda b, h, i: (b * nq + i, h)
    km_ = lambda b, h, i: (b, h)
    ins = [(q, (tq, 512), qm), (k, (mlen, 512), km_), (v, (mlen, 512), km_)]
    if cts is None:
        return p_fwd("xattn", xattn_f, grid, ins, [((t, 2048), BF16, (tq, 512), qm)])[0]
    return p_bwd("xattn_bwd", xattn_f, grid, ins, [(cts, (tq, 512), qm)], [(0, None, None, BF16), (1, (2,)), (2, (2,))])


DIL_SPAN = 128
DIL_BRANCHES = ((128, 1), (512, 4), (2048, 16))


def dil_attn_f(q1, q2, k1c, k2c, vc, k1p, k2p, vp):
    gb, sp = q1.shape[0], DIL_SPAN
    scale = 128.0 ** -0.5
    sc_c = (mm(q1, k1c, tb=True) + mm(q2, k2c, tb=True)) * scale
    sc_p = (mm(q1, k1p, tb=True) + mm(q2, k2p, tb=True)) * scale
    ql = lax.broadcasted_iota(jnp.int32, (gb, sp, sp), 1)
    kl = lax.broadcasted_iota(jnp.int32, (gb, sp, sp), 2)
    has_prev = pl.program_id(1) > 0
    sc_c = jnp.where(kl <= ql, sc_c, NEG_INF)
    sc_p = jnp.where(jnp.logical_and(kl >= ql, has_prev), sc_p, NEG_INF)
    m = lax.stop_gradient(jnp.maximum(jnp.max(sc_c, axis=-1, keepdims=True), jnp.max(sc_p, axis=-1, keepdims=True)))
    e_c, e_p = jnp.exp(sc_c - m), jnp.exp(sc_p - m)
    den = jnp.sum(e_c, axis=-1, keepdims=True) + jnp.sum(e_p, axis=-1, keepdims=True)
    o = mm(e_c / den, vc) + mm(e_p / den, vp)
    return o, m + jnp.log(den)


def dil_branch(q1, q2, k1, k2, v, cts=None, gb=8):
    g, l, _ = q1.shape
    nb = l // DIL_SPAN
    grid = (g // gb, nb)
    cm = lambda i, n: (i, n, 0)
    pm = lambda i, n: (i, jnp.maximum(n - 1, 0), 0)
    b64, b128, b1 = (gb, DIL_SPAN, 64), (gb, DIL_SPAN, 128), (gb, DIL_SPAN, 1)
    ins = [(q1, b64, cm), (q2, b64, cm), (k1, b64, cm), (k2, b64, cm), (v, b128, cm),
           (k1, b64, pm), (k2, b64, pm), (v, b128, pm)]
    if cts is None:
        return p_fwd("dil_attn", dil_attn_f, grid, ins, [((g, l, 128), F32, b128, cm), ((g, l, 1), F32, b1, cm)])
    do, dlse = cts
    dq1, dq2, dk1c, dk2c, dvc, dk1p, dk2p, dvp = p_bwd(
        "dil_attn_bwd", dil_attn_f, grid, ins, [(do, b128, cm), (dlse, b1, cm)],
        [(i, None) for i in range(5)] + [(i, None, cm) for i in (5, 6, 7)])

    def fold(dc, dp):
        return dc + jnp.pad(dp[:, DIL_SPAN:], ((0, 0), (0, DIL_SPAN), (0, 0)))

    return dq1, dq2, fold(dk1c, dk1p), fold(dk2c, dk2p), fold(dvc, dvp)


def dil_mix_f(o1, o2, o3, l1, l2, l3):
    m = lax.stop_gradient(jnp.maximum(jnp.maximum(l1, l2), l3))
    e1, e2, e3 = jnp.exp(l1 - m), jnp.exp(l2 - m), jnp.exp(l3 - m)
    den = e1 + e2 + e3
    return (e1 / den) * o1 + (e2 / den) * o2 + (e3 / den) * o3


def _to_res(xh, dil):
    b, h, s, d = xh.shape
    return xh.reshape(b, h, s // dil, dil, d).transpose(0, 1, 3, 2, 4).reshape(b * h * dil, s // dil, d)


def _from_res(xr, b, h, dil):
    g, l, d = xr.shape
    return xr.reshape(b, h, dil, l, d).transpose(0, 1, 3, 2, 4).reshape(b, h, l * dil, d)


def dil_block(bsz, q1, q2, k1, k2, v, cts=None):
    hh = 8
    heads = [_to_heads(x, bsz, hh) for x in (q1, q2, k1, k2, v)]
    s = heads[0].shape[2]
    outs, res_in = [], []
    for window, dil in DIL_BRANCHES:
        assert window // dil == DIL_SPAN and (s // dil) % DIL_SPAN == 0
        rin = [_to_res(x, dil) for x in heads]
        o, lse = dil_branch(*rin)
        res_in.append(rin)
        outs.append((_from_res(o, bsz, hh, dil), _from_res(lse, bsz, hh, dil)))
    tq = 512
    ob, lb = (None, None, tq, 128), (None, None, tq, 1)
    hm = lambda b, h, i: (b, h, i, 0)
    mix_ins = [(o, ob, hm) for (o, _) in outs] + [(l, lb, hm) for (_, l) in outs]
    grid = (bsz, hh, s // tq)
    if cts is None:
        (mix,) = p_fwd("dil_mix", dil_mix_f, grid, mix_ins, [((bsz, hh, s, 128), BF16, ob, hm)])
        return _from_heads(mix)
    dmix = _to_heads(cts, bsz, hh)
    dml = p_bwd("dil_mix_bwd", dil_mix_f, grid, mix_ins, [(dmix, ob, hm)], [(i, None) for i in range(6)])
    tot = None
    for j, (window, dil) in enumerate(DIL_BRANCHES):
        do, dl = _to_res(dml[j], dil), _to_res(dml[3 + j], dil)
        gr = dil_branch(*res_in[j], cts=(do, dl))
        gr = [_from_res(x, bsz, hh, dil) for x in gr]
        tot = gr if tot is None else [a + b for a, b in zip(tot, gr)]
    return tuple(_from_heads(x) for x in tot)


_ANY = pl.BlockSpec(memory_space=pl.ANY)


def _me_and_peers():
    x, y, c = lax.axis_index("x"), lax.axis_index("y"), lax.axis_index("c")
    me = 4 * x + 2 * y + c
    peers = []
    for k in range(1, N_DEV):
        px = 1 - x if k & 4 else x
        py = 1 - y if k & 2 else y
        pc = 1 - c if k & 1 else c
        peers.append(((px, py, pc), 4 * px + 2 * py + pc))
    return me, peers


def _exchange(name, x, scatter):
    shape = x.shape[1:] if scatter else x.shape

    def body(x_ref, out_ref, send_sems, recv_sems, local_sem):
        me, peers = _me_and_peers()
        src_me = x_ref.at[me] if scatter else x_ref
        local = pltpu.make_async_copy(src_me, out_ref.at[me], local_sem)
        local.start()
        sends = []
        for k, (dev, idx) in enumerate(peers):
            cp = pltpu.make_async_remote_copy(
                src_ref=x_ref.at[idx] if scatter else x_ref, dst_ref=out_ref.at[me],
                send_sem=send_sems.at[k], recv_sem=recv_sems.at[k],
                device_id=dev, device_id_type=pl.DeviceIdType.MESH)
            cp.start()
            sends.append(cp)
        for k, (dev, idx) in enumerate(peers):
            pltpu.make_async_remote_copy(
                src_ref=src_me, dst_ref=out_ref.at[idx], send_sem=send_sems.at[k], recv_sem=recv_sems.at[k],
                device_id=dev, device_id_type=pl.DeviceIdType.MESH).wait_recv()
        for cp in sends:
            cp.wait_send()
        local.wait()

    return _pcall(
        body, name=name, in_specs=[_ANY], out_specs=_ANY,
        out_shape=jax.ShapeDtypeStruct((N_DEV,) + tuple(shape), x.dtype),
        scratch_shapes=[pltpu.SemaphoreType.DMA((N_DEV - 1,)), pltpu.SemaphoreType.DMA((N_DEV - 1,)),
                        pltpu.SemaphoreType.DMA],
        compiler_params=pltpu.CompilerParams(has_side_effects=True),
    )(x)


def all_gather(name, x):
    return _exchange(name, x, False)


def reduce_scatter_exchange(name, x):
    return _exchange(name, x, True)


def run_exchange(name, spec):
    n_in, n_out = len(spec["ins"]), len(spec["outs"])

    def body(*refs):
        parts = refs[:n_in], refs[n_in:n_in + n_out], refs[n_in + n_out:]
        spec["start"](*parts)
        spec["finish"](*parts)

    return _pcall(
        body, name=name, in_specs=[_ANY] * n_in, out_specs=[_ANY] * n_out, out_shape=list(spec["outs"]),
        scratch_shapes=list(spec["sems"]), compiler_params=pltpu.CompilerParams(has_side_effects=True),
    )(*spec["ins"])


def all_gather_2level_spec(x):
    def parts(ins, outs, sems):
        (x_ref,), (out_ref,), (send_sems, recv_sems, local_sem) = ins, outs, sems
        x_, y_, c_ = lax.axis_index("x"), lax.axis_index("y"), lax.axis_index("c")
        sibling = (x_, y_, 1 - c_)
        chips = [(1 - x_, y_), (x_, 1 - y_), (1 - x_, 1 - y_)]

        def slot(px, py, pc):
            return out_ref.at[4 * px + 2 * py + pc]

        def copy(k, block, to, src=None):
            return pltpu.make_async_remote_copy(
                src_ref=slot(*block) if src is None else src, dst_ref=slot(*block),
                send_sem=send_sems.at[k], recv_sem=recv_sems.at[k], device_id=to, device_id_type=pl.DeviceIdType.MESH)

        me = (x_, y_, c_)
        mine = pltpu.make_async_copy(x_ref, slot(*me), local_sem)
        first = [copy(0, me, sibling, src=x_ref)]
        first += [copy(1 + j, me, (*chip, c_), src=x_ref) for j, chip in enumerate(chips)]
        return copy, me, sibling, chips, c_, mine, first

    def start(ins, outs, sems):
        *_, mine, first = parts(ins, outs, sems)
        mine.start()
        for cp in first:
            cp.start()

    def finish(ins, outs, sems):
        copy, me, sibling, chips, c_, mine, first = parts(ins, outs, sems)
        passed = [copy(4 + j, (*chip, c_), sibling) for j, chip in enumerate(chips)]
        for j, chip in enumerate(chips):
            copy(1 + j, (*chip, c_), me).wait_recv()
            passed[j].start()
        copy(0, sibling, me).wait_recv()
        for j, chip in enumerate(chips):
            copy(4 + j, (*chip, 1 - c_), me).wait_recv()
        for cp in first + passed:
            cp.wait_send()
        mine.wait()

    return dict(ins=[x], outs=[jax.ShapeDtypeStruct((N_DEV,) + tuple(x.shape), x.dtype)],
                sems=[pltpu.SemaphoreType.DMA((7,)), pltpu.SemaphoreType.DMA((7,)), pltpu.SemaphoreType.DMA],
                start=start, finish=finish)


def all_gather_2level(name, x):
    return run_exchange(name, all_gather_2level_spec(x))[0]


def sibling_swap_spec(x):
    def copies(ins, outs, sems):
        (x_ref,), (theirs_ref,), (send_sems, recv_sems) = ins, outs, sems
        x_, y_, c_ = lax.axis_index("x"), lax.axis_index("y"), lax.axis_index("c")
        return [pltpu.make_async_remote_copy(
            src_ref=x_ref.at[2 * j + 1 - c_], dst_ref=theirs_ref.at[j], send_sem=send_sems.at[j], recv_sem=recv_sems.at[j],
            device_id=(x_, y_, 1 - c_), device_id_type=pl.DeviceIdType.MESH) for j in range(4)]

    def start(ins, outs, sems):
        for cp in copies(ins, outs, sems):
            cp.start()

    def finish(ins, outs, sems):
        for cp in copies(ins, outs, sems):
            cp.wait()

    return dict(ins=[x], outs=[jax.ShapeDtypeStruct((4,) + tuple(x.shape[1:]), x.dtype)],
                sems=[pltpu.SemaphoreType.DMA((4,)), pltpu.SemaphoreType.DMA((4,))], start=start, finish=finish)


def sibling_swap(name, x):
    return run_exchange(name, sibling_swap_spec(x))[0]


def pair_add(name, stack, theirs, out_dtype, comm=None):
    shp = theirs.shape
    c = shp[-1]
    r = math.prod(shp[1:-1])
    br = r
    for cand in (1024, 512, 256, 128, 64, 32, 16, 8):
        if r % cand == 0 and cand * c <= 256 * 1024:
            br = cand
            break

    def body(s0_ref, s1_ref, t_ref, o_ref):
        mine = jnp.where(lax.axis_index("c") == 0, s0_ref[...], s1_ref[...])
        o_ref[...] = (mine + t_ref[...]).astype(o_ref.dtype)

    s4 = stack.reshape(4, 2, r, c)
    out, *comm_out = _gcall(
        body, name, (4, r // br),
        [_bs((None, None, br, c), lambda j, i: (j, 0, i, 0)), _bs((None, None, br, c), lambda j, i: (j, 1, i, 0)),
         _bs((None, br, c), lambda j, i: (j, i, 0))],
        [_bs((None, br, c), lambda j, i: (j, i, 0))], [jax.ShapeDtypeStruct((4, r, c), out_dtype)], [],
        (s4, s4, theirs.reshape(4, r, c)), comm)
    return out.reshape(shp), comm_out


def chip_exchange_spec(p):
    def parts(ins, outs, sems, sending=False):
        (p_ref,), (out_ref,), (send_sems, recv_sems, local_sem) = ins, outs, sems
        x_, y_, c_ = lax.axis_index("x"), lax.axis_index("y"), lax.axis_index("c")
        my_chip = 2 * x_ + y_
        chips = [(1 - x_, y_), (x_, 1 - y_), (1 - x_, 1 - y_)]
        local = pltpu.make_async_copy(p_ref.at[my_chip], out_ref.at[my_chip], local_sem)
        sends = [pltpu.make_async_remote_copy(
            src_ref=p_ref.at[2 * px + py], dst_ref=out_ref.at[my_chip], send_sem=send_sems.at[k], recv_sem=recv_sems.at[k],
            device_id=(px, py, c_), device_id_type=pl.DeviceIdType.MESH) for k, (px, py) in enumerate(chips)]
        recvs = [] if sending else [pltpu.make_async_remote_copy(
            src_ref=p_ref.at[my_chip], dst_ref=out_ref.at[2 * px + py], send_sem=send_sems.at[k], recv_sem=recv_sems.at[k],
            device_id=(px, py, c_), device_id_type=pl.DeviceIdType.MESH) for k, (px, py) in enumerate(chips)]
        return local, sends, recvs

    def start(ins, outs, sems):
        local, sends, _ = parts(ins, outs, sems, sending=True)
        local.start()
        for cp in sends:
            cp.start()

    def finish(ins, outs, sems):
        local, sends, recvs = parts(ins, outs, sems)
        for cp in recvs:
            cp.wait_recv()
        for cp in sends:
            cp.wait_send()
        local.wait()

    return dict(ins=[p], outs=[jax.ShapeDtypeStruct(p.shape, p.dtype)],
                sems=[pltpu.SemaphoreType.DMA((3,)), pltpu.SemaphoreType.DMA((3,)), pltpu.SemaphoreType.DMA],
                start=start, finish=finish)


def chip_exchange(name, p):
    return run_exchange(name, chip_exchange_spec(p))[0]


def chip_partials(name, stack):
    return chip_partials_many([name], [stack])[0]


def chip_partials_many(names, stacks):
    theirs = sibling_swap(names[0] + "_d2d", stacks[0])
    parts = []
    for i, (name, stack) in enumerate(zip(names, stacks)):
        nxt = [sibling_swap_spec(stacks[i + 1])] if i + 1 < len(stacks) else None
        part, got = pair_add(name + "_add", stack, theirs, BF16, comm=nxt)
        parts.append(part)
        theirs = got[0] if got else None
    return parts


def reduce_scatter_2level(name, stack):
    return chip_exchange(name + "_ici", chip_partials(name, stack))


def ln_res_f(h, r, g, b):
    x = ALPHA * h + r
    mu = jnp.mean(x, axis=-1, keepdims=True)
    var = jnp.mean(jnp.square(x - mu), axis=-1, keepdims=True)
    return (x - mu) * lax.rsqrt(var + LN_EPS) * g + b


def rms_f(x, g):
    return x * lax.rsqrt(jnp.mean(x * x, axis=-1, keepdims=True) + RMS_EPS) * g


def rope_f(x1, x2, c, s):
    return x1 * c - x2 * s, x1 * s + x2 * c


def swiglu_f(g, u):
    return jax.nn.silu(g) * u


def tshift_f(d, dprev, mu):
    return d + (dprev - d) * mu


def loss_f(y, tgt):
    e = y - tgt
    return e / y.shape[-1], 0.5 * jnp.mean(e * e, axis=-1, keepdims=True)


def adamw(name, w, m, v, gstack):
    shp = w.shape
    c = shp[-1]
    r = math.prod(shp[:-1])
    br = r
    for cand in (512, 256, 128, 64, 32, 16, 8):
        if r % cand == 0 and cand * c <= 128 * 1024:
            br = cand
            break
    k = gstack.shape[0]

    def body(w_ref, m_ref, v_ref, g_ref, go_ref, d_ref, mo_ref, vo_ref):
        g = g_ref[0].astype(F32)
        for j in range(1, k):
            g = g + g_ref[j].astype(F32)
        m_new = ADAM_B1 * m_ref[...] + (1.0 - ADAM_B1) * g
        v_new = ADAM_B2 * v_ref[...] + (1.0 - ADAM_B2) * jnp.square(g)
        m_hat = m_new / (1.0 - ADAM_B1 ** ADAM_STEP)
        v_hat = v_new / (1.0 - ADAM_B2 ** ADAM_STEP)
        go_ref[...] = g
        d_ref[...] = -ADAM_LR * (m_hat / (jnp.sqrt(v_hat) + ADAM_EPS) + ADAM_WD * w_ref[...])
        mo_ref[...] = m_new
        vo_ref[...] = v_new

    spec = _bs((br, c), lambda i: (i, 0))
    outs = _pcall(
        body, name=name, grid=(r // br,),
        in_specs=[spec, spec, spec, _bs((k, br, c), lambda i: (0, i, 0))],
        out_specs=[spec] * 4, out_shape=[jax.ShapeDtypeStruct((r, c), F32)] * 4,
        compiler_params=_cparams(),
    )(w.reshape(r, c), m.reshape(r, c), v.reshape(r, c), gstack.reshape(k, r, c))
    return tuple(o.reshape(shp) for o in outs)


EV_W = (512, 256, 64, 512, 512, 1024, 1024, 16)
EV_IN, EV_PAD = 3920, 4096
OD_IN, OD_PAD = 6592, 6656
TM = 256


def _offsets(widths):
    offs, acc = [], 0
    for w in widths:
        offs.append((acc, acc + w))
        acc += w
    return offs


def _rope_tables(seq, dim):
    inv = 10000.0 ** (-jnp.arange(0, dim, 2, dtype=F32) / dim)
    ang = jnp.arange(seq, dtype=F32)[:, None] * inv[None, :]
    return jnp.cos(ang), jnp.sin(ang)


def _halves(x, nh):
    t, w = x.shape
    x3 = x.reshape(t, nh, w // nh)
    hd = w // nh // 2
    return x3[:, :, :hd].reshape(t, nh * hd), x3[:, :, hd:].reshape(t, nh * hd)


def _unhalves(x1, x2, nh):
    t = x1.shape[0]
    return jnp.concatenate([x1.reshape(t, nh, -1), x2.reshape(t, nh, -1)], axis=2).reshape(t, -1)


def ln_res2_f(h, r, g, b):
    y = ln_res_f(h, r, g, b)
    return y, y


def _ln(name, h, r, g, b, cts=None):
    if cts is None:
        return _rows(name, ln_res2_f, [h, r], [g, b], [h.shape[1]] * 2, TM, dtypes=[F32, BF16])
    return _rows(name + "_bwd", ln_res_f, [h, r], [g, b], None, TM, fwd=False, cts=[cts], wrt_rows=(0, 1), wrt_params=(0, 1),
                 dtypes=[F32, BF16])


def _gather_specs(late_local, keys):
    return [all_gather_2level_spec(late_local[k]) for k in keys]


def _put_gathered(p, late_local, keys, gathered):
    for k, g in zip(keys, gathered):
        p[k[0]][k[1]] = _unshard(g, late_local[k].shape, BIG[k[0]])


def _tail_fwd(l, h, hb, mem2b, bsz, p, late_local=None):
    qx = matmul(f"xa_q{l}", hb, p["xa_w_q"][l])
    kx = matmul(f"xa_k{l}", mem2b, p["xa_w_k"][l])
    vx = matmul(f"xa_v{l}", mem2b, p["xa_w_v"][l])
    ox = xattn(bsz, qx, kx, vx)
    xa = matmul(f"xa_o{l}", ox, p["xa_w_o"][l])
    h2, h2b = _ln(f"ln_xa{l}", h, xa, p["ln_xa_g"][l:l + 1], p["ln_xa_b"][l:l + 1])
    if late_local is None:
        gg = matmul(f"ffn_g{l}", h2b, p["ffn_w_gate"][l])
        uu = matmul(f"ffn_u{l}", h2b, p["ffn_w_up"][l])
    else:
        k_down, k_in = [("ffn_w_down", l)], [("od_w_in", 0)]
        gg, got = matmul(f"ffn_g{l}", h2b, p["ffn_w_gate"][l], comm=_gather_specs(late_local, k_down))
        _put_gathered(p, late_local, k_down, got)
        uu, got = matmul(f"ffn_u{l}", h2b, p["ffn_w_up"][l], comm=_gather_specs(late_local, k_in))
        _put_gathered(p, late_local, k_in, got)
    act = _rows(f"swiglu{l}", swiglu_f, [gg, uu], [], [gg.shape[1]], TM, dtypes=[BF16])[0]
    if late_local is None:
        ff = matmul(f"ffn_d{l}", act, p["ffn_w_down"][l])
    else:
        keys = [("od_w_out", 0)]
        ff, got = matmul(f"ffn_d{l}", act, p["ffn_w_down"][l], comm=_gather_specs(late_local, keys))
        _put_gathered(p, late_local, keys, got)
    h3, h3b = _ln(f"ln_ffn{l}", h2, ff, p["ln_ffn_g"][l:l + 1], p["ln_ffn_b"][l:l + 1])
    return h3, h3b, (h, hb, qx, kx, vx, ox, xa, h2, h2b, gg, uu, act, ff)


def _tail_bwd(l, dh3, saved, mem2b, bsz, p, gr):
    h, hb, qx, kx, vx, ox, xa, h2, h2b, gg, uu, act, ff = saved
    dh2, dff, gr["ln_ffn_g"][l], gr["ln_ffn_b"][l] = _ln(f"ln_ffn{l}", h2, ff, p["ln_ffn_g"][l:l + 1], p["ln_ffn_b"][l:l + 1], cts=dh3)
    dact = matmul(f"ffn_d_dx{l}", dff, p["ffn_w_down"][l], tb=True)
    gr["ffn_w_down"][l] = matmul(f"ffn_d_dw{l}", act, dff, ta=True)
    dgg, duu = _rows(f"swiglu_bwd{l}", swiglu_f, [gg, uu], [], None, TM, fwd=False, cts=[dact], wrt_rows=(0, 1),
                     dtypes=[BF16, BF16])
    gr["ffn_w_gate"][l] = matmul(f"ffn_g_dw{l}", h2b, dgg, ta=True)
    gr["ffn_w_up"][l] = matmul(f"ffn_u_dw{l}", h2b, duu, ta=True)
    dh2 = matmul(f"ffn_g_dx{l}", dgg, p["ffn_w_gate"][l], tb=True, add=dh2)
    dh2 = matmul(f"ffn_u_dx{l}", duu, p["ffn_w_up"][l], tb=True, add=dh2)
    dh, dxa, gr["ln_xa_g"][l], gr["ln_xa_b"][l] = _ln(f"ln_xa{l}", h, xa, p["ln_xa_g"][l:l + 1], p["ln_xa_b"][l:l + 1], cts=dh2)
    dox = matmul(f"xa_o_dx{l}", dxa, p["xa_w_o"][l], tb=True)
    gr["xa_w_o"][l] = matmul(f"xa_o_dw{l}", ox, dxa, ta=True)
    dqx, dkx, dvx = xattn(bsz, qx, kx, vx, cts=dox)
    gr["xa_w_q"][l] = matmul(f"xa_q_dw{l}", hb, dqx, ta=True)
    gr["xa_w_k"][l] = matmul(f"xa_k_dw{l}", mem2b, dkx, ta=True)
    gr["xa_w_v"][l] = matmul(f"xa_v_dw{l}", mem2b, dvx, ta=True)
    return matmul(f"xa_q_dx{l}", dqx, p["xa_w_q"][l], tb=True, add=dh)


def _uq_perm(w):
    w3 = w.reshape(w.shape[0], 8, 192)
    return jnp.concatenate([w3[:, :, :128].reshape(-1, 1024), w3[:, :, 128:160].reshape(-1, 256),
                            w3[:, :, 160:].reshape(-1, 256)], axis=1)


def _uq_unperm(g):
    r = g.shape[0]
    return jnp.concatenate([g[:, :1024].reshape(r, 8, 128), g[:, 1024:1280].reshape(r, 8, 32),
                            g[:, 1280:].reshape(r, 8, 32)], axis=2).reshape(r, 1536)


def _ukv_perm(w):
    w3 = w.reshape(w.shape[0], 8, 256)
    return jnp.concatenate([w3[:, :, :128].reshape(-1, 1024), w3[:, :, 128:].reshape(-1, 1024)], axis=1)


def _ukv_unperm(g):
    r = g.shape[0]
    return jnp.concatenate([g[:, :1024].reshape(r, 8, 128), g[:, 1024:].reshape(r, 8, 128)], axis=2).reshape(r, 2048)


def _pad_cols(w, n):
    return jnp.pad(w, ((0, 0), (0, n - w.shape[1])))


def _shift_prev(x, bsz):
    t, w = x.shape
    x3 = x.reshape(bsz, t // bsz, w)
    return jnp.pad(x3, ((0, 0), (1, 0), (0, 0)))[:, :-1].reshape(t, w)


def _shift_next(x, bsz):
    t, w = x.shape
    x3 = x.reshape(bsz, t // bsz, w)
    return jnp.pad(x3[:, 1:], ((0, 0), (0, 1), (0, 0))).reshape(t, w)


LATE = ("xa_w_q", "xa_w_k", "xa_w_v", "xa_w_o", "ffn_w_gate", "ffn_w_up", "ffn_w_down")


def _late_partials(layer, gr):
    return chip_partials_many([f"rs{layer}_{k}" for k in LATE], [_shard_stack(gr[k][layer], BIG[k]) for k in LATE])


def device_step(x, mem, tgt, p, late_local):
    bsz, seq, d = x.shape
    t = bsz * seq
    x2, mem2, tgt2 = x.reshape(t, d), mem.reshape(bsz * mem.shape[1], d), tgt.reshape(t, d)
    x2b, mem2b = x2.astype(BF16), mem2.astype(BF16)
    gr = {k: [None] * DEPTH for k in ("ln_mix_g", "ln_mix_b", "xa_w_q", "xa_w_k", "xa_w_v", "xa_w_o", "ln_xa_g", "ln_xa_b",
                                      "ffn_w_gate", "ffn_w_up", "ffn_w_down", "ln_ffn_g", "ln_ffn_b")}
    cos_pe, sin_pe = _rope_tables(seq, 64)
    cos_c, sin_c = _rope_tables(seq, 128)
    cq, sq = jnp.tile(cos_pe, (bsz, 8)), jnp.tile(sin_pe, (bsz, 8))
    ck, sk = jnp.tile(cos_pe, (bsz, 1)), jnp.tile(sin_pe, (bsz, 1))
    cd, sd = jnp.tile(cos_c, (bsz, 8)), jnp.tile(sin_c, (bsz, 8))

    w_in0 = _pad_cols(p["ev_w_in"][0], EV_PAD)
    keys = [(k, 0) for k in ("ev_mla_w_uq", "ev_mla_w_ukv", "ev_w_out")]
    z0, got = matmul("ev_in", x2b, w_in0, comm=_gather_specs(late_local, keys))
    _put_gathered(p, late_local, keys, got)
    w_uq, w_ukv = _uq_perm(p["ev_mla_w_uq"][0]), _ukv_perm(p["ev_mla_w_ukv"][0])
    c_q, c_kv, k_pe, q_g, k_g, v_g, r_g, lr_g = (z0[:, a:b] for a, b in _offsets(EV_W))
    qr = _rows("q_rms", rms_f, [c_q], [p["ev_mla_q_norm"]], [512], TM, dtypes=[BF16])[0]
    kvr = _rows("kv_rms", rms_f, [c_kv], [p["ev_mla_kv_norm"]], [256], TM, dtypes=[BF16])[0]
    q = matmul("mla_uq", qr, w_uq)
    kv = matmul("mla_ukv", kvr, w_ukv)
    qn, qp1, qp2 = q[:, :1024], q[:, 1024:1280], q[:, 1280:]
    kn, vv = kv[:, :1024], kv[:, 1024:]
    kp1, kp2 = k_pe[:, :32], k_pe[:, 32:]
    r1, r2 = _rows("rope_q", rope_f, [qp1, qp2, cq, sq], [], [256, 256], TM)
    kr1, kr2 = _rows("rope_k", rope_f, [kp1, kp2, ck, sk], [], [32, 32], TM)
    r1h, r2h = _to_heads(r1, bsz, 8), _to_heads(r2, bsz, 8)
    keys = [(k, 0) for k in ("xa_w_q", "xa_w_k", "xa_w_v", "xa_w_o")]
    a_out, got = mla_attn(bsz, qn, r1h, r2h, kn, kr1, kr2, vv, comm=_gather_specs(late_local, keys))
    _put_gathered(p, late_local, keys, got)
    gla_prm = (p["ev_gla_w_gate2"][0], p["ev_gla_b_gate"], p["ev_gla_norm_g"], p["ev_gla_norm_b"])
    keys = [("ffn_w_gate", 0), ("ffn_w_up", 0)]
    b_out, gla_sv, got = gla_block(bsz, q_g, k_g, v_g, r_g, lr_g, *gla_prm, comm=_gather_specs(late_local, keys))
    _put_gathered(p, late_local, keys, got)
    mixin0 = jnp.concatenate([a_out, b_out], axis=1)
    mix0 = matmul("ev_out", mixin0, p["ev_w_out"][0])
    h1, h1b = _ln("ln_mix0", x2, mix0, p["ln_mix_g"][0:1], p["ln_mix_b"][0:1])
    h3, h3b, tail0 = _tail_fwd(0, h1, h1b, mem2b, bsz, p, late_local)

    w_in1 = _pad_cols(p["od_w_in"][0], OD_PAD)
    z1 = matmul("od_in", h3b, w_in1)
    dq_, dk_, dv_ = z1[:, :1024], z1[:, 1024:2048], z1[:, 2048:3072]
    d_in = z1[:, 3072:OD_IN]
    q1, q2 = _halves(dq_, 8)
    k1, k2 = _halves(dk_, 8)
    qd1, qd2 = _rows("rope_dq", rope_f, [q1, q2, cd, sd], [], [512, 512], TM)
    kd1, kd2 = _rows("rope_dk", rope_f, [k1, k2, cd, sd], [], [512, 512], TM)
    c_out = dil_block(bsz, qd1, qd2, kd1, kd2, dv_)
    d_prev = _shift_prev(d_in, bsz)
    mu = p["od_rwkv_mu"]
    ds = _rows("tshift", tshift_f, [d_in, d_prev], [mu], [d_in.shape[1]], TM)[0]
    rw_in = tuple(ds[:, a:b] for a, b in _offsets((1024, 1024, 1024, 96, 96, 256)))
    rw_prm = dict(w0=p["od_rwkv_w0"], wd2=p["od_rwkv_w_decay2"][0], a0=p["od_rwkv_a0"], wa2=p["od_rwkv_w_a2"][0],
                  wg2=p["od_rwkv_w_gate2"][0], k_k=p["od_rwkv_k_k"], k_a=p["od_rwkv_k_a"], r_k=p["od_rwkv_r_k"][0],
                  gn_g=p["od_rwkv_gn_g"], gn_b=p["od_rwkv_gn_b"])
    keys = [(k, 1) for k in LATE]
    d_out, rw_scan, got = rwkv_block(bsz, *rw_in, rw_prm, comm=_gather_specs(late_local, keys))
    _put_gathered(p, late_local, keys, got)
    mixin1 = jnp.concatenate([c_out, d_out], axis=1)
    mix1 = matmul("od_out", mixin1, p["od_w_out"][0])
    h4, h4b = _ln("ln_mix1", h3, mix1, p["ln_mix_g"][1:2], p["ln_mix_b"][1:2])
    y, _, tail1 = _tail_fwd(1, h4, h4b, mem2b, bsz, p)

    dy, row_loss = _rows("loss", loss_f, [y, tgt2], [], [d, 1], TM)
    loss = jnp.sum(row_loss)

    dh4 = _tail_bwd(1, dy, tail1, mem2b, bsz, p, gr)
    dh3, dmix1, gr["ln_mix_g"][1], gr["ln_mix_b"][1] = _ln("ln_mix1", h3, mix1, p["ln_mix_g"][1:2], p["ln_mix_b"][1:2], cts=dh4)
    dmixin1 = matmul("od_out_dx", dmix1, p["od_w_out"][0], tb=True)
    gr["od_w_out"] = matmul("od_out_dw", mixin1, dmix1, ta=True)[None]
    dc_out, dd_out = dmixin1[:, :1024], dmixin1[:, 1024:]
    drw_in, drw_prm, done1 = rwkv_block(bsz, *rw_in, rw_prm, cts=dd_out, scan=rw_scan,
                                        comm=[chip_exchange_spec(x_) for x_ in _late_partials(1, gr)])
    dds = jnp.concatenate(drw_in, axis=1)
    dd_in, dd_prev, dmu = _rows("tshift_bwd", tshift_f, [d_in, d_prev], [mu], None, TM, fwd=False, cts=[dds],
                                wrt_rows=(0, 1), wrt_params=(0,))
    dd_in = dd_in + _shift_next(dd_prev, bsz)
    dqd1, dqd2, dkd1, dkd2, ddv = dil_block(bsz, qd1, qd2, kd1, kd2, dv_, cts=dc_out)
    dq1, dq2 = _rows("rope_dq_bwd", rope_f, [q1, q2, cd, sd], [], None, TM, fwd=False, cts=[dqd1, dqd2], wrt_rows=(0, 1))
    dk1, dk2 = _rows("rope_dk_bwd", rope_f, [k1, k2, cd, sd], [], None, TM, fwd=False, cts=[dkd1, dkd2], wrt_rows=(0, 1))
    dz1 = jnp.concatenate([_unhalves(dq1, dq2, 8), _unhalves(dk1, dk2, 8), ddv, dd_in,
                           jnp.zeros((t, OD_PAD - OD_IN), F32)], axis=1).astype(BF16)
    gr["od_w_in"] = matmul("od_in_dw", h3b, dz1, ta=True)[:, :OD_IN][None]
    dh3 = matmul("od_in_dx", dz1, w_in1, tb=True, add=dh3)
    gr["od_rwkv_mu"] = dmu
    gr["od_rwkv_w0"], gr["od_rwkv_w_decay2"], gr["od_rwkv_a0"] = drw_prm["w0"], drw_prm["wd2"][None], drw_prm["a0"]
    gr["od_rwkv_w_a2"], gr["od_rwkv_w_gate2"] = drw_prm["wa2"][None], drw_prm["wg2"][None]
    gr["od_rwkv_k_k"], gr["od_rwkv_k_a"], gr["od_rwkv_r_k"] = drw_prm["k_k"], drw_prm["k_a"], drw_prm["r_k"][None]
    gr["od_rwkv_gn_g"], gr["od_rwkv_gn_b"] = drw_prm["gn_g"], drw_prm["gn_b"]

    dh1 = _tail_bwd(0, dh3, tail0, mem2b, bsz, p, gr)
    dx2, dmix0, gr["ln_mix_g"][0], gr["ln_mix_b"][0] = _ln("ln_mix0", x2, mix0, p["ln_mix_g"][0:1], p["ln_mix_b"][0:1], cts=dh1)
    dmixin0 = matmul("ev_out_dx", dmix0, p["ev_w_out"][0], tb=True)
    gr["ev_w_out"] = matmul("ev_out_dw", mixin0, dmix0, ta=True)[None]
    da_out, db_out = dmixin0[:, :1024], dmixin0[:, 1024:]
    od_keys = ("od_w_in", "od_w_out", "ev_w_out")
    od_parts = chip_partials_many(["rs_" + k for k in od_keys], [_shard_stack(gr[k], BIG[k]) for k in od_keys])
    dq_g, dk_g, dv_g, dlr4, dr_g, dw2, dbg, dng, dnb, done_in, done_out, done_ev_out = gla_block(
        bsz, q_g, k_g, v_g, r_g, lr_g, *gla_prm, cts=(db_out, gla_sv), comm=[chip_exchange_spec(x_) for x_ in od_parts])
    dlr_g = jnp.sum(dlr4, axis=0)
    dqn, dr1h, dr2h, dkn, dkr1, dkr2, dvv, *done0 = mla_attn(bsz, qn, r1h, r2h, kn, kr1, kr2, vv, cts=da_out, tq=512,
                                                             comm=[chip_exchange_spec(x_) for x_ in _late_partials(0, gr)])
    dqp1, dqp2 = _rows("rope_q_bwd", rope_f, [qp1, qp2, cq, sq], [], None, TM, fwd=False,
                       cts=[_from_heads(dr1h), _from_heads(dr2h)], wrt_rows=(0, 1))
    dkp1, dkp2 = _rows("rope_k_bwd", rope_f, [kp1, kp2, ck, sk], [], None, TM, fwd=False, cts=[dkr1, dkr2], wrt_rows=(0, 1))
    dq = jnp.concatenate([dqn, dqp1, dqp2], axis=1).astype(BF16)
    dkv = jnp.concatenate([dkn, dvv], axis=1).astype(BF16)
    dqr = matmul("mla_uq_dx", dq, w_uq, tb=True)
    gr["ev_mla_w_uq"] = _uq_unperm(matmul("mla_uq_dw", qr, dq, ta=True))[None]
    dkvr = matmul("mla_ukv_dx", dkv, w_ukv, tb=True)
    gr["ev_mla_w_ukv"] = _ukv_unperm(matmul("mla_ukv_dw", kvr, dkv, ta=True))[None]
    dc_q, gr["ev_mla_q_norm"] = _rows("q_rms_bwd", rms_f, [c_q], [p["ev_mla_q_norm"]], None, TM, fwd=False, cts=[dqr],
                                      wrt_rows=(0,), wrt_params=(0,))
    dc_kv, gr["ev_mla_kv_norm"] = _rows("kv_rms_bwd", rms_f, [c_kv], [p["ev_mla_kv_norm"]], None, TM, fwd=False, cts=[dkvr],
                                        wrt_rows=(0,), wrt_params=(0,))
    dz0 = jnp.concatenate([dc_q, dc_kv, dkp1, dkp2, dq_g, dk_g, dv_g, dr_g, dlr_g,
                           jnp.zeros((t, EV_PAD - EV_IN), F32)], axis=1).astype(BF16)
    gr["ev_w_in"] = matmul("ev_in_dw", x2b, dz0, ta=True)[:, :EV_IN][None]
    dx2 = matmul("ev_in_dx", dz0, w_in0, tb=True, add=dx2)
    gr["ev_gla_w_gate2"], gr["ev_gla_b_gate"] = dw2[None], dbg
    gr["ev_gla_norm_g"], gr["ev_gla_norm_b"] = dng, dnb
    stacks = {k: jnp.stack([s0, s1], axis=1) for k, s0, s1 in zip(LATE, done0, done1)}
    stacks.update(od_w_in=done_in, od_w_out=done_out, ev_w_out=done_ev_out)
    for k in stacks:
        del gr[k]
    for k in list(gr):
        if isinstance(gr[k], list):
            gr[k] = jnp.stack([g[0] if k.startswith("ln_") else g for g in gr[k]])
    return loss, dx2.reshape(bsz, seq, d), gr, stacks


WEIGHTS = ['ev_w_in', 'ev_mla_q_norm', 'ev_mla_w_uq', 'ev_mla_kv_norm', 'ev_mla_w_ukv', 'ev_gla_w_gate2', 'ev_gla_b_gate',
           'ev_gla_norm_g', 'ev_gla_norm_b', 'ev_w_out', 'od_w_in', 'od_rwkv_mu', 'od_rwkv_w0', 'od_rwkv_w_decay2',
           'od_rwkv_a0', 'od_rwkv_w_a2', 'od_rwkv_w_gate2', 'od_rwkv_k_k', 'od_rwkv_k_a', 'od_rwkv_r_k', 'od_rwkv_gn_g',
           'od_rwkv_gn_b', 'od_w_out', 'ln_mix_g', 'ln_mix_b', 'xa_w_q', 'xa_w_k', 'xa_w_v', 'xa_w_o', 'ln_xa_g', 'ln_xa_b',
           'ffn_w_gate', 'ffn_w_up', 'ffn_w_down', 'ln_ffn_g', 'ln_ffn_b']
BIG = {'ev_w_in': -1, 'ev_mla_w_uq': -1, 'ev_mla_w_ukv': -1, 'ev_w_out': -2, 'od_w_in': -1, 'od_w_out': -2,
       'xa_w_q': -2, 'xa_w_k': -2, 'xa_w_v': -2, 'xa_w_o': -2, 'ffn_w_gate': -1, 'ffn_w_up': -1, 'ffn_w_down': -2}
SMALL = ['ev_gla_w_gate2', 'od_rwkv_mu', 'od_rwkv_w0', 'od_rwkv_w_decay2', 'od_rwkv_a0', 'od_rwkv_w_a2', 'od_rwkv_w_gate2',
         'od_rwkv_k_k', 'od_rwkv_k_a', 'od_rwkv_gn_g', 'od_rwkv_gn_b']
REPL = ['ev_mla_q_norm', 'ev_mla_kv_norm', 'ev_gla_b_gate', 'ev_gla_norm_g', 'ev_gla_norm_b', 'od_rwkv_r_k',
        'ln_mix_g', 'ln_mix_b', 'ln_xa_g', 'ln_xa_b', 'ln_ffn_g', 'ln_ffn_b']
PACK_COLS = 128


def _unshard(g, shape, axis):
    axis %= len(shape)
    full = list(shape)
    full[axis] *= N_DEV
    return jnp.moveaxis(g, 0, axis).reshape(full)


def _shard_stack(gfull, axis):
    axis %= gfull.ndim
    shp = list(gfull.shape)
    shp[axis:axis + 1] = [N_DEV, shp[axis] // N_DEV]
    return jnp.moveaxis(gfull.reshape(shp), axis, 0)


def _pack(arrs):
    lead = arrs[0].shape[0]
    flat = jnp.concatenate([a.reshape(lead, -1) for a in arrs], axis=1)
    n = flat.shape[1]
    rows = -(-n // (8 * PACK_COLS)) * 8
    return jnp.pad(flat, ((0, 0), (0, rows * PACK_COLS - n))).reshape(lead, rows, PACK_COLS)


def _unpack(buf, shapes):
    lead = buf.shape[0]
    flat = buf.reshape(lead, -1)
    out, off = [], 0
    for shp in shapes:
        n = math.prod(shp)
        out.append(flat[:, off:off + n].reshape((lead,) + tuple(shp)))
        off += n
    return out


def train_step(x, mem, loss_target, w, m, v):
    p, late_local = {}, {}
    for k, ax in BIG.items():
        if k != "ev_w_in":
            p[k] = [None] * w[k].shape[0]
            for l in range(w[k].shape[0]):
                late_local[(k, l)] = w[k][l].astype(BF16)
        else:
            p[k] = _unshard(all_gather_2level("ag_" + k, w[k].astype(BF16)), w[k].shape, ax)
    small_loc = _pack([w[k][None] for k in SMALL])[0]
    small_all = _unpack(all_gather("ag_small", small_loc), [w[k].shape for k in SMALL])
    for k, g in zip(SMALL, small_all):
        p[k] = _unshard(g, w[k].shape, -1)
    for k in REPL:
        p[k] = w[k]
    loss, dx, gr, stacks = device_step(x, mem, loss_target, p, late_local)
    loss = lax.psum(loss, ("x", "y", "c"))

    rest = [k for k in BIG if k not in stacks]
    parts = chip_partials_many(["rs_" + k for k in rest], [_shard_stack(gr[k], BIG[k]) for k in rest])
    for k, part in zip(rest, parts):
        stacks[k] = chip_exchange("rs_" + k + "_ici", part)
    small_send = _pack([_shard_stack(gr[k], -1) for k in SMALL])
    small_recv = _unpack(reduce_scatter_exchange("rs_small", small_send), [w[k].shape for k in SMALL])
    stacks.update(zip(SMALL, small_recv))
    repl_loc = _pack([gr[k][None] for k in REPL])[0]
    repl_all = _unpack(all_gather("ag_repl_grads", repl_loc), [w[k].shape for k in REPL])
    stacks.update(zip(REPL, repl_all))

    grads, deltas, new_m, new_v = [], [], [], []
    for k in WEIGHTS:
        g, dl, mn, vn = adamw("adamw_" + k, w[k], m[k], v[k], stacks[k])
        grads.append(g), deltas.append(dl), new_m.append(mn), new_v.append(vn)
    return (loss, dx, *grads, *deltas, *new_m, *new_v)


def kernel(x, mem, ev_w_in, ev_mla_q_norm, ev_mla_w_uq, ev_mla_kv_norm, ev_mla_w_ukv, ev_gla_w_gate2, ev_gla_b_gate, ev_gla_norm_g, ev_gla_norm_b, ev_w_out, od_w_in, od_rwkv_mu, od_rwkv_w0, od_rwkv_w_decay2, od_rwkv_a0, od_rwkv_w_a2, od_rwkv_w_gate2, od_rwkv_k_k, od_rwkv_k_a, od_rwkv_r_k, od_rwkv_gn_g, od_rwkv_gn_b, od_w_out, ln_mix_g, ln_mix_b, xa_w_q, xa_w_k, xa_w_v, xa_w_o, ln_xa_g, ln_xa_b, ffn_w_gate, ffn_w_up, ffn_w_down, ln_ffn_g, ln_ffn_b, loss_target, m_ev_w_in, m_ev_mla_q_norm, m_ev_mla_w_uq, m_ev_mla_kv_norm, m_ev_mla_w_ukv, m_ev_gla_w_gate2, m_ev_gla_b_gate, m_ev_gla_norm_g, m_ev_gla_norm_b, m_ev_w_out, m_od_w_in, m_od_rwkv_mu, m_od_rwkv_w0, m_od_rwkv_w_decay2, m_od_rwkv_a0, m_od_rwkv_w_a2, m_od_rwkv_w_gate2, m_od_rwkv_k_k, m_od_rwkv_k_a, m_od_rwkv_r_k, m_od_rwkv_gn_g, m_od_rwkv_gn_b, m_od_w_out, m_ln_mix_g, m_ln_mix_b, m_xa_w_q, m_xa_w_k, m_xa_w_v, m_xa_w_o, m_ln_xa_g, m_ln_xa_b, m_ffn_w_gate, m_ffn_w_up, m_ffn_w_down, m_ln_ffn_g, m_ln_ffn_b, v_ev_w_in, v_ev_mla_q_norm, v_ev_mla_w_uq, v_ev_mla_kv_norm, v_ev_mla_w_ukv, v_ev_gla_w_gate2, v_ev_gla_b_gate, v_ev_gla_norm_g, v_ev_gla_norm_b, v_ev_w_out, v_od_w_in, v_od_rwkv_mu, v_od_rwkv_w0, v_od_rwkv_w_decay2, v_od_rwkv_a0, v_od_rwkv_w_a2, v_od_rwkv_w_gate2, v_od_rwkv_k_k, v_od_rwkv_k_a, v_od_rwkv_r_k, v_od_rwkv_gn_g, v_od_rwkv_gn_b, v_od_w_out, v_ln_mix_g, v_ln_mix_b, v_xa_w_q, v_xa_w_k, v_xa_w_v, v_xa_w_o, v_ln_xa_g, v_ln_xa_b, v_ffn_w_gate, v_ffn_w_up, v_ffn_w_down, v_ln_ffn_g, v_ln_ffn_b):
    given = dict(locals())
    w = {k: given[k] for k in WEIGHTS}
    m = {k: given["m_" + k] for k in WEIGHTS}
    v = {k: given["v_" + k] for k in WEIGHTS}
    return train_step(given["x"], given["mem"], given["loss_target"], w, m, v)
```

```python
import functools
import math

import jax
import jax.numpy as jnp
from jax import lax
from jax.experimental import pallas as pl
from jax.experimental.pallas import tpu as pltpu

F32 = jnp.float32
BF16 = jnp.bfloat16
VMEM_LIMIT = 56 * 1024 * 1024
ROWS_VMEM = 20 * 1024 * 1024

N_DEV = 8
DEPTH = 2
ALPHA = (2.0 * DEPTH) ** 0.25
LN_EPS = 1e-5
RMS_EPS = 1e-6
RWKV_GN_EPS = 64e-5
ADAM_LR, ADAM_B1, ADAM_B2, ADAM_EPS, ADAM_WD, ADAM_STEP = 0.001, 0.9, 0.999, 1e-08, 0.01, 10
NEG_INF = float("-inf")


def _pcall(body, **kw):
    return pl.pallas_call(body, **kw)


def _cparams(**kw):
    return pltpu.CompilerParams(vmem_limit_bytes=VMEM_LIMIT, **kw)


def _dg(a, b, ca, cb, batch):
    nb = 1 if batch else 0
    dims = (((ca + nb,), (cb + nb,)), ((0,), (0,)) if batch else ((), ()))
    return lax.dot_general(a.astype(BF16), b.astype(BF16), dims, preferred_element_type=F32)


@functools.partial(jax.custom_vjp, nondiff_argnums=(2, 3, 4))
def _mm(a, b, ta, tb, batch):
    return _dg(a, b, 0 if ta else 1, 1 if tb else 0, batch)


def _mm_fwd(a, b, ta, tb, batch):
    return _mm(a, b, ta, tb, batch), (a, b)


def _mm_bwd(ta, tb, batch, res, g):
    a, b = res
    if not ta and not tb:
        da, db = _mm(g, b, False, True, batch), _mm(a, g, True, False, batch)
    elif not ta and tb:
        da, db = _mm(g, b, False, False, batch), _mm(g, a, True, False, batch)
    elif ta and not tb:
        da, db = _mm(b, g, False, True, batch), _mm(a, g, False, False, batch)
    else:
        da, db = _mm(b, g, True, True, batch), _mm(g, a, True, True, batch)
    return da.astype(a.dtype), db.astype(b.dtype)


_mm.defvjp(_mm_fwd, _mm_bwd)


def mm(a, b, ta=False, tb=False):
    return _mm(a, b, ta, tb, a.ndim == 3)


def _bs(block, imap):
    return pl.BlockSpec(block, imap)


def _rev_imap(imap, n):
    def r(*idx):
        return imap(*idx[:-1], n - 1 - idx[-1])
    return r


def _gcall(body, name, grid, in_specs, out_specs, out_shape, scratch_shapes, args, comm=None):
    comm = comm or []
    n_in, n_out, n_scr = len(in_specs), len(out_specs), len(scratch_shapes)
    c_in = [a for c in comm for a in c["ins"]]
    c_out = [o for c in comm for o in c["outs"]]
    c_sem = [s for c in comm for s in c["sems"]]

    def body2(*refs):
        i = 0
        r_in, i = refs[i:i + n_in], i + n_in
        k_in, i = refs[i:i + len(c_in)], i + len(c_in)
        r_out, i = refs[i:i + n_out], i + n_out
        k_out, i = refs[i:i + len(c_out)], i + len(c_out)
        r_scr, k_sem = refs[i:i + n_scr], refs[i + n_scr:]

        def each(which):
            a = b = s = 0
            for c in comm:
                na, nb, ns = len(c["ins"]), len(c["outs"]), len(c["sems"])
                c[which](k_in[a:a + na], k_out[b:b + nb], k_sem[s:s + ns])
                a, b, s = a + na, b + nb, s + ns

        if comm:
            first = last = None
            for ax, n in enumerate(grid):
                f0, l0 = pl.program_id(ax) == 0, pl.program_id(ax) == n - 1
                first = f0 if first is None else jnp.logical_and(first, f0)
                last = l0 if last is None else jnp.logical_and(last, l0)
            pl.when(first)(lambda: each("start"))
        body(*r_in, *r_out, *r_scr)
        if comm:
            pl.when(last)(lambda: each("finish"))

    return _pcall(
        body2, name=name, grid=grid,
        in_specs=list(in_specs) + [_ANY] * len(c_in), out_specs=list(out_specs) + [_ANY] * len(c_out),
        out_shape=list(out_shape) + c_out, scratch_shapes=list(scratch_shapes) + c_sem,
        compiler_params=_cparams(has_side_effects=True) if comm else _cparams(),
    )(*args, *c_in)


def p_fwd(name, f, grid, ins, outs, carry=None, save_carry=None, comm=None):
    n_in, n_out = len(ins), len(outs)

    def body(*refs):
        in_refs = refs[:n_in]
        out_refs = refs[n_in:n_in + n_out]
        rest = refs[n_in + n_out:]
        vals = [r[...] for r in in_refs]
        if carry is None:
            res = f(*vals)
        else:
            if save_carry is not None:
                sv_ref, c_ref = rest
            else:
                (c_ref,) = rest

            @pl.when(pl.program_id(len(grid) - 1) == 0)
            def _():
                c_ref[...] = jnp.zeros(c_ref.shape, c_ref.dtype)

            c = c_ref[...]
            if save_carry is not None:
                sv_ref[...] = c
            res = f(c, *vals)
            c_ref[...] = res[0]
            res = res[1:]
        if not isinstance(res, (tuple, list)):
            res = (res,)
        for r, v in zip(out_refs, res):
            r[...] = v.astype(r.dtype)

    out_shape = [jax.ShapeDtypeStruct(s, d) for (s, d, _, _) in outs]
    out_specs = [_bs(b, m) for (_, _, b, m) in outs]
    scratch = []
    if carry is not None:
        if save_carry is not None:
            out_shape.append(jax.ShapeDtypeStruct(save_carry[0], carry[1]))
            out_specs.append(_bs(save_carry[1], save_carry[2]))
        scratch.append(pltpu.VMEM(carry[0], carry[1]))
    return _gcall(body, name, grid, [_bs(b, m) for (_, b, m) in ins], out_specs, out_shape, scratch,
                  [a for (a, _, _) in ins], comm)


def p_bwd(name, f, grid, ins, cts, wrt, carry=None, saved=None, comm=None):
    n_in, n_ct, n_w = len(ins), len(cts), len(wrt)
    rev = carry is not None
    n_last = grid[-1]

    def fix(imap):
        return _rev_imap(imap, n_last) if rev else imap

    def body(*refs):
        in_refs = refs[:n_in]
        ct_refs = refs[n_in:n_in + n_ct]
        k = n_in + n_ct
        if rev:
            sv_ref = refs[k]
            k += 1
        out_refs = refs[k:k + n_w]
        rest = refs[k + n_w:]
        vals = [r[...] for r in in_refs]
        ct_vals = [r[...].astype(F32) for r in ct_refs]
        widx = [w[0] for w in wrt]

        if rev:
            (dc_ref,) = rest

            @pl.when(pl.program_id(len(grid) - 1) == 0)
            def _():
                dc_ref[...] = jnp.zeros(dc_ref.shape, dc_ref.dtype)

            c_in = sv_ref[...]

            def g(c, *dv):
                full = list(vals)
                for i, d in zip(widx, dv):
                    full[i] = d
                return tuple(f(c, *full))

            _, vjp = jax.vjp(g, c_in, *[vals[i] for i in widx])
            grads = vjp((dc_ref[...],) + tuple(ct_vals))
            dc_ref[...] = grads[0]
            grads = grads[1:]
        else:
            def g(*dv):
                full = list(vals)
                for i, d in zip(widx, dv):
                    full[i] = d
                r = f(*full)
                return tuple(r) if isinstance(r, (tuple, list)) else (r,)

            _, vjp = jax.vjp(g, *[vals[i] for i in widx])
            grads = vjp(tuple(ct_vals))

        for w, o_ref, gr in zip(wrt, out_refs, grads):
            acc = w[1]
            if acc is None:
                o_ref[...] = gr.astype(o_ref.dtype)
            else:
                first = None
                for ax in acc:
                    c0 = pl.program_id(ax) == 0
                    first = c0 if first is None else jnp.logical_and(first, c0)

                @pl.when(first)
                def _():
                    o_ref[...] = jnp.zeros(o_ref.shape, o_ref.dtype)

                o_ref[...] += gr.astype(o_ref.dtype)

    in_specs = [_bs(b, fix(m)) for (_, b, m) in ins] + [_bs(b, fix(m)) for (_, b, m) in cts]
    args = [a for (a, _, _) in ins] + [a for (a, _, _) in cts]
    if rev:
        in_specs.append(_bs(saved[1], fix(saved[2])))
        args.append(saved[0])
    out_shape, out_specs = [], []
    for w in wrt:
        a, b, m = ins[w[0]]
        if len(w) > 2 and w[2] is not None:
            m = w[2]
        out_shape.append(jax.ShapeDtypeStruct(a.shape, w[3] if len(w) > 3 else F32))
        out_specs.append(_bs(b, fix(m)))
    scratch = [pltpu.VMEM(carry[0], carry[1])] if rev else []
    return _gcall(body, name, grid, in_specs, out_specs, out_shape, scratch, args, comm)


def _rows(name, f, row_ins, params, out_widths, tm, fwd=True, cts=None, wrt_rows=(), wrt_params=(), dtypes=None):
    t = row_ins[0].shape[0]
    width = sum(a.shape[1] for a in row_ins)
    width += sum(out_widths) if fwd else sum(c.shape[1] for c in cts) + sum(row_ins[i].shape[1] for i in wrt_rows)
    tm = min(tm, t)
    while tm > 8 and 2 * 4 * tm * width > ROWS_VMEM:
        tm //= 2
    rmap = lambda i: (i, 0)
    pmap = lambda i: (0, 0)
    ins = [(a, (tm, a.shape[1]), rmap) for a in row_ins] + [(p, p.shape, pmap) for p in params]
    if fwd:
        dtypes = dtypes or [F32] * len(out_widths)
        outs = [((t, w), dt, (tm, w), rmap) for w, dt in zip(out_widths, dtypes)]
        return p_fwd(name, f, (t // tm,), ins, outs)
    ct_specs = [(c, (tm, c.shape[1]), rmap) for c in cts]
    dtypes = dtypes or [F32] * len(wrt_rows)
    wrt = [(i, None, None, dt) for i, dt in zip(wrt_rows, dtypes)] + [(len(row_ins) + i, (0,)) for i in wrt_params]
    return p_bwd(name, f, (t // tm,), ins, ct_specs, wrt)


def _pick(n, cands):
    for c in cands:
        if n % c == 0:
            return c
    return n


def _wide(n, cap=1664):
    best = None
    for w in range(128, min(n, cap) + 1, 128):
        if n % w == 0:
            best = w
    return best or n


MM_VMEM = 40 * 1024 * 1024


def _mm_tiles(m, n, k, sa, sb, so, has_add):
    bm, bn = _pick(m, (512, 256, 128)), _wide(n)
    bk = k if k <= 2048 else _wide(k, 3328)

    def vmem(bm, bn, bk):
        acc = 0 if bk == k else 4 * bm * bn
        return 2 * (bm * bk * sa + bk * bn * sb + bm * bn * so * (2 if has_add else 1)) + acc

    while vmem(bm, bn, bk) > MM_VMEM and bk % 256 == 0:
        bk //= 2
    while vmem(bm, bn, bk) > MM_VMEM and bn % 256 == 0:
        bn //= 2
    return bm, bn, bk


def matmul(name, a, b, ta=False, tb=False, out_dtype=F32, add=None, comm=None):
    m = a.shape[1] if ta else a.shape[0]
    k = a.shape[0] if ta else a.shape[1]
    n = b.shape[0] if tb else b.shape[1]
    assert (b.shape[1] if tb else b.shape[0]) == k, (a.shape, b.shape, ta, tb)
    bm, bn, bk = _mm_tiles(m, n, k, a.dtype.itemsize, b.dtype.itemsize, jnp.dtype(out_dtype).itemsize, add is not None)
    nk = k // bk

    def body(a_ref, b_ref, *rest):
        o_ref, acc_ref = rest[-2:]
        if nk == 1:
            r = mm(a_ref[...], b_ref[...], ta, tb)
            o_ref[...] = (r if add is None else r + rest[0][...].astype(F32)).astype(o_ref.dtype)
            return

        @pl.when(pl.program_id(2) == 0)
        def _():
            acc_ref[...] = jnp.zeros(acc_ref.shape, F32) if add is None else rest[0][...].astype(F32)

        acc_ref[...] += mm(a_ref[...], b_ref[...], ta, tb)

        @pl.when(pl.program_id(2) == nk - 1)
        def _():
            o_ref[...] = acc_ref[...].astype(o_ref.dtype)

    a_spec = _bs((bk, bm), lambda i, j, l: (l, i)) if ta else _bs((bm, bk), lambda i, j, l: (i, l))
    b_spec = _bs((bn, bk), lambda i, j, l: (j, l)) if tb else _bs((bk, bn), lambda i, j, l: (l, j))
    o_spec = _bs((bm, bn), lambda i, j, l: (i, j))
    if comm:
        out, *comm_out = _gcall(
            body, name, (m // bm, n // bn, nk), [a_spec, b_spec] + ([] if add is None else [o_spec]), [o_spec],
            [jax.ShapeDtypeStruct((m, n), out_dtype)], [pltpu.VMEM((bm, bn) if nk > 1 else (8, 128), F32)],
            (a, b) if add is None else (a, b, add), comm)
        return out, comm_out
    return _pcall(
        body, name=name, grid=(m // bm, n // bn, nk),
        in_specs=[a_spec, b_spec] + ([] if add is None else [o_spec]), out_specs=o_spec,
        out_shape=jax.ShapeDtypeStruct((m, n), out_dtype),
        scratch_shapes=[pltpu.VMEM((bm, bn) if nk > 1 else (8, 128), F32)],
        compiler_params=_cparams(dimension_semantics=("parallel", "parallel", "arbitrary")),
    )(*((a, b) if add is None else (a, b, add)))


def _to_heads(x, b, h):
    t, w = x.shape
    return x.reshape(b, t // b, h, w // h).transpose(0, 2, 1, 3)


def _from_heads(x):
    b, h, s, d = x.shape
    return x.transpose(0, 2, 1, 3).reshape(b * s, h * d)


RW_STEPS = 16
RW_G = 16


def rwkv_group(st, w8, kk8, ka8, kh8, r8, vc):
    ys = []
    for t in range(RW_STEPS):
        row = lambda x: x[:, t:t + 1, :]
        sa = jnp.sum(st * row(kk8), axis=2, keepdims=True)
        st = st * row(w8) - sa * row(ka8) + vc[:, :, t:t + 1] * row(kh8)
        ys.append(jnp.sum(st * row(r8), axis=2, keepdims=True))
    return st, jnp.concatenate(ys, axis=2)


def rwkv_group_f(st, w8, kkraw8, a8, kh8, r8, vc):
    nrm = jnp.sqrt(jnp.sum(kkraw8 * kkraw8, axis=2, keepdims=True))
    kk8 = kkraw8 / jnp.maximum(nrm, 1e-12)
    return rwkv_group(st, w8, kk8, kk8 * a8, kh8, r8, vc)


def rwkv_pre_f(kd, w_lr, a_lr, g_lr, w0, wd2, a0, wa2, wg2, k_k, k_a):
    wpre = w0 + mm(jnp.tanh(w_lr), wd2)
    w = -jax.nn.softplus(-wpre) - 0.5
    decay = jnp.exp(-jnp.exp(w))
    a = jax.nn.sigmoid(a0 + mm(a_lr, wa2))
    g = mm(jax.nn.sigmoid(g_lr), wg2)
    kkraw = kd * k_k
    kh = kd * (1.0 + (a - 1.0) * k_a)
    return decay, a, g, kkraw, kh


def rwkv_post_f(y, r, kh, v, g, gn_g, gn_b, r_k):
    mu = jnp.mean(y, axis=-1, keepdims=True)
    var = jnp.mean(jnp.square(y - mu), axis=-1, keepdims=True)
    yn = (y - mu) * lax.rsqrt(var + RWKV_GN_EPS) * gn_g + gn_b
    bonus = jnp.sum(r * kh * r_k, axis=-1, keepdims=True) * v
    return (yn + bonus) * g


def _to_cols(xh):
    b, h, s, d = xh.shape
    return xh.reshape(b, h, s // RW_STEPS, RW_STEPS, d).transpose(0, 1, 2, 4, 3)


def _from_cols(xc):
    b, h, n, d, k = xc.shape
    return xc.transpose(0, 1, 2, 4, 3).reshape(b, h, n * k, d)


def rwkv_block(bsz, r, kd, vd, w_lr, a_lr, g_lr, prm, cts=None, scan=None, comm=None):
    t = r.shape[0]
    hh, n = 16, 64
    tm = 256
    pre_rows = [kd, w_lr, a_lr, g_lr]
    pre_prm = [prm[k] for k in ("w0", "wd2", "a0", "wa2", "wg2", "k_k", "k_a")]
    decay, a, g, kkraw, kh = _rows("rwkv_pre", rwkv_pre_f, pre_rows, pre_prm, [1024] * 5, tm)
    heads = lambda x: _to_heads(x, bsz, hh)
    rh, khh, vh, gh, dech, kkrawh, ah = (heads(x) for x in (r, kh, vd, g, decay, kkraw, a))
    s = t // bsz
    ng = s // RW_STEPS
    ts = 512
    hb = (None, None, ts, n)
    hm = lambda h, b, i: (b, h, i, 0)
    pgrid = (hh, bsz, s // ts)
    grp = lambda x: x.reshape(bsz, hh, ng, RW_STEPS, n)
    cb, rb, sb = (None, RW_G, None, n, RW_STEPS), (None, RW_G, None, RW_STEPS, n), (None, RW_G, None, n, n)
    cm = lambda b, h, i: (b, h, i, 0, 0)
    sc_ins = [(grp(x), rb, cm) for x in (dech, kkrawh, ah, khh, rh)] + [(_to_cols(vh), cb, cm)]
    grid = (bsz, hh // RW_G, ng)
    comm_out = []
    if scan is None:
        yc, sv, *comm_out = p_fwd("rwkv_scan", rwkv_group_f, grid, sc_ins, [((bsz, hh, ng, n, RW_STEPS), F32, cb, cm)],
                                  carry=((RW_G, n, n), F32), save_carry=((bsz, hh, ng, n, n), sb, cm), comm=comm)
    else:
        yc, sv = scan
    yh = _from_cols(yc)
    pb = (None, 1, n)
    pm = lambda h, b, i: (h, 0, 0)
    gn_g, gn_b, r_k = (prm[k].reshape(hh, 1, n) for k in ("gn_g", "gn_b", "r_k"))
    post_ins = [(x, hb, hm) for x in (yh, rh, khh, vh, gh)] + [(p, pb, pm) for p in (gn_g, gn_b, r_k)]
    if cts is None:
        (oh,) = p_fwd("rwkv_post", rwkv_post_f, pgrid, post_ins, [((bsz, hh, s, n), BF16, hb, hm)])
        return _from_heads(oh), (yc, sv), comm_out
    doh = _to_heads(cts, bsz, hh)
    dyh, drh1, dkhh1, dvh1, dgh, dgn_g, dgn_b, dr_k = p_bwd(
        "rwkv_post_bwd", rwkv_post_f, pgrid, post_ins, [(doh, hb, hm)],
        [(i, None) for i in range(5)] + [(5 + i, (1, 2)) for i in range(3)])
    drows_v = p_bwd("rwkv_scan_bwd", rwkv_group_f, grid, sc_ins, [(_to_cols(dyh), cb, cm)], [(i, None) for i in range(6)],
                    carry=((RW_G, n, n), F32), saved=(sv, sb, cm), comm=comm)
    drows_v, comm_out = drows_v[:6], list(drows_v[6:])
    ddech, dkkrawh, dah, dkhh2, drh2 = (x.reshape(bsz, hh, s, n) for x in drows_v[:5])
    ddecay, dkkraw, da = _from_heads(ddech), _from_heads(dkkrawh), _from_heads(dah)
    dv = _from_heads(_from_cols(drows_v[5]) + dvh1)
    dr = _from_heads(drh2 + drh1)
    dkh = _from_heads(dkhh2 + dkhh1)
    dg = _from_heads(dgh)
    res = _rows("rwkv_pre_bwd", rwkv_pre_f, pre_rows, pre_prm, None, tm, fwd=False,
                cts=[ddecay, da, dg, dkkraw, dkh], wrt_rows=(0, 1, 2, 3), wrt_params=tuple(range(7)))
    dkd, dw_lr, da_lr, dg_lr = res[:4]
    dprm = dict(zip(("w0", "wd2", "a0", "wa2", "wg2", "k_k", "k_a"), res[4:]))
    dprm.update(gn_g=dgn_g.reshape(1, -1), gn_b=dgn_b.reshape(1, -1), r_k=dr_k.reshape(hh, n))
    return (dr, dkd, dv, dw_lr, da_lr, dg_lr), dprm, comm_out


GLA_C, GLA_DK, GLA_DV, GLA_H, GLA_TAU = 64, 128, 256, 4, 16.0


def gla_chunk_f(st, q, k, v, lr, r, w2, bg, ng, nb):
    la = jax.nn.log_sigmoid(mm(lr, w2) + bg) / GLA_TAU
    ri = lax.broadcasted_iota(jnp.int32, (GLA_C, GLA_C), 0)
    ci = lax.broadcasted_iota(jnp.int32, (GLA_C, GLA_C), 1)
    causal = ci <= ri
    b = jnp.dot(causal.astype(F32), la, precision=lax.Precision.HIGHEST, preferred_element_type=F32)
    b_last = jnp.sum(la, axis=0, keepdims=True)
    q_dec = (q * (GLA_DK ** -0.5)) * jnp.exp(b)
    k_inv = k * jnp.exp(-b)
    k_end = k * jnp.exp(b_last - b)
    att = jnp.where(causal, mm(q_dec, k_inv, tb=True), 0.0)
    o = mm(att, v) + mm(q_dec, st, tb=True)
    st_new = st * jnp.exp(b_last) + mm(v, k_end, ta=True)
    mu = jnp.mean(o, axis=-1, keepdims=True)
    var = jnp.mean(jnp.square(o - mu), axis=-1, keepdims=True)
    on = (o - mu) * lax.rsqrt(var + LN_EPS) * ng + nb
    return st_new, on * jax.nn.silu(r)


def gla_block(bsz, q, k, v, r, lr, w2, bg, ng, nb, cts=None, comm=None):
    t = q.shape[0]
    nc = t // bsz // GLA_C
    grid = (GLA_H, bsz, nc)
    rm = lambda h, b, c: (b * nc + c, h)
    ins = [(q, (GLA_C, GLA_DK), rm), (k, (GLA_C, GLA_DK), rm), (v, (GLA_C, GLA_DV), rm),
           (jnp.broadcast_to(lr[None], (GLA_H,) + lr.shape), (None, GLA_C, lr.shape[1]), lambda h, b, c: (h, b * nc + c, 0)),
           (r, (GLA_C, GLA_DV), rm),
           (w2, (w2.shape[0], GLA_DK), lambda h, b, c: (0, h)), (bg, (1, GLA_DK), lambda h, b, c: (0, h)),
           (ng, (1, GLA_DV), lambda h, b, c: (0, 0)), (nb, (1, GLA_DV), lambda h, b, c: (0, 0))]
    ob = (GLA_C, GLA_DV)
    sshape, sblock = (GLA_H, bsz, nc, GLA_DV, GLA_DK), (None, None, None, GLA_DV, GLA_DK)
    sm = lambda h, b, c: (h, b, c, 0, 0)
    carry = ((GLA_DV, GLA_DK), F32)
    if cts is None:
        out, sv, *comm_out = p_fwd("gla_scan", gla_chunk_f, grid, ins, [((t, GLA_H * GLA_DV), BF16, ob, rm)],
                                   carry=carry, save_carry=(sshape, sblock, sm), comm=comm)
        return out, sv, comm_out
    dout, sv = cts
    return p_bwd("gla_scan_bwd", gla_chunk_f, grid, ins, [(dout, ob, rm)],
                 [(0, None), (1, None), (2, None), (3, None), (4, None), (5, (1, 2)), (6, (1, 2)), (7, (0, 1, 2)), (8, (0, 1, 2))],
                 carry=carry, saved=(sv, sblock, sm), comm=comm)


def _softmax_rows(sc):
    m = lax.stop_gradient(jnp.max(sc, axis=-1, keepdims=True))
    e = jnp.exp(sc - m)
    return e / jnp.sum(e, axis=-1, keepdims=True)


def mla_attn_f(qn, r1, r2, kn, kr1, kr2, v):
    tq, s = qn.shape[0], kn.shape[0]
    sc = (mm(qn, kn, tb=True) + mm(r1, kr1, tb=True) + mm(r2, kr2, tb=True)) * (192.0 ** -0.5)
    qpos = pl.program_id(2) * tq + lax.broadcasted_iota(jnp.int32, (tq, s), 0)
    kpos = lax.broadcasted_iota(jnp.int32, (tq, s), 1)
    sc = jnp.where(kpos <= qpos, sc, NEG_INF)
    return mm(_softmax_rows(sc), v)


def mla_attn(bsz, qn, r1, r2, kn, kr1, kr2, v, cts=None, tq=256, comm=None):
    t = qn.shape[0]
    s = t // bsz
    nq = s // tq
    hh = 8
    grid = (bsz, hh, nq)
    qm = lambda b, h, i: (b * nq + i, h)
    km_ = lambda b, h, i: (b, h)
    ins = [(qn, (tq, 128), qm),
           (r1, (None, None, tq, 32), lambda b, h, i: (b, h, i, 0)), (r2, (None, None, tq, 32), lambda b, h, i: (b, h, i, 0)),
           (kn, (s, 128), km_), (kr1, (s, 32), lambda b, h, i: (b, 0)), (kr2, (s, 32), lambda b, h, i: (b, 0)),
           (v, (s, 128), km_)]
    if cts is None:
        out, *comm_out = p_fwd("mla_attn", mla_attn_f, grid, ins, [((t, 1024), BF16, (tq, 128), qm)], comm=comm)
        return out, comm_out
    return p_bwd("mla_attn_bwd", mla_attn_f, grid, ins, [(cts, (tq, 128), qm)],
                 [(0, None), (1, None), (2, None), (3, (2,)), (4, (1, 2)), (5, (1, 2)), (6, (2,))], comm=comm)


def xattn_f(q, k, v):
    sc = mm(q, k, tb=True) * (512.0 ** -0.5)
    return mm(_softmax_rows(sc), v)


def xattn(bsz, q, k, v, cts=None, tq=512):
    t = q.shape[0]
    nq = t // bsz // tq
    mlen = k.shape[0] // bsz
    grid = (bsz, 4, nq)
    qm = lambda b, h, i: (b * nq + i, h)
    km_ = lambda b, h, i: (b, h)
    ins = [(q, (tq, 512), qm), (k, (mlen, 512), km_), (v, (mlen, 512), km_)]
    if cts is None:
        return p_fwd("xattn", xattn_f, grid, ins, [((t, 2048), BF16, (tq, 512), qm)])[0]
    return p_bwd("xattn_bwd", xattn_f, grid, ins, [(cts, (tq, 512), qm)], [(0, None, None, BF16), (1, (2,)), (2, (2,))])


DIL_SPAN = 128
DIL_BRANCHES = ((128, 1), (512, 4), (2048, 16))


def dil_attn_f(q1, q2, k1c, k2c, vc, k1p, k2p, vp):
    gb, sp = q1.shape[0], DIL_SPAN
    scale = 128.0 ** -0.5
    sc_c = (mm(q1, k1c, tb=True) + mm(q2, k2c, tb=True)) * scale
    sc_p = (mm(q1, k1p, tb=True) + mm(q2, k2p, tb=True)) * scale
    ql = lax.broadcasted_iota(jnp.int32, (gb, sp, sp), 1)
    kl = lax.broadcasted_iota(jnp.int32, (gb, sp, sp), 2)
    has_prev = pl.program_id(1) > 0
    sc_c = jnp.where(kl <= ql, sc_c, NEG_INF)
    sc_p = jnp.where(jnp.logical_and(kl >= ql, has_prev), sc_p, NEG_INF)
    m = lax.stop_gradient(jnp.maximum(jnp.max(sc_c, axis=-1, keepdims=True), jnp.max(sc_p, axis=-1, keepdims=True)))
    e_c, e_p = jnp.exp(sc_c - m), jnp.exp(sc_p - m)
    den = jnp.sum(e_c, axis=-1, keepdims=True) + jnp.sum(e_p, axis=-1, keepdims=True)
    o = mm(e_c / den, vc) + mm(e_p / den, vp)
    return o, m + jnp.log(den)


def dil_branch(q1, q2, k1, k2, v, cts=None, gb=8):
    g, l, _ = q1.shape
    nb = l // DIL_SPAN
    grid = (g // gb, nb)
    cm = lambda i, n: (i, n, 0)
    pm = lambda i, n: (i, jnp.maximum(n - 1, 0), 0)
    b64, b128, b1 = (gb, DIL_SPAN, 64), (gb, DIL_SPAN, 128), (gb, DIL_SPAN, 1)
    ins = [(q1, b64, cm), (q2, b64, cm), (k1, b64, cm), (k2, b64, cm), (v, b128, cm),
           (k1, b64, pm), (k2, b64, pm), (v, b128, pm)]
    if cts is None:
        return p_fwd("dil_attn", dil_attn_f, grid, ins, [((g, l, 128), F32, b128, cm), ((g, l, 1), F32, b1, cm)])
    do, dlse = cts
    dq1, dq2, dk1c, dk2c, dvc, dk1p, dk2p, dvp = p_bwd(
        "dil_attn_bwd", dil_attn_f, grid, ins, [(do, b128, cm), (dlse, b1, cm)],
        [(i, None) for i in range(5)] + [(i, None, cm) for i in (5, 6, 7)])

    def fold(dc, dp):
        return dc + jnp.pad(dp[:, DIL_SPAN:], ((0, 0), (0, DIL_SPAN), (0, 0)))

    return dq1, dq2, fold(dk1c, dk1p), fold(dk2c, dk2p), fold(dvc, dvp)


def dil_mix_f(o1, o2, o3, l1, l2, l3):
    m = lax.stop_gradient(jnp.maximum(jnp.maximum(l1, l2), l3))
    e1, e2, e3 = jnp.exp(l1 - m), jnp.exp(l2 - m), jnp.exp(l3 - m)
    den = e1 + e2 + e3
    return (e1 / den) * o1 + (e2 / den) * o2 + (e3 / den) * o3


def _to_res(xh, dil):
    b, h, s, d = xh.shape
    return xh.reshape(b, h, s // dil, dil, d).transpose(0, 1, 3, 2, 4).reshape(b * h * dil, s // dil, d)


def _from_res(xr, b, h, dil):
    g, l, d = xr.shape
    return xr.reshape(b, h, dil, l, d).transpose(0, 1, 3, 2, 4).reshape(b, h, l * dil, d)


def dil_block(bsz, q1, q2, k1, k2, v, cts=None):
    hh = 8
    heads = [_to_heads(x, bsz, hh) for x in (q1, q2, k1, k2, v)]
    s = heads[0].shape[2]
    outs, res_in = [], []
    for window, dil in DIL_BRANCHES:
        assert window // dil == DIL_SPAN and (s // dil) % DIL_SPAN == 0
        rin = [_to_res(x, dil) for x in heads]
        o, lse = dil_branch(*rin)
        res_in.append(rin)
        outs.append((_from_res(o, bsz, hh, dil), _from_res(lse, bsz, hh, dil)))
    tq = 512
    ob, lb = (None, None, tq, 128), (None, None, tq, 1)
    hm = lambda b, h, i: (b, h, i, 0)
    mix_ins = [(o, ob, hm) for (o, _) in outs] + [(l, lb, hm) for (_, l) in outs]
    grid = (bsz, hh, s // tq)
    if cts is None:
        (mix,) = p_fwd("dil_mix", dil_mix_f, grid, mix_ins, [((bsz, hh, s, 128), BF16, ob, hm)])
        return _from_heads(mix)
    dmix = _to_heads(cts, bsz, hh)
    dml = p_bwd("dil_mix_bwd", dil_mix_f, grid, mix_ins, [(dmix, ob, hm)], [(i, None) for i in range(6)])
    tot = None
    for j, (window, dil) in enumerate(DIL_BRANCHES):
        do, dl = _to_res(dml[j], dil), _to_res(dml[3 + j], dil)
        gr = dil_branch(*res_in[j], cts=(do, dl))
        gr = [_from_res(x, bsz, hh, dil) for x in gr]
        tot = gr if tot is None else [a + b for a, b in zip(tot, gr)]
    return tuple(_from_heads(x) for x in tot)


_ANY = pl.BlockSpec(memory_space=pl.ANY)


def _me_and_peers():
    x, y, c = lax.axis_index("x"), lax.axis_index("y"), lax.axis_index("c")
    me = 4 * x + 2 * y + c
    peers = []
    for k in range(1, N_DEV):
        px = 1 - x if k & 4 else x
        py = 1 - y if k & 2 else y
        pc = 1 - c if k & 1 else c
        peers.append(((px, py, pc), 4 * px + 2 * py + pc))
    return me, peers


def _exchange(name, x, scatter):
    shape = x.shape[1:] if scatter else x.shape

    def body(x_ref, out_ref, send_sems, recv_sems, local_sem):
        me, peers = _me_and_peers()
        src_me = x_ref.at[me] if scatter else x_ref
        local = pltpu.make_async_copy(src_me, out_ref.at[me], local_sem)
        local.start()
        sends = []
        for k, (dev, idx) in enumerate(peers):
            cp = pltpu.make_async_remote_copy(
                src_ref=x_ref.at[idx] if scatter else x_ref, dst_ref=out_ref.at[me],
                send_sem=send_sems.at[k], recv_sem=recv_sems.at[k],
                device_id=dev, device_id_type=pl.DeviceIdType.MESH)
            cp.start()
            sends.append(cp)
        for k, (dev, idx) in enumerate(peers):
            pltpu.make_async_remote_copy(
                src_ref=src_me, dst_ref=out_ref.at[idx], send_sem=send_sems.at[k], recv_sem=recv_sems.at[k],
                device_id=dev, device_id_type=pl.DeviceIdType.MESH).wait_recv()
        for cp in sends:
            cp.wait_send()
        local.wait()

    return _pcall(
        body, name=name, in_specs=[_ANY], out_specs=_ANY,
        out_shape=jax.ShapeDtypeStruct((N_DEV,) + tuple(shape), x.dtype),
        scratch_shapes=[pltpu.SemaphoreType.DMA((N_DEV - 1,)), pltpu.SemaphoreType.DMA((N_DEV - 1,)),
                        pltpu.SemaphoreType.DMA],
        compiler_params=pltpu.CompilerParams(has_side_effects=True),
    )(x)


def all_gather(name, x):
    return _exchange(name, x, False)


def reduce_scatter_exchange(name, x):
    return _exchange(name, x, True)


def run_exchange(name, spec):
    n_in, n_out = len(spec["ins"]), len(spec["outs"])

    def body(*refs):
        parts = refs[:n_in], refs[n_in:n_in + n_out], refs[n_in + n_out:]
        spec["start"](*parts)
        spec["finish"](*parts)

    return _pcall(
        body, name=name, in_specs=[_ANY] * n_in, out_specs=[_ANY] * n_out, out_shape=list(spec["outs"]),
        scratch_shapes=list(spec["sems"]), compiler_params=pltpu.CompilerParams(has_side_effects=True),
    )(*spec["ins"])


def all_gather_2level_spec(x):
    def parts(ins, outs, sems):
        (x_ref,), (out_ref,), (send_sems, recv_sems, local_sem) = ins, outs, sems
        x_, y_, c_ = lax.axis_index("x"), lax.axis_index("y"), lax.axis_index("c")
        sibling = (x_, y_, 1 - c_)
        chips = [(1 - x_, y_), (x_, 1 - y_), (1 - x_, 1 - y_)]

        def slot(px, py, pc):
            return out_ref.at[4 * px + 2 * py + pc]

        def copy(k, block, to, src=None):
            return pltpu.make_async_remote_copy(
                src_ref=slot(*block) if src is None else src, dst_ref=slot(*block),
                send_sem=send_sems.at[k], recv_sem=recv_sems.at[k], device_id=to, device_id_type=pl.DeviceIdType.MESH)

        me = (x_, y_, c_)
        mine = pltpu.make_async_copy(x_ref, slot(*me), local_sem)
        first = [copy(0, me, sibling, src=x_ref)]
        first += [copy(1 + j, me, (*chip, c_), src=x_ref) for j, chip in enumerate(chips)]
        return copy, me, sibling, chips, c_, mine, first

    def start(ins, outs, sems):
        *_, mine, first = parts(ins, outs, sems)
        mine.start()
        for cp in first:
            cp.start()

    def finish(ins, outs, sems):
        copy, me, sibling, chips, c_, mine, first = parts(ins, outs, sems)
        passed = [copy(4 + j, (*chip, c_), sibling) for j, chip in enumerate(chips)]
        for j, chip in enumerate(chips):
            copy(1 + j, (*chip, c_), me).wait_recv()
            passed[j].start()
        copy(0, sibling, me).wait_recv()
        for j, chip in enumerate(chips):
            copy(4 + j, (*chip, 1 - c_), me).wait_recv()
        for cp in first + passed:
            cp.wait_send()
        mine.wait()

    return dict(ins=[x], outs=[jax.ShapeDtypeStruct((N_DEV,) + tuple(x.shape), x.dtype)],
                sems=[pltpu.SemaphoreType.DMA((7,)), pltpu.SemaphoreType.DMA((7,)), pltpu.SemaphoreType.DMA],
                start=start, finish=finish)


def all_gather_2level(name, x):
    return run_exchange(name, all_gather_2level_spec(x))[0]


def sibling_swap_spec(x):
    def copies(ins, outs, sems):
        (x_ref,), (theirs_ref,), (send_sems, recv_sems) = ins, outs, sems
        x_, y_, c_ = lax.axis_index("x"), lax.axis_index("y"), lax.axis_index("c")
        return [pltpu.make_async_remote_copy(
            src_ref=x_ref.at[2 * j + 1 - c_], dst_ref=theirs_ref.at[j], send_sem=send_sems.at[j], recv_sem=recv_sems.at[j],
            device_id=(x_, y_, 1 - c_), device_id_type=pl.DeviceIdType.MESH) for j in range(4)]

    def start(ins, outs, sems):
        for cp in copies(ins, outs, sems):
            cp.start()

    def finish(ins, outs, sems):
        for cp in copies(ins, outs, sems):
            cp.wait()

    return dict(ins=[x], outs=[jax.ShapeDtypeStruct((4,) + tuple(x.shape[1:]), x.dtype)],
                sems=[pltpu.SemaphoreType.DMA((4,)), pltpu.SemaphoreType.DMA((4,))], start=start, finish=finish)


def sibling_swap(name, x):
    return run_exchange(name, sibling_swap_spec(x))[0]


def pair_add(name, stack, theirs, out_dtype, comm=None):
    shp = theirs.shape
    c = shp[-1]
    r = math.prod(shp[1:-1])
    br = r
    for cand in (1024, 512, 256, 128, 64, 32, 16, 8):
        if r % cand == 0 and cand * c <= 256 * 1024:
            br = cand
            break

    def body(s0_ref, s1_ref, t_ref, o_ref):
        mine = jnp.where(lax.axis_index("c") == 0, s0_ref[...], s1_ref[...])
        o_ref[...] = (mine + t_ref[...]).astype(o_ref.dtype)

    s4 = stack.reshape(4, 2, r, c)
    out, *comm_out = _gcall(
        body, name, (4, r // br),
        [_bs((None, None, br, c), lambda j, i: (j, 0, i, 0)), _bs((None, None, br, c), lambda j, i: (j, 1, i, 0)),
         _bs((None, br, c), lambda j, i: (j, i, 0))],
        [_bs((None, br, c), lambda j, i: (j, i, 0))], [jax.ShapeDtypeStruct((4, r, c), out_dtype)], [],
        (s4, s4, theirs.reshape(4, r, c)), comm)
    return out.reshape(shp), comm_out


def chip_exchange_spec(p):
    def parts(ins, outs, sems, sending=False):
        (p_ref,), (out_ref,), (send_sems, recv_sems, local_sem) = ins, outs, sems
        x_, y_, c_ = lax.axis_index("x"), lax.axis_index("y"), lax.axis_index("c")
        my_chip = 2 * x_ + y_
        chips = [(1 - x_, y_), (x_, 1 - y_), (1 - x_, 1 - y_)]
        local = pltpu.make_async_copy(p_ref.at[my_chip], out_ref.at[my_chip], local_sem)
        sends = [pltpu.make_async_remote_copy(
            src_ref=p_ref.at[2 * px + py], dst_ref=out_ref.at[my_chip], send_sem=send_sems.at[k], recv_sem=recv_sems.at[k],
            device_id=(px, py, c_), device_id_type=pl.DeviceIdType.MESH) for k, (px, py) in enumerate(chips)]
        recvs = [] if sending else [pltpu.make_async_remote_copy(
            src_ref=p_ref.at[my_chip], dst_ref=out_ref.at[2 * px + py], send_sem=send_sems.at[k], recv_sem=recv_sems.at[k],
            device_id=(px, py, c_), device_id_type=pl.DeviceIdType.MESH) for k, (px, py) in enumerate(chips)]
        return local, sends, recvs

    def start(ins, outs, sems):
        local, sends, _ = parts(ins, outs, sems, sending=True)
        local.start()
        for cp in sends:
            cp.start()

    def finish(ins, outs, sems):
        local, sends, recvs = parts(ins, outs, sems)
        for cp in recvs:
            cp.wait_recv()
        for cp in sends:
            cp.wait_send()
        local.wait()

    return dict(ins=[p], outs=[jax.ShapeDtypeStruct(p.shape, p.dtype)],
                sems=[pltpu.SemaphoreType.DMA((3,)), pltpu.SemaphoreType.DMA((3,)), pltpu.SemaphoreType.DMA],
                start=start, finish=finish)


def chip_exchange(name, p):
    return run_exchange(name, chip_exchange_spec(p))[0]


def chip_partials(name, stack):
    return chip_partials_many([name], [stack])[0]


def chip_partials_many(names, stacks):
    theirs = sibling_swap(names[0] + "_d2d", stacks[0])
    parts = []
    for i, (name, stack) in enumerate(zip(names, stacks)):
        nxt = [sibling_swap_spec(stacks[i + 1])] if i + 1 < len(stacks) else None
        part, got = pair_add(name + "_add", stack, theirs, BF16, comm=nxt)
        parts.append(part)
        theirs = got[0] if got else None
    return parts


def reduce_scatter_2level(name, stack):
    return chip_exchange(name + "_ici", chip_partials(name, stack))


def ln_res_f(h, r, g, b):
    x = ALPHA * h + r
    mu = jnp.mean(x, axis=-1, keepdims=True)
    var = jnp.mean(jnp.square(x - mu), axis=-1, keepdims=True)
    return (x - mu) * lax.rsqrt(var + LN_EPS) * g + b


def rms_f(x, g):
    return x * lax.rsqrt(jnp.mean(x * x, axis=-1, keepdims=True) + RMS_EPS) * g


def rope_f(x1, x2, c, s):
    return x1 * c - x2 * s, x1 * s + x2 * c


def swiglu_f(g, u):
    return jax.nn.silu(g) * u


def tshift_f(d, dprev, mu):
    return d + (dprev - d) * mu


def loss_f(y, tgt):
    e = y - tgt
    return e / y.shape[-1], 0.5 * jnp.mean(e * e, axis=-1, keepdims=True)


def adamw(name, w, m, v, gstack):
    shp = w.shape
    c = shp[-1]
    r = math.prod(shp[:-1])
    br = r
    for cand in (512, 256, 128, 64, 32, 16, 8):
        if r % cand == 0 and cand * c <= 128 * 1024:
            br = cand
            break
    k = gstack.shape[0]

    def body(w_ref, m_ref, v_ref, g_ref, go_ref, d_ref, mo_ref, vo_ref):
        g = g_ref[0].astype(F32)
        for j in range(1, k):
            g = g + g_ref[j].astype(F32)
        m_new = ADAM_B1 * m_ref[...] + (1.0 - ADAM_B1) * g
        v_new = ADAM_B2 * v_ref[...] + (1.0 - ADAM_B2) * jnp.square(g)
        m_hat = m_new / (1.0 - ADAM_B1 ** ADAM_STEP)
        v_hat = v_new / (1.0 - ADAM_B2 ** ADAM_STEP)
        go_ref[...] = g
        d_ref[...] = -ADAM_LR * (m_hat / (jnp.sqrt(v_hat) + ADAM_EPS) + ADAM_WD * w_ref[...])
        mo_ref[...] = m_new
        vo_ref[...] = v_new

    spec = _bs((br, c), lambda i: (i, 0))
    outs = _pcall(
        body, name=name, grid=(r // br,),
        in_specs=[spec, spec, spec, _bs((k, br, c), lambda i: (0, i, 0))],
        out_specs=[spec] * 4, out_shape=[jax.ShapeDtypeStruct((r, c), F32)] * 4,
        compiler_params=_cparams(),
    )(w.reshape(r, c), m.reshape(r, c), v.reshape(r, c), gstack.reshape(k, r, c))
    return tuple(o.reshape(shp) for o in outs)


EV_W = (512, 256, 64, 512, 512, 1024, 1024, 16)
EV_IN, EV_PAD = 3920, 4096
OD_IN, OD_PAD = 6592, 6656
TM = 256


def _offsets(widths):
    offs, acc = [], 0
    for w in widths:
        offs.append((acc, acc + w))
        acc += w
    return offs


def _rope_tables(seq, dim):
    inv = 10000.0 ** (-jnp.arange(0, dim, 2, dtype=F32) / dim)
    ang = jnp.arange(seq, dtype=F32)[:, None] * inv[None, :]
    return jnp.cos(ang), jnp.sin(ang)


def _halves(x, nh):
    t, w = x.shape
    x3 = x.reshape(t, nh, w // nh)
    hd = w // nh // 2
    return x3[:, :, :hd].reshape(t, nh * hd), x3[:, :, hd:].reshape(t, nh * hd)


def _unhalves(x1, x2, nh):
    t = x1.shape[0]
    return jnp.concatenate([x1.reshape(t, nh, -1), x2.reshape(t, nh, -1)], axis=2).reshape(t, -1)


def ln_res2_f(h, r, g, b):
    y = ln_res_f(h, r, g, b)
    return y, y


def _ln(name, h, r, g, b, cts=None):
    if cts is None:
        return _rows(name, ln_res2_f, [h, r], [g, b], [h.shape[1]] * 2, TM, dtypes=[F32, BF16])
    return _rows(name + "_bwd", ln_res_f, [h, r], [g, b], None, TM, fwd=False, cts=[cts], wrt_rows=(0, 1), wrt_params=(0, 1),
                 dtypes=[F32, BF16])


def _gather_specs(late_local, keys):
    return [all_gather_2level_spec(late_local[k]) for k in keys]


def _put_gathered(p, late_local, keys, gathered):
    for k, g in zip(keys, gathered):
        p[k[0]][k[1]] = _unshard(g, late_local[k].shape, BIG[k[0]])


def _tail_fwd(l, h, hb, mem2b, bsz, p, late_local=None):
    qx = matmul(f"xa_q{l}", hb, p["xa_w_q"][l])
    kx = matmul(f"xa_k{l}", mem2b, p["xa_w_k"][l])
    vx = matmul(f"xa_v{l}", mem2b, p["xa_w_v"][l])
    ox = xattn(bsz, qx, kx, vx)
    xa = matmul(f"xa_o{l}", ox, p["xa_w_o"][l])
    h2, h2b = _ln(f"ln_xa{l}", h, xa, p["ln_xa_g"][l:l + 1], p["ln_xa_b"][l:l + 1])
    if late_local is None:
        gg = matmul(f"ffn_g{l}", h2b, p["ffn_w_gate"][l])
        uu = matmul(f"ffn_u{l}", h2b, p["ffn_w_up"][l])
    else:
        k_down, k_in = [("ffn_w_down", l)], [("od_w_in", 0)]
        gg, got = matmul(f"ffn_g{l}", h2b, p["ffn_w_gate"][l], comm=_gather_specs(late_local, k_down))
        _put_gathered(p, late_local, k_down, got)
        uu, got = matmul(f"ffn_u{l}", h2b, p["ffn_w_up"][l], comm=_gather_specs(late_local, k_in))
        _put_gathered(p, late_local, k_in, got)
    act = _rows(f"swiglu{l}", swiglu_f, [gg, uu], [], [gg.shape[1]], TM, dtypes=[BF16])[0]
    if late_local is None:
        ff = matmul(f"ffn_d{l}", act, p["ffn_w_down"][l])
    else:
        keys = [("od_w_out", 0)]
        ff, got = matmul(f"ffn_d{l}", act, p["ffn_w_down"][l], comm=_gather_specs(late_local, keys))
        _put_gathered(p, late_local, keys, got)
    h3, h3b = _ln(f"ln_ffn{l}", h2, ff, p["ln_ffn_g"][l:l + 1], p["ln_ffn_b"][l:l + 1])
    return h3, h3b, (h, hb, qx, kx, vx, ox, xa, h2, h2b, gg, uu, act, ff)


def _tail_bwd(l, dh3, saved, mem2b, bsz, p, gr):
    h, hb, qx, kx, vx, ox, xa, h2, h2b, gg, uu, act, ff = saved
    dh2, dff, gr["ln_ffn_g"][l], gr["ln_ffn_b"][l] = _ln(f"ln_ffn{l}", h2, ff, p["ln_ffn_g"][l:l + 1], p["ln_ffn_b"][l:l + 1], cts=dh3)
    dact = matmul(f"ffn_d_dx{l}", dff, p["ffn_w_down"][l], tb=True)
    gr["ffn_w_down"][l] = matmul(f"ffn_d_dw{l}", act, dff, ta=True)
    dgg, duu = _rows(f"swiglu_bwd{l}", swiglu_f, [gg, uu], [], None, TM, fwd=False, cts=[dact], wrt_rows=(0, 1),
                     dtypes=[BF16, BF16])
    gr["ffn_w_gate"][l] = matmul(f"ffn_g_dw{l}", h2b, dgg, ta=True)
    gr["ffn_w_up"][l] = matmul(f"ffn_u_dw{l}", h2b, duu, ta=True)
    dh2 = matmul(f"ffn_g_dx{l}", dgg, p["ffn_w_gate"][l], tb=True, add=dh2)
    dh2 = matmul(f"ffn_u_dx{l}", duu, p["ffn_w_up"][l], tb=True, add=dh2)
    dh, dxa, gr["ln_xa_g"][l], gr["ln_xa_b"][l] = _ln(f"ln_xa{l}", h, xa, p["ln_xa_g"][l:l + 1], p["ln_xa_b"][l:l + 1], cts=dh2)
    dox = matmul(f"xa_o_dx{l}", dxa, p["xa_w_o"][l], tb=True)
    gr["xa_w_o"][l] = matmul(f"xa_o_dw{l}", ox, dxa, ta=True)
    dqx, dkx, dvx = xattn(bsz, qx, kx, vx, cts=dox)
    gr["xa_w_q"][l] = matmul(f"xa_q_dw{l}", hb, dqx, ta=True)
    gr["xa_w_k"][l] = matmul(f"xa_k_dw{l}", mem2b, dkx, ta=True)
    gr["xa_w_v"][l] = matmul(f"xa_v_dw{l}", mem2b, dvx, ta=True)
    return matmul(f"xa_q_dx{l}", dqx, p["xa_w_q"][l], tb=True, add=dh)


def _uq_perm(w):
    w3 = w.reshape(w.shape[0], 8, 192)
    return jnp.concatenate([w3[:, :, :128].reshape(-1, 1024), w3[:, :, 128:160].reshape(-1, 256),
                            w3[:, :, 160:].reshape(-1, 256)], axis=1)


def _uq_unperm(g):
    r = g.shape[0]
    return jnp.concatenate([g[:, :1024].reshape(r, 8, 128), g[:, 1024:1280].reshape(r, 8, 32),
                            g[:, 1280:].reshape(r, 8, 32)], axis=2).reshape(r, 1536)


def _ukv_perm(w):
    w3 = w.reshape(w.shape[0], 8, 256)
    return jnp.concatenate([w3[:, :, :128].reshape(-1, 1024), w3[:, :, 128:].reshape(-1, 1024)], axis=1)


def _ukv_unperm(g):
    r = g.shape[0]
    return jnp.concatenate([g[:, :1024].reshape(r, 8, 128), g[:, 1024:].reshape(r, 8, 128)], axis=2).reshape(r, 2048)


def _pad_cols(w, n):
    return jnp.pad(w, ((0, 0), (0, n - w.shape[1])))


def _shift_prev(x, bsz):
    t, w = x.shape
    x3 = x.reshape(bsz, t // bsz, w)
    return jnp.pad(x3, ((0, 0), (1, 0), (0, 0)))[:, :-1].reshape(t, w)


def _shift_next(x, bsz):
    t, w = x.shape
    x3 = x.reshape(bsz, t // bsz, w)
    return jnp.pad(x3[:, 1:], ((0, 0), (0, 1), (0, 0))).reshape(t, w)


LATE = ("xa_w_q", "xa_w_k", "xa_w_v", "xa_w_o", "ffn_w_gate", "ffn_w_up", "ffn_w_down")


def _late_partials(layer, gr):
    return chip_partials_many([f"rs{layer}_{k}" for k in LATE], [_shard_stack(gr[k][layer], BIG[k]) for k in LATE])


def device_step(x, mem, tgt, p, late_local):
    bsz, seq, d = x.shape
    t = bsz * seq
    x2, mem2, tgt2 = x.reshape(t, d), mem.reshape(bsz * mem.shape[1], d), tgt.reshape(t, d)
    x2b, mem2b = x2.astype(BF16), mem2.astype(BF16)
    gr = {k: [None] * DEPTH for k in ("ln_mix_g", "ln_mix_b", "xa_w_q", "xa_w_k", "xa_w_v", "xa_w_o", "ln_xa_g", "ln_xa_b",
                                      "ffn_w_gate", "ffn_w_up", "ffn_w_down", "ln_ffn_g", "ln_ffn_b")}
    cos_pe, sin_pe = _rope_tables(seq, 64)
    cos_c, sin_c = _rope_tables(seq, 128)
    cq, sq = jnp.tile(cos_pe, (bsz, 8)), jnp.tile(sin_pe, (bsz, 8))
    ck, sk = jnp.tile(cos_pe, (bsz, 1)), jnp.tile(sin_pe, (bsz, 1))
    cd, sd = jnp.tile(cos_c, (bsz, 8)), jnp.tile(sin_c, (bsz, 8))

    w_in0 = _pad_cols(p["ev_w_in"][0], EV_PAD)
    keys = [(k, 0) for k in ("ev_mla_w_uq", "ev_mla_w_ukv", "ev_w_out")]
    z0, got = matmul("ev_in", x2b, w_in0, comm=_gather_specs(late_local, keys))
    _put_gathered(p, late_local, keys, got)
    w_uq, w_ukv = _uq_perm(p["ev_mla_w_uq"][0]), _ukv_perm(p["ev_mla_w_ukv"][0])
    c_q, c_kv, k_pe, q_g, k_g, v_g, r_g, lr_g = (z0[:, a:b] for a, b in _offsets(EV_W))
    qr = _rows("q_rms", rms_f, [c_q], [p["ev_mla_q_norm"]], [512], TM, dtypes=[BF16])[0]
    kvr = _rows("kv_rms", rms_f, [c_kv], [p["ev_mla_kv_norm"]], [256], TM, dtypes=[BF16])[0]
    q = matmul("mla_uq", qr, w_uq)
    kv = matmul("mla_ukv", kvr, w_ukv)
    qn, qp1, qp2 = q[:, :1024], q[:, 1024:1280], q[:, 1280:]
    kn, vv = kv[:, :1024], kv[:, 1024:]
    kp1, kp2 = k_pe[:, :32], k_pe[:, 32:]
    r1, r2 = _rows("rope_q", rope_f, [qp1, qp2, cq, sq], [], [256, 256], TM)
    kr1, kr2 = _rows("rope_k", rope_f, [kp1, kp2, ck, sk], [], [32, 32], TM)
    r1h, r2h = _to_heads(r1, bsz, 8), _to_heads(r2, bsz, 8)
    keys = [(k, 0) for k in ("xa_w_q", "xa_w_k", "xa_w_v", "xa_w_o")]
    a_out, got = mla_attn(bsz, qn, r1h, r2h, kn, kr1, kr2, vv, comm=_gather_specs(late_local, keys))
    _put_gathered(p, late_local, keys, got)
    gla_prm = (p["ev_gla_w_gate2"][0], p["ev_gla_b_gate"], p["ev_gla_norm_g"], p["ev_gla_norm_b"])
    keys = [("ffn_w_gate", 0), ("ffn_w_up", 0)]
    b_out, gla_sv, got = gla_block(bsz, q_g, k_g, v_g, r_g, lr_g, *gla_prm, comm=_gather_specs(late_local, keys))
    _put_gathered(p, late_local, keys, got)
    mixin0 = jnp.concatenate([a_out, b_out], axis=1)
    mix0 = matmul("ev_out", mixin0, p["ev_w_out"][0])
    h1, h1b = _ln("ln_mix0", x2, mix0, p["ln_mix_g"][0:1], p["ln_mix_b"][0:1])
    h3, h3b, tail0 = _tail_fwd(0, h1, h1b, mem2b, bsz, p, late_local)

    w_in1 = _pad_cols(p["od_w_in"][0], OD_PAD)
    z1 = matmul("od_in", h3b, w_in1)
    dq_, dk_, dv_ = z1[:, :1024], z1[:, 1024:2048], z1[:, 2048:3072]
    d_in = z1[:, 3072:OD_IN]
    q1, q2 = _halves(dq_, 8)
    k1, k2 = _halves(dk_, 8)
    qd1, qd2 = _rows("rope_dq", rope_f, [q1, q2, cd, sd], [], [512, 512], TM)
    kd1, kd2 = _rows("rope_dk", rope_f, [k1, k2, cd, sd], [], [512, 512], TM)
    c_out = dil_block(bsz, qd1, qd2, kd1, kd2, dv_)
    d_prev = _shift_prev(d_in, bsz)
    mu = p["od_rwkv_mu"]
    ds = _rows("tshift", tshift_f, [d_in, d_prev], [mu], [d_in.shape[1]], TM)[0]
    rw_in = tuple(ds[:, a:b] for a, b in _offsets((1024, 1024, 1024, 96, 96, 256)))
    rw_prm = dict(w0=p["od_rwkv_w0"], wd2=p["od_rwkv_w_decay2"][0], a0=p["od_rwkv_a0"], wa2=p["od_rwkv_w_a2"][0],
                  wg2=p["od_rwkv_w_gate2"][0], k_k=p["od_rwkv_k_k"], k_a=p["od_rwkv_k_a"], r_k=p["od_rwkv_r_k"][0],
                  gn_g=p["od_rwkv_gn_g"], gn_b=p["od_rwkv_gn_b"])
    keys = [(k, 1) for k in LATE]
    d_out, rw_scan, got = rwkv_block(bsz, *rw_in, rw_prm, comm=_gather_specs(late_local, keys))
    _put_gathered(p, late_local, keys, got)
    mixin1 = jnp.concatenate([c_out, d_out], axis=1)
    mix1 = matmul("od_out", mixin1, p["od_w_out"][0])
    h4, h4b = _ln("ln_mix1", h3, mix1, p["ln_mix_g"][1:2], p["ln_mix_b"][1:2])
    y, _, tail1 = _tail_fwd(1, h4, h4b, mem2b, bsz, p)

    dy, row_loss = _rows("loss", loss_f, [y, tgt2], [], [d, 1], TM)
    loss = jnp.sum(row_loss)

    dh4 = _tail_bwd(1, dy, tail1, mem2b, bsz, p, gr)
    dh3, dmix1, gr["ln_mix_g"][1], gr["ln_mix_b"][1] = _ln("ln_mix1", h3, mix1, p["ln_mix_g"][1:2], p["ln_mix_b"][1:2], cts=dh4)
    dmixin1 = matmul("od_out_dx", dmix1, p["od_w_out"][0], tb=True)
    gr["od_w_out"] = matmul("od_out_dw", mixin1, dmix1, ta=True)[None]
    dc_out, dd_out = dmixin1[:, :1024], dmixin1[:, 1024:]
    drw_in, drw_prm, done1 = rwkv_block(bsz, *rw_in, rw_prm, cts=dd_out, scan=rw_scan,
                                        comm=[chip_exchange_spec(x_) for x_ in _late_partials(1, gr)])
    dds = jnp.concatenate(drw_in, axis=1)
    dd_in, dd_prev, dmu = _rows("tshift_bwd", tshift_f, [d_in, d_prev], [mu], None, TM, fwd=False, cts=[dds],
                                wrt_rows=(0, 1), wrt_params=(0,))
    dd_in = dd_in + _shift_next(dd_prev, bsz)
    dqd1, dqd2, dkd1, dkd2, ddv = dil_block(bsz, qd1, qd2, kd1, kd2, dv_, cts=dc_out)
    dq1, dq2 = _rows("rope_dq_bwd", rope_f, [q1, q2, cd, sd], [], None, TM, fwd=False, cts=[dqd1, dqd2], wrt_rows=(0, 1))
    dk1, dk2 = _rows("rope_dk_bwd", rope_f, [k1, k2, cd, sd], [], None, TM, fwd=False, cts=[dkd1, dkd2], wrt_rows=(0, 1))
    dz1 = jnp.concatenate([_unhalves(dq1, dq2, 8), _unhalves(dk1, dk2, 8), ddv, dd_in,
                           jnp.zeros((t, OD_PAD - OD_IN), F32)], axis=1).astype(BF16)
    gr["od_w_in"] = matmul("od_in_dw", h3b, dz1, ta=True)[:, :OD_IN][None]
    dh3 = matmul("od_in_dx", dz1, w_in1, tb=True, add=dh3)
    gr["od_rwkv_mu"] = dmu
    gr["od_rwkv_w0"], gr["od_rwkv_w_decay2"], gr["od_rwkv_a0"] = drw_prm["w0"], drw_prm["wd2"][None], drw_prm["a0"]
    gr["od_rwkv_w_a2"], gr["od_rwkv_w_gate2"] = drw_prm["wa2"][None], drw_prm["wg2"][None]
    gr["od_rwkv_k_k"], gr["od_rwkv_k_a"], gr["od_rwkv_r_k"] = drw_prm["k_k"], drw_prm["k_a"], drw_prm["r_k"][None]
    gr["od_rwkv_gn_g"], gr["od_rwkv_gn_b"] = drw_prm["gn_g"], drw_prm["gn_b"]

    dh1 = _tail_bwd(0, dh3, tail0, mem2b, bsz, p, gr)
    dx2, dmix0, gr["ln_mix_g"][0], gr["ln_mix_b"][0] = _ln("ln_mix0", x2, mix0, p["ln_mix_g"][0:1], p["ln_mix_b"][0:1], cts=dh1)
    dmixin0 = matmul("ev_out_dx", dmix0, p["ev_w_out"][0], tb=True)
    gr["ev_w_out"] = matmul("ev_out_dw", mixin0, dmix0, ta=True)[None]
    da_out, db_out = dmixin0[:, :1024], dmixin0[:, 1024:]
    od_keys = ("od_w_in", "od_w_out", "ev_w_out")
    od_parts = chip_partials_many(["rs_" + k for k in od_keys], [_shard_stack(gr[k], BIG[k]) for k in od_keys])
    dq_g, dk_g, dv_g, dlr4, dr_g, dw2, dbg, dng, dnb, done_in, done_out, done_ev_out = gla_block(
        bsz, q_g, k_g, v_g, r_g, lr_g, *gla_prm, cts=(db_out, gla_sv), comm=[chip_exchange_spec(x_) for x_ in od_parts])
    dlr_g = jnp.sum(dlr4, axis=0)
    dqn, dr1h, dr2h, dkn, dkr1, dkr2, dvv, *done0 = mla_attn(bsz, qn, r1h, r2h, kn, kr1, kr2, vv, cts=da_out, tq=512,
                                                             comm=[chip_exchange_spec(x_) for x_ in _late_partials(0, gr)])
    dqp1, dqp2 = _rows("rope_q_bwd", rope_f, [qp1, qp2, cq, sq], [], None, TM, fwd=False,
                       cts=[_from_heads(dr1h), _from_heads(dr2h)], wrt_rows=(0, 1))
    dkp1, dkp2 = _rows("rope_k_bwd", rope_f, [kp1, kp2, ck, sk], [], None, TM, fwd=False, cts=[dkr1, dkr2], wrt_rows=(0, 1))
    dq = jnp.concatenate([dqn, dqp1, dqp2], axis=1).astype(BF16)
    dkv = jnp.concatenate([dkn, dvv], axis=1).astype(BF16)
    dqr = matmul("mla_uq_dx", dq, w_uq, tb=True)
    gr["ev_mla_w_uq"] = _uq_unperm(matmul("mla_uq_dw", qr, dq, ta=True))[None]
    dkvr = matmul("mla_ukv_dx", dkv, w_ukv, tb=True)
    gr["ev_mla_w_ukv"] = _ukv_unperm(matmul("mla_ukv_dw", kvr, dkv, ta=True))[None]
    dc_q, gr["ev_mla_q_norm"] = _rows("q_rms_bwd", rms_f, [c_q], [p["ev_mla_q_norm"]], None, TM, fwd=False, cts=[dqr],
                                      wrt_rows=(0,), wrt_params=(0,))
    dc_kv, gr["ev_mla_kv_norm"] = _rows("kv_rms_bwd", rms_f, [c_kv], [p["ev_mla_kv_norm"]], None, TM, fwd=False, cts=[dkvr],
                                        wrt_rows=(0,), wrt_params=(0,))
    dz0 = jnp.concatenate([dc_q, dc_kv, dkp1, dkp2, dq_g, dk_g, dv_g, dr_g, dlr_g,
                           jnp.zeros((t, EV_PAD - EV_IN), F32)], axis=1).astype(BF16)
    gr["ev_w_in"] = matmul("ev_in_dw", x2b, dz0, ta=True)[:, :EV_IN][None]
    dx2 = matmul("ev_in_dx", dz0, w_in0, tb=True, add=dx2)
    gr["ev_gla_w_gate2"], gr["ev_gla_b_gate"] = dw2[None], dbg
    gr["ev_gla_norm_g"], gr["ev_gla_norm_b"] = dng, dnb
    stacks = {k: jnp.stack([s0, s1], axis=1) for k, s0, s1 in zip(LATE, done0, done1)}
    stacks.update(od_w_in=done_in, od_w_out=done_out, ev_w_out=done_ev_out)
    for k in stacks:
        del gr[k]
    for k in list(gr):
        if isinstance(gr[k], list):
            gr[k] = jnp.stack([g[0] if k.startswith("ln_") else g for g in gr[k]])
    return loss, dx2.reshape(bsz, seq, d), gr, stacks


WEIGHTS = ['ev_w_in', 'ev_mla_q_norm', 'ev_mla_w_uq', 'ev_mla_kv_norm', 'ev_mla_w_ukv', 'ev_gla_w_gate2', 'ev_gla_b_gate',
           'ev_gla_norm_g', 'ev_gla_norm_b', 'ev_w_out', 'od_w_in', 'od_rwkv_mu', 'od_rwkv_w0', 'od_rwkv_w_decay2',
           'od_rwkv_a0', 'od_rwkv_w_a2', 'od_rwkv_w_gate2', 'od_rwkv_k_k', 'od_rwkv_k_a', 'od_rwkv_r_k', 'od_rwkv_gn_g',
           'od_rwkv_gn_b', 'od_w_out', 'ln_mix_g', 'ln_mix_b', 'xa_w_q', 'xa_w_k', 'xa_w_v', 'xa_w_o', 'ln_xa_g', 'ln_xa_b',
           'ffn_w_gate', 'ffn_w_up', 'ffn_w_down', 'ln_ffn_g', 'ln_ffn_b']
BIG = {'ev_w_in': -1, 'ev_mla_w_uq': -1, 'ev_mla_w_ukv': -1, 'ev_w_out': -2, 'od_w_in': -1, 'od_w_out': -2,
       'xa_w_q': -2, 'xa_w_k': -2, 'xa_w_v': -2, 'xa_w_o': -2, 'ffn_w_gate': -1, 'ffn_w_up': -1, 'ffn_w_down': -2}
SMALL = ['ev_gla_w_gate2', 'od_rwkv_mu', 'od_rwkv_w0', 'od_rwkv_w_decay2', 'od_rwkv_a0', 'od_rwkv_w_a2', 'od_rwkv_w_gate2',
         'od_rwkv_k_k', 'od_rwkv_k_a', 'od_rwkv_gn_g', 'od_rwkv_gn_b']
REPL = ['ev_mla_q_norm', 'ev_mla_kv_norm', 'ev_gla_b_gate', 'ev_gla_norm_g', 'ev_gla_norm_b', 'od_rwkv_r_k',
        'ln_mix_g', 'ln_mix_b', 'ln_xa_g', 'ln_xa_b', 'ln_ffn_g', 'ln_ffn_b']
PACK_COLS = 128


def _unshard(g, shape, axis):
    axis %= len(shape)
    full = list(shape)
    full[axis] *= N_DEV
    return jnp.moveaxis(g, 0, axis).reshape(full)


def _shard_stack(gfull, axis):
    axis %= gfull.ndim
    shp = list(gfull.shape)
    shp[axis:axis + 1] = [N_DEV, shp[axis] // N_DEV]
    return jnp.moveaxis(gfull.reshape(shp), axis, 0)


def _pack(arrs):
    lead = arrs[0].shape[0]
    flat = jnp.concatenate([a.reshape(lead, -1) for a in arrs], axis=1)
    n = flat.shape[1]
    rows = -(-n // (8 * PACK_COLS)) * 8
    return jnp.pad(flat, ((0, 0), (0, rows * PACK_COLS - n))).reshape(lead, rows, PACK_COLS)


def _unpack(buf, shapes):
    lead = buf.shape[0]
    flat = buf.reshape(lead, -1)
    out, off = [], 0
    for shp in shapes:
        n = math.prod(shp)
        out.append(flat[:, off:off + n].reshape((lead,) + tuple(shp)))
        off += n
    return out


def train_step(x, mem, loss_target, w, m, v):
    p, late_local = {}, {}
    for k, ax in BIG.items():
        if k != "ev_w_in":
            p[k] = [None] * w[k].shape[0]
            for l in range(w[k].shape[0]):
                late_local[(k, l)] = w[k][l].astype(BF16)
        else:
            p[k] = _unshard(all_gather_2level("ag_" + k, w[k].astype(BF16)), w[k].shape, ax)
    small_loc = _pack([w[k][None] for k in SMALL])[0]
    small_all = _unpack(all_gather("ag_small", small_loc), [w[k].shape for k in SMALL])
    for k, g in zip(SMALL, small_all):
        p[k] = _unshard(g, w[k].shape, -1)
    for k in REPL:
        p[k] = w[k]
    loss, dx, gr, stacks = device_step(x, mem, loss_target, p, late_local)
    loss = lax.psum(loss, ("x", "y", "c"))

    rest = [k for k in BIG if k not in stacks]
    parts = chip_partials_many(["rs_" + k for k in rest], [_shard_stack(gr[k], BIG[k]) for k in rest])
    for k, part in zip(rest, parts):
        stacks[k] = chip_exchange("rs_" + k + "_ici", part)
    small_send = _pack([_shard_stack(gr[k], -1) for k in SMALL])
    small_recv = _unpack(reduce_scatter_exchange("rs_small", small_send), [w[k].shape for k in SMALL])
    stacks.update(zip(SMALL, small_recv))
    repl_loc = _pack([gr[k][None] for k in REPL])[0]
    repl_all = _unpack(all_gather("ag_repl_grads", repl_loc), [w[k].shape for k in REPL])
    stacks.update(zip(REPL, repl_all))

    grads, deltas, new_m, new_v = [], [], [], []
    for k in WEIGHTS:
        g, dl, mn, vn = adamw("adamw_" + k, w[k], m[k], v[k], stacks[k])
        grads.append(g), deltas.append(dl), new_m.append(mn), new_v.append(vn)
    return (loss, dx, *grads, *deltas, *new_m, *new_v)


def kernel(x, mem, ev_w_in, ev_mla_q_norm, ev_mla_w_uq, ev_mla_kv_norm, ev_mla_w_ukv, ev_gla_w_gate2, ev_gla_b_gate, ev_gla_norm_g, ev_gla_norm_b, ev_w_out, od_w_in, od_rwkv_mu, od_rwkv_w0, od_rwkv_w_decay2, od_rwkv_a0, od_rwkv_w_a2, od_rwkv_w_gate2, od_rwkv_k_k, od_rwkv_k_a, od_rwkv_r_k, od_rwkv_gn_g, od_rwkv_gn_b, od_w_out, ln_mix_g, ln_mix_b, xa_w_q, xa_w_k, xa_w_v, xa_w_o, ln_xa_g, ln_xa_b, ffn_w_gate, ffn_w_up, ffn_w_down, ln_ffn_g, ln_ffn_b, loss_target, m_ev_w_in, m_ev_mla_q_norm, m_ev_mla_w_uq, m_ev_mla_kv_norm, m_ev_mla_w_ukv, m_ev_gla_w_gate2, m_ev_gla_b_gate, m_ev_gla_norm_g, m_ev_gla_norm_b, m_ev_w_out, m_od_w_in, m_od_rwkv_mu, m_od_rwkv_w0, m_od_rwkv_w_decay2, m_od_rwkv_a0, m_od_rwkv_w_a2, m_od_rwkv_w_gate2, m_od_rwkv_k_k, m_od_rwkv_k_a, m_od_rwkv_r_k, m_od_rwkv_gn_g, m_od_rwkv_gn_b, m_od_w_out, m_ln_mix_g, m_ln_mix_b, m_xa_w_q, m_xa_w_k, m_xa_w_v, m_xa_w_o, m_ln_xa_g, m_ln_xa_b, m_ffn_w_gate, m_ffn_w_up, m_ffn_w_down, m_ln_ffn_g, m_ln_ffn_b, v_ev_w_in, v_ev_mla_q_norm, v_ev_mla_w_uq, v_ev_mla_kv_norm, v_ev_mla_w_ukv, v_ev_gla_w_gate2, v_ev_gla_b_gate, v_ev_gla_norm_g, v_ev_gla_norm_b, v_ev_w_out, v_od_w_in, v_od_rwkv_mu, v_od_rwkv_w0, v_od_rwkv_w_decay2, v_od_rwkv_a0, v_od_rwkv_w_a2, v_od_rwkv_w_gate2, v_od_rwkv_k_k, v_od_rwkv_k_a, v_od_rwkv_r_k, v_od_rwkv_gn_g, v_od_rwkv_gn_b, v_od_w_out, v_ln_mix_g, v_ln_mix_b, v_xa_w_q, v_xa_w_k, v_xa_w_v, v_xa_w_o, v_ln_xa_g, v_ln_xa_b, v_ffn_w_gate, v_ffn_w_up, v_ffn_w_down, v_ln_ffn_g, v_ln_ffn_b):
    given = dict(locals())
    w = {k: given[k] for k in WEIGHTS}
    m = {k: given["m_" + k] for k in WEIGHTS}
    v = {k: given["v_" + k] for k in WEIGHTS}
    return train_step(given["x"], given["mem"], given["loss_target"], w, m, v)
```
